```python
import jax
import jax.numpy as jnp
from jax import lax
import numpy as np

D_MODEL = 4096
BATCH = 4
SEQ = 2048
DEPTH = 2
DEC_BATCH = 8
DEC_SEQ = 8
PAST_LEN = 16384
PAGE_SIZE = 128

HEAD_DIM = 128
H_A = 16
H_B = 16
W_A = H_A * HEAD_DIM
W_B = H_B * HEAD_DIM
CONV_A = 4
DELTA_CHUNK = 64
Q_BLOCK = 128
C_GROUPS = ((128, 1), (512, 4), (2048, 16))
H_C = 16
W_C = H_C * HEAD_DIM
BAND_BLOCK = 128
D_FF = 11008
FFN_CONV = 3
N_EVEN = (DEPTH + 1) // 2
N_ODD = DEPTH // 2
N_MOD = 6
RMS_EPS = 1e-6
FORGET_BIAS = 7.0
PROJ_E = 4 * W_A + 2 * H_A + 3 * W_B + H_B
PROJ_O = len(C_GROUPS) * 3 * W_C

kernel_name = 'hybrid_deltanet_fox_dilated_convffn_step'


def rmsnorm(x, w):
    xf = x.astype(jnp.float32)
    y = xf * lax.rsqrt(jnp.mean(xf * xf, axis=-1, keepdims=True) + RMS_EPS)
    return (y * w.astype(jnp.float32)).astype(x.dtype)


def l2norm(x):
    xf = x.astype(jnp.float32)
    return xf * lax.rsqrt(jnp.sum(xf * xf, axis=-1, keepdims=True) + 1e-6)


def modulate(x, w, shift, scale):
    return rmsnorm(x, w) * (1 + scale[:, None, :]) + shift[:, None, :]


def causal_dwconv(u, buf, w):
    t = u.shape[1]
    width = w.shape[0]
    ext = jnp.concatenate([buf.astype(u.dtype), u], axis=1)
    out = ext[:, 0:t] * w[0]
    for j in range(1, width):
        out = out + ext[:, j:j + t] * w[j]
    return out, ext[:, ext.shape[1] - (width - 1):]


def gated_delta_rule(q, k, v, g, beta, state):
    bsz, t, nh, dk = q.shape
    c = min(DELTA_CHUNK, t)
    n = -(-t // c)
    pad = n * c - t

    def prep(a):
        a = jnp.moveaxis(a.astype(jnp.float32), 2, 1)
        a = jnp.pad(a, [(0, 0), (0, 0), (0, pad)] + [(0, 0)] * (a.ndim - 3))
        return a.reshape((bsz, nh, n, c) + a.shape[3:])

    q, k, v, g, beta = (prep(a) for a in (q, k, v, g, beta))
    q = q * (dk ** -0.5)
    gc = jnp.cumsum(g, axis=-1)
    incl = jnp.tril(jnp.ones((c, c), dtype=bool))
    strict = jnp.tril(jnp.ones((c, c), dtype=bool), -1)
    decay = jnp.exp(jnp.where(incl, gc[..., :, None] - gc[..., None, :], -jnp.inf))
    kb = k * beta[..., None]
    vb = v * beta[..., None]
    lower = jnp.where(strict, jnp.einsum('bhncd,bhnsd->bhncs', kb, k) * decay, 0.0)
    eye = jnp.eye(c, dtype=jnp.float32)
    tmat = lax.linalg.triangular_solve(lower + eye, jnp.broadcast_to(eye, lower.shape),
                                       left_side=True, lower=True, unit_diagonal=True)
    u = jnp.einsum('bhncs,bhnse->bhnce', tmat, vb)
    w = jnp.einsum('bhncs,bhnsd->bhncd', tmat, kb * jnp.exp(gc)[..., None])
    attn = jnp.einsum('bhncd,bhnsd->bhncs', q, k) * decay

    def step(s, xs):
        qc, kc, uc, wc, gcc, ac = xs
        v_new = uc - jnp.einsum('bhcd,bhde->bhce', wc, s)
        o = (jnp.einsum('bhcd,bhde->bhce', qc * jnp.exp(gcc)[..., None], s)
             + jnp.einsum('bhcs,bhse->bhce', ac, v_new))
        g_last = gcc[..., -1]
        k_dec = kc * jnp.exp(g_last[..., None] - gcc)[..., None]
        s = s * jnp.exp(g_last)[..., None, None] + jnp.einsum('bhcd,bhce->bhde', k_dec, v_new)
        return s, o

    xs = tuple(jnp.moveaxis(a, 2, 0) for a in (q, k, u, w, gc, attn))
    s_fin, o = lax.scan(step, state.astype(jnp.float32), xs)
    o = jnp.moveaxis(o, 0, 2).reshape(bsz, nh, n * c, -1)[:, :, :t]
    return jnp.moveaxis(o, 1, 2), s_fin


def fox_attention(q, k, v, f_q, f_k, q_offset):
    bsz, t, nh, hd = q.shape
    lk = k.shape[1]
    blk = min(Q_BLOCK, t)
    nb = t // blk
    key_pos = jnp.arange(lk)
    fk = jnp.swapaxes(f_k.astype(jnp.float32), 1, 2)[:, :, None, :]
    qb = jnp.moveaxis(q.reshape(bsz, nb, blk, nh, hd), 1, 0)
    fqb = jnp.moveaxis(f_q.astype(jnp.float32).reshape(bsz, nb, blk, nh), 1, 0)

    def one_block(args):
        qi, fqi, bi = args
        s = jnp.einsum('bqhd,bkhd->bhqk', qi, k, preferred_element_type=jnp.float32) * (hd ** -0.5)
        s = s + jnp.swapaxes(fqi, 1, 2)[..., None] - fk
        q_pos = q_offset + bi * blk + jnp.arange(blk)
        s = jnp.where(key_pos[None, :] <= q_pos[:, None], s, -jnp.inf)
        p = jax.nn.softmax(s, axis=-1).astype(v.dtype)
        return jnp.einsum('bhqk,bkhd->bqhd', p, v)

    o = lax.map(one_block, (qb, fqb, jnp.arange(nb)))
    return jnp.moveaxis(o, 0, 1).reshape(bsz, t, nh, hd)


def band_attention(q, k, v, reach):
    n, l, nh, hd = q.shape
    blk = BAND_BLOCK
    nb = -(-l // blk)
    pad = nb * blk - l

    def prep(a):
        return jnp.pad(a, [(0, 0), (0, pad), (0, 0), (0, 0)]).reshape(n, nb, blk, nh, hd)

    def with_prev(a):
        prev = jnp.concatenate([jnp.zeros_like(a[:, :1]), a[:, :-1]], axis=1)
        return jnp.concatenate([prev, a], axis=2)

    qb = prep(q)
    kw = with_prev(prep(k))
    vw = with_prev(prep(v))
    s = jnp.einsum('nbqhd,nbkhd->nbhqk', qb, kw, preferred_element_type=jnp.float32) * (hd ** -0.5)
    q_pos = jnp.arange(nb)[:, None] * blk + jnp.arange(blk)[None, :]
    k_pos = jnp.arange(nb)[:, None] * blk - blk + jnp.arange(2 * blk)[None, :]
    dist = q_pos[:, :, None] - k_pos[:, None, :]
    valid = (dist >= 0) & (dist <= reach) & (k_pos[:, None, :] >= 0)
    s = jnp.where(valid[None, :, None], s, -jnp.inf)
    lse = jax.nn.logsumexp(s, axis=-1)
    p = jnp.exp(s - lse[..., None]).astype(v.dtype)
    o = jnp.einsum('nbhqk,nbkhd->nbqhd', p, vw).reshape(n, nb * blk, nh, hd)[:, :l]
    lse = jnp.swapaxes(lse, 2, 3).reshape(n, nb * blk, nh)[:, :l]
    return o, lse


def dilated_prompt(q, k, v, window, dil):
    bsz, s, nh, hd = q.shape
    l = s // dil

    def sub(a):
        return a.reshape(bsz, l, dil, nh, hd).transpose(0, 2, 1, 3, 4).reshape(bsz * dil, l, nh, hd)

    o, lse = band_attention(sub(q), sub(k), sub(v), window // dil)
    o = o.reshape(bsz, dil, l, nh, hd).transpose(0, 2, 1, 3, 4).reshape(bsz, s, nh, hd)
    lse = lse.reshape(bsz, dil, l, nh).transpose(0, 2, 1, 3).reshape(bsz, s, nh)
    return o, lse


def dilated_sample(q, k, v, buf_k, buf_v, window, dil):
    bsz, t, nh, hd = q.shape
    wb = buf_k.shape[1]
    ext_k = jnp.concatenate([buf_k.astype(k.dtype), k], axis=1)
    ext_v = jnp.concatenate([buf_v.astype(v.dtype), v], axis=1)
    n_keys = window // dil
    idx = wb + jnp.arange(t)[:, None] - dil * jnp.arange(n_keys + 1)[None, :]
    valid = idx >= 0
    idx_c = jnp.maximum(idx, 0)
    kg = ext_k[:, idx_c]
    vg = ext_v[:, idx_c]
    s = jnp.einsum('bthd,btjhd->bhtj', q, kg, preferred_element_type=jnp.float32) * (hd ** -0.5)
    s = jnp.where(valid[None, None], s, -jnp.inf)
    lse = jax.nn.logsumexp(s, axis=-1)
    p = jnp.exp(s - lse[..., None]).astype(v.dtype)
    o = jnp.einsum('bhtj,btjhd->bthd', p, vg)
    return o, jnp.swapaxes(lse, 1, 2), ext_k[:, ext_k.shape[1] - wb:], ext_v[:, ext_v.shape[1] - wb:]


def even_mixer(h, w_in, conv_w, a_log, dt_bias, gnorm_w, b_f, w_out, conv_buf, s0, past):
    f32 = jnp.float32
    bsz, t, _ = h.shape
    proj = h @ w_in
    o1 = 3 * W_A
    o2 = o1 + W_A
    o3 = o2 + H_A
    o4 = o3 + H_A
    o5 = o4 + 3 * W_B
    qkv_a, z, b_in, a_in, qkv_b, f_in = jnp.split(proj, [o1, o2, o3, o4, o5], axis=-1)
    qkv_a, new_conv = causal_dwconv(qkv_a, conv_buf, conv_w)
    qkv_a = jax.nn.silu(qkv_a).reshape(bsz, t, 3, H_A, HEAD_DIM)
    qa, ka, va = l2norm(qkv_a[:, :, 0]), l2norm(qkv_a[:, :, 1]), qkv_a[:, :, 2]
    beta = jax.nn.sigmoid(b_in.astype(f32))
    g = -jnp.exp(a_log.astype(f32)) * jax.nn.softplus(a_in.astype(f32) + dt_bias.astype(f32))
    o_a, s_new = gated_delta_rule(qa, ka, va, g, beta, s0)
    o_a = rmsnorm(o_a.astype(h.dtype), gnorm_w) * jax.nn.silu(z.reshape(bsz, t, H_A, HEAD_DIM))
    qkv_b = qkv_b.reshape(bsz, t, 3, H_B, HEAD_DIM)
    qb, kb, vb = qkv_b[:, :, 0], qkv_b[:, :, 1], qkv_b[:, :, 2]
    logf = jax.nn.log_sigmoid(f_in.astype(f32) + b_f.astype(f32))
    f_new = jnp.cumsum(logf, axis=1)
    if past is None:
        o_b = fox_attention(qb, kb, vb, f_new, f_new, 0)
    else:
        k_past, v_past, logf_past = past
        lp = logf_past.astype(f32)
        f_past = lp - lax.cumsum(lp, axis=1, reverse=True)
        k_all = jnp.concatenate([k_past.astype(kb.dtype), kb], axis=1)
        v_all = jnp.concatenate([v_past.astype(vb.dtype), vb], axis=1)
        o_b = fox_attention(qb, k_all, v_all, f_new, jnp.concatenate([f_past, f_new], axis=1), k_past.shape[1])
    mix = jnp.concatenate([o_a.reshape(bsz, t, W_A), o_b.reshape(bsz, t, W_B)], axis=-1)
    return mix @ w_out, s_new, new_conv, kb, vb, logf


def odd_mixer(h, w_in, w_out, bufs):
    bsz, t, _ = h.shape
    proj = (h @ w_in).reshape(bsz, t, len(C_GROUPS), 3, H_C, HEAD_DIM)
    outs, lses, new_bufs = [], [], []
    for gi, (window, dil) in enumerate(C_GROUPS):
        q, k, v = proj[:, :, gi, 0], proj[:, :, gi, 1], proj[:, :, gi, 2]
        if bufs is None:
            o, lse = dilated_prompt(q, k, v, window, dil)
            keep = min(window, t)
            new_bufs.append((k[:, t - keep:], v[:, t - keep:]))
        else:
            o, lse, nk, nv = dilated_sample(q, k, v, bufs[gi][0], bufs[gi][1], window, dil)
            new_bufs.append((nk, nv))
        outs.append(o)
        lses.append(lse)
    wts = jax.nn.softmax(jnp.stack(lses, axis=0), axis=0).astype(h.dtype)
    o = jnp.einsum('gbth,gbthd->bthd', wts, jnp.stack(outs, axis=0))
    return o.reshape(bsz, t, W_C) @ w_out, new_bufs


def conv_ffn(h, w_up, w_gate, conv_w, w_down, buf):
    up = h @ w_up
    gate_c, new_buf = causal_dwconv(h @ w_gate, buf, conv_w)
    return (jax.nn.gelu(gate_c) * up) @ w_down, new_buf


def setup_inputs(seed: int = 0) -> dict:
    key = jax.random.key(seed)
    keys = iter(jax.random.split(key, 48))
    f32 = jnp.float32

    def nrm(shape, scale=1.0):
        return jax.random.normal(next(keys), shape, f32) * scale

    n_pages = PAST_LEN // PAGE_SIZE
    n_used = DEC_BATCH * n_pages
    n_pool = n_used + max(1, n_used // 4)
    win_rows = [min(w, PAST_LEN) for w, _ in C_GROUPS]
    x_prompt = nrm((BATCH, SEQ, D_MODEL))
    x_sample = nrm((DEC_BATCH, DEC_SEQ, D_MODEL))
    c_prompt = nrm((BATCH, D_MODEL))
    c_sample = nrm((DEC_BATCH, D_MODEL))
    state_delta = nrm((N_EVEN, DEC_BATCH, H_A, HEAD_DIM, HEAD_DIM), HEAD_DIM ** -0.5)
    state_conv_qkv = nrm((N_EVEN, DEC_BATCH, CONV_A - 1, 3 * W_A))
    cache_k = nrm((N_EVEN, n_pool, PAGE_SIZE, H_B, HEAD_DIM))
    cache_v = nrm((N_EVEN, n_pool, PAGE_SIZE, H_B, HEAD_DIM))
    cache_logf = jax.nn.log_sigmoid(FORGET_BIAS + nrm((N_EVEN, n_pool, PAGE_SIZE, H_B), 0.5))
    cache_win_k0 = nrm((N_ODD, DEC_BATCH, win_rows[0], H_C, HEAD_DIM))
    cache_win_v0 = nrm((N_ODD, DEC_BATCH, win_rows[0], H_C, HEAD_DIM))
    cache_win_k1 = nrm((N_ODD, DEC_BATCH, win_rows[1], H_C, HEAD_DIM))
    cache_win_v1 = nrm((N_ODD, DEC_BATCH, win_rows[1], H_C, HEAD_DIM))
    cache_win_k2 = nrm((N_ODD, DEC_BATCH, win_rows[2], H_C, HEAD_DIM))
    cache_win_v2 = nrm((N_ODD, DEC_BATCH, win_rows[2], H_C, HEAD_DIM))
    state_ffn_conv = nrm((DEPTH, DEC_BATCH, FFN_CONV - 1, D_FF))
    page_table = jax.random.permutation(next(keys), n_pool)[:n_used].reshape(DEC_BATCH, n_pages).astype(jnp.int32)
    w_mod = nrm((DEPTH, D_MODEL, N_MOD * D_MODEL), D_MODEL ** -0.5)
    b_mod = nrm((DEPTH, N_MOD * D_MODEL), 0.02)
    norm_pre_mix = 1.0 + nrm((DEPTH, D_MODEL), 0.05)
    norm_post_mix = 1.0 + nrm((DEPTH, D_MODEL), 0.05)
    norm_pre_ffn = 1.0 + nrm((DEPTH, D_MODEL), 0.05)
    norm_post_ffn = 1.0 + nrm((DEPTH, D_MODEL), 0.05)
    w_in_e = nrm((N_EVEN, D_MODEL, PROJ_E), D_MODEL ** -0.5)
    conv_a = nrm((N_EVEN, CONV_A, 3 * W_A), CONV_A ** -0.5)
    a_log = jnp.log(jax.random.uniform(next(keys), (N_EVEN, H_A), f32, 1.0, 16.0))
    dt = jnp.exp(jax.random.uniform(next(keys), (N_EVEN, H_A), f32, jnp.log(1e-3), jnp.log(1e-1)))
    dt_bias = dt + jnp.log(-jnp.expm1(-dt))
    gnorm_a = 1.0 + nrm((N_EVEN, HEAD_DIM), 0.05)
    b_forget = FORGET_BIAS + nrm((N_EVEN, H_B), 0.5)
    w_out_e = nrm((N_EVEN, W_A + W_B, D_MODEL), (W_A + W_B) ** -0.5)
    w_in_o = nrm((N_ODD, D_MODEL, PROJ_O), D_MODEL ** -0.5)
    w_out_o = nrm((N_ODD, W_C, D_MODEL), W_C ** -0.5)
    w_up = nrm((DEPTH, D_MODEL, D_FF), D_MODEL ** -0.5)
    w_gate = nrm((DEPTH, D_MODEL, D_FF), D_MODEL ** -0.5)
    conv_ffn_w = nrm((DEPTH, FFN_CONV, D_FF), FFN_CONV ** -0.5)
    w_down = nrm((DEPTH, D_FF, D_MODEL), D_FF ** -0.5)
    return {'x_prompt': x_prompt, 'x_sample': x_sample, 'c_prompt': c_prompt, 'c_sample': c_sample,
            'state_delta': state_delta, 'state_conv_qkv': state_conv_qkv,
            'cache_k': cache_k, 'cache_v': cache_v, 'cache_logf': cache_logf,
            'cache_win_k0': cache_win_k0, 'cache_win_v0': cache_win_v0,
            'cache_win_k1': cache_win_k1, 'cache_win_v1': cache_win_v1,
            'cache_win_k2': cache_win_k2, 'cache_win_v2': cache_win_v2,
            'state_ffn_conv': state_ffn_conv, 'page_table': page_table,
            'w_mod': w_mod, 'b_mod': b_mod, 'norm_pre_mix': norm_pre_mix, 'norm_post_mix': norm_post_mix,
            'norm_pre_ffn': norm_pre_ffn, 'norm_post_ffn': norm_post_ffn,
            'w_in_e': w_in_e, 'conv_a': conv_a, 'a_log': a_log, 'dt_bias': dt_bias, 'gnorm_a': gnorm_a,
            'b_forget': b_forget, 'w_out_e': w_out_e, 'w_in_o': w_in_o, 'w_out_o': w_out_o,
            'w_up': w_up, 'w_gate': w_gate, 'conv_ffn_w': conv_ffn_w, 'w_down': w_down}


def reference(x_prompt, x_sample, c_prompt, c_sample, state_delta, state_conv_qkv, cache_k, cache_v, cache_logf,
              cache_win_k0, cache_win_v0, cache_win_k1, cache_win_v1, cache_win_k2, cache_win_v2,
              state_ffn_conv, page_table, w_mod, b_mod, norm_pre_mix, norm_post_mix, norm_pre_ffn, norm_post_ffn,
              w_in_e, conv_a, a_log, dt_bias, gnorm_a, b_forget, w_out_e, w_in_o, w_out_o,
              w_up, w_gate, conv_ffn_w, w_down):
    bsz = x_prompt.shape[0]
    dbsz = x_sample.shape[0]
    win_caches = ((cache_win_k0, cache_win_v0), (cache_win_k1, cache_win_v1), (cache_win_k2, cache_win_v2))

    def gather_pages(pool):
        rows = pool[page_table]
        return rows.reshape((dbsz, -1) + pool.shape[2:])

    xp, xs = x_prompt, x_sample
    dp, ds, cqp, cqs, kp_l, ks_l, vp_l, vs_l, lp_l, ls_l, ffp, ffs = ([] for _ in range(12))
    wkp = [[] for _ in C_GROUPS]
    wks = [[] for _ in C_GROUPS]
    wvp = [[] for _ in C_GROUPS]
    wvs = [[] for _ in C_GROUPS]
    for l in range(DEPTH):
        mp = (jax.nn.silu(c_prompt) @ w_mod[l] + b_mod[l]).reshape(bsz, N_MOD, D_MODEL)
        ms = (jax.nn.silu(c_sample) @ w_mod[l] + b_mod[l]).reshape(dbsz, N_MOD, D_MODEL)
        hp = modulate(xp, norm_pre_mix[l], mp[:, 0], mp[:, 1])
        hs = modulate(xs, norm_pre_mix[l], ms[:, 0], ms[:, 1])
        if l % 2 == 0:
            e = l // 2
            wts = (w_in_e[e], conv_a[e], a_log[e], dt_bias[e], gnorm_a[e], b_forget[e], w_out_e[e])
            op, sp, cp, kp, vp, lfp = even_mixer(
                hp, *wts, jnp.zeros((bsz, CONV_A - 1, 3 * W_A), hp.dtype),
                jnp.zeros((bsz, H_A, HEAD_DIM, HEAD_DIM), jnp.float32), None)
            past = (gather_pages(cache_k[e]), gather_pages(cache_v[e]), gather_pages(cache_logf[e]))
            os_, ss, cs, ks, vs, lfs = even_mixer(hs, *wts, state_conv_qkv[e], state_delta[e], past)
            dp.append(sp)
            ds.append(ss)
            cqp.append(cp)
            cqs.append(cs)
            kp_l.append(kp)
            ks_l.append(ks)
            vp_l.append(vp)
            vs_l.append(vs)
            lp_l.append(lfp)
            ls_l.append(lfs)
        else:
            o = l // 2
            op, bp = odd_mixer(hp, w_in_o[o], w_out_o[o], None)
            bufs = tuple((bk[o], bv[o]) for bk, bv in win_caches)
            os_, bs = odd_mixer(hs, w_in_o[o], w_out_o[o], bufs)
            for gi in range(len(C_GROUPS)):
                wkp[gi].append(bp[gi][0])
                wvp[gi].append(bp[gi][1])
                wks[gi].append(bs[gi][0])
                wvs[gi].append(bs[gi][1])
        xp = xp + mp[:, 2, None, :] * rmsnorm(op, norm_post_mix[l])
        xs = xs + ms[:, 2, None, :] * rmsnorm(os_, norm_post_mix[l])
        hp = modulate(xp, norm_pre_ffn[l], mp[:, 3], mp[:, 4])
        hs = modulate(xs, norm_pre_ffn[l], ms[:, 3], ms[:, 4])
        fp, nbp = conv_ffn(hp, w_up[l], w_gate[l], conv_ffn_w[l], w_down[l],
                           jnp.zeros((bsz, FFN_CONV - 1, D_FF), hp.dtype))
        fs, nbs = conv_ffn(hs, w_up[l], w_gate[l], conv_ffn_w[l], w_down[l], state_ffn_conv[l])
        ffp.append(nbp)
        ffs.append(nbs)
        xp = xp + mp[:, 5, None, :] * rmsnorm(fp, norm_post_ffn[l])
        xs = xs + ms[:, 5, None, :] * rmsnorm(fs, norm_post_ffn[l])

    stk = lambda lst: jnp.stack(lst, axis=0)
    y_prompt, y_sample = xp, xs
    delta_p, delta_s, convqkv_p, convqkv_s = stk(dp), stk(ds), stk(cqp), stk(cqs)
    k_p, k_s, v_p, v_s, logf_p, logf_s = stk(kp_l), stk(ks_l), stk(vp_l), stk(vs_l), stk(lp_l), stk(ls_l)
    wk0_p, wk0_s, wv0_p, wv0_s = stk(wkp[0]), stk(wks[0]), stk(wvp[0]), stk(wvs[0])
    wk1_p, wk1_s, wv1_p, wv1_s = stk(wkp[1]), stk(wks[1]), stk(wvp[1]), stk(wvs[1])
    wk2_p, wk2_s, wv2_p, wv2_s = stk(wkp[2]), stk(wks[2]), stk(wvp[2]), stk(wvs[2])
    ffn_p, ffn_s = stk(ffp), stk(ffs)
    return (y_prompt, y_sample, delta_p, delta_s, convqkv_p, convqkv_s, k_p, k_s, v_p, v_s, logf_p, logf_s,
            wk0_p, wk0_s, wv0_p, wv0_s, wk1_p, wk1_s, wv1_p, wv1_s, wk2_p, wk2_s, wv2_p, wv2_s, ffn_p, ffn_s)
```

```python
import functools

import jax
import jax.numpy as jnp
from jax import lax
from jax.experimental import pallas as pl
from jax.experimental.pallas import tpu as pltpu

F32 = jnp.float32
BF16 = jnp.bfloat16
HI = lax.Precision.HIGHEST

C_GROUPS = ((128, 1), (512, 4), (2048, 16))
DELTA_CHUNK = 64
RMS_EPS = 1e-6
L2_EPS = 1e-6
NEG = -1e30
LANE = 128
SUBLANE = 8
VMEM_LIMIT = 56 * 1024 * 1024
MOD_ROWS = 16


def _cp(*sem):
    return pltpu.CompilerParams(dimension_semantics=sem, vmem_limit_bytes=VMEM_LIMIT)


def _dot(a, b, prec=None):
    return jnp.dot(a, b, preferred_element_type=F32, precision=prec)


def _dot_nt(a, b, prec=None):
    return lax.dot_general(a, b, (((1,), (1,)), ((), ())), preferred_element_type=F32, precision=prec)


def _dot_tn(a, b, prec=None):
    return lax.dot_general(a, b, (((0,), (0,)), ((), ())), preferred_element_type=F32, precision=prec)


def _sigmoid(x):
    return 1.0 / (1.0 + jnp.exp(-x))


def _silu(x):
    return x * _sigmoid(x)


def _softplus(x):
    return jnp.maximum(x, 0.0) + jnp.log1p(jnp.exp(-jnp.abs(x)))


def _gelu_tanh(x):
    return 0.5 * x * (1.0 + jnp.tanh(0.7978845608028654 * (x + 0.044715 * (x * x * x))))


def _rms(x, w):
    return x * lax.rsqrt(jnp.mean(x * x, axis=-1, keepdims=True) + RMS_EPS) * w


def _iota(shape, dim):
    return lax.broadcasted_iota(jnp.int32, shape, dim)


def _row_to_col(row):
    n = row.shape[1]
    eye = _iota((n, n), 0) == _iota((n, n), 1)
    return jnp.sum(jnp.where(eye, jnp.broadcast_to(row, (n, n)), 0.0), axis=1, keepdims=True)


def _col_to_row(col):
    n = col.shape[0]
    eye = _iota((n, n), 0) == _iota((n, n), 1)
    return jnp.sum(jnp.where(eye, jnp.broadcast_to(col, (n, n)), 0.0), axis=0, keepdims=True)


def _pick_tile(n, pref):
    if n <= pref:
        return n
    t = (pref // LANE) * LANE
    while t >= LANE:
        if n % t == 0:
            return t
        t -= LANE
    return n


def _log2(n):
    assert n > 0 and n & (n - 1) == 0, n
    return n.bit_length() - 1


def _mod_kernel(c_ref, w_ref, b_ref, o_ref):
    a = _silu(c_ref[...]).astype(BF16)
    o_ref[...] = _dot(a, w_ref[...].astype(BF16)) + b_ref[...]


def _mod(c_all, w_mod, b_mod):
    depth, d, n = w_mod.shape
    r = c_all.shape[0]
    tn = _pick_tile(n, 512)
    return pl.pallas_call(
        _mod_kernel,
        grid=(depth, n // tn),
        in_specs=[pl.BlockSpec((r, d), lambda l, j: (0, 0)),
                  pl.BlockSpec((None, d, tn), lambda l, j: (l, 0, j)),
                  pl.BlockSpec((None, 1, tn), lambda l, j: (l, 0, j))],
        out_specs=pl.BlockSpec((None, r, tn), lambda l, j: (l, 0, j)),
        out_shape=jax.ShapeDtypeStruct((depth, r, n), F32),
        compiler_params=_cp("arbitrary", "arbitrary"),
        name="mod",
    )(c_all, w_mod, b_mod.reshape(depth, 1, n))


def _post_pre_kernel(*refs, has_o, has_h):
    refs = list(refs)
    x = refs.pop(0)[...]
    if has_o:
        o_ref, gate_ref, wpost_ref = refs.pop(0), refs.pop(0), refs.pop(0)
    if has_h:
        shift_ref, scale_ref, wpre_ref = refs.pop(0), refs.pop(0), refs.pop(0)
    if has_o:
        x = x + gate_ref[...] * _rms(o_ref[...], wpost_ref[...])
        refs.pop(0)[...] = x
    if has_h:
        h = _rms(x, wpre_ref[...]) * (1.0 + scale_ref[...]) + shift_ref[...]
        refs.pop(0)[...] = h.astype(BF16)


def _post_pre(x, nb, t, o=None, gate=None, w_post=None, shift=None, scale=None, w_pre=None):
    m, d = x.shape
    has_o, has_h = o is not None, shift is not None
    if t >= 128:
        tr = 128
        per = t // tr
        vec = lambda v: v.reshape(nb, 1, d)
        vec_spec = pl.BlockSpec((None, 1, d), lambda i: (i // per, 0, 0))
    else:
        tr = m
        vec = lambda v: jnp.repeat(v, t, axis=0).reshape(1, m, d)
        vec_spec = pl.BlockSpec((None, m, d), lambda i: (0, 0, 0))
    row_spec = pl.BlockSpec((tr, d), lambda i: (i, 0))
    w_spec = pl.BlockSpec((1, d), lambda i: (0, 0))
    args, in_specs, out_specs, out_shape = [x], [row_spec], [], []
    if has_o:
        args += [o, vec(gate), w_post.reshape(1, d)]
        in_specs += [row_spec, vec_spec, w_spec]
        out_specs.append(row_spec)
        out_shape.append(jax.ShapeDtypeStruct((m, d), F32))
    if has_h:
        args += [vec(shift), vec(scale), w_pre.reshape(1, d)]
        in_specs += [vec_spec, vec_spec, w_spec]
        out_specs.append(row_spec)
        out_shape.append(jax.ShapeDtypeStruct((m, d), BF16))
    outs = pl.pallas_call(
        functools.partial(_post_pre_kernel, has_o=has_o, has_h=has_h),
        grid=(m // tr,), in_specs=in_specs, out_specs=out_specs, out_shape=out_shape,
        compiler_params=_cp("arbitrary"), name="post_pre",
    )(*args)
    outs = list(outs)
    x_new = outs.pop(0) if has_o else x
    h = outs.pop(0) if has_h else None
    return x_new, h


def _mm_kernel(*refs, n_pairs, nk):
    a_refs, b_refs = refs[:n_pairs], refs[n_pairs:2 * n_pairs]
    o_ref = refs[2 * n_pairs]
    part = _dot(a_refs[0][...], b_refs[0][...])
    for a_ref, b_ref in zip(a_refs[1:], b_refs[1:]):
        part = part + _dot(a_ref[...], b_ref[...])
    if nk == 1:
        o_ref[...] = part.astype(o_ref.dtype)
        return
    acc_ref = refs[2 * n_pairs + 1]
    k = pl.program_id(2)

    @pl.when(k == 0)
    def _():
        acc_ref[...] = part

    @pl.when(k > 0)
    def _():
        acc_ref[...] += part

    @pl.when(k == nk - 1)
    def _():
        o_ref[...] = acc_ref[...].astype(o_ref.dtype)


def _matmul(a_list, b_list, out_dtype, tm=1024, tn=1024, tk=None):
    m, kdim = a_list[0].shape
    n = b_list[0].shape[1]
    tm = min(tm, m)
    tn = _pick_tile(n, tn)
    tk = kdim if tk is None else _pick_tile(kdim, tk)
    nk = kdim // tk
    assert m % tm == 0 and n % tn == 0 and kdim % tk == 0
    n_pairs = len(a_list)
    in_specs = ([pl.BlockSpec((tm, tk), lambda i, j, k: (i, k))] * n_pairs
                + [pl.BlockSpec((tk, tn), lambda i, j, k: (k, j))] * n_pairs)
    scratch = [pltpu.VMEM((tm, tn), F32)] if nk > 1 else []
    return pl.pallas_call(
        functools.partial(_mm_kernel, n_pairs=n_pairs, nk=nk),
        grid=(m // tm, n // tn, nk), in_specs=in_specs,
        out_specs=pl.BlockSpec((tm, tn), lambda i, j, k: (i, j)),
        out_shape=jax.ShapeDtypeStruct((m, n), out_dtype),
        scratch_shapes=scratch,
        compiler_params=_cp("arbitrary", "arbitrary", "arbitrary"), name="matmul",
    )(*a_list, *b_list)


def _gates_kernel(h_ref, w_ref, par_ref, g_ref, f_ref, carry_ref, *, ha, hb, seg):
    @pl.when(pl.program_id(1) == 0)
    def _():
        carry_ref[...] = jnp.zeros_like(carry_ref)

    p = _dot(h_ref[...], w_ref[...])
    tr = p.shape[0]
    lane = _iota(p.shape, 1)
    a_log, bias = par_ref[0:1, :], par_ref[1:2, :]
    x = p + bias
    beta = _sigmoid(p)
    g = -jnp.exp(a_log) * _softplus(x)
    logf = -_softplus(-x)
    out = jnp.where(lane < ha, beta,
                    jnp.where(lane < 2 * ha, g, jnp.where(lane < 2 * ha + hb, logf, 0.0)))
    g_ref[...] = out
    rr, cc = _iota((tr, tr), 0), _iota((tr, tr), 1)
    tri = rr >= cc
    if seg is not None:
        tri = tri & ((rr >> _log2(seg)) == (cc >> _log2(seg)))
    cs = _dot(tri.astype(F32), out, HI) + carry_ref[0:1, :]
    f_ref[...] = cs
    carry_ref[0:1, :] = cs[tr - 1:tr, :]


def _gates(h, w_small, par, nb, t, ha, hb):
    m, d = h.shape
    if t >= 128:
        tr, per, seg, nb_grid = min(t, 512), t // min(t, 512), None, nb
    else:
        tr, per, seg, nb_grid = m, 1, t, 1
    spec = pl.BlockSpec((tr, LANE), lambda b, j: (b * per + j, 0))
    return pl.pallas_call(
        functools.partial(_gates_kernel, ha=ha, hb=hb, seg=seg),
        grid=(nb_grid, per),
        in_specs=[pl.BlockSpec((tr, d), lambda b, j: (b * per + j, 0)),
                  pl.BlockSpec((d, LANE), lambda b, j: (0, 0)),
                  pl.BlockSpec((SUBLANE, LANE), lambda b, j: (0, 0))],
        out_specs=[spec, spec],
        out_shape=[jax.ShapeDtypeStruct((m, LANE), F32)] * 2,
        scratch_shapes=[pltpu.VMEM((SUBLANE, LANE), F32)],
        compiler_params=_cp("arbitrary", "arbitrary"), name="gates",
    )(h, w_small, par)


def _unit_lower_inverse(low):
    c = low.shape[0]
    assert c == 64
    r, s = _iota((c, c), 0), _iota((c, c), 1)
    eye = (r == s).astype(F32)
    same16 = (r >> 4) == (s >> 4)
    same32 = (r >> 5) == (s >> 5)
    ld = jnp.where(same16, low, 0.0)
    x = eye - ld
    p = _dot(ld, ld, HI)
    x = x + _dot(x, p, HI)
    p = _dot(p, p, HI)
    x = x + _dot(x, p, HI)
    p = _dot(p, p, HI)
    x = x + _dot(x, p, HI)
    lo1 = jnp.where(same32 & jnp.logical_not(same16), low, 0.0)
    x = x - _dot(_dot(x, lo1, HI), x, HI)
    lo2 = jnp.where(same32, 0.0, low)
    x = x - _dot(_dot(x, lo2, HI), x, HI)
    return x


def _delta_kernel(q_ref, k_ref, v_ref, z_ref, gt_ref, cwq_ref, cwk_ref, cwv_ref,
                  cbq_ref, cbk_ref, cbv_ref, s0_ref, gn_ref, o_ref, sfin_ref, ext_ref, s_ref,
                  *, hpb, ha, kw, valid_rows, nc):
    c_idx = pl.program_id(2)
    hblk = pl.program_id(1)
    c = q_ref.shape[0]
    hd = LANE

    @pl.when(c_idx == 0)
    def _():
        s_ref[...] = s0_ref[...]
        for i, cb in enumerate((cbq_ref, cbk_ref, cbv_ref)):
            ext_ref[i, 0:SUBLANE, :] = cb[...]

    conv = []
    for i, (r_ref, cw_ref) in enumerate(((q_ref, cwq_ref), (k_ref, cwk_ref), (v_ref, cwv_ref))):
        ext_ref[i, SUBLANE:SUBLANE + c, :] = r_ref[...]
        acc = ext_ref[i, SUBLANE:SUBLANE + c, :] * cw_ref[kw - 1:kw, :]
        for j in range(kw - 1):
            acc = acc + ext_ref[i, pl.ds(SUBLANE - (kw - 1) + j, c), :] * cw_ref[j:j + 1, :]
        ext_ref[i, 0:SUBLANE, :] = ext_ref[i, c:c + SUBLANE, :]
        acc = _silu(acc)
        if valid_rows < c:
            acc = jnp.where(_iota(acc.shape, 0) < valid_rows, acc, 0.0)
        conv.append(acc)
    xq, xk, xv = conv

    gt = gt_ref[...]
    rr, ss = _iota((c, c), 0), _iota((c, c), 1)
    incl, strict = rr >= ss, rr > ss
    gcum = _dot(incl.astype(F32), gt, HI)
    lane = _iota(gt.shape, 1)
    gn = gn_ref[...]
    for j in range(hpb):
        sl = slice(j * hd, (j + 1) * hd)
        head = hblk * hpb + j
        beta = jnp.sum(jnp.where(lane == head, gt, 0.0), axis=1, keepdims=True)
        gc = jnp.sum(jnp.where(lane == ha + head, gcum, 0.0), axis=1, keepdims=True)
        qh, kh, vh = xq[:, sl], xk[:, sl], xv[:, sl]
        qh = qh * lax.rsqrt(jnp.sum(qh * qh, axis=-1, keepdims=True) + L2_EPS) * (hd ** -0.5)
        kh = kh * lax.rsqrt(jnp.sum(kh * kh, axis=-1, keepdims=True) + L2_EPS)
        decay = jnp.where(incl, jnp.exp(gc - _col_to_row(gc)), 0.0)
        kb, vb = kh * beta, vh * beta
        low = jnp.where(strict, _dot_nt(kb, kh, HI) * decay, 0.0)
        tmat = _unit_lower_inverse(low)
        u = _dot(tmat, vb, HI)
        w = _dot(tmat, kb * jnp.exp(gc), HI)
        attn = _dot_nt(qh, kh, HI) * decay
        st = s_ref[j]
        v_new = u - _dot(w, st, HI)
        o = _dot(qh * jnp.exp(gc), st, HI) + _dot(attn, v_new, HI)
        g_last = gc[c - 1:c, :]
        k_dec = kh * jnp.exp(g_last - gc)
        s_ref[j] = st * jnp.exp(g_last) + _dot_tn(k_dec, v_new, HI)
        o = _rms(o, gn) * _silu(z_ref[:, sl])
        o_ref[:, sl] = o.astype(o_ref.dtype)

    @pl.when(c_idx == nc - 1)
    def _():
        sfin_ref[...] = s_ref[...]


def _delta(proj, gates, conv_w, conv_buf8, s0, gnorm, nb, tp, valid_rows, ha, hpb=4):
    c = DELTA_CHUNK
    hd = LANE
    hpb = min(hpb, ha)
    w = hpb * hd
    nc = tp // c
    nhb = ha // hpb
    kw = conv_w.shape[0]
    cw8 = jnp.pad(conv_w, ((0, SUBLANE - kw), (0, 0)))

    def col(off):
        return pl.BlockSpec((c, w), lambda b, h, i, off=off: (b * nc + i, off * nhb + h))

    def cwspec(off):
        return pl.BlockSpec((SUBLANE, w), lambda b, h, i, off=off: (0, off * nhb + h))

    def cbspec(off):
        return pl.BlockSpec((None, SUBLANE, w), lambda b, h, i, off=off: (b, 0, off * nhb + h))

    state_spec = pl.BlockSpec((None, hpb, hd, hd), lambda b, h, i: (b, h, 0, 0))
    return pl.pallas_call(
        functools.partial(_delta_kernel, hpb=hpb, ha=ha, kw=kw, valid_rows=valid_rows, nc=nc),
        grid=(nb, nhb, nc),
        in_specs=[col(0), col(1), col(2), col(3),
                  pl.BlockSpec((c, LANE), lambda b, h, i: (b * nc + i, 0)),
                  cwspec(0), cwspec(1), cwspec(2), cbspec(0), cbspec(1), cbspec(2),
                  state_spec, pl.BlockSpec((1, hd), lambda b, h, i: (0, 0))],
        out_specs=[pl.BlockSpec((c, w), lambda b, h, i: (b * nc + i, h)), state_spec],
        out_shape=[jax.ShapeDtypeStruct((nb * tp, ha * hd), BF16),
                   jax.ShapeDtypeStruct(s0.shape, F32)],
        scratch_shapes=[pltpu.VMEM((3, c + SUBLANE, w), F32), pltpu.VMEM((hpb, hd, hd), F32)],
        compiler_params=_cp("arbitrary", "arbitrary", "arbitrary"), name="delta",
    )(proj, proj, proj, proj, gates, cw8, cw8, cw8, conv_buf8, conv_buf8, conv_buf8, s0,
      gnorm.reshape(1, hd))


def _softmax_step(carry, s, v, valid=None):
    m, l, acc = carry
    m_new = jnp.maximum(m, jnp.max(s, axis=-1, keepdims=True))
    alpha = jnp.exp(m - m_new)
    p = jnp.exp(s - m_new)
    if valid is not None:
        p = jnp.where(valid, p, 0.0)
    l = alpha * l + jnp.sum(p, axis=-1, keepdims=True)
    acc = alpha * acc + _dot(p.astype(BF16), v)
    return m_new, l, acc


def _softmax_init(rows, hd):
    return (jnp.full((rows, 1), NEG, F32), jnp.zeros((rows, 1), F32), jnp.zeros((rows, hd), F32))


def _fox_prompt_kernel(q_ref, k_ref, v_ref, f_ref, o_ref, *, tq, scale):
    qi = pl.program_id(2)
    q0 = pl.multiple_of(qi * tq, tq)
    q = (q_ref[...] * scale).astype(BF16)
    fq = _row_to_col(f_ref[:, pl.ds(q0, tq)])
    rpos = q0 + _iota((tq, tq), 0)
    cidx = _iota((tq, tq), 1)

    def body(i, carry):
        k0 = pl.multiple_of((qi - i) * tq, tq)
        k = k_ref[pl.ds(k0, tq), :].astype(BF16)
        v = v_ref[pl.ds(k0, tq), :].astype(BF16)
        s = _dot_nt(q, k) + fq - f_ref[:, pl.ds(k0, tq)]
        s = jnp.where(k0 + cidx <= rpos, s, NEG)
        return _softmax_step(carry, s, v)

    _, l, acc = lax.fori_loop(0, qi + 1, body, _softmax_init(tq, LANE))
    o_ref[...] = (acc / l).astype(o_ref.dtype)


def _fox_prompt(proj, f_rows, nb, s, nh, col0, tq=256):
    tq = min(tq, s)
    nq = s // tq
    hd = LANE
    return pl.pallas_call(
        functools.partial(_fox_prompt_kernel, tq=tq, scale=hd ** -0.5),
        grid=(nb, nh, nq),
        in_specs=[pl.BlockSpec((tq, hd), lambda b, h, i: (b * nq + i, col0 + h)),
                  pl.BlockSpec((s, hd), lambda b, h, i: (b, col0 + nh + h)),
                  pl.BlockSpec((s, hd), lambda b, h, i: (b, col0 + 2 * nh + h)),
                  pl.BlockSpec((None, None, 1, s), lambda b, h, i: (b, h, 0, 0))],
        out_specs=pl.BlockSpec((tq, hd), lambda b, h, i: (b * nq + i, h)),
        out_shape=jax.ShapeDtypeStruct((nb * s, nh * hd), BF16),
        compiler_params=_cp("arbitrary", "arbitrary", "arbitrary"), name="fox_prompt",
    )(proj, proj, proj, f_rows)


def _fpast_kernel(pt_ref, *refs, group):
    lp_refs, o_ref, carry_ref = refs[:group], refs[group], refs[group + 1]

    @pl.when(pl.program_id(1) == 0)
    def _():
        carry_ref[...] = jnp.zeros_like(carry_ref)

    p = lp_refs[0].shape[0]
    upper = (_iota((p, p), 1) > _iota((p, p), 0)).astype(F32)
    for i in range(group):
        lp = lp_refs[i][...]
        o_ref[group - 1 - i] = -(_dot(upper, lp, HI) + carry_ref[0:1, :])
        carry_ref[0:1, :] = carry_ref[0:1, :] + jnp.sum(lp, axis=0, keepdims=True)


def _fpast(logf_pool, page_table):
    db, n_pages = page_table.shape
    _, p, nh = logf_pool.shape
    group = 8 if n_pages % 8 == 0 else 1
    nblk = n_pages // group

    def lp_spec(i):
        return pl.BlockSpec((None, p, nh),
                            lambda b, j, pt, i=i: (pt[b, n_pages - 1 - (j * group + i)], 0, 0))

    return pl.pallas_call(
        functools.partial(_fpast_kernel, group=group),
        grid_spec=pltpu.PrefetchScalarGridSpec(
            num_scalar_prefetch=1, grid=(db, nblk),
            in_specs=[lp_spec(i) for i in range(group)],
            out_specs=pl.BlockSpec((None, group, p, nh), lambda b, j, pt: (b, nblk - 1 - j, 0, 0)),
            scratch_shapes=[pltpu.VMEM((SUBLANE, nh), F32)]),
        out_shape=jax.ShapeDtypeStruct((db, n_pages, p, nh), F32),
        compiler_params=_cp("arbitrary", "arbitrary"), name="fpast",
    )(page_table, *([logf_pool] * group))


def _fox_sample_kernel(pt_ref, q_ref, k_ref, v_ref, fp_ref, kn_ref, vn_ref, fn_ref, o_ref,
                       m_ref, l_ref, acc_ref, *, nh, scale, n_pages):
    p = pl.program_id(1)
    r = q_ref.shape[0]
    lognh = _log2(nh)

    @pl.when(p == 0)
    def _():
        m_ref[...] = jnp.full_like(m_ref, NEG)
        l_ref[...] = jnp.zeros_like(l_ref)
        acc_ref[...] = jnp.zeros_like(acc_ref)

    q = (q_ref[...] * scale).astype(BF16)
    fn = fn_ref[...]
    fq = _row_to_col(fn)

    def update(s, v):
        m, l, acc = _softmax_step((m_ref[...], l_ref[...], acc_ref[...]), s, v)
        m_ref[...], l_ref[...], acc_ref[...] = m, l, acc

    rows = k_ref.shape[0]
    s = _dot_nt(q, k_ref[...].astype(BF16)) + fq - fp_ref[...]
    match = (_iota((r, rows), 0) & (nh - 1)) == (_iota((r, rows), 1) & (nh - 1))
    update(jnp.where(match, s, NEG), v_ref[...].astype(BF16))

    @pl.when(p == n_pages - 1)
    def _():
        rr, cc = _iota((r, r), 0), _iota((r, r), 1)
        ok = ((rr & (nh - 1)) == (cc & (nh - 1))) & ((cc >> lognh) <= (rr >> lognh))
        s2 = _dot_nt(q, kn_ref[...].astype(BF16)) + fq - fn
        update(jnp.where(ok, s2, NEG), vn_ref[...].astype(BF16))
        o_ref[...] = (acc_ref[...] / l_ref[...]).astype(o_ref.dtype)


def _fox_sample(q, k_new, v_new, f_new, k_pool, v_pool, f_past, page_table, nh):
    db, r, hd = q.shape
    n_pages = page_table.shape[1]
    rows = k_pool.shape[1]
    tok = pl.BlockSpec((None, r, hd), lambda b, p, pt: (b, 0, 0))
    page = pl.BlockSpec((None, rows, hd), lambda b, p, pt: (pt[b, p], 0, 0))
    return pl.pallas_call(
        functools.partial(_fox_sample_kernel, nh=nh, scale=hd ** -0.5, n_pages=n_pages),
        grid_spec=pltpu.PrefetchScalarGridSpec(
            num_scalar_prefetch=1, grid=(db, n_pages),
            in_specs=[tok, page, page,
                      pl.BlockSpec((None, None, 1, rows), lambda b, p, pt: (b, p, 0, 0)),
                      tok, tok, pl.BlockSpec((None, 1, r), lambda b, p, pt: (b, 0, 0))],
            out_specs=tok,
            scratch_shapes=[pltpu.VMEM((r, 1), F32), pltpu.VMEM((r, 1), F32),
                            pltpu.VMEM((r, hd), F32)]),
        out_shape=jax.ShapeDtypeStruct((db, r, hd), BF16),
        compiler_params=_cp("arbitrary", "arbitrary"), name="fox_sample",
    )(page_table, q, k_pool, v_pool, f_past, k_new, v_new, f_new)


def _dil_prompt_kernel(*refs, groups, tq, scale):
    ng = len(groups)
    q_refs, k_refs, v_refs, o_ref = refs[:ng], refs[ng:2 * ng], refs[2 * ng:3 * ng], refs[3 * ng]
    qi = pl.program_id(2)
    q0 = qi * tq
    base = _iota((tq, tq), 0) - _iota((tq, tq), 1)
    carry = _softmax_init(tq, LANE)
    for g, (window, dil) in enumerate(groups):
        q = (q_refs[g][...] * scale).astype(BF16)
        lo = jnp.maximum(q0 - window, 0) // tq

        def body(i, carry, g=g, q=q, window=window, dil=dil):
            k0 = pl.multiple_of((qi - i) * tq, tq)
            k = k_refs[g][pl.ds(k0, tq), :].astype(BF16)
            v = v_refs[g][pl.ds(k0, tq), :].astype(BF16)
            dist = base + (q0 - k0)
            valid = (dist >= 0) & (dist <= window) & ((dist & (dil - 1)) == 0)
            s = jnp.where(valid, _dot_nt(q, k), NEG)
            return _softmax_step(carry, s, v, valid)

        carry = lax.fori_loop(0, qi - lo + 1, body, carry)
    _, l, acc = carry
    o_ref[...] = (acc / l).astype(o_ref.dtype)


def _dil_prompt(proj, nb, s, nh, groups, tq=256):
    tq = min(tq, s)
    nq = s // tq
    hd = LANE
    ng = len(groups)
    for _, dil in groups:
        _log2(dil)
    q_specs = [pl.BlockSpec((tq, hd), lambda b, h, i, g=g: (b * nq + i, g * 3 * nh + h))
               for g in range(ng)]
    k_specs = [pl.BlockSpec((s, hd), lambda b, h, i, g=g: (b, g * 3 * nh + nh + h))
               for g in range(ng)]
    v_specs = [pl.BlockSpec((s, hd), lambda b, h, i, g=g: (b, g * 3 * nh + 2 * nh + h))
               for g in range(ng)]
    return pl.pallas_call(
        functools.partial(_dil_prompt_kernel, groups=groups, tq=tq, scale=hd ** -0.5),
        grid=(nb, nh, nq),
        in_specs=q_specs + k_specs + v_specs,
        out_specs=pl.BlockSpec((tq, hd), lambda b, h, i: (b * nq + i, h)),
        out_shape=jax.ShapeDtypeStruct((nb * s, nh * hd), BF16),
        compiler_params=_cp("arbitrary", "arbitrary", "arbitrary"), name="dil_prompt",
    )(*([proj] * (3 * ng)))


def _dil_sample_kernel(*refs, groups, wbs, tb, nh, scale, t):
    ng = len(groups)
    q_refs, kb_refs, vb_refs = refs[:ng], refs[ng:2 * ng], refs[2 * ng:3 * ng]
    kn_refs, vn_refs = refs[3 * ng:4 * ng], refs[4 * ng:5 * ng]
    o_ref, m_ref, l_ref, acc_ref = refs[5 * ng:5 * ng + 4]
    j = pl.program_id(1)
    r = q_refs[0].shape[0]
    rows = tb * nh
    lognh = _log2(nh)
    nblks = [wb // tb for wb in wbs]
    total = sum(nblks)

    @pl.when(j == 0)
    def _():
        m_ref[...] = jnp.full_like(m_ref, NEG)
        l_ref[...] = jnp.zeros_like(l_ref)
        acc_ref[...] = jnp.zeros_like(acc_ref)

    def update(s, v, valid):
        m, l, acc = _softmax_step((m_ref[...], l_ref[...], acc_ref[...]),
                                  jnp.where(valid, s, NEG), v, valid)
        m_ref[...], l_ref[...], acc_ref[...] = m, l, acc

    start = 0
    for g, (window, dil) in enumerate(groups):
        def block(g=g, window=window, dil=dil, start=start):
            rr, cc = _iota((r, rows), 0), _iota((r, rows), 1)
            key_tok = (j - start) * tb + (cc >> lognh)
            delta = wbs[g] + (rr >> lognh) - key_tok
            valid = (((rr & (nh - 1)) == (cc & (nh - 1))) & ((delta & (dil - 1)) == 0)
                     & (delta <= window))
            q = (q_refs[g][...] * scale).astype(BF16)
            update(_dot_nt(q, kb_refs[g][...].astype(BF16)), vb_refs[g][...].astype(BF16), valid)

        pl.when((j >= start) & (j < start + nblks[g]))(block)
        start += nblks[g]

    @pl.when(j == total - 1)
    def _():
        rr, cc = _iota((r, r), 0), _iota((r, r), 1)
        delta = (rr >> lognh) - (cc >> lognh)
        match = (rr & (nh - 1)) == (cc & (nh - 1))
        for g, (window, dil) in enumerate(groups):
            valid = match & (delta >= 0) & ((delta & (dil - 1)) == 0) & (delta <= window)
            q = (q_refs[g][...] * scale).astype(BF16)
            update(_dot_nt(q, kn_refs[g][...].astype(BF16)), vn_refs[g][...].astype(BF16), valid)
        o_ref[...] = (acc_ref[...] / l_ref[...]).astype(o_ref.dtype)


def _dil_sample(qs, kns, vns, kbufs, vbufs, groups, nh, t):
    db, r, hd = qs[0].shape
    ng = len(groups)
    wbs = [kb.shape[1] // nh for kb in kbufs]
    tb = 128
    while any(wb % tb for wb in wbs):
        tb //= 2
    nblks = [wb // tb for wb in wbs]
    starts = [sum(nblks[:g]) for g in range(ng)]
    total = sum(nblks)
    tok = pl.BlockSpec((None, r, hd), lambda b, j: (b, 0, 0))

    def buf_spec(g):
        return pl.BlockSpec(
            (None, tb * nh, hd),
            lambda b, j, g=g: (b, jnp.clip(j - starts[g], 0, nblks[g] - 1), 0))

    bspecs = [buf_spec(g) for g in range(ng)]
    return pl.pallas_call(
        functools.partial(_dil_sample_kernel, groups=groups, wbs=wbs, tb=tb, nh=nh,
                          scale=hd ** -0.5, t=t),
        grid=(db, total),
        in_specs=[tok] * ng + bspecs + bspecs + [tok] * (2 * ng),
        out_specs=tok,
        out_shape=jax.ShapeDtypeStruct((db, r, hd), BF16),
        scratch_shapes=[pltpu.VMEM((r, 1), F32), pltpu.VMEM((r, 1), F32), pltpu.VMEM((r, hd), F32)],
        compiler_params=_cp("arbitrary", "arbitrary"), name="dil_sample",
    )(*qs, *kbufs, *vbufs, *kns, *vns)


def _ffn_up_kernel(*refs, carry_rows, kw, tiles_per_batch, t):
    h_ref, wu_ref, wg_ref, cw_ref = refs[:4]
    nh = kw - 1
    if carry_rows:
        act_ref, tail_ref, ext_ref, carry_ref = refs[4:]
    else:
        halo_refs = refs[4:4 + nh]
        act_ref, tail_ref, ext_ref = refs[4 + nh:]
    m, n = pl.program_id(0), pl.program_id(1)
    h = h_ref[...]
    tm = h.shape[0]
    up = _dot(h, wu_ref[...])
    gate = _dot(h, wg_ref[...])
    ext_ref[SUBLANE:SUBLANE + tm, :] = gate
    if carry_rows:
        first = (m % tiles_per_batch) == 0

        @pl.when(first)
        def _():
            ext_ref[0:SUBLANE, :] = jnp.zeros((SUBLANE, gate.shape[1]), F32)

        @pl.when(jnp.logical_not(first))
        def _():
            ext_ref[0:SUBLANE, :] = carry_ref[n]

        carry_ref[n] = ext_ref[tm:tm + SUBLANE, :]
        tail_ref[...] = ext_ref[tm:tm + SUBLANE, :]
    else:
        ext_ref[0:SUBLANE, :] = jnp.zeros((SUBLANE, gate.shape[1]), F32)
        tail_ref[...] = gate
        pos = _iota(gate.shape, 0) & (t - 1)
    gc = gate * cw_ref[kw - 1:kw, :]
    for j in range(kw - 1):
        shift = kw - 1 - j
        sh = ext_ref[pl.ds(SUBLANE - shift, tm), :]
        if not carry_rows:
            sh = jnp.where(pos < shift, halo_refs[shift - 1][...], sh)
        gc = gc + sh * cw_ref[j:j + 1, :]
    act_ref[...] = (_gelu_tanh(gc) * up).astype(act_ref.dtype)


def _ffn_up(h, wu, wg, cw8, kw, rows_per_batch, halos=None, tm=1024, tn=512):
    m, d = h.shape
    n = wu.shape[1]
    carry_rows = halos is None
    tm = min(tm, rows_per_batch) if carry_rows else m
    tn = _pick_tile(n, tn)
    nm, nn = m // tm, n // tn
    in_specs = [pl.BlockSpec((tm, d), lambda i, j: (i, 0)),
                pl.BlockSpec((d, tn), lambda i, j: (0, j)),
                pl.BlockSpec((d, tn), lambda i, j: (0, j)),
                pl.BlockSpec((SUBLANE, tn), lambda i, j: (0, j))]
    args = [h, wu, wg, cw8]
    scratch = [pltpu.VMEM((tm + SUBLANE, tn), F32)]
    if carry_rows:
        tail_spec = pl.BlockSpec((None, SUBLANE, tn), lambda i, j: (i, 0, j))
        tail_shape = jax.ShapeDtypeStruct((nm, SUBLANE, n), F32)
        scratch.append(pltpu.VMEM((nn, SUBLANE, tn), F32))
    else:
        _log2(rows_per_batch)
        in_specs += [pl.BlockSpec((tm, tn), lambda i, j: (i, j))] * len(halos)
        args += list(halos)
        tail_spec = pl.BlockSpec((tm, tn), lambda i, j: (i, j))
        tail_shape = jax.ShapeDtypeStruct((m, n), F32)
    return pl.pallas_call(
        functools.partial(_ffn_up_kernel, carry_rows=carry_rows, kw=kw,
                          tiles_per_batch=max(rows_per_batch // tm, 1), t=rows_per_batch),
        grid=(nm, nn), in_specs=in_specs,
        out_specs=[pl.BlockSpec((tm, tn), lambda i, j: (i, j)), tail_spec],
        out_shape=[jax.ShapeDtypeStruct((m, n), BF16), tail_shape],
        scratch_shapes=scratch,
        compiler_params=_cp("arbitrary", "arbitrary"), name="ffn_up",
    )(*args)


def _even_weights(w_in, conv_w, a_log, dt_bias, b_f, w_out, ha, hb, hd):
    wa, wb = ha * hd, hb * hd
    o2 = 4 * wa
    o4 = o2 + 2 * ha
    o5 = o4 + 3 * wb
    w_main = jnp.concatenate([w_in[:, :o2], w_in[:, o4:o5]], axis=1).astype(BF16)
    small = jnp.concatenate([w_in[:, o2:o4], w_in[:, o5:]], axis=1)
    assert small.shape[1] <= LANE
    w_small = jnp.pad(small, ((0, 0), (0, LANE - small.shape[1]))).astype(BF16)
    par = jnp.zeros((SUBLANE, LANE), F32)
    par = par.at[0, ha:2 * ha].set(a_log.astype(F32))
    par = par.at[1, ha:2 * ha].set(dt_bias.astype(F32))
    par = par.at[1, 2 * ha:2 * ha + hb].set(b_f.astype(F32))
    w_out = w_out.astype(BF16)
    return w_main, w_small, par, conv_w, w_out[:wa], w_out[wa:]


def _even_mixer(h, nb, t, wts, gnorm, conv_buf, s0, ha, hb, hd, past=None):
    w_main, w_small, par, conv_w, w_out_a, w_out_b = wts
    wa, wb = ha * hd, hb * hd
    kw = conv_w.shape[0]
    proj = _matmul([h], [w_main], F32, tm=min(1024, nb * t))
    gates, fcum = _gates(h, w_small, par, nb, t, ha, hb)

    c = DELTA_CHUNK
    tp = -(-t // c) * c
    conv_buf8 = jnp.pad(conv_buf.astype(F32), ((0, 0), (SUBLANE - (kw - 1), 0), (0, 0)))
    if tp == t:
        proj_d, gates_d = proj, gates
    else:
        pad = lambda a: jnp.pad(a.reshape(nb, t, -1), ((0, 0), (0, tp - t), (0, 0))).reshape(nb * tp, -1)
        proj_d, gates_d = pad(proj[:, :4 * wa]), pad(gates)
    o_a, s_new = _delta(proj_d, gates_d, conv_w, conv_buf8, s0, gnorm, nb, tp, min(t, c) if tp != t else c,
                        ha)
    if tp != t:
        o_a = o_a.reshape(nb, tp, wa)[:, :t].reshape(nb * t, wa)

    col0 = 4 * wa // hd
    logf = gates[:, 2 * ha:2 * ha + hb].reshape(nb, t, hb)
    f_new = fcum[:, 2 * ha:2 * ha + hb].reshape(nb, t, hb)
    qkv_b = proj[:, 4 * wa:].reshape(nb, t, 3, hb, hd)
    kb, vb = qkv_b[:, :, 1], qkv_b[:, :, 2]
    if past is None:
        f_rows = jnp.swapaxes(f_new, 1, 2).reshape(nb, hb, 1, t)
        o_b = _fox_prompt(proj, f_rows, nb, t, hb, col0)
    else:
        k_pool, v_pool, logf_pool, page_table = past
        n_pool, page = k_pool.shape[0], k_pool.shape[1]
        f_past = _fpast(logf_pool.astype(F32), page_table)
        f_past = f_past.reshape(nb, page_table.shape[1], 1, page * hb)
        o_b = _fox_sample(qkv_b[:, :, 0].reshape(nb, t * hb, hd), kb.reshape(nb, t * hb, hd),
                          vb.reshape(nb, t * hb, hd), f_new.reshape(nb, 1, t * hb),
                          k_pool.reshape(n_pool, page * hb, hd), v_pool.reshape(n_pool, page * hb, hd),
                          f_past, page_table, hb)
        o_b = o_b.reshape(nb * t, wb)
    out = _matmul([o_a, o_b], [w_out_a, w_out_b], F32, tm=min(1024, nb * t))
    raw = proj[:, :3 * wa].reshape(nb, t, 3 * wa)
    new_conv = jnp.concatenate([conv_buf.astype(F32), raw], axis=1)[:, -(kw - 1):]
    return out, s_new, new_conv, kb, vb, logf


def _odd_mixer(h, nb, t, w_in, w_out, nh, hd, bufs=None):
    ng = len(C_GROUPS)
    proj = _matmul([h], [w_in], F32, tm=min(1024, nb * t))
    p6 = proj.reshape(nb, t, ng, 3, nh, hd)
    new_bufs = []
    if bufs is None:
        o = _dil_prompt(proj, nb, t, nh, C_GROUPS)
        for g, (window, _) in enumerate(C_GROUPS):
            keep = min(window, t)
            new_bufs.append((p6[:, t - keep:, g, 1], p6[:, t - keep:, g, 2]))
    else:
        flat = lambda a: a.reshape(nb, -1, hd)
        qs = [flat(p6[:, :, g, 0]) for g in range(ng)]
        kns = [flat(p6[:, :, g, 1]) for g in range(ng)]
        vns = [flat(p6[:, :, g, 2]) for g in range(ng)]
        o = _dil_sample(qs, kns, vns, [flat(bk) for bk, _ in bufs], [flat(bv) for _, bv in bufs],
                        C_GROUPS, nh, t)
        o = o.reshape(nb * t, nh * hd)
        for g in range(ng):
            bk, bv = bufs[g]
            wb = bk.shape[1]
            new_bufs.append((jnp.concatenate([bk, p6[:, :, g, 1]], axis=1)[:, -wb:],
                             jnp.concatenate([bv, p6[:, :, g, 2]], axis=1)[:, -wb:]))
    return _matmul([o], [w_out], F32, tm=min(1024, nb * t)), new_bufs


def _ffn_weights(w_up, w_gate, conv_w, w_down):
    d_ff = w_up.shape[1]
    d_ffp = -(-d_ff // 1024) * 1024 if d_ff > 1024 else -(-d_ff // LANE) * LANE
    padc = ((0, 0), (0, d_ffp - d_ff))
    kw = conv_w.shape[0]
    cw8 = jnp.pad(conv_w.astype(F32), ((0, SUBLANE - kw), (0, d_ffp - d_ff)))
    return (jnp.pad(w_up, padc).astype(BF16), jnp.pad(w_gate, padc).astype(BF16), cw8,
            jnp.pad(w_down, ((0, d_ffp - d_ff), (0, 0))).astype(BF16), kw, d_ff)


def _conv_ffn(h, nb, t, wts, buf=None):
    wu, wg, cw8, wd, kw, d_ff = wts
    d_ffp = wu.shape[1]
    if buf is None:
        act, tail = _ffn_up(h, wu, wg, cw8, kw, t)
        tiles = tail.shape[0] // nb
        gate_tail = tail.reshape(nb, tiles, SUBLANE, d_ffp)[:, -1, :, :d_ff]
        new_buf = gate_tail[:, SUBLANE - (kw - 1):]
    else:
        bufp = jnp.pad(buf.astype(F32), ((0, 0), (0, 0), (0, d_ffp - d_ff)))
        halos = []
        for shift in range(1, kw):
            hl = jnp.zeros((nb, t, d_ffp), F32)
            hl = hl.at[:, :shift].set(bufp[:, (kw - 1) - shift:])
            halos.append(hl.reshape(nb * t, d_ffp))
        act, gate = _ffn_up(h, wu, wg, cw8, kw, t, halos=halos)
        gate = gate.reshape(nb, t, d_ffp)[:, :, :d_ff]
        new_buf = jnp.concatenate([buf.astype(F32), gate], axis=1)[:, -(kw - 1):]
    out = _matmul([act], [wd], F32, tm=min(1024, nb * t), tn=2048, tk=1024)
    return out, new_buf


def kernel(x_prompt, x_sample, c_prompt, c_sample, state_delta, state_conv_qkv, cache_k, cache_v, cache_logf, cache_win_k0, cache_win_v0, cache_win_k1, cache_win_v1, cache_win_k2, cache_win_v2, state_ffn_conv, page_table, w_mod, b_mod, norm_pre_mix, norm_post_mix, norm_pre_ffn, norm_post_ffn, w_in_e, conv_a, a_log, dt_bias, gnorm_a, b_forget, w_out_e, w_in_o, w_out_o, w_up, w_gate, conv_ffn_w, w_down):
    bsz, seq, d = x_prompt.shape
    dbsz, dseq, _ = x_sample.shape
    depth = w_mod.shape[0]
    hd = gnorm_a.shape[-1]
    ha, hb, hc = a_log.shape[-1], b_forget.shape[-1], cache_win_k0.shape[3]
    assert hd == LANE and seq % DELTA_CHUNK == 0
    kw_a = conv_a.shape[1]
    win_caches = ((cache_win_k0, cache_win_v0), (cache_win_k1, cache_win_v1), (cache_win_k2, cache_win_v2))

    c_all = jnp.concatenate([c_prompt, c_sample], axis=0)
    c_all = jnp.pad(c_all, ((0, MOD_ROWS - c_all.shape[0]), (0, 0)))
    mod = _mod(c_all, w_mod, b_mod)
    mods = [(mod[l, :bsz].reshape(bsz, 6, d), mod[l, bsz:bsz + dbsz].reshape(dbsz, 6, d))
            for l in range(depth)]

    groups = [dict(x=x_prompt.reshape(bsz * seq, d), nb=bsz, t=seq, gi=0),
              dict(x=x_sample.reshape(dbsz * dseq, d), nb=dbsz, t=dseq, gi=1)]
    for gr in groups:
        m_0 = mods[0][gr["gi"]]
        _, gr["h"] = _post_pre(gr["x"], gr["nb"], gr["t"], shift=m_0[:, 0], scale=m_0[:, 1],
                               w_pre=norm_pre_mix[0])

    outs = {k: ([], []) for k in ("delta", "conv", "k", "v", "logf", "ffn")}
    wk = [([], []) for _ in C_GROUPS]
    wv = [([], []) for _ in C_GROUPS]
    for l in range(depth):
        if l % 2 == 0:
            e = l // 2
            ewts = _even_weights(w_in_e[e], conv_a[e], a_log[e], dt_bias[e], b_forget[e], w_out_e[e],
                                 ha, hb, hd)
        else:
            o = l // 2
            w_in_o_b, w_out_o_b = w_in_o[o].astype(BF16), w_out_o[o].astype(BF16)
        fwts = _ffn_weights(w_up[l], w_gate[l], conv_ffn_w[l], w_down[l])
        for gr in groups:
            gi, nb, t = gr["gi"], gr["nb"], gr["t"]
            m_l = mods[l][gi]
            if l % 2 == 0:
                if gi == 0:
                    conv_buf = jnp.zeros((nb, kw_a - 1, 3 * ha * hd), F32)
                    s0 = jnp.zeros((nb, ha, hd, hd), F32)
                    past = None
                else:
                    conv_buf, s0 = state_conv_qkv[e], state_delta[e]
                    past = (cache_k[e], cache_v[e], cache_logf[e], page_table)
                om, s_new, new_conv, kb, vb, logf = _even_mixer(
                    gr["h"], nb, t, ewts, gnorm_a[e], conv_buf, s0, ha, hb, hd, past)
                for key, val in (("delta", s_new), ("conv", new_conv), ("k", kb), ("v", vb), ("logf", logf)):
                    outs[key][gi].append(val)
            else:
                bufs = None if gi == 0 else tuple((bk[o], bv[o]) for bk, bv in win_caches)
                om, new_bufs = _odd_mixer(gr["h"], nb, t, w_in_o_b, w_out_o_b, hc, hd, bufs)
                for g in range(len(C_GROUPS)):
                    wk[g][gi].append(new_bufs[g][0])
                    wv[g][gi].append(new_bufs[g][1])
            gr["x"], gr["h"] = _post_pre(gr["x"], nb, t, o=om, gate=m_l[:, 2], w_post=norm_post_mix[l],
                                         shift=m_l[:, 3], scale=m_l[:, 4], w_pre=norm_pre_ffn[l])
            of, new_buf = _conv_ffn(gr["h"], nb, t, fwts, None if gi == 0 else state_ffn_conv[l])
            outs["ffn"][gi].append(new_buf)
            if l + 1 < depth:
                m_n = mods[l + 1][gi]
                gr["x"], gr["h"] = _post_pre(gr["x"], nb, t, o=of, gate=m_l[:, 5], w_post=norm_post_ffn[l],
                                             shift=m_n[:, 0], scale=m_n[:, 1], w_pre=norm_pre_mix[l + 1])
            else:
                gr["x"], _ = _post_pre(gr["x"], nb, t, o=of, gate=m_l[:, 5], w_post=norm_post_ffn[l])

    stk = lambda lst: jnp.stack(lst, axis=0)
    res = [groups[0]["x"].reshape(bsz, seq, d), groups[1]["x"].reshape(dbsz, dseq, d)]
    for key in ("delta", "conv", "k", "v", "logf"):
        res += [stk(outs[key][0]), stk(outs[key][1])]
    for g in range(len(C_GROUPS)):
        res += [stk(wk[g][0]), stk(wk[g][1]), stk(wv[g][0]), stk(wv[g][1])]
    res += [stk(outs["ffn"][0]), stk(outs["ffn"][1])]
    return tuple(res)
```

```python
import functools
import math

import jax
import jax.numpy as jnp
from jax import lax
from jax.experimental import pallas as pl
from jax.experimental.pallas import tpu as pltpu

F32 = jnp.float32
BF16 = jnp.bfloat16
HI = lax.Precision.HIGHEST

C_GROUPS = ((128, 1), (512, 4), (2048, 16))
DELTA_CHUNK = 64
RMS_EPS = 1e-6
L2_EPS = 1e-6
NEG = -1e30
LANE = 128
SUBLANE = 8
VMEM_LIMIT = 56 * 1024 * 1024
MOD_ROWS = 16
FFN_TILE = 512


def _cp(*sem):
    return pltpu.CompilerParams(dimension_semantics=sem, vmem_limit_bytes=VMEM_LIMIT)


def _dot(a, b, prec=None):
    return jnp.dot(a, b, preferred_element_type=F32, precision=prec)


def _dot_nt(a, b, prec=None):
    return lax.dot_general(a, b, (((1,), (1,)), ((), ())), preferred_element_type=F32, precision=prec)


def _dot_tn(a, b, prec=None):
    return lax.dot_general(a, b, (((0,), (0,)), ((), ())), preferred_element_type=F32, precision=prec)


def _dot3(a, b):
    ah, bh = a.astype(BF16), b.astype(BF16)
    al = (a - ah.astype(F32)).astype(BF16)
    bl = (b - bh.astype(F32)).astype(BF16)
    return _dot(ah, bh) + (_dot(ah, bl) + _dot(al, bh))


def _sigmoid(x):
    return 1.0 / (1.0 + jnp.exp(-x))


def _silu(x):
    return x * _sigmoid(x)


def _softplus(x):
    return jnp.maximum(x, 0.0) + jnp.log1p(jnp.exp(-jnp.abs(x)))


def _gelu_tanh(x):
    return 0.5 * x * (1.0 + jnp.tanh(0.7978845608028654 * (x + 0.044715 * (x * x * x))))


def _rms(x, w):
    return x * lax.rsqrt(jnp.mean(x * x, axis=-1, keepdims=True) + RMS_EPS) * w


def _iota(shape, dim):
    return lax.broadcasted_iota(jnp.int32, shape, dim)


def _row_to_col(row):
    n = row.shape[1]
    eye = _iota((n, n), 0) == _iota((n, n), 1)
    return jnp.sum(jnp.where(eye, jnp.broadcast_to(row, (n, n)), 0.0), axis=1, keepdims=True)


def _col_to_row(col):
    n = col.shape[0]
    eye = _iota((n, n), 0) == _iota((n, n), 1)
    return jnp.sum(jnp.where(eye, jnp.broadcast_to(col, (n, n)), 0.0), axis=0, keepdims=True)


def _pick_tile(n, pref):
    if n <= pref:
        return n
    t = (pref // LANE) * LANE
    while t >= LANE:
        if n % t == 0:
            return t
        t -= LANE
    return n


def _log2(n):
    assert n > 0 and n & (n - 1) == 0, n
    return n.bit_length() - 1


def _mod_kernel(c_ref, w_ref, b_ref, o_ref):
    a = _silu(c_ref[...]).astype(BF16)
    o_ref[...] = _dot(a, w_ref[...].astype(BF16)) + b_ref[...]


def _mod(c_all, w_mod, b_mod):
    depth, d, n = w_mod.shape
    r = c_all.shape[0]
    tn = _pick_tile(n, 512)
    return pl.pallas_call(
        _mod_kernel,
        grid=(depth, n // tn),
        in_specs=[pl.BlockSpec((r, d), lambda l, j: (0, 0)),
                  pl.BlockSpec((None, d, tn), lambda l, j: (l, 0, j)),
                  pl.BlockSpec((None, 1, tn), lambda l, j: (l, 0, j))],
        out_specs=pl.BlockSpec((None, r, tn), lambda l, j: (l, 0, j)),
        out_shape=jax.ShapeDtypeStruct((depth, r, n), F32),
        compiler_params=_cp("arbitrary", "arbitrary"),
        name="mod",
    )(c_all, w_mod, b_mod.reshape(depth, 1, n))


def _post_pre_kernel(*refs, has_o, has_h):
    refs = list(refs)
    x = refs.pop(0)[...]
    if has_o:
        o_ref, gate_ref, wpost_ref = refs.pop(0), refs.pop(0), refs.pop(0)
    if has_h:
        shift_ref, scale_ref, wpre_ref = refs.pop(0), refs.pop(0), refs.pop(0)
    if has_o:
        x = x + gate_ref[...] * _rms(o_ref[...], wpost_ref[...])
        refs.pop(0)[...] = x
    if has_h:
        h = _rms(x, wpre_ref[...]) * (1.0 + scale_ref[...]) + shift_ref[...]
        refs.pop(0)[...] = h.astype(BF16)


def _post_pre(x, nb, t, o=None, gate=None, w_post=None, shift=None, scale=None, w_pre=None):
    m, d = x.shape
    has_o, has_h = o is not None, shift is not None
    if t >= 128:
        tr = 128
        per = t // tr
        vec = lambda v: v.reshape(nb, 1, d)
        vec_spec = pl.BlockSpec((None, 1, d), lambda i: (i // per, 0, 0))
    else:
        tr = m
        vec = lambda v: jnp.repeat(v, t, axis=0).reshape(1, m, d)
        vec_spec = pl.BlockSpec((None, m, d), lambda i: (0, 0, 0))
    row_spec = pl.BlockSpec((tr, d), lambda i: (i, 0))
    w_spec = pl.BlockSpec((1, d), lambda i: (0, 0))
    args, in_specs, out_specs, out_shape = [x], [row_spec], [], []
    if has_o:
        args += [o, vec(gate), w_post.reshape(1, d)]
        in_specs += [row_spec, vec_spec, w_spec]
        out_specs.append(row_spec)
        out_shape.append(jax.ShapeDtypeStruct((m, d), F32))
    if has_h:
        args += [vec(shift), vec(scale), w_pre.reshape(1, d)]
        in_specs += [vec_spec, vec_spec, w_spec]
        out_specs.append(row_spec)
        out_shape.append(jax.ShapeDtypeStruct((m, d), BF16))
    outs = pl.pallas_call(
        functools.partial(_post_pre_kernel, has_o=has_o, has_h=has_h),
        grid=(m // tr,), in_specs=in_specs, out_specs=out_specs, out_shape=out_shape,
        compiler_params=_cp("arbitrary"), name="post_pre",
    )(*args)
    outs = list(outs)
    x_new = outs.pop(0) if has_o else x
    h = outs.pop(0) if has_h else None
    return x_new, h


def _mm_kernel(*refs, n_pairs, nk):
    a_refs, b_refs = refs[:n_pairs], refs[n_pairs:2 * n_pairs]
    o_ref = refs[2 * n_pairs]
    part = _dot(a_refs[0][...], b_refs[0][...])
    for a_ref, b_ref in zip(a_refs[1:], b_refs[1:]):
        part = part + _dot(a_ref[...], b_ref[...])
    if nk == 1:
        o_ref[...] = part.astype(o_ref.dtype)
        return
    acc_ref = refs[2 * n_pairs + 1]
    k = pl.program_id(2)

    @pl.when(k == 0)
    def _():
        acc_ref[...] = part

    @pl.when(k > 0)
    def _():
        acc_ref[...] += part

    @pl.when(k == nk - 1)
    def _():
        o_ref[...] = acc_ref[...].astype(o_ref.dtype)


def _matmul(a_list, b_list, out_dtype, tm=1024, tn=1024, tk=None, cols=None):
    m, kdim = a_list[0].shape
    c0, n = (0, b_list[0].shape[1]) if cols is None else cols
    tm = min(tm, m)
    tn = _pick_tile(n, tn)
    tk = kdim if tk is None else _pick_tile(kdim, tk)
    nk = kdim // tk
    assert m % tm == 0 and n % tn == 0 and kdim % tk == 0 and c0 % tn == 0
    jb = c0 // tn
    n_pairs = len(a_list)
    in_specs = ([pl.BlockSpec((tm, tk), lambda i, j, k: (i, k))] * n_pairs
                + [pl.BlockSpec((tk, tn), lambda i, j, k: (k, j + jb))] * n_pairs)
    scratch = [pltpu.VMEM((tm, tn), F32)] if nk > 1 else []
    return pl.pallas_call(
        functools.partial(_mm_kernel, n_pairs=n_pairs, nk=nk),
        grid=(m // tm, n // tn, nk), in_specs=in_specs,
        out_specs=pl.BlockSpec((tm, tn), lambda i, j, k: (i, j)),
        out_shape=jax.ShapeDtypeStruct((m, n), out_dtype),
        scratch_shapes=scratch,
        compiler_params=_cp("arbitrary", "arbitrary", "arbitrary"), name="matmul",
    )(*a_list, *b_list)


def _mmw_kernel(*refs, n_pairs, ranges):
    n_out = len(ranges)
    a_refs, b_refs = refs[:n_pairs], refs[n_pairs:2 * n_pairs]
    o_refs = refs[2 * n_pairs:2 * n_pairs + n_out]
    wb_refs = refs[2 * n_pairs + n_out:]
    j, i = pl.program_id(0), pl.program_id(1)

    @pl.when(i == 0)
    def _():
        for b_ref, wb_ref in zip(b_refs, wb_refs):
            wb_ref[...] = b_ref[...].astype(BF16)

    acc = _dot(a_refs[0][...], wb_refs[0][...])
    for a_ref, wb_ref in zip(a_refs[1:], wb_refs[1:]):
        acc = acc + _dot(a_ref[...], wb_ref[...])
    if n_out == 1:
        o_refs[0][...] = acc.astype(o_refs[0].dtype)
        return
    for o_ref, (lo, hi) in zip(o_refs, ranges):
        @pl.when((j >= lo) & (j < hi))
        def _(o_ref=o_ref):
            o_ref[...] = acc.astype(o_ref.dtype)


def _matmul_w32(a_list, w_list, out_dtype, out_cols=None, tm=512, tn=1024, w_buffers=2):
    m, kdim = a_list[0].shape
    n = w_list[0].shape[1]
    tm = min(tm, m)
    out_cols = [n] if out_cols is None else out_cols
    tn = _pick_tile(math.gcd(*out_cols), tn)
    assert m % tm == 0 and n % tn == 0
    nm = m // tm
    assert sum(out_cols) == n and all(c % tn == 0 for c in out_cols)
    row_blk = [p if w.shape[0] != kdim else 0 for p, w in enumerate(w_list)]
    ranges, lo = [], 0
    for c in out_cols:
        ranges.append((lo, lo + c // tn))
        lo += c // tn
    n_pairs = len(a_list)

    def out_map(lo, hi):
        def index(j, i):
            inside = (j >= lo) & (j < hi)
            ii = jnp.where(inside, i, jnp.where(j < lo, 0, nm - 1))
            return ii, jnp.clip(j - lo, 0, hi - lo - 1)
        return index

    outs = pl.pallas_call(
        functools.partial(_mmw_kernel, n_pairs=n_pairs, ranges=ranges),
        grid=(n // tn, nm),
        in_specs=([pl.BlockSpec((tm, kdim), lambda j, i: (i, 0))] * n_pairs
                  + [pl.BlockSpec((kdim, tn), lambda j, i, rb=rb: (rb, j),
                                  pipeline_mode=pl.Buffered(w_buffers)) for rb in row_blk]),
        out_specs=[pl.BlockSpec((tm, tn), out_map(lo, hi)) for lo, hi in ranges],
        out_shape=[jax.ShapeDtypeStruct((m, c), out_dtype) for c in out_cols],
        scratch_shapes=[pltpu.VMEM((kdim, tn), BF16)] * n_pairs,
        compiler_params=_cp("arbitrary", "arbitrary"), name="matmul_w32",
    )(*a_list, *w_list)
    return outs[0] if len(out_cols) == 1 else outs


def _gates_kernel(h_ref, w_ref, par_ref, g_ref, f_ref, carry_ref, *, ha, hb, seg):
    @pl.when(pl.program_id(1) == 0)
    def _():
        carry_ref[...] = jnp.zeros_like(carry_ref)

    p = _dot(h_ref[...], w_ref[...])
    tr = p.shape[0]
    lane = _iota(p.shape, 1)
    a_log, bias = par_ref[0:1, :], par_ref[1:2, :]
    x = p + bias
    beta = _sigmoid(p)
    g = -jnp.exp(a_log) * _softplus(x)
    logf = -_softplus(-x)
    out = jnp.where(lane < ha, beta,
                    jnp.where(lane < 2 * ha, g, jnp.where(lane < 2 * ha + hb, logf, 0.0)))
    g_ref[...] = out
    rr, cc = _iota((tr, tr), 0), _iota((tr, tr), 1)
    tri = rr >= cc
    if seg is not None:
        tri = tri & ((rr >> _log2(seg)) == (cc >> _log2(seg)))
    cs = _dot(tri.astype(F32), out, HI) + carry_ref[0:1, :]
    f_ref[...] = cs
    carry_ref[0:1, :] = cs[tr - 1:tr, :]


def _gates(h, w_small, par, nb, t, ha, hb):
    m, d = h.shape
    if t >= 128:
        tr, per, seg, nb_grid = min(t, 512), t // min(t, 512), None, nb
    else:
        tr, per, seg, nb_grid = m, 1, t, 1
    spec = pl.BlockSpec((tr, LANE), lambda b, j: (b * per + j, 0))
    return pl.pallas_call(
        functools.partial(_gates_kernel, ha=ha, hb=hb, seg=seg),
        grid=(nb_grid, per),
        in_specs=[pl.BlockSpec((tr, d), lambda b, j: (b * per + j, 0)),
                  pl.BlockSpec((d, LANE), lambda b, j: (0, 0)),
                  pl.BlockSpec((SUBLANE, LANE), lambda b, j: (0, 0))],
        out_specs=[spec, spec],
        out_shape=[jax.ShapeDtypeStruct((m, LANE), F32)] * 2,
        scratch_shapes=[pltpu.VMEM((SUBLANE, LANE), F32)],
        compiler_params=_cp("arbitrary", "arbitrary"), name="gates",
    )(h, w_small, par)


def _each(f, *lists):
    return [f(*args) for args in zip(*lists)]


def _unit_lower_inverse(lows):
    c = lows[0].shape[0]
    assert c == 64
    r, s = _iota((c, c), 0), _iota((c, c), 1)
    eye = (r == s).astype(F32)
    same16 = (r >> 4) == (s >> 4)
    same32 = (r >> 5) == (s >> 5)
    ld = _each(lambda low: jnp.where(same16, low, 0.0), lows)
    x = _each(lambda a: eye - a, ld)
    p = _each(lambda a: _dot3(a, a), ld)
    for _ in range(2):
        xp = _each(lambda a, b: _dot3(jnp.concatenate([a, b], axis=0), b), x, p)
        x = _each(lambda a, b: a + b[:c], x, xp)
        p = _each(lambda b: b[c:], xp)
    x = _each(lambda a, b: a + _dot3(a, b), x, p)
    for keep in (same32 & jnp.logical_not(same16), jnp.logical_not(same32)):
        off = _each(lambda low: jnp.where(keep, low, 0.0), lows)
        y = _each(_dot3, x, off)
        x = _each(lambda a, b: a - _dot3(b, a), x, y)
    return x


def _delta_kernel(q_ref, k_ref, v_ref, z_ref, gt_ref, cwq_ref, cwk_ref, cwv_ref,
                  cbq_ref, cbk_ref, cbv_ref, s0_ref, gn_ref, o_ref, sfin_ref, ext_ref, s_ref,
                  *, hpb, ha, kw, valid_rows, nc):
    c_idx = pl.program_id(2)
    hblk = pl.program_id(1)
    c = q_ref.shape[0]
    hd = LANE

    @pl.when(c_idx == 0)
    def _():
        s_ref[...] = s0_ref[...]
        for i, cb in enumerate((cbq_ref, cbk_ref, cbv_ref)):
            ext_ref[i, 0:SUBLANE, :] = cb[...]

    conv = []
    for i, (r_ref, cw_ref) in enumerate(((q_ref, cwq_ref), (k_ref, cwk_ref), (v_ref, cwv_ref))):
        ext_ref[i, SUBLANE:SUBLANE + c, :] = r_ref[...]
        acc = ext_ref[i, SUBLANE:SUBLANE + c, :] * cw_ref[kw - 1:kw, :]
        for j in range(kw - 1):
            acc = acc + ext_ref[i, pl.ds(SUBLANE - (kw - 1) + j, c), :] * cw_ref[j:j + 1, :]
        ext_ref[i, 0:SUBLANE, :] = ext_ref[i, c:c + SUBLANE, :]
        acc = _silu(acc)
        if valid_rows < c:
            acc = jnp.where(_iota(acc.shape, 0) < valid_rows, acc, 0.0)
        conv.append(acc)
    xq, xk, xv = conv

    gt = gt_ref[...]
    rr, ss = _iota((c, c), 0), _iota((c, c), 1)
    incl, strict = rr >= ss, rr > ss
    gcum = _dot(incl.astype(F32), gt, HI)
    lane = _iota(gt.shape, 1)
    gn = gn_ref[...]
    heads = list(range(hpb))
    sls = [slice(j * hd, (j + 1) * hd) for j in heads]
    beta = [jnp.sum(jnp.where(lane == hblk * hpb + j, gt, 0.0), axis=1, keepdims=True) for j in heads]
    gc = [jnp.sum(jnp.where(lane == ha + hblk * hpb + j, gcum, 0.0), axis=1, keepdims=True) for j in heads]
    qh = [xq[:, sl] for sl in sls]
    qh = _each(lambda a: a * lax.rsqrt(jnp.sum(a * a, axis=-1, keepdims=True) + L2_EPS) * (hd ** -0.5), qh)
    kh = [xk[:, sl] for sl in sls]
    kh = _each(lambda a: a * lax.rsqrt(jnp.sum(a * a, axis=-1, keepdims=True) + L2_EPS), kh)
    decay = _each(lambda g: jnp.where(incl, jnp.exp(g - _col_to_row(g)), 0.0), gc)
    kb = _each(lambda a, b: a * b, kh, beta)
    vb = [xv[:, sl] * b for sl, b in zip(sls, beta)]
    egc = _each(jnp.exp, gc)
    kk = _each(lambda a, q, k: _dot_nt(jnp.concatenate([a, q], axis=0).astype(BF16), k.astype(BF16)),
               kb, qh, kh)
    low = _each(lambda a, d: jnp.where(strict, a[:c] * d, 0.0), kk, decay)
    attn = _each(lambda a, d: (a[c:] * d).astype(BF16), kk, decay)
    tmat = _unit_lower_inverse(low)
    uw = _each(lambda t, v, k, e: _dot(t.astype(BF16), jnp.concatenate([v, k * e], axis=1).astype(BF16)),
               tmat, vb, kb, egc)
    st = [s_ref[j] for j in heads]
    ws = _each(lambda a, q, e, s: _dot(jnp.concatenate([a[:, hd:], q * e], axis=0).astype(BF16),
                                       s.astype(BF16)), uw, qh, egc, st)
    v_new = _each(lambda a, b: (a[:, :hd] - b[:c]).astype(BF16), uw, ws)
    o = _each(lambda b, a, v: b[c:] + _dot(a, v), ws, attn, v_new)
    g_last = _each(lambda g: g[c - 1:c, :], gc)
    k_dec = _each(lambda k, gl, g: (k * jnp.exp(gl - g)).astype(BF16), kh, g_last, gc)
    s_new = _each(lambda s, gl, k, v: s * jnp.exp(gl) + _dot_tn(k, v), st, g_last, k_dec, v_new)
    for j in heads:
        s_ref[j] = s_new[j]
        o_ref[:, sls[j]] = (_rms(o[j], gn) * _silu(z_ref[:, sls[j]])).astype(o_ref.dtype)

    @pl.when(c_idx == nc - 1)
    def _():
        sfin_ref[...] = s_ref[...]


def _delta(proj, gates, conv_w, conv_buf8, s0, gnorm, nb, tp, valid_rows, ha, hpb=8):
    c = DELTA_CHUNK
    hd = LANE
    hpb = min(hpb, ha)
    w = hpb * hd
    nc = tp // c
    nhb = ha // hpb
    kw = conv_w.shape[0]
    cw8 = jnp.pad(conv_w, ((0, SUBLANE - kw), (0, 0)))

    def col(off):
        return pl.BlockSpec((c, w), lambda b, h, i, off=off: (b * nc + i, off * nhb + h))

    def cwspec(off):
        return pl.BlockSpec((SUBLANE, w), lambda b, h, i, off=off: (0, off * nhb + h))

    def cbspec(off):
        return pl.BlockSpec((None, SUBLANE, w), lambda b, h, i, off=off: (b, 0, off * nhb + h))

    state_spec = pl.BlockSpec((None, hpb, hd, hd), lambda b, h, i: (b, h, 0, 0))
    return pl.pallas_call(
        functools.partial(_delta_kernel, hpb=hpb, ha=ha, kw=kw, valid_rows=valid_rows, nc=nc),
        grid=(nb, nhb, nc),
        in_specs=[col(0), col(1), col(2), col(3),
                  pl.BlockSpec((c, LANE), lambda b, h, i: (b * nc + i, 0)),
                  cwspec(0), cwspec(1), cwspec(2), cbspec(0), cbspec(1), cbspec(2),
                  state_spec, pl.BlockSpec((1, hd), lambda b, h, i: (0, 0))],
        out_specs=[pl.BlockSpec((c, w), lambda b, h, i: (b * nc + i, h)), state_spec],
        out_shape=[jax.ShapeDtypeStruct((nb * tp, ha * hd), BF16),
                   jax.ShapeDtypeStruct(s0.shape, F32)],
        scratch_shapes=[pltpu.VMEM((3, c + SUBLANE, w), F32), pltpu.VMEM((hpb, hd, hd), F32)],
        compiler_params=_cp("arbitrary", "arbitrary", "arbitrary"), name="delta",
    )(proj, proj, proj, proj, gates, cw8, cw8, cw8, conv_buf8, conv_buf8, conv_buf8, s0,
      gnorm.reshape(1, hd))


def _softmax_step(carry, s, v, valid=None):
    m, l, acc = carry
    m_new = jnp.maximum(m, jnp.max(s, axis=-1, keepdims=True))
    alpha = jnp.exp(m - m_new)
    p = jnp.exp(s - m_new)
    if valid is not None:
        p = jnp.where(valid, p, 0.0)
    l = alpha * l + jnp.sum(p, axis=-1, keepdims=True)
    acc = alpha * acc + _dot(p.astype(BF16), v)
    return m_new, l, acc


def _softmax_init(rows, hd):
    return (jnp.full((rows, 1), NEG, F32), jnp.zeros((rows, 1), F32), jnp.zeros((rows, hd), F32))


def _fox_prompt_kernel(q_ref, k_ref, v_ref, f_ref, o_ref, *, tq, scale):
    qi = pl.program_id(2)
    q0 = pl.multiple_of(qi * tq, tq)
    q = (q_ref[...] * scale).astype(BF16)
    fq = _row_to_col(f_ref[:, pl.ds(q0, tq)])
    rpos = q0 + _iota((tq, tq), 0)
    cidx = _iota((tq, tq), 1)

    def body(i, carry):
        k0 = pl.multiple_of((qi - i) * tq, tq)
        k = k_ref[pl.ds(k0, tq), :].astype(BF16)
        v = v_ref[pl.ds(k0, tq), :].astype(BF16)
        s = _dot_nt(q, k) + fq - f_ref[:, pl.ds(k0, tq)]
        s = jnp.where(k0 + cidx <= rpos, s, NEG)
        return _softmax_step(carry, s, v)

    _, l, acc = lax.fori_loop(0, qi + 1, body, _softmax_init(tq, LANE))
    o_ref[...] = (acc / l).astype(o_ref.dtype)


def _fox_prompt(q_arr, q_col, k_arr, v_arr, f_rows, nb, s, nh, tq=256):
    tq = min(tq, s)
    nq = s // tq
    hd = LANE
    return pl.pallas_call(
        functools.partial(_fox_prompt_kernel, tq=tq, scale=hd ** -0.5),
        grid=(nb, nh, nq),
        in_specs=[pl.BlockSpec((tq, hd), lambda b, h, i: (b * nq + i, q_col + h)),
                  pl.BlockSpec((s, hd), lambda b, h, i: (b, h)),
                  pl.BlockSpec((s, hd), lambda b, h, i: (b, h)),
                  pl.BlockSpec((None, None, 1, s), lambda b, h, i: (b, h, 0, 0))],
        out_specs=pl.BlockSpec((tq, hd), lambda b, h, i: (b * nq + i, h)),
        out_shape=jax.ShapeDtypeStruct((nb * s, nh * hd), BF16),
        compiler_params=_cp("arbitrary", "arbitrary", "arbitrary"), name="fox_prompt",
    )(q_arr, k_arr, v_arr, f_rows)


def _fpast_kernel(pt_ref, *refs, group):
    lp_refs, o_ref, carry_ref = refs[:group], refs[group], refs[group + 1]

    @pl.when(pl.program_id(1) == 0)
    def _():
        carry_ref[...] = jnp.zeros_like(carry_ref)

    p = lp_refs[0].shape[0]
    upper = (_iota((p, p), 1) > _iota((p, p), 0)).astype(F32)
    for i in range(group):
        lp = lp_refs[i][...]
        o_ref[group - 1 - i] = -(_dot(upper, lp, HI) + carry_ref[0:1, :])
        carry_ref[0:1, :] = carry_ref[0:1, :] + jnp.sum(lp, axis=0, keepdims=True)


def _fpast(logf_pool, page_table):
    db, n_pages = page_table.shape
    _, p, nh = logf_pool.shape
    group = 8 if n_pages % 8 == 0 else 1
    nblk = n_pages // group

    def lp_spec(i):
        return pl.BlockSpec((None, p, nh),
                            lambda b, j, pt, i=i: (pt[b, n_pages - 1 - (j * group + i)], 0, 0))

    return pl.pallas_call(
        functools.partial(_fpast_kernel, group=group),
        grid_spec=pltpu.PrefetchScalarGridSpec(
            num_scalar_prefetch=1, grid=(db, nblk),
            in_specs=[lp_spec(i) for i in range(group)],
            out_specs=pl.BlockSpec((None, group, p, nh), lambda b, j, pt: (b, nblk - 1 - j, 0, 0)),
            scratch_shapes=[pltpu.VMEM((SUBLANE, nh), F32)]),
        out_shape=jax.ShapeDtypeStruct((db, n_pages, p, nh), F32),
        compiler_params=_cp("arbitrary", "arbitrary"), name="fpast",
    )(page_table, *([logf_pool] * group))


def _fox_sample_kernel(pt_ref, q_ref, k_ref, v_ref, fp_ref, kn_ref, vn_ref, fn_ref, o_ref,
                       m_ref, l_ref, acc_ref, bias_ref, *, nh, scale, n_pages):
    p = pl.program_id(1)
    r = q_ref.shape[0]
    rows = k_ref.shape[0]
    lognh = _log2(nh)
    fn = fn_ref[...]

    @pl.when(p == 0)
    def _():
        m_ref[...] = jnp.full_like(m_ref, NEG)
        l_ref[...] = jnp.zeros_like(l_ref)
        acc_ref[...] = jnp.zeros_like(acc_ref)
        match = (_iota((r, rows), 0) & (nh - 1)) == (_iota((r, rows), 1) & (nh - 1))
        bias_ref[...] = jnp.where(match, _row_to_col(fn), NEG)

    q = (q_ref[...] * scale).astype(BF16)

    def update(s, v):
        m, l, acc = _softmax_step((m_ref[...], l_ref[...], acc_ref[...]), s, v)
        m_ref[...], l_ref[...], acc_ref[...] = m, l, acc

    update(_dot_nt(q, k_ref[...].astype(BF16)) + (bias_ref[...] - fp_ref[...]), v_ref[...].astype(BF16))

    @pl.when(p == n_pages - 1)
    def _():
        rr, cc = _iota((r, r), 0), _iota((r, r), 1)
        ok = ((rr & (nh - 1)) == (cc & (nh - 1))) & ((cc >> lognh) <= (rr >> lognh))
        s2 = _dot_nt(q, kn_ref[...].astype(BF16)) + _row_to_col(fn) - fn
        update(jnp.where(ok, s2, NEG), vn_ref[...].astype(BF16))
        o_ref[...] = (acc_ref[...] / l_ref[...]).astype(o_ref.dtype)


def _fox_sample(q, k_new, v_new, f_new, k_pool, v_pool, f_past, page_table, nh):
    db, r, hd = q.shape
    n_pages = page_table.shape[1]
    rows = k_pool.shape[1]
    tok = pl.BlockSpec((None, r, hd), lambda b, p, pt: (b, 0, 0))
    page = pl.BlockSpec((None, rows, hd), lambda b, p, pt: (pt[b, p], 0, 0))
    return pl.pallas_call(
        functools.partial(_fox_sample_kernel, nh=nh, scale=hd ** -0.5, n_pages=n_pages),
        grid_spec=pltpu.PrefetchScalarGridSpec(
            num_scalar_prefetch=1, grid=(db, n_pages),
            in_specs=[tok, page, page,
                      pl.BlockSpec((None, None, 1, rows), lambda b, p, pt: (b, p, 0, 0)),
                      tok, tok, pl.BlockSpec((None, 1, r), lambda b, p, pt: (b, 0, 0))],
            out_specs=tok,
            scratch_shapes=[pltpu.VMEM((r, 1), F32), pltpu.VMEM((r, 1), F32),
                            pltpu.VMEM((r, hd), F32), pltpu.VMEM((r, rows), F32)]),
        out_shape=jax.ShapeDtypeStruct((db, r, hd), BF16),
        compiler_params=_cp("arbitrary", "arbitrary"), name="fox_sample",
    )(page_table, q, k_pool, v_pool, f_past, k_new, v_new, f_new)


def _dil_prompt_kernel(*refs, groups, tb, scale, par):
    ng = len(groups)
    q_refs, k_refs, v_refs = refs[:ng], refs[ng:2 * ng], refs[2 * ng:3 * ng]
    o_ref, m_ref, l_ref, acc_ref = refs[3 * ng:]
    seq = q_refs[0].shape[0]
    diff = _iota((tb, tb), 0) - _iota((tb, tb), 1)

    def rows_of(res, blk, dil):
        return pl.ds(res + blk * tb * dil, tb, stride=dil) if dil > 1 else pl.ds(blk * tb, tb)

    for g, (window, dil) in enumerate(groups):
        reach = window // dil
        nqb = seq // dil // tb
        back = -(-reach // tb)
        units = [(res, qb) for res in range(dil) for qb in range(nqb)]
        for c0 in range(0, len(units), par):
            chunk = units[c0:c0 + par]
            rows = [rows_of(res, qb, dil) for res, qb in chunk]
            qs = [(q_refs[g][rw, :] * scale).astype(BF16) for rw in rows]
            if g == 0:
                carries = [_softmax_init(tb, LANE) for _ in chunk]
            else:
                carries = [(m_ref[rw, :], l_ref[rw, :], acc_ref[rw, :]) for rw in rows]
            for step in range(back + 1):
                live = [i for i, (_, qb) in enumerate(chunk) if qb - step >= 0]
                if not live:
                    continue
                krows = [rows_of(chunk[i][0], chunk[i][1] - step, dil) for i in live]
                ks = [k_refs[g][rw, :].astype(BF16) for rw in krows]
                vs = [v_refs[g][rw, :].astype(BF16) for rw in krows]
                ss = [_dot_nt(qs[i], k) for i, k in zip(live, ks)]
                if step * tb - (tb - 1) < 0 or step * tb + (tb - 1) > reach:
                    dist = diff + step * tb
                    valid = (dist >= 0) & (dist <= reach)
                    ss = [jnp.where(valid, s, NEG) for s in ss]
                ms = [jnp.maximum(carries[i][0], jnp.max(s, axis=-1, keepdims=True)) for i, s in zip(live, ss)]
                ps = [jnp.exp(s - m) for s, m in zip(ss, ms)]
                alphas = [jnp.exp(carries[i][0] - m) for i, m in zip(live, ms)]
                pvs = [_dot(p.astype(BF16), v) for p, v in zip(ps, vs)]
                for i, m, p, a, pv in zip(live, ms, ps, alphas, pvs):
                    carries[i] = (m, a * carries[i][1] + jnp.sum(p, axis=-1, keepdims=True),
                                  a * carries[i][2] + pv)
            for rw, (m, l, acc) in zip(rows, carries):
                m_ref[rw, :], l_ref[rw, :], acc_ref[rw, :] = m, l, acc
    o_ref[...] = (acc_ref[...] / l_ref[...]).astype(o_ref.dtype)


def _dil_prompt(srcs, nb, s, nh, groups, tb=128, par=4):
    hd = LANE
    ng = len(groups)
    tb = min(tb, s)
    for window, dil in groups:
        assert s % (dil * tb) == 0 and window % dil == 0
    args, specs = [], []
    for which in range(3):
        for g in range(ng):
            arr, col = srcs[g][which]
            args.append(arr)
            specs.append(pl.BlockSpec((s, hd), lambda b, h, col=col: (b, col + h)))
    return pl.pallas_call(
        functools.partial(_dil_prompt_kernel, groups=groups, tb=tb, scale=hd ** -0.5, par=par),
        grid=(nb, nh),
        in_specs=specs,
        out_specs=pl.BlockSpec((s, hd), lambda b, h: (b, h)),
        out_shape=jax.ShapeDtypeStruct((nb * s, nh * hd), BF16),
        scratch_shapes=[pltpu.VMEM((s, 1), F32), pltpu.VMEM((s, 1), F32), pltpu.VMEM((s, hd), F32)],
        compiler_params=_cp("arbitrary", "arbitrary"), name="dil_prompt",
    )(*args)


def _dil_sample_kernel(*refs, groups, wbs, tb, nh, scale, t):
    ng = len(groups)
    q_refs, kb_refs, vb_refs = refs[:ng], refs[ng:2 * ng], refs[2 * ng:3 * ng]
    kn_refs, vn_refs = refs[3 * ng:4 * ng], refs[4 * ng:5 * ng]
    o_ref, m_ref, l_ref, acc_ref = refs[5 * ng:5 * ng + 4]
    j = pl.program_id(1)
    r = q_refs[0].shape[0]
    rows = tb * nh
    lognh = _log2(nh)
    nblks = [wb // tb for wb in wbs]
    total = sum(nblks)

    @pl.when(j == 0)
    def _():
        m_ref[...] = jnp.full_like(m_ref, NEG)
        l_ref[...] = jnp.zeros_like(l_ref)
        acc_ref[...] = jnp.zeros_like(acc_ref)

    def update(s, v, valid):
        m, l, acc = _softmax_step((m_ref[...], l_ref[...], acc_ref[...]),
                                  jnp.where(valid, s, NEG), v, valid)
        m_ref[...], l_ref[...], acc_ref[...] = m, l, acc

    start = 0
    for g, (window, dil) in enumerate(groups):
        def block(g=g, window=window, dil=dil, start=start):
            rr, cc = _iota((r, rows), 0), _iota((r, rows), 1)
            key_tok = (j - start) * tb + (cc >> lognh)
            delta = wbs[g] + (rr >> lognh) - key_tok
            valid = (((rr & (nh - 1)) == (cc & (nh - 1))) & ((delta & (dil - 1)) == 0)
                     & (delta <= window))
            q = (q_refs[g][...] * scale).astype(BF16)
            update(_dot_nt(q, kb_refs[g][...].astype(BF16)), vb_refs[g][...].astype(BF16), valid)

        pl.when((j >= start) & (j < start + nblks[g]))(block)
        start += nblks[g]

    @pl.when(j == total - 1)
    def _():
        rr, cc = _iota((r, r), 0), _iota((r, r), 1)
        delta = (rr >> lognh) - (cc >> lognh)
        match = (rr & (nh - 1)) == (cc & (nh - 1))
        for g, (window, dil) in enumerate(groups):
            valid = match & (delta >= 0) & ((delta & (dil - 1)) == 0) & (delta <= window)
            q = (q_refs[g][...] * scale).astype(BF16)
            update(_dot_nt(q, kn_refs[g][...].astype(BF16)), vn_refs[g][...].astype(BF16), valid)
        o_ref[...] = (acc_ref[...] / l_ref[...]).astype(o_ref.dtype)


def _dil_sample(qs, kns, vns, kbufs, vbufs, groups, nh, t):
    db, r, hd = qs[0].shape
    ng = len(groups)
    wbs = [kb.shape[1] // nh for kb in kbufs]
    tb = 128
    while any(wb % tb for wb in wbs):
        tb //= 2
    nblks = [wb // tb for wb in wbs]
    starts = [sum(nblks[:g]) for g in range(ng)]
    total = sum(nblks)
    tok = pl.BlockSpec((None, r, hd), lambda b, j: (b, 0, 0))

    def buf_spec(g):
        return pl.BlockSpec(
            (None, tb * nh, hd),
            lambda b, j, g=g: (b, jnp.clip(j - starts[g], 0, nblks[g] - 1), 0))

    bspecs = [buf_spec(g) for g in range(ng)]
    return pl.pallas_call(
        functools.partial(_dil_sample_kernel, groups=groups, wbs=wbs, tb=tb, nh=nh,
                          scale=hd ** -0.5, t=t),
        grid=(db, total),
        in_specs=[tok] * ng + bspecs + bspecs + [tok] * (2 * ng),
        out_specs=tok,
        out_shape=jax.ShapeDtypeStruct((db, r, hd), BF16),
        scratch_shapes=[pltpu.VMEM((r, 1), F32), pltpu.VMEM((r, 1), F32), pltpu.VMEM((r, hd), F32)],
        compiler_params=_cp("arbitrary", "arbitrary"), name="dil_sample",
    )(*qs, *kbufs, *vbufs, *kns, *vns)


def _ffn_up_kernel(*refs, carry_rows, kw, tiles_per_batch, t, n_valid):
    h_ref, wu_ref, wg_ref, cw_ref = refs[:4]
    nh = kw - 1
    if carry_rows:
        act_ref, tail_ref, ext_ref, wub_ref, wgb_ref = refs[4:]
    else:
        halo_refs = refs[4:4 + nh]
        act_ref, tail_ref, ext_ref, wub_ref, wgb_ref = refs[4 + nh:]
    n, m = pl.program_id(0), pl.program_id(1)

    @pl.when(m == 0)
    def _():
        wub_ref[...] = wu_ref[...].astype(BF16)
        wgb_ref[...] = wg_ref[...].astype(BF16)

    h = h_ref[...]
    tm = h.shape[0]
    up = _dot(h, wub_ref[...])
    gate = _dot(h, wgb_ref[...])
    tn = gate.shape[1]
    if carry_rows:
        @pl.when((m % tiles_per_batch) == 0)
        def _():
            ext_ref[0:SUBLANE, :] = jnp.zeros((SUBLANE, tn), F32)

        @pl.when((m % tiles_per_batch) != 0)
        def _():
            ext_ref[0:SUBLANE, :] = ext_ref[tm:tm + SUBLANE, :]

        ext_ref[SUBLANE:SUBLANE + tm, :] = gate
        tail_ref[...] = gate[tm - SUBLANE:, :]
    else:
        ext_ref[0:SUBLANE, :] = jnp.zeros((SUBLANE, tn), F32)
        ext_ref[SUBLANE:SUBLANE + tm, :] = gate
        tail_ref[...] = gate
        pos = _iota(gate.shape, 0) & (t - 1)
    gc = gate * cw_ref[kw - 1:kw, :]
    for j in range(kw - 1):
        shift = kw - 1 - j
        sh = ext_ref[pl.ds(SUBLANE - shift, tm), :]
        if not carry_rows:
            sh = jnp.where(pos < shift, halo_refs[shift - 1][...], sh)
        gc = gc + sh * cw_ref[j:j + 1, :]
    act = _gelu_tanh(gc) * up
    act = jnp.where(n * tn + _iota(act.shape, 1) < n_valid, act, 0.0)
    act_ref[...] = act.astype(act_ref.dtype)


def _ffn_up(h, wu, wg, cw8, kw, rows_per_batch, n_pad, halos=None, tm=512, tn=512):
    m, d = h.shape
    n_valid = wu.shape[1]
    carry_rows = halos is None
    tm = min(tm, rows_per_batch) if carry_rows else m
    tn = _pick_tile(n_pad, tn)
    nm, nn = m // tm, n_pad // tn
    in_specs = [pl.BlockSpec((tm, d), lambda j, i: (i, 0)),
                pl.BlockSpec((d, tn), lambda j, i: (0, j)),
                pl.BlockSpec((d, tn), lambda j, i: (0, j)),
                pl.BlockSpec((SUBLANE, tn), lambda j, i: (0, j))]
    args = [h, wu, wg, cw8]
    if carry_rows:
        tail_spec = pl.BlockSpec((None, SUBLANE, tn), lambda j, i: (i, 0, j))
        tail_shape = jax.ShapeDtypeStruct((nm, SUBLANE, n_pad), F32)
    else:
        _log2(rows_per_batch)
        in_specs += [pl.BlockSpec((tm, tn), lambda j, i: (i, j))] * len(halos)
        args += list(halos)
        tail_spec = pl.BlockSpec((tm, tn), lambda j, i: (i, j))
        tail_shape = jax.ShapeDtypeStruct((m, n_pad), F32)
    return pl.pallas_call(
        functools.partial(_ffn_up_kernel, carry_rows=carry_rows, kw=kw,
                          tiles_per_batch=max(rows_per_batch // tm, 1), t=rows_per_batch,
                          n_valid=n_valid),
        grid=(nn, nm), in_specs=in_specs,
        out_specs=[pl.BlockSpec((tm, tn), lambda j, i: (i, j)), tail_spec],
        out_shape=[jax.ShapeDtypeStruct((m, n_pad), BF16), tail_shape],
        scratch_shapes=[pltpu.VMEM((tm + SUBLANE, tn), F32), pltpu.VMEM((d, tn), BF16),
                        pltpu.VMEM((d, tn), BF16)],
        compiler_params=_cp("arbitrary", "arbitrary"), name="ffn_up",
    )(*args)


def _even_weights(w_in, conv_w, a_log, dt_bias, b_f, w_out, ha, hb, hd):
    wa, wb = ha * hd, hb * hd
    o2 = 4 * wa
    o4 = o2 + 2 * ha
    o5 = o4 + 3 * wb
    w_main = jnp.concatenate([w_in[:, :o2], w_in[:, o4:o5]], axis=1).astype(BF16)
    small = jnp.concatenate([w_in[:, o2:o4], w_in[:, o5:]], axis=1)
    assert small.shape[1] <= LANE
    w_small = jnp.pad(small, ((0, 0), (0, LANE - small.shape[1]))).astype(BF16)
    par = jnp.zeros((SUBLANE, LANE), F32)
    par = par.at[0, ha:2 * ha].set(a_log.astype(F32))
    par = par.at[1, ha:2 * ha].set(dt_bias.astype(F32))
    par = par.at[1, 2 * ha:2 * ha + hb].set(b_f.astype(F32))
    return w_main, w_small, par, conv_w, w_out


def _even_mixer(h, nb, t, wts, gnorm, conv_buf, s0, ha, hb, hd, past=None):
    w_main, w_small, par, conv_w, w_out = wts
    wa, wb = ha * hd, hb * hd
    kw = conv_w.shape[0]
    tm = min(1024, nb * t)
    proj = _matmul([h], [w_main], F32, tm=tm, cols=(0, 4 * wa + wb))
    k_new = _matmul([h], [w_main], F32, tm=tm, cols=(4 * wa + wb, wb))
    v_new = _matmul([h], [w_main], F32, tm=tm, cols=(4 * wa + 2 * wb, wb))
    gates, fcum = _gates(h, w_small, par, nb, t, ha, hb)

    c = DELTA_CHUNK
    tp = -(-t // c) * c
    conv_buf8 = jnp.pad(conv_buf.astype(F32), ((0, 0), (SUBLANE - (kw - 1), 0), (0, 0)))
    if tp == t:
        proj_d, gates_d = proj, gates
    else:
        pad = lambda a: jnp.pad(a.reshape(nb, t, -1), ((0, 0), (0, tp - t), (0, 0))).reshape(nb * tp, -1)
        proj_d, gates_d = pad(proj[:, :4 * wa]), pad(gates)
    o_a, s_new = _delta(proj_d, gates_d, conv_w, conv_buf8, s0, gnorm, nb, tp, min(t, c) if tp != t else c,
                        ha)
    if tp != t:
        o_a = o_a.reshape(nb, tp, wa)[:, :t].reshape(nb * t, wa)

    logf = gates[:, 2 * ha:2 * ha + hb].reshape(nb, t, hb)
    f_new = fcum[:, 2 * ha:2 * ha + hb].reshape(nb, t, hb)
    kb, vb = k_new.reshape(nb, t, hb, hd), v_new.reshape(nb, t, hb, hd)
    if past is None:
        f_rows = jnp.swapaxes(f_new, 1, 2).reshape(nb, hb, 1, t)
        o_b = _fox_prompt(proj, 4 * wa // hd, k_new, v_new, f_rows, nb, t, hb)
    else:
        k_pool, v_pool, logf_pool, page_table = past
        n_pool, page = k_pool.shape[0], k_pool.shape[1]
        f_past = _fpast(logf_pool.astype(F32), page_table)
        f_past = f_past.reshape(nb, page_table.shape[1], 1, page * hb)
        o_b = _fox_sample(proj[:, 4 * wa:].reshape(nb, t * hb, hd), k_new.reshape(nb, t * hb, hd),
                          v_new.reshape(nb, t * hb, hd), f_new.reshape(nb, 1, t * hb),
                          k_pool.reshape(n_pool, page * hb, hd), v_pool.reshape(n_pool, page * hb, hd),
                          f_past, page_table, hb)
        o_b = o_b.reshape(nb * t, wb)
    out = _matmul_w32([o_a, o_b], [w_out, w_out], F32, tm=min(512, nb * t))
    raw = proj[:, :3 * wa].reshape(nb, t, 3 * wa)
    new_conv = jnp.concatenate([conv_buf.astype(F32), raw], axis=1)[:, -(kw - 1):]
    return out, s_new, new_conv, kb, vb, logf


def _odd_mixer(h, nb, t, w_in, w_out, nh, hd, bufs=None):
    ng = len(C_GROUPS)
    wc = nh * hd
    new_bufs = []
    if bufs is None:
        main, k_last, v_last = _matmul_w32([h], [w_in], F32, out_cols=[(3 * ng - 2) * wc, wc, wc],
                                           tm=min(512, nb * t), w_buffers=1)
        srcs = [((main, g * 3 * nh), (main, g * 3 * nh + nh), (main, g * 3 * nh + 2 * nh))
                for g in range(ng - 1)]
        srcs.append(((main, (ng - 1) * 3 * nh), (k_last, 0), (v_last, 0)))
        o = _dil_prompt(srcs, nb, t, nh, C_GROUPS)
        m3 = main.reshape(nb, t, -1)
        for g, (window, _) in enumerate(C_GROUPS):
            keep = min(window, t)
            if g < ng - 1:
                kv = [m3[:, t - keep:, (g * 3 + r) * wc:(g * 3 + r + 1) * wc] for r in (1, 2)]
            else:
                kv = [a.reshape(nb, t, wc)[:, t - keep:] for a in (k_last, v_last)]
            new_bufs.append(tuple(a.reshape(nb, keep, nh, hd) for a in kv))
        return _matmul_w32([o], [w_out], F32, tm=min(512, nb * t)), new_bufs
    proj = _matmul_w32([h], [w_in], F32, tm=min(512, nb * t), w_buffers=1)
    p6 = proj.reshape(nb, t, ng, 3, nh, hd)
    flat = lambda a: a.reshape(nb, -1, hd)
    qs = [flat(p6[:, :, g, 0]) for g in range(ng)]
    kns = [flat(p6[:, :, g, 1]) for g in range(ng)]
    vns = [flat(p6[:, :, g, 2]) for g in range(ng)]
    o = _dil_sample(qs, kns, vns, [flat(bk) for bk, _ in bufs], [flat(bv) for _, bv in bufs],
                    C_GROUPS, nh, t)
    o = o.reshape(nb * t, nh * hd)
    for g in range(ng):
        bk, bv = bufs[g]
        wb = bk.shape[1]
        new_bufs.append((jnp.concatenate([bk, p6[:, :, g, 1]], axis=1)[:, -wb:],
                         jnp.concatenate([bv, p6[:, :, g, 2]], axis=1)[:, -wb:]))
    return _matmul_w32([o], [w_out], F32, tm=min(512, nb * t)), new_bufs


def _ffn_weights(w_up, w_gate, conv_w, w_down):
    d_ff = w_up.shape[1]
    d_ffp = -(-d_ff // FFN_TILE) * FFN_TILE
    kw = conv_w.shape[0]
    cw8 = jnp.pad(conv_w.astype(F32), ((0, SUBLANE - kw), (0, d_ffp - d_ff)))
    return (w_up, w_gate, cw8, jnp.pad(w_down.astype(BF16), ((0, d_ffp - d_ff), (0, 0))), kw, d_ff)


def _conv_ffn(h, nb, t, wts, buf=None):
    wu, wg, cw8, wd, kw, d_ff = wts
    d_ffp = wd.shape[0]
    if buf is None:
        act, tail = _ffn_up(h, wu, wg, cw8, kw, t, d_ffp)
        tiles = tail.shape[0] // nb
        gate_tail = tail.reshape(nb, tiles, SUBLANE, d_ffp)[:, -1, :, :d_ff]
        new_buf = gate_tail[:, SUBLANE - (kw - 1):]
    else:
        bufp = jnp.pad(buf.astype(F32), ((0, 0), (0, 0), (0, d_ffp - d_ff)))
        halos = []
        for shift in range(1, kw):
            hl = jnp.zeros((nb, t, d_ffp), F32)
            hl = hl.at[:, :shift].set(bufp[:, (kw - 1) - shift:])
            halos.append(hl.reshape(nb * t, d_ffp))
        act, gate = _ffn_up(h, wu, wg, cw8, kw, t, d_ffp, halos=halos)
        gate = gate.reshape(nb, t, d_ffp)[:, :, :d_ff]
        new_buf = jnp.concatenate([buf.astype(F32), gate], axis=1)[:, -(kw - 1):]
    out = _matmul([act], [wd], F32, tm=min(1024, nb * t), tn=2048, tk=1024)
    return out, new_buf


def kernel(x_prompt, x_sample, c_prompt, c_sample, state_delta, state_conv_qkv, cache_k, cache_v, cache_logf, cache_win_k0, cache_win_v0, cache_win_k1, cache_win_v1, cache_win_k2, cache_win_v2, state_ffn_conv, page_table, w_mod, b_mod, norm_pre_mix, norm_post_mix, norm_pre_ffn, norm_post_ffn, w_in_e, conv_a, a_log, dt_bias, gnorm_a, b_forget, w_out_e, w_in_o, w_out_o, w_up, w_gate, conv_ffn_w, w_down):
    bsz, seq, d = x_prompt.shape
    dbsz, dseq, _ = x_sample.shape
    depth = w_mod.shape[0]
    hd = gnorm_a.shape[-1]
    ha, hb, hc = a_log.shape[-1], b_forget.shape[-1], cache_win_k0.shape[3]
    assert hd == LANE and seq % DELTA_CHUNK == 0
    kw_a = conv_a.shape[1]
    win_caches = ((cache_win_k0, cache_win_v0), (cache_win_k1, cache_win_v1), (cache_win_k2, cache_win_v2))

    c_all = jnp.concatenate([c_prompt, c_sample], axis=0)
    c_all = jnp.pad(c_all, ((0, MOD_ROWS - c_all.shape[0]), (0, 0)))
    mod = _mod(c_all, w_mod, b_mod)
    mods = [(mod[l, :bsz].reshape(bsz, 6, d), mod[l, bsz:bsz + dbsz].reshape(dbsz, 6, d))
            for l in range(depth)]

    groups = [dict(x=x_prompt.reshape(bsz * seq, d), nb=bsz, t=seq, gi=0),
              dict(x=x_sample.reshape(dbsz * dseq, d), nb=dbsz, t=dseq, gi=1)]
    for gr in groups:
        m_0 = mods[0][gr["gi"]]
        _, gr["h"] = _post_pre(gr["x"], gr["nb"], gr["t"], shift=m_0[:, 0], scale=m_0[:, 1],
                               w_pre=norm_pre_mix[0])

    outs = {k: ([], []) for k in ("delta", "conv", "k", "v", "logf", "ffn")}
    wk = [([], []) for _ in C_GROUPS]
    wv = [([], []) for _ in C_GROUPS]
    for l in range(depth):
        if l % 2 == 0:
            e = l // 2
            ewts = _even_weights(w_in_e[e], conv_a[e], a_log[e], dt_bias[e], b_forget[e], w_out_e[e],
                                 ha, hb, hd)
        else:
            o = l // 2
            w_in_o_b, w_out_o_b = w_in_o[o], w_out_o[o]
        fwts = _ffn_weights(w_up[l], w_gate[l], conv_ffn_w[l], w_down[l])
        for gr in groups:
            gi, nb, t = gr["gi"], gr["nb"], gr["t"]
            m_l = mods[l][gi]
            if l % 2 == 0:
                if gi == 0:
                    conv_buf = jnp.zeros((nb, kw_a - 1, 3 * ha * hd), F32)
                    s0 = jnp.zeros((nb, ha, hd, hd), F32)
                    past = None
                else:
                    conv_buf, s0 = state_conv_qkv[e], state_delta[e]
                    past = (cache_k[e], cache_v[e], cache_logf[e], page_table)
                om, s_new, new_conv, kb, vb, logf = _even_mixer(
                    gr["h"], nb, t, ewts, gnorm_a[e], conv_buf, s0, ha, hb, hd, past)
                for key, val in (("delta", s_new), ("conv", new_conv), ("k", kb), ("v", vb), ("logf", logf)):
                    outs[key][gi].append(val)
            else:
                bufs = None if gi == 0 else tuple((bk[o], bv[o]) for bk, bv in win_caches)
                om, new_bufs = _odd_mixer(gr["h"], nb, t, w_in_o_b, w_out_o_b, hc, hd, bufs)
                for g in range(len(C_GROUPS)):
                    wk[g][gi].append(new_bufs[g][0])
                    wv[g][gi].append(new_bufs[g][1])
            gr["x"], gr["h"] = _post_pre(gr["x"], nb, t, o=om, gate=m_l[:, 2], w_post=norm_post_mix[l],
                                         shift=m_l[:, 3], scale=m_l[:, 4], w_pre=norm_pre_ffn[l])
            of, new_buf = _conv_ffn(gr["h"], nb, t, fwts, None if gi == 0 else state_ffn_conv[l])
            outs["ffn"][gi].append(new_buf)
            if l + 1 < depth:
                m_n = mods[l + 1][gi]
                gr["x"], gr["h"] = _post_pre(gr["x"], nb, t, o=of, gate=m_l[:, 5], w_post=norm_post_ffn[l],
                                             shift=m_n[:, 0], scale=m_n[:, 1], w_pre=norm_pre_mix[l + 1])
            else:
                gr["x"], _ = _post_pre(gr["x"], nb, t, o=of, gate=m_l[:, 5], w_post=norm_post_ffn[l])

    stk = lambda lst: jnp.stack(lst, axis=0)
    res = [groups[0]["x"].reshape(bsz, seq, d), groups[1]["x"].reshape(dbsz, dseq, d)]
    for key in ("delta", "conv", "k", "v", "logf"):
        res += [stk(outs[key][0]), stk(outs[key][1])]
    for g in range(len(C_GROUPS)):
        res += [stk(wk[g][0]), stk(wk[g][1]), stk(wv[g][0]), stk(wv[g][1])]
    res += [stk(outs["ffn"][0]), stk(outs["ffn"][1])]
    return tuple(res)
```

```python
import functools
import math

import jax
import jax.numpy as jnp
from jax import lax
from jax.experimental import pallas as pl
from jax.experimental.pallas import tpu as pltpu

F32 = jnp.float32
BF16 = jnp.bfloat16
HI = lax.Precision.HIGHEST

C_GROUPS = ((128, 1), (512, 4), (2048, 16))
DELTA_CHUNK = 64
RMS_EPS = 1e-6
L2_EPS = 1e-6
NEG = -1e30
LANE = 128
SUBLANE = 8
VMEM_LIMIT = 56 * 1024 * 1024
MOD_ROWS = 16
FFN_TILE = 512


def _cp(*sem):
    return pltpu.CompilerParams(dimension_semantics=sem, vmem_limit_bytes=VMEM_LIMIT)


def _dot(a, b, prec=None):
    return jnp.dot(a, b, preferred_element_type=F32, precision=prec)


def _dot_nt(a, b, prec=None):
    return lax.dot_general(a, b, (((1,), (1,)), ((), ())), preferred_element_type=F32, precision=prec)


def _dot_tn(a, b, prec=None):
    return lax.dot_general(a, b, (((0,), (0,)), ((), ())), preferred_element_type=F32, precision=prec)


def _dot3(a, b):
    ah, bh = a.astype(BF16), b.astype(BF16)
    al = (a - ah.astype(F32)).astype(BF16)
    bl = (b - bh.astype(F32)).astype(BF16)
    return _dot(ah, bh) + (_dot(ah, bl) + _dot(al, bh))


def _sigmoid(x):
    return 1.0 / (1.0 + jnp.exp(-x))


def _silu(x):
    return x * _sigmoid(x)


def _softplus(x):
    return jnp.maximum(x, 0.0) + jnp.log1p(jnp.exp(-jnp.abs(x)))


def _gelu_tanh(x):
    return 0.5 * x * (1.0 + jnp.tanh(0.7978845608028654 * (x + 0.044715 * (x * x * x))))


def _rms(x, w):
    return x * lax.rsqrt(jnp.mean(x * x, axis=-1, keepdims=True) + RMS_EPS) * w


def _iota(shape, dim):
    return lax.broadcasted_iota(jnp.int32, shape, dim)


def _row_to_col(row):
    n = row.shape[1]
    eye = _iota((n, n), 0) == _iota((n, n), 1)
    return jnp.sum(jnp.where(eye, jnp.broadcast_to(row, (n, n)), 0.0), axis=1, keepdims=True)


def _col_to_row(col):
    n = col.shape[0]
    eye = _iota((n, n), 0) == _iota((n, n), 1)
    return jnp.sum(jnp.where(eye, jnp.broadcast_to(col, (n, n)), 0.0), axis=0, keepdims=True)


def _pick_tile(n, pref):
    if n <= pref:
        return n
    t = (pref // LANE) * LANE
    while t >= LANE:
        if n % t == 0:
            return t
        t -= LANE
    return n


def _log2(n):
    assert n > 0 and n & (n - 1) == 0, n
    return n.bit_length() - 1


def _mod_kernel(c_ref, w_ref, b_ref, o_ref):
    a = _silu(c_ref[...]).astype(BF16)
    o_ref[...] = _dot(a, w_ref[...].astype(BF16)) + b_ref[...]


def _mod(c_all, w_mod, b_mod):
    depth, d, n = w_mod.shape
    r = c_all.shape[0]
    tn = _pick_tile(n, 512)
    return pl.pallas_call(
        _mod_kernel,
        grid=(depth, n // tn),
        in_specs=[pl.BlockSpec((r, d), lambda l, j: (0, 0)),
                  pl.BlockSpec((None, d, tn), lambda l, j: (l, 0, j)),
                  pl.BlockSpec((None, 1, tn), lambda l, j: (l, 0, j))],
        out_specs=pl.BlockSpec((None, r, tn), lambda l, j: (l, 0, j)),
        out_shape=jax.ShapeDtypeStruct((depth, r, n), F32),
        compiler_params=_cp("arbitrary", "arbitrary"),
        name="mod",
    )(c_all, w_mod, b_mod.reshape(depth, 1, n))


def _post_pre_kernel(*refs, has_o, has_h):
    refs = list(refs)
    x = refs.pop(0)[...]
    if has_o:
        o_ref, gate_ref, wpost_ref = refs.pop(0), refs.pop(0), refs.pop(0)
    if has_h:
        shift_ref, scale_ref, wpre_ref = refs.pop(0), refs.pop(0), refs.pop(0)
    if has_o:
        x = x + gate_ref[...] * _rms(o_ref[...], wpost_ref[...])
        refs.pop(0)[...] = x
    if has_h:
        h = _rms(x, wpre_ref[...]) * (1.0 + scale_ref[...]) + shift_ref[...]
        refs.pop(0)[...] = h.astype(BF16)


def _post_pre(x, nb, t, o=None, gate=None, w_post=None, shift=None, scale=None, w_pre=None):
    m, d = x.shape
    has_o, has_h = o is not None, shift is not None
    if t >= 128:
        tr = 128
        per = t // tr
        vec = lambda v: v.reshape(nb, 1, d)
        vec_spec = pl.BlockSpec((None, 1, d), lambda i: (i // per, 0, 0))
    else:
        tr = m
        vec = lambda v: jnp.repeat(v, t, axis=0).reshape(1, m, d)
        vec_spec = pl.BlockSpec((None, m, d), lambda i: (0, 0, 0))
    row_spec = pl.BlockSpec((tr, d), lambda i: (i, 0))
    w_spec = pl.BlockSpec((1, d), lambda i: (0, 0))
    args, in_specs, out_specs, out_shape = [x], [row_spec], [], []
    if has_o:
        args += [o, vec(gate), w_post.reshape(1, d)]
        in_specs += [row_spec, vec_spec, w_spec]
        out_specs.append(row_spec)
        out_shape.append(jax.ShapeDtypeStruct((m, d), F32))
    if has_h:
        args += [vec(shift), vec(scale), w_pre.reshape(1, d)]
        in_specs += [vec_spec, vec_spec, w_spec]
        out_specs.append(row_spec)
        out_shape.append(jax.ShapeDtypeStruct((m, d), BF16))
    outs = pl.pallas_call(
        functools.partial(_post_pre_kernel, has_o=has_o, has_h=has_h),
        grid=(m // tr,), in_specs=in_specs, out_specs=out_specs, out_shape=out_shape,
        compiler_params=_cp("arbitrary"), name="post_pre",
    )(*args)
    outs = list(outs)
    x_new = outs.pop(0) if has_o else x
    h = outs.pop(0) if has_h else None
    return x_new, h


def _mm_kernel(*refs, n_pairs, nk):
    a_refs, b_refs = refs[:n_pairs], refs[n_pairs:2 * n_pairs]
    o_ref = refs[2 * n_pairs]
    part = _dot(a_refs[0][...], b_refs[0][...])
    for a_ref, b_ref in zip(a_refs[1:], b_refs[1:]):
        part = part + _dot(a_ref[...], b_ref[...])
    if nk == 1:
        o_ref[...] = part.astype(o_ref.dtype)
        return
    acc_ref = refs[2 * n_pairs + 1]
    k = pl.program_id(2)

    @pl.when(k == 0)
    def _():
        acc_ref[...] = part

    @pl.when(k > 0)
    def _():
        acc_ref[...] += part

    @pl.when(k == nk - 1)
    def _():
        o_ref[...] = acc_ref[...].astype(o_ref.dtype)


def _matmul(a_list, b_list, out_dtype, tm=1024, tn=1024, tk=None, cols=None, layer=None):
    m, kdim = a_list[0].shape
    c0, n = (0, b_list[0].shape[-1]) if cols is None else cols
    tm = min(tm, m)
    tn = _pick_tile(n, tn)
    tk = kdim if tk is None else _pick_tile(kdim, tk)
    nk = kdim // tk
    assert m % tm == 0 and n % tn == 0 and kdim % tk == 0 and c0 % tn == 0
    jb = c0 // tn
    n_pairs = len(a_list)
    if layer is None:
        b_spec = pl.BlockSpec((tk, tn), lambda i, j, k: (k, j + jb))
    else:
        b_spec = pl.BlockSpec((None, tk, tn), lambda i, j, k: (layer, k, j + jb))
    in_specs = [pl.BlockSpec((tm, tk), lambda i, j, k: (i, k))] * n_pairs + [b_spec] * n_pairs
    scratch = [pltpu.VMEM((tm, tn), F32)] if nk > 1 else []
    return pl.pallas_call(
        functools.partial(_mm_kernel, n_pairs=n_pairs, nk=nk),
        grid=(m // tm, n // tn, nk), in_specs=in_specs,
        out_specs=pl.BlockSpec((tm, tn), lambda i, j, k: (i, j)),
        out_shape=jax.ShapeDtypeStruct((m, n), out_dtype),
        scratch_shapes=scratch,
        compiler_params=_cp("arbitrary", "arbitrary", "arbitrary"), name="matmul",
    )(*a_list, *b_list)


def _mmw_kernel(*refs, n_pairs, ranges):
    n_out = len(ranges)
    a_refs, b_refs = refs[:n_pairs], refs[n_pairs:2 * n_pairs]
    o_refs = refs[2 * n_pairs:2 * n_pairs + n_out]
    wb_refs = refs[2 * n_pairs + n_out:]
    j, i = pl.program_id(0), pl.program_id(1)

    @pl.when(i == 0)
    def _():
        for b_ref, wb_ref in zip(b_refs, wb_refs):
            wb_ref[...] = b_ref[...].astype(BF16)

    acc = _dot(a_refs[0][...], wb_refs[0][...])
    for a_ref, wb_ref in zip(a_refs[1:], wb_refs[1:]):
        acc = acc + _dot(a_ref[...], wb_ref[...])
    if n_out == 1:
        o_refs[0][...] = acc.astype(o_refs[0].dtype)
        return
    for o_ref, (lo, hi) in zip(o_refs, ranges):
        @pl.when((j >= lo) & (j < hi))
        def _(o_ref=o_ref):
            o_ref[...] = acc.astype(o_ref.dtype)


def _matmul_w32(a_list, w_list, out_dtype, out_cols=None, col0=0, tm=512, tn=1024, w_buffers=2):
    m, kdim = a_list[0].shape
    tm = min(tm, m)
    out_cols = [w_list[0].shape[1]] if out_cols is None else out_cols
    n = sum(out_cols)
    tn = _pick_tile(math.gcd(col0, *out_cols), tn)
    assert m % tm == 0
    nm = m // tm
    jb = col0 // tn
    row_blk = [p if w.shape[0] != kdim else 0 for p, w in enumerate(w_list)]
    ranges, lo = [], 0
    for c in out_cols:
        ranges.append((lo, lo + c // tn))
        lo += c // tn
    n_pairs = len(a_list)

    def out_map(lo, hi):
        def index(j, i):
            inside = (j >= lo) & (j < hi)
            ii = jnp.where(inside, i, jnp.where(j < lo, 0, nm - 1))
            return ii, jnp.clip(j - lo, 0, hi - lo - 1)
        return index

    outs = pl.pallas_call(
        functools.partial(_mmw_kernel, n_pairs=n_pairs, ranges=ranges),
        grid=(n // tn, nm),
        in_specs=([pl.BlockSpec((tm, kdim), lambda j, i: (i, 0))] * n_pairs
                  + [pl.BlockSpec((kdim, tn), lambda j, i, rb=rb: (rb, j + jb),
                                  pipeline_mode=pl.Buffered(w_buffers)) for rb in row_blk]),
        out_specs=[pl.BlockSpec((tm, tn), out_map(lo, hi)) for lo, hi in ranges],
        out_shape=[jax.ShapeDtypeStruct((m, c), out_dtype) for c in out_cols],
        scratch_shapes=[pltpu.VMEM((kdim, tn), BF16)] * n_pairs,
        compiler_params=_cp("arbitrary", "arbitrary"), name="matmul_w32",
    )(*a_list, *w_list)
    return outs[0] if len(out_cols) == 1 else outs


def _gates_kernel(h_ref, w_ref, par_ref, g_ref, f_ref, carry_ref, *, ha, hb, seg):
    @pl.when(pl.program_id(1) == 0)
    def _():
        carry_ref[...] = jnp.zeros_like(carry_ref)

    p = _dot(h_ref[...], w_ref[...])
    tr = p.shape[0]
    lane = _iota(p.shape, 1)
    a_log, bias = par_ref[0:1, :], par_ref[1:2, :]
    x = p + bias
    beta = _sigmoid(p)
    g = -jnp.exp(a_log) * _softplus(x)
    logf = -_softplus(-x)
    out = jnp.where(lane < ha, beta,
                    jnp.where(lane < 2 * ha, g, jnp.where(lane < 2 * ha + hb, logf, 0.0)))
    g_ref[...] = out
    rr, cc = _iota((tr, tr), 0), _iota((tr, tr), 1)
    tri = rr >= cc
    if seg is not None:
        tri = tri & ((rr >> _log2(seg)) == (cc >> _log2(seg)))
    cs = _dot(tri.astype(F32), out, HI) + carry_ref[0:1, :]
    f_ref[...] = cs
    carry_ref[0:1, :] = cs[tr - 1:tr, :]


def _gates(h, w_small, par, nb, t, ha, hb):
    m, d = h.shape
    if t >= 128:
        tr, per, seg, nb_grid = min(t, 512), t // min(t, 512), None, nb
    else:
        tr, per, seg, nb_grid = m, 1, t, 1
    spec = pl.BlockSpec((tr, LANE), lambda b, j: (b * per + j, 0))
    return pl.pallas_call(
        functools.partial(_gates_kernel, ha=ha, hb=hb, seg=seg),
        grid=(nb_grid, per),
        in_specs=[pl.BlockSpec((tr, d), lambda b, j: (b * per + j, 0)),
                  pl.BlockSpec((d, LANE), lambda b, j: (0, 0)),
                  pl.BlockSpec((SUBLANE, LANE), lambda b, j: (0, 0))],
        out_specs=[spec, spec],
        out_shape=[jax.ShapeDtypeStruct((m, LANE), F32)] * 2,
        scratch_shapes=[pltpu.VMEM((SUBLANE, LANE), F32)],
        compiler_params=_cp("arbitrary", "arbitrary"), name="gates",
    )(h, w_small, par)


def _each(f, *lists):
    return [f(*args) for args in zip(*lists)]


def _unit_lower_inverse(lows):
    c = lows[0].shape[0]
    assert c == 64
    r, s = _iota((c, c), 0), _iota((c, c), 1)
    eye = (r == s).astype(F32)
    same16 = (r >> 4) == (s >> 4)
    same32 = (r >> 5) == (s >> 5)
    ld = _each(lambda low: jnp.where(same16, low, 0.0), lows)
    x = _each(lambda a: eye - a, ld)
    p = _each(lambda a: _dot3(a, a), ld)
    for _ in range(2):
        xp = _each(lambda a, b: _dot3(jnp.concatenate([a, b], axis=0), b), x, p)
        x = _each(lambda a, b: a + b[:c], x, xp)
        p = _each(lambda b: b[c:], xp)
    x = _each(lambda a, b: a + _dot3(a, b), x, p)
    for keep in (same32 & jnp.logical_not(same16), jnp.logical_not(same32)):
        off = _each(lambda low: jnp.where(keep, low, 0.0), lows)
        y = _each(_dot3, x, off)
        x = _each(lambda a, b: a - _dot3(b, a), x, y)
    return x


def _delta_kernel(q_ref, k_ref, v_ref, z_ref, gt_ref, cwq_ref, cwk_ref, cwv_ref,
                  cbq_ref, cbk_ref, cbv_ref, s0_ref, gn_ref, o_ref, sfin_ref, ext_ref, s_ref,
                  *, hpb, ha, kw, valid_rows, nc):
    c_idx = pl.program_id(2)
    hblk = pl.program_id(1)
    c = q_ref.shape[0]
    hd = LANE

    @pl.when(c_idx == 0)
    def _():
        s_ref[...] = s0_ref[...]
        for i, cb in enumerate((cbq_ref, cbk_ref, cbv_ref)):
            ext_ref[i, 0:SUBLANE, :] = cb[...]

    conv = []
    for i, (r_ref, cw_ref) in enumerate(((q_ref, cwq_ref), (k_ref, cwk_ref), (v_ref, cwv_ref))):
        ext_ref[i, SUBLANE:SUBLANE + c, :] = r_ref[...]
        acc = ext_ref[i, SUBLANE:SUBLANE + c, :] * cw_ref[kw - 1:kw, :]
        for j in range(kw - 1):
            acc = acc + ext_ref[i, pl.ds(SUBLANE - (kw - 1) + j, c), :] * cw_ref[j:j + 1, :]
        ext_ref[i, 0:SUBLANE, :] = ext_ref[i, c:c + SUBLANE, :]
        acc = _silu(acc)
        if valid_rows < c:
            acc = jnp.where(_iota(acc.shape, 0) < valid_rows, acc, 0.0)
        conv.append(acc)
    xq, xk, xv = conv

    gt = gt_ref[...]
    rr, ss = _iota((c, c), 0), _iota((c, c), 1)
    incl, strict = rr >= ss, rr > ss
    gcum = _dot(incl.astype(F32), gt, HI)
    lane = _iota(gt.shape, 1)
    gn = gn_ref[...]
    heads = list(range(hpb))
    sls = [slice(j * hd, (j + 1) * hd) for j in heads]
    beta = [jnp.sum(jnp.where(lane == hblk * hpb + j, gt, 0.0), axis=1, keepdims=True) for j in heads]
    gc = [jnp.sum(jnp.where(lane == ha + hblk * hpb + j, gcum, 0.0), axis=1, keepdims=True) for j in heads]
    qh = [xq[:, sl] for sl in sls]
    qh = _each(lambda a: a * lax.rsqrt(jnp.sum(a * a, axis=-1, keepdims=True) + L2_EPS) * (hd ** -0.5), qh)
    kh = [xk[:, sl] for sl in sls]
    kh = _each(lambda a: a * lax.rsqrt(jnp.sum(a * a, axis=-1, keepdims=True) + L2_EPS), kh)
    decay = _each(lambda g: jnp.where(incl, jnp.exp(g - _col_to_row(g)), 0.0), gc)
    kb = _each(lambda a, b: a * b, kh, beta)
    vb = [xv[:, sl] * b for sl, b in zip(sls, beta)]
    egc = _each(jnp.exp, gc)
    kk = _each(lambda a, q, k: _dot_nt(jnp.concatenate([a, q], axis=0).astype(BF16), k.astype(BF16)),
               kb, qh, kh)
    low = _each(lambda a, d: jnp.where(strict, a[:c] * d, 0.0), kk, decay)
    attn = _each(lambda a, d: (a[c:] * d).astype(BF16), kk, decay)
    tmat = _unit_lower_inverse(low)
    uw = _each(lambda t, v, k, e: _dot(t.astype(BF16), jnp.concatenate([v, k * e], axis=1).astype(BF16)),
               tmat, vb, kb, egc)
    st = [s_ref[j] for j in heads]
    ws = _each(lambda a, q, e, s: _dot(jnp.concatenate([a[:, hd:], q * e], axis=0).astype(BF16),
                                       s.astype(BF16)), uw, qh, egc, st)
    v_new = _each(lambda a, b: (a[:, :hd] - b[:c]).astype(BF16), uw, ws)
    o = _each(lambda b, a, v: b[c:] + _dot(a, v), ws, attn, v_new)
    g_last = _each(lambda g: g[c - 1:c, :], gc)
    k_dec = _each(lambda k, gl, g: (k * jnp.exp(gl - g)).astype(BF16), kh, g_last, gc)
    s_new = _each(lambda s, gl, k, v: s * jnp.exp(gl) + _dot_tn(k, v), st, g_last, k_dec, v_new)
    for j in heads:
        s_ref[j] = s_new[j]
        o_ref[:, sls[j]] = (_rms(o[j], gn) * _silu(z_ref[:, sls[j]])).astype(o_ref.dtype)

    @pl.when(c_idx == nc - 1)
    def _():
        sfin_ref[...] = s_ref[...]


def _delta(proj, gates, conv_w, conv_buf8, s0, gnorm, nb, tp, valid_rows, ha, hpb=8):
    c = DELTA_CHUNK
    hd = LANE
    hpb = min(hpb, ha)
    w = hpb * hd
    nc = tp // c
    nhb = ha // hpb
    kw = conv_w.shape[0]
    cw8 = jnp.pad(conv_w, ((0, SUBLANE - kw), (0, 0)))

    def col(off):
        return pl.BlockSpec((c, w), lambda b, h, i, off=off: (b * nc + i, off * nhb + h))

    def cwspec(off):
        return pl.BlockSpec((SUBLANE, w), lambda b, h, i, off=off: (0, off * nhb + h))

    def cbspec(off):
        return pl.BlockSpec((None, SUBLANE, w), lambda b, h, i, off=off: (b, 0, off * nhb + h))

    state_spec = pl.BlockSpec((None, hpb, hd, hd), lambda b, h, i: (b, h, 0, 0))
    return pl.pallas_call(
        functools.partial(_delta_kernel, hpb=hpb, ha=ha, kw=kw, valid_rows=valid_rows, nc=nc),
        grid=(nb, nhb, nc),
        in_specs=[col(0), col(1), col(2), col(3),
                  pl.BlockSpec((c, LANE), lambda b, h, i: (b * nc + i, 0)),
                  cwspec(0), cwspec(1), cwspec(2), cbspec(0), cbspec(1), cbspec(2),
                  state_spec, pl.BlockSpec((1, hd), lambda b, h, i: (0, 0))],
        out_specs=[pl.BlockSpec((c, w), lambda b, h, i: (b * nc + i, h)), state_spec],
        out_shape=[jax.ShapeDtypeStruct((nb * tp, ha * hd), BF16),
                   jax.ShapeDtypeStruct(s0.shape, F32)],
        scratch_shapes=[pltpu.VMEM((3, c + SUBLANE, w), F32), pltpu.VMEM((hpb, hd, hd), F32)],
        compiler_params=_cp("arbitrary", "arbitrary", "arbitrary"), name="delta",
    )(proj, proj, proj, proj, gates, cw8, cw8, cw8, conv_buf8, conv_buf8, conv_buf8, s0,
      gnorm.reshape(1, hd))


def _softmax_step(carry, s, v, valid=None):
    m, l, acc = carry
    m_new = jnp.maximum(m, jnp.max(s, axis=-1, keepdims=True))
    alpha = jnp.exp(m - m_new)
    p = jnp.exp(s - m_new)
    if valid is not None:
        p = jnp.where(valid, p, 0.0)
    l = alpha * l + jnp.sum(p, axis=-1, keepdims=True)
    acc = alpha * acc + _dot(p.astype(BF16), v)
    return m_new, l, acc


def _softmax_init(rows, hd):
    return (jnp.full((rows, 1), NEG, F32), jnp.zeros((rows, 1), F32), jnp.zeros((rows, hd), F32))


def _fox_prompt_kernel(q_ref, k_ref, v_ref, f_ref, o_ref, kb_ref, vb_ref, *, tq, tk, scale, par):
    seq = q_ref.shape[0]
    kb_ref[...] = k_ref[...].astype(BF16)
    vb_ref[...] = v_ref[...].astype(BF16)
    col_minus_row = _iota((tq, tk), 1) - _iota((tq, tk), 0)
    nqb = seq // tq
    for c0 in range(0, nqb, par):
        chunk = list(range(c0, min(c0 + par, nqb)))
        qs = [(q_ref[pl.ds(qb * tq, tq), :] * scale).astype(BF16) for qb in chunk]
        fqs = [_row_to_col(f_ref[:, pl.ds(qb * tq, tq)]) for qb in chunk]
        carries = [_softmax_init(tq, LANE) for _ in chunk]
        n_steps = [-(-((qb + 1) * tq) // tk) for qb in chunk]
        for step in range(max(n_steps)):
            live = [i for i in range(len(chunk)) if step < n_steps[i]]
            k = kb_ref[pl.ds(step * tk, tk), :]
            v = vb_ref[pl.ds(step * tk, tk), :]
            fk = f_ref[:, pl.ds(step * tk, tk)]
            ss = [_dot_nt(qs[i], k) + (fqs[i] - fk) for i in live]
            for n, i in enumerate(live):
                if step * tk + tk - 1 > chunk[i] * tq:
                    ss[n] = jnp.where(col_minus_row <= chunk[i] * tq - step * tk, ss[n], NEG)
            ms = [jnp.maximum(carries[i][0], jnp.max(s, axis=-1, keepdims=True)) for i, s in zip(live, ss)]
            ps = [jnp.exp(s - m) for s, m in zip(ss, ms)]
            alphas = [jnp.exp(carries[i][0] - m) for i, m in zip(live, ms)]
            pvs = [_dot(p.astype(BF16), v) for p in ps]
            for i, m, p, a, pv in zip(live, ms, ps, alphas, pvs):
                carries[i] = (m, a * carries[i][1] + jnp.sum(p, axis=-1, keepdims=True),
                              a * carries[i][2] + pv)
        for qb, (_, l, acc) in zip(chunk, carries):
            o_ref[pl.ds(qb * tq, tq), :] = (acc / l).astype(o_ref.dtype)


def _fox_prompt(q_arr, q_col, k_arr, v_arr, f_rows, nb, s, nh, tq=128, tk=512, par=4):
    tq, tk = min(tq, s), min(tk, s)
    assert s % tq == 0 and s % tk == 0
    hd = LANE
    return pl.pallas_call(
        functools.partial(_fox_prompt_kernel, tq=tq, tk=tk, scale=hd ** -0.5, par=par),
        grid=(nb, nh),
        in_specs=[pl.BlockSpec((s, hd), lambda b, h: (b, q_col + h)),
                  pl.BlockSpec((s, hd), lambda b, h: (b, h)),
                  pl.BlockSpec((s, hd), lambda b, h: (b, h)),
                  pl.BlockSpec((None, None, 1, s), lambda b, h: (b, h, 0, 0))],
        out_specs=pl.BlockSpec((s, hd), lambda b, h: (b, h)),
        out_shape=jax.ShapeDtypeStruct((nb * s, nh * hd), BF16),
        scratch_shapes=[pltpu.VMEM((s, hd), BF16), pltpu.VMEM((s, hd), BF16)],
        compiler_params=_cp("arbitrary", "arbitrary"), name="fox_prompt",
    )(q_arr, k_arr, v_arr, f_rows)


def _fpast_kernel(pt_ref, *refs, group):
    lp_refs, o_ref, carry_ref = refs[:group], refs[group], refs[group + 1]

    @pl.when(pl.program_id(1) == 0)
    def _():
        carry_ref[...] = jnp.zeros_like(carry_ref)

    p = lp_refs[0].shape[0]
    upper = (_iota((p, p), 1) > _iota((p, p), 0)).astype(F32)
    for i in range(group):
        lp = lp_refs[i][...]
        o_ref[group - 1 - i] = -(_dot(upper, lp, HI) + carry_ref[0:1, :])
        carry_ref[0:1, :] = carry_ref[0:1, :] + jnp.sum(lp, axis=0, keepdims=True)


def _fpast(logf_pool, page_table):
    db, n_pages = page_table.shape
    _, p, nh = logf_pool.shape
    group = 8 if n_pages % 8 == 0 else 1
    nblk = n_pages // group

    def lp_spec(i):
        return pl.BlockSpec((None, p, nh),
                            lambda b, j, pt, i=i: (pt[b, n_pages - 1 - (j * group + i)], 0, 0))

    return pl.pallas_call(
        functools.partial(_fpast_kernel, group=group),
        grid_spec=pltpu.PrefetchScalarGridSpec(
            num_scalar_prefetch=1, grid=(db, nblk),
            in_specs=[lp_spec(i) for i in range(group)],
            out_specs=pl.BlockSpec((None, group, p, nh), lambda b, j, pt: (b, nblk - 1 - j, 0, 0)),
            scratch_shapes=[pltpu.VMEM((SUBLANE, nh), F32)]),
        out_shape=jax.ShapeDtypeStruct((db, n_pages, p, nh), F32),
        compiler_params=_cp("arbitrary", "arbitrary"), name="fpast",
    )(page_table, *([logf_pool] * group))


def _fox_sample_kernel(pt_ref, q_ref, k_ref, v_ref, fp_ref, kn_ref, vn_ref, fn_ref, o_ref,
                       m_ref, l_ref, acc_ref, bias_ref, *, nh, scale, n_pages):
    p = pl.program_id(1)
    r = q_ref.shape[0]
    rows = k_ref.shape[0]
    lognh = _log2(nh)
    fn = fn_ref[...]

    @pl.when(p == 0)
    def _():
        m_ref[...] = jnp.full_like(m_ref, NEG)
        l_ref[...] = jnp.zeros_like(l_ref)
        acc_ref[...] = jnp.zeros_like(acc_ref)
        match = (_iota((r, rows), 0) & (nh - 1)) == (_iota((r, rows), 1) & (nh - 1))
        bias_ref[...] = jnp.where(match, _row_to_col(fn), NEG)

    q = (q_ref[...] * scale).astype(BF16)

    def update(s, v):
        m, l, acc = _softmax_step((m_ref[...], l_ref[...], acc_ref[...]), s, v)
        m_ref[...], l_ref[...], acc_ref[...] = m, l, acc

    update(_dot_nt(q, k_ref[...].astype(BF16)) + (bias_ref[...] - fp_ref[...]), v_ref[...].astype(BF16))

    @pl.when(p == n_pages - 1)
    def _():
        rr, cc = _iota((r, r), 0), _iota((r, r), 1)
        ok = ((rr & (nh - 1)) == (cc & (nh - 1))) & ((cc >> lognh) <= (rr >> lognh))
        s2 = _dot_nt(q, kn_ref[...].astype(BF16)) + _row_to_col(fn) - fn
        update(jnp.where(ok, s2, NEG), vn_ref[...].astype(BF16))
        o_ref[...] = (acc_ref[...] / l_ref[...]).astype(o_ref.dtype)


def _fox_sample(q, k_new, v_new, f_new, k_pool, v_pool, f_past, page_table, nh):
    db, r, hd = q.shape
    n_pages = page_table.shape[1]
    rows = k_pool.shape[1]
    tok = pl.BlockSpec((None, r, hd), lambda b, p, pt: (b, 0, 0))
    page = pl.BlockSpec((None, rows, hd), lambda b, p, pt: (pt[b, p], 0, 0))
    return pl.pallas_call(
        functools.partial(_fox_sample_kernel, nh=nh, scale=hd ** -0.5, n_pages=n_pages),
        grid_spec=pltpu.PrefetchScalarGridSpec(
            num_scalar_prefetch=1, grid=(db, n_pages),
            in_specs=[tok, page, page,
                      pl.BlockSpec((None, None, 1, rows), lambda b, p, pt: (b, p, 0, 0)),
                      tok, tok, pl.BlockSpec((None, 1, r), lambda b, p, pt: (b, 0, 0))],
            out_specs=tok,
            scratch_shapes=[pltpu.VMEM((r, 1), F32), pltpu.VMEM((r, 1), F32),
                            pltpu.VMEM((r, hd), F32), pltpu.VMEM((r, rows), F32)]),
        out_shape=jax.ShapeDtypeStruct((db, r, hd), BF16),
        compiler_params=_cp("arbitrary", "arbitrary"), name="fox_sample",
    )(page_table, q, k_pool, v_pool, f_past, k_new, v_new, f_new)


def _dil_prompt_kernel(*refs, groups, tb, scale, par):
    ng = len(groups)
    q_refs, k_refs, v_refs = refs[:ng], refs[ng:2 * ng], refs[2 * ng:3 * ng]
    o_ref, m_ref, l_ref, acc_ref = refs[3 * ng:]
    seq = q_refs[0].shape[0]
    diff = _iota((tb, tb), 0) - _iota((tb, tb), 1)

    def rows_of(res, blk, dil):
        return pl.ds(res + blk * tb * dil, tb, stride=dil) if dil > 1 else pl.ds(blk * tb, tb)

    for g, (window, dil) in enumerate(groups):
        reach = window // dil
        nqb = seq // dil // tb
        back = -(-reach // tb)
        units = [(res, qb) for res in range(dil) for qb in range(nqb)]
        for c0 in range(0, len(units), par):
            chunk = units[c0:c0 + par]
            rows = [rows_of(res, qb, dil) for res, qb in chunk]
            qs = [(q_refs[g][rw, :] * scale).astype(BF16) for rw in rows]
            if g == 0:
                carries = [_softmax_init(tb, LANE) for _ in chunk]
            else:
                carries = [(m_ref[rw, :], l_ref[rw, :], acc_ref[rw, :]) for rw in rows]
            for step in range(back + 1):
                live = [i for i, (_, qb) in enumerate(chunk) if qb - step >= 0]
                if not live:
                    continue
                krows = [rows_of(chunk[i][0], chunk[i][1] - step, dil) for i in live]
                ks = [k_refs[g][rw, :].astype(BF16) for rw in krows]
                vs = [v_refs[g][rw, :].astype(BF16) for rw in krows]
                ss = [_dot_nt(qs[i], k) for i, k in zip(live, ks)]
                if step * tb - (tb - 1) < 0 or step * tb + (tb - 1) > reach:
                    dist = diff + step * tb
                    valid = (dist >= 0) & (dist <= reach)
                    ss = [jnp.where(valid, s, NEG) for s in ss]
                ms = [jnp.maximum(carries[i][0], jnp.max(s, axis=-1, keepdims=True)) for i, s in zip(live, ss)]
                ps = [jnp.exp(s - m) for s, m in zip(ss, ms)]
                alphas = [jnp.exp(carries[i][0] - m) for i, m in zip(live, ms)]
                pvs = [_dot(p.astype(BF16), v) for p, v in zip(ps, vs)]
                for i, m, p, a, pv in zip(live, ms, ps, alphas, pvs):
                    carries[i] = (m, a * carries[i][1] + jnp.sum(p, axis=-1, keepdims=True),
                                  a * carries[i][2] + pv)
            for rw, (m, l, acc) in zip(rows, carries):
                m_ref[rw, :], l_ref[rw, :], acc_ref[rw, :] = m, l, acc
    o_ref[...] = (acc_ref[...] / l_ref[...]).astype(o_ref.dtype)


def _dil_prompt(srcs, nb, s, nh, groups, tb=128, par=4):
    hd = LANE
    ng = len(groups)
    tb = min(tb, s)
    for window, dil in groups:
        assert s % (dil * tb) == 0 and window % dil == 0
    args, specs = [], []
    for which in range(3):
        for g in range(ng):
            arr, col = srcs[g][which]
            args.append(arr)
            specs.append(pl.BlockSpec((s, hd), lambda b, h, col=col: (b, col + h)))
    return pl.pallas_call(
        functools.partial(_dil_prompt_kernel, groups=groups, tb=tb, scale=hd ** -0.5, par=par),
        grid=(nb, nh),
        in_specs=specs,
        out_specs=pl.BlockSpec((s, hd), lambda b, h: (b, h)),
        out_shape=jax.ShapeDtypeStruct((nb * s, nh * hd), BF16),
        scratch_shapes=[pltpu.VMEM((s, 1), F32), pltpu.VMEM((s, 1), F32), pltpu.VMEM((s, hd), F32)],
        compiler_params=_cp("arbitrary", "arbitrary"), name="dil_prompt",
    )(*args)


def _dil_sample_kernel(*refs, groups, wbs, tb, nh, scale, t):
    ng = len(groups)
    q_refs, kb_refs, vb_refs = refs[:ng], refs[ng:2 * ng], refs[2 * ng:3 * ng]
    kn_refs, vn_refs = refs[3 * ng:4 * ng], refs[4 * ng:5 * ng]
    o_ref, m_ref, l_ref, acc_ref = refs[5 * ng:5 * ng + 4]
    j = pl.program_id(1)
    r = q_refs[0].shape[0]
    rows = tb * nh
    lognh = _log2(nh)
    nblks = [wb // tb for wb in wbs]
    total = sum(nblks)

    @pl.when(j == 0)
    def _():
        m_ref[...] = jnp.full_like(m_ref, NEG)
        l_ref[...] = jnp.zeros_like(l_ref)
        acc_ref[...] = jnp.zeros_like(acc_ref)

    def update(s, v, valid):
        m, l, acc = _softmax_step((m_ref[...], l_ref[...], acc_ref[...]),
                                  jnp.where(valid, s, NEG), v, valid)
        m_ref[...], l_ref[...], acc_ref[...] = m, l, acc

    start = 0
    for g, (window, dil) in enumerate(groups):
        def block(g=g, window=window, dil=dil, start=start):
            rr, cc = _iota((r, rows), 0), _iota((r, rows), 1)
            key_tok = (j - start) * tb + (cc >> lognh)
            delta = wbs[g] + (rr >> lognh) - key_tok
            valid = (((rr & (nh - 1)) == (cc & (nh - 1))) & ((delta & (dil - 1)) == 0)
                     & (delta <= window))
            q = (q_refs[g][...] * scale).astype(BF16)
            update(_dot_nt(q, kb_refs[g][...].astype(BF16)), vb_refs[g][...].astype(BF16), valid)

        pl.when((j >= start) & (j < start + nblks[g]))(block)
        start += nblks[g]

    @pl.when(j == total - 1)
    def _():
        rr, cc = _iota((r, r), 0), _iota((r, r), 1)
        delta = (rr >> lognh) - (cc >> lognh)
        match = (rr & (nh - 1)) == (cc & (nh - 1))
        for g, (window, dil) in enumerate(groups):
            valid = match & (delta >= 0) & ((delta & (dil - 1)) == 0) & (delta <= window)
            q = (q_refs[g][...] * scale).astype(BF16)
            update(_dot_nt(q, kn_refs[g][...].astype(BF16)), vn_refs[g][...].astype(BF16), valid)
        o_ref[...] = (acc_ref[...] / l_ref[...]).astype(o_ref.dtype)


def _dil_sample(qs, kns, vns, kbufs, vbufs, groups, nh, t):
    db, r, hd = qs[0].shape
    ng = len(groups)
    wbs = [kb.shape[1] // nh for kb in kbufs]
    tb = 128
    while any(wb % tb for wb in wbs):
        tb //= 2
    nblks = [wb // tb for wb in wbs]
    starts = [sum(nblks[:g]) for g in range(ng)]
    total = sum(nblks)
    tok = pl.BlockSpec((None, r, hd), lambda b, j: (b, 0, 0))

    def buf_spec(g):
        return pl.BlockSpec(
            (None, tb * nh, hd),
            lambda b, j, g=g: (b, jnp.clip(j - starts[g], 0, nblks[g] - 1), 0))

    bspecs = [buf_spec(g) for g in range(ng)]
    return pl.pallas_call(
        functools.partial(_dil_sample_kernel, groups=groups, wbs=wbs, tb=tb, nh=nh,
                          scale=hd ** -0.5, t=t),
        grid=(db, total),
        in_specs=[tok] * ng + bspecs + bspecs + [tok] * (2 * ng),
        out_specs=tok,
        out_shape=jax.ShapeDtypeStruct((db, r, hd), BF16),
        scratch_shapes=[pltpu.VMEM((r, 1), F32), pltpu.VMEM((r, 1), F32), pltpu.VMEM((r, hd), F32)],
        compiler_params=_cp("arbitrary", "arbitrary"), name="dil_sample",
    )(*qs, *kbufs, *vbufs, *kns, *vns)


def _ffn_up_kernel(*refs, carry_rows, kw, tiles_per_batch, t, n_valid):
    h_ref, wu_ref, wg_ref, cw_ref = refs[:4]
    nh = kw - 1
    if carry_rows:
        act_ref, tail_ref, ext_ref, wub_ref, wgb_ref = refs[4:]
    else:
        halo_refs = refs[4:4 + nh]
        act_ref, tail_ref, ext_ref, wub_ref, wgb_ref = refs[4 + nh:]
    n, m = pl.program_id(0), pl.program_id(1)

    @pl.when(m == 0)
    def _():
        wub_ref[...] = wu_ref[...].astype(BF16)
        wgb_ref[...] = wg_ref[...].astype(BF16)

    h = h_ref[...]
    tm = h.shape[0]
    up = _dot(h, wub_ref[...])
    gate = _dot(h, wgb_ref[...])
    tn = gate.shape[1]
    if carry_rows:
        @pl.when((m % tiles_per_batch) == 0)
        def _():
            ext_ref[0:SUBLANE, :] = jnp.zeros((SUBLANE, tn), F32)

        @pl.when((m % tiles_per_batch) != 0)
        def _():
            ext_ref[0:SUBLANE, :] = ext_ref[tm:tm + SUBLANE, :]

        ext_ref[SUBLANE:SUBLANE + tm, :] = gate
        tail_ref[...] = gate[tm - SUBLANE:, :]
    else:
        ext_ref[0:SUBLANE, :] = jnp.zeros((SUBLANE, tn), F32)
        ext_ref[SUBLANE:SUBLANE + tm, :] = gate
        tail_ref[...] = gate
        pos = _iota(gate.shape, 0) & (t - 1)
    gc = gate * cw_ref[kw - 1:kw, :]
    for j in range(kw - 1):
        shift = kw - 1 - j
        sh = ext_ref[pl.ds(SUBLANE - shift, tm), :]
        if not carry_rows:
            sh = jnp.where(pos < shift, halo_refs[shift - 1][...], sh)
        gc = gc + sh * cw_ref[j:j + 1, :]
    act = _gelu_tanh(gc) * up
    act = jnp.where(n * tn + _iota(act.shape, 1) < n_valid, act, 0.0)
    act_ref[...] = act.astype(act_ref.dtype)


def _ffn_up(h, wu, wg, layer, cw8, kw, rows_per_batch, n_pad, halos=None, tm=512, tn=FFN_TILE):
    m, d = h.shape
    n_valid = wu.shape[-1]
    carry_rows = halos is None
    tm = min(tm, rows_per_batch) if carry_rows else m
    tn = _pick_tile(n_pad, tn)
    nm, nn = m // tm, n_pad // tn
    in_specs = [pl.BlockSpec((tm, d), lambda j, i: (i, 0)),
                pl.BlockSpec((None, d, tn), lambda j, i: (layer, 0, j)),
                pl.BlockSpec((None, d, tn), lambda j, i: (layer, 0, j)),
                pl.BlockSpec((SUBLANE, tn), lambda j, i: (0, j))]
    args = [h, wu, wg, cw8]
    if carry_rows:
        tail_spec = pl.BlockSpec((None, SUBLANE, tn), lambda j, i: (i, 0, j))
        tail_shape = jax.ShapeDtypeStruct((nm, SUBLANE, n_pad), F32)
    else:
        _log2(rows_per_batch)
        in_specs += [pl.BlockSpec((tm, tn), lambda j, i: (i, j))] * len(halos)
        args += list(halos)
        tail_spec = pl.BlockSpec((tm, tn), lambda j, i: (i, j))
        tail_shape = jax.ShapeDtypeStruct((m, n_pad), F32)
    return pl.pallas_call(
        functools.partial(_ffn_up_kernel, carry_rows=carry_rows, kw=kw,
                          tiles_per_batch=max(rows_per_batch // tm, 1), t=rows_per_batch,
                          n_valid=n_valid),
        grid=(nn, nm), in_specs=in_specs,
        out_specs=[pl.BlockSpec((tm, tn), lambda j, i: (i, j)), tail_spec],
        out_shape=[jax.ShapeDtypeStruct((m, n_pad), BF16), tail_shape],
        scratch_shapes=[pltpu.VMEM((tm + SUBLANE, tn), F32), pltpu.VMEM((d, tn), BF16),
                        pltpu.VMEM((d, tn), BF16)],
        compiler_params=_cp("arbitrary", "arbitrary"), name="ffn_up",
    )(*args)


def _even_weights(w_in, conv_w, a_log, dt_bias, b_f, w_out, ha, hb, hd):
    wa, wb = ha * hd, hb * hd
    o2 = 4 * wa
    o4 = o2 + 2 * ha
    o5 = o4 + 3 * wb
    w_main = jnp.concatenate([w_in[:, :o2], w_in[:, o4:o5]], axis=1).astype(BF16)
    small = jnp.concatenate([w_in[:, o2:o4], w_in[:, o5:]], axis=1)
    assert small.shape[1] <= LANE
    w_small = jnp.pad(small, ((0, 0), (0, LANE - small.shape[1]))).astype(BF16)
    par = jnp.zeros((SUBLANE, LANE), F32)
    par = par.at[0, ha:2 * ha].set(a_log.astype(F32))
    par = par.at[1, ha:2 * ha].set(dt_bias.astype(F32))
    par = par.at[1, 2 * ha:2 * ha + hb].set(b_f.astype(F32))
    return w_main, w_small, par, conv_w, w_out


def _even_mixer(h, nb, t, wts, gnorm, conv_buf, s0, ha, hb, hd, past=None):
    w_main, w_small, par, conv_w, w_out = wts
    wa, wb = ha * hd, hb * hd
    kw = conv_w.shape[0]
    tm = min(1024, nb * t)
    proj = _matmul([h], [w_main], F32, tm=tm, cols=(0, 4 * wa + wb))
    k_new = _matmul([h], [w_main], F32, tm=tm, cols=(4 * wa + wb, wb))
    v_new = _matmul([h], [w_main], F32, tm=tm, cols=(4 * wa + 2 * wb, wb))
    gates, fcum = _gates(h, w_small, par, nb, t, ha, hb)

    c = DELTA_CHUNK
    tp = -(-t // c) * c
    conv_buf8 = jnp.pad(conv_buf.astype(F32), ((0, 0), (SUBLANE - (kw - 1), 0), (0, 0)))
    if tp == t:
        proj_d, gates_d = proj, gates
    else:
        pad = lambda a: jnp.pad(a.reshape(nb, t, -1), ((0, 0), (0, tp - t), (0, 0))).reshape(nb * tp, -1)
        proj_d, gates_d = pad(proj[:, :4 * wa]), pad(gates)
    o_a, s_new = _delta(proj_d, gates_d, conv_w, conv_buf8, s0, gnorm, nb, tp, min(t, c) if tp != t else c,
                        ha)
    if tp != t:
        o_a = o_a.reshape(nb, tp, wa)[:, :t].reshape(nb * t, wa)

    logf = gates[:, 2 * ha:2 * ha + hb].reshape(nb, t, hb)
    f_new = fcum[:, 2 * ha:2 * ha + hb].reshape(nb, t, hb)
    kb, vb = k_new.reshape(nb, t, hb, hd), v_new.reshape(nb, t, hb, hd)
    if past is None:
        f_rows = jnp.swapaxes(f_new, 1, 2).reshape(nb, hb, 1, t)
        o_b = _fox_prompt(proj, 4 * wa // hd, k_new, v_new, f_rows, nb, t, hb)
    else:
        k_pool, v_pool, logf_pool, page_table = past
        n_pool, page = k_pool.shape[0], k_pool.shape[1]
        f_past = _fpast(logf_pool.astype(F32), page_table)
        f_past = f_past.reshape(nb, page_table.shape[1], 1, page * hb)
        o_b = _fox_sample(proj[:, 4 * wa:].reshape(nb, t * hb, hd), k_new.reshape(nb, t * hb, hd),
                          v_new.reshape(nb, t * hb, hd), f_new.reshape(nb, 1, t * hb),
                          k_pool.reshape(n_pool, page * hb, hd), v_pool.reshape(n_pool, page * hb, hd),
                          f_past, page_table, hb)
        o_b = o_b.reshape(nb * t, wb)
    out = _matmul_w32([o_a, o_b], [w_out, w_out], F32, tm=min(512, nb * t))
    keep = min(kw - 1, t)
    raw_tail = proj.reshape(nb, t, -1)[:, t - keep:, :3 * wa]
    new_conv = jnp.concatenate([conv_buf.astype(F32), raw_tail], axis=1)[:, -(kw - 1):]
    return out, s_new, new_conv, kb, vb, logf


def _odd_mixer(h, nb, t, w_in, w_out, nh, hd, bufs=None):
    ng = len(C_GROUPS)
    wc = nh * hd
    new_bufs = []
    if bufs is None:
        n_main = (3 * ng - 2) * wc
        main = _matmul_w32([h], [w_in], F32, out_cols=[n_main], tm=min(1024, nb * t), w_buffers=1)
        k_last, v_last = _matmul_w32([h], [w_in], F32, out_cols=[wc, wc], col0=n_main,
                                     tm=min(512, nb * t), w_buffers=1)
        srcs = [((main, g * 3 * nh), (main, g * 3 * nh + nh), (main, g * 3 * nh + 2 * nh))
                for g in range(ng - 1)]
        srcs.append(((main, (ng - 1) * 3 * nh), (k_last, 0), (v_last, 0)))
        o = _dil_prompt(srcs, nb, t, nh, C_GROUPS)
        m3 = main.reshape(nb, t, -1)
        for g, (window, _) in enumerate(C_GROUPS):
            keep = min(window, t)
            if g < ng - 1:
                kv = [m3[:, t - keep:, (g * 3 + r) * wc:(g * 3 + r + 1) * wc] for r in (1, 2)]
            else:
                kv = [a.reshape(nb, t, wc)[:, t - keep:] for a in (k_last, v_last)]
            new_bufs.append(tuple(a.reshape(nb, keep, nh, hd) for a in kv))
        return _matmul_w32([o], [w_out], F32, tm=min(512, nb * t)), new_bufs
    proj = _matmul_w32([h], [w_in], F32, tm=min(512, nb * t), w_buffers=1)
    p6 = proj.reshape(nb, t, ng, 3, nh, hd)
    flat = lambda a: a.reshape(nb, -1, hd)
    qs = [flat(p6[:, :, g, 0]) for g in range(ng)]
    kns = [flat(p6[:, :, g, 1]) for g in range(ng)]
    vns = [flat(p6[:, :, g, 2]) for g in range(ng)]
    o = _dil_sample(qs, kns, vns, [flat(bk) for bk, _ in bufs], [flat(bv) for _, bv in bufs],
                    C_GROUPS, nh, t)
    o = o.reshape(nb * t, nh * hd)
    for g in range(ng):
        bk, bv = bufs[g]
        wb = bk.shape[1]
        new_bufs.append((jnp.concatenate([bk, p6[:, :, g, 1]], axis=1)[:, -wb:],
                         jnp.concatenate([bv, p6[:, :, g, 2]], axis=1)[:, -wb:]))
    return _matmul_w32([o], [w_out], F32, tm=min(512, nb * t)), new_bufs


def _ffn_weights(w_up, w_gate, conv_w, w_down):
    d_ff = w_up.shape[-1]
    d_ffp = -(-d_ff // FFN_TILE) * FFN_TILE
    kw = conv_w.shape[1]
    cw8 = jnp.pad(conv_w.astype(F32), ((0, 0), (0, SUBLANE - kw), (0, d_ffp - d_ff)))
    wd = jnp.pad(w_down.astype(BF16), ((0, 0), (0, d_ffp - d_ff), (0, 0)))
    return w_up, w_gate, cw8, wd, kw, d_ff


def _conv_ffn(h, nb, t, wts, layer, buf=None):
    wu, wg, cw8, wd, kw, d_ff = wts
    d_ffp = wd.shape[1]
    cw8 = cw8[layer]
    if buf is None:
        act, tail = _ffn_up(h, wu, wg, layer, cw8, kw, t, d_ffp)
        tiles = tail.shape[0] // nb
        gate_tail = tail.reshape(nb, tiles, SUBLANE, d_ffp)[:, -1, :, :d_ff]
        new_buf = gate_tail[:, SUBLANE - (kw - 1):]
    else:
        bufp = jnp.pad(buf.astype(F32), ((0, 0), (0, 0), (0, d_ffp - d_ff)))
        halos = []
        for shift in range(1, kw):
            hl = jnp.zeros((nb, t, d_ffp), F32)
            hl = hl.at[:, :shift].set(bufp[:, (kw - 1) - shift:])
            halos.append(hl.reshape(nb * t, d_ffp))
        act, gate = _ffn_up(h, wu, wg, layer, cw8, kw, t, d_ffp, halos=halos)
        gate = gate.reshape(nb, t, d_ffp)[:, :, :d_ff]
        new_buf = jnp.concatenate([buf.astype(F32), gate], axis=1)[:, -(kw - 1):]
    out = _matmul([act], [wd], F32, tm=min(512, nb * t), tn=512, layer=layer)
    return out, new_buf


def kernel(x_prompt, x_sample, c_prompt, c_sample, state_delta, state_conv_qkv, cache_k, cache_v, cache_logf, cache_win_k0, cache_win_v0, cache_win_k1, cache_win_v1, cache_win_k2, cache_win_v2, state_ffn_conv, page_table, w_mod, b_mod, norm_pre_mix, norm_post_mix, norm_pre_ffn, norm_post_ffn, w_in_e, conv_a, a_log, dt_bias, gnorm_a, b_forget, w_out_e, w_in_o, w_out_o, w_up, w_gate, conv_ffn_w, w_down):
    bsz, seq, d = x_prompt.shape
    dbsz, dseq, _ = x_sample.shape
    depth = w_mod.shape[0]
    hd = gnorm_a.shape[-1]
    ha, hb, hc = a_log.shape[-1], b_forget.shape[-1], cache_win_k0.shape[3]
    assert hd == LANE and seq % DELTA_CHUNK == 0
    kw_a = conv_a.shape[1]
    win_caches = ((cache_win_k0, cache_win_v0), (cache_win_k1, cache_win_v1), (cache_win_k2, cache_win_v2))

    c_all = jnp.concatenate([c_prompt, c_sample], axis=0)
    c_all = jnp.pad(c_all, ((0, MOD_ROWS - c_all.shape[0]), (0, 0)))
    mod = _mod(c_all, w_mod, b_mod)
    mods = [(mod[l, :bsz].reshape(bsz, 6, d), mod[l, bsz:bsz + dbsz].reshape(dbsz, 6, d))
            for l in range(depth)]

    groups = [dict(x=x_prompt.reshape(bsz * seq, d), nb=bsz, t=seq, gi=0),
              dict(x=x_sample.reshape(dbsz * dseq, d), nb=dbsz, t=dseq, gi=1)]
    for gr in groups:
        m_0 = mods[0][gr["gi"]]
        _, gr["h"] = _post_pre(gr["x"], gr["nb"], gr["t"], shift=m_0[:, 0], scale=m_0[:, 1],
                               w_pre=norm_pre_mix[0])

    fwts = _ffn_weights(w_up, w_gate, conv_ffn_w, w_down)
    outs = {k: ([], []) for k in ("delta", "conv", "k", "v", "logf", "ffn")}
    wk = [([], []) for _ in C_GROUPS]
    wv = [([], []) for _ in C_GROUPS]
    for l in range(depth):
        if l % 2 == 0:
            e = l // 2
            ewts = _even_weights(w_in_e[e], conv_a[e], a_log[e], dt_bias[e], b_forget[e], w_out_e[e],
                                 ha, hb, hd)
        else:
            o = l // 2
            w_in_o_b, w_out_o_b = w_in_o[o], w_out_o[o]
        for gr in groups:
            gi, nb, t = gr["gi"], gr["nb"], gr["t"]
            m_l = mods[l][gi]
            if l % 2 == 0:
                if gi == 0:
                    conv_buf = jnp.zeros((nb, kw_a - 1, 3 * ha * hd), F32)
                    s0 = jnp.zeros((nb, ha, hd, hd), F32)
                    past = None
                else:
                    conv_buf, s0 = state_conv_qkv[e], state_delta[e]
                    past = (cache_k[e], cache_v[e], cache_logf[e], page_table)
                om, s_new, new_conv, kb, vb, logf = _even_mixer(
                    gr["h"], nb, t, ewts, gnorm_a[e], conv_buf, s0, ha, hb, hd, past)
                for key, val in (("delta", s_new), ("conv", new_conv), ("k", kb), ("v", vb), ("logf", logf)):
                    outs[key][gi].append(val)
            else:
                bufs = None if gi == 0 else tuple((bk[o], bv[o]) for bk, bv in win_caches)
                om, new_bufs = _odd_mixer(gr["h"], nb, t, w_in_o_b, w_out_o_b, hc, hd, bufs)
                for g in range(len(C_GROUPS)):
                    wk[g][gi].append(new_bufs[g][0])
                    wv[g][gi].append(new_bufs[g][1])
            gr["x"], gr["h"] = _post_pre(gr["x"], nb, t, o=om, gate=m_l[:, 2], w_post=norm_post_mix[l],
                                         shift=m_l[:, 3], scale=m_l[:, 4], w_pre=norm_pre_ffn[l])
            of, new_buf = _conv_ffn(gr["h"], nb, t, fwts, l, None if gi == 0 else state_ffn_conv[l])
            outs["ffn"][gi].append(new_buf)
            if l + 1 < depth:
                m_n = mods[l + 1][gi]
                gr["x"], gr["h"] = _post_pre(gr["x"], nb, t, o=of, gate=m_l[:, 5], w_post=norm_post_ffn[l],
                                             shift=m_n[:, 0], scale=m_n[:, 1], w_pre=norm_pre_mix[l + 1])
            else:
                gr["x"], _ = _post_pre(gr["x"], nb, t, o=of, gate=m_l[:, 5], w_post=norm_post_ffn[l])

    stk = lambda lst: jnp.stack(lst, axis=0)
    res = [groups[0]["x"].reshape(bsz, seq, d), groups[1]["x"].reshape(dbsz, dseq, d)]
    for key in ("delta", "conv", "k", "v", "logf"):
        res += [stk(outs[key][0]), stk(outs[key][1])]
    for g in range(len(C_GROUPS)):
        res += [stk(wk[g][0]), stk(wk[g][1]), stk(wv[g][0]), stk(wv[g][1])]
    res += [stk(outs["ffn"][0]), stk(outs["ffn"][1])]
    return tuple(res)
```

```python
import functools
import math

import jax
import jax.numpy as jnp
from jax import lax
from jax.experimental import pallas as pl
from jax.experimental.pallas import tpu as pltpu

F32 = jnp.float32
BF16 = jnp.bfloat16
HI = lax.Precision.HIGHEST

C_GROUPS = ((128, 1), (512, 4), (2048, 16))
DELTA_CHUNK = 64
RMS_EPS = 1e-6
L2_EPS = 1e-6
NEG = -1e30
LANE = 128
SUBLANE = 8
VMEM_LIMIT = 56 * 1024 * 1024
MOD_ROWS = 16
FFN_TILE = 512


def _cp(*sem):
    return pltpu.CompilerParams(dimension_semantics=sem, vmem_limit_bytes=VMEM_LIMIT)


def _dot(a, b, prec=None):
    return jnp.dot(a, b, preferred_element_type=F32, precision=prec)


def _dot_nt(a, b, prec=None):
    return lax.dot_general(a, b, (((1,), (1,)), ((), ())), preferred_element_type=F32, precision=prec)


def _dot_tn(a, b, prec=None):
    return lax.dot_general(a, b, (((0,), (0,)), ((), ())), preferred_element_type=F32, precision=prec)


def _dot3(a, b):
    ah, bh = a.astype(BF16), b.astype(BF16)
    al = (a - ah.astype(F32)).astype(BF16)
    bl = (b - bh.astype(F32)).astype(BF16)
    return _dot(ah, bh) + (_dot(ah, bl) + _dot(al, bh))


def _sigmoid(x):
    return 1.0 / (1.0 + jnp.exp(-x))


def _silu(x):
    return x * _sigmoid(x)


def _softplus(x):
    return jnp.maximum(x, 0.0) + jnp.log1p(jnp.exp(-jnp.abs(x)))


def _gelu_tanh(x):
    return 0.5 * x * (1.0 + jnp.tanh(0.7978845608028654 * (x + 0.044715 * (x * x * x))))


def _rms(x, w):
    return x * lax.rsqrt(jnp.mean(x * x, axis=-1, keepdims=True) + RMS_EPS) * w


def _iota(shape, dim):
    return lax.broadcasted_iota(jnp.int32, shape, dim)


def _row_to_col(row):
    n = row.shape[1]
    eye = _iota((n, n), 0) == _iota((n, n), 1)
    return jnp.sum(jnp.where(eye, jnp.broadcast_to(row, (n, n)), 0.0), axis=1, keepdims=True)


def _col_to_row(col):
    n = col.shape[0]
    eye = _iota((n, n), 0) == _iota((n, n), 1)
    return jnp.sum(jnp.where(eye, jnp.broadcast_to(col, (n, n)), 0.0), axis=0, keepdims=True)


def _pick_tile(n, pref):
    if n <= pref:
        return n
    t = (pref // LANE) * LANE
    while t >= LANE:
        if n % t == 0:
            return t
        t -= LANE
    return n


def _log2(n):
    assert n > 0 and n & (n - 1) == 0, n
    return n.bit_length() - 1


def _mod_kernel(c_ref, w_ref, b_ref, o_ref):
    a = _silu(c_ref[...]).astype(BF16)
    o_ref[...] = _dot(a, w_ref[...].astype(BF16)) + b_ref[...]


def _mod(c_all, w_mod, b_mod):
    depth, d, n = w_mod.shape
    r = c_all.shape[0]
    tn = _pick_tile(n, 512)
    return pl.pallas_call(
        _mod_kernel,
        grid=(depth, n // tn),
        in_specs=[pl.BlockSpec((r, d), lambda l, j: (0, 0)),
                  pl.BlockSpec((None, d, tn), lambda l, j: (l, 0, j)),
                  pl.BlockSpec((None, 1, tn), lambda l, j: (l, 0, j))],
        out_specs=pl.BlockSpec((None, r, tn), lambda l, j: (l, 0, j)),
        out_shape=jax.ShapeDtypeStruct((depth, r, n), F32),
        compiler_params=_cp("arbitrary", "arbitrary"),
        name="mod",
    )(c_all, w_mod, b_mod.reshape(depth, 1, n))


def _post_pre_kernel(*refs, has_o, has_h):
    refs = list(refs)
    x = refs.pop(0)[...]
    if has_o:
        o_ref, gate_ref, wpost_ref = refs.pop(0), refs.pop(0), refs.pop(0)
    if has_h:
        shift_ref, scale_ref, wpre_ref = refs.pop(0), refs.pop(0), refs.pop(0)
    if has_o:
        x = x + gate_ref[...] * _rms(o_ref[...].astype(F32), wpost_ref[...])
        refs.pop(0)[...] = x
    if has_h:
        h = _rms(x, wpre_ref[...]) * (1.0 + scale_ref[...]) + shift_ref[...]
        refs.pop(0)[...] = h.astype(BF16)


def _post_pre(x, nb, t, o=None, gate=None, w_post=None, shift=None, scale=None, w_pre=None):
    m, d = x.shape
    has_o, has_h = o is not None, shift is not None
    if t >= 128:
        tr = 128
        per = t // tr
        vec = lambda v: v.reshape(nb, 1, d)
        vec_spec = pl.BlockSpec((None, 1, d), lambda i: (i // per, 0, 0))
    else:
        tr = m
        vec = lambda v: jnp.repeat(v, t, axis=0).reshape(1, m, d)
        vec_spec = pl.BlockSpec((None, m, d), lambda i: (0, 0, 0))
    row_spec = pl.BlockSpec((tr, d), lambda i: (i, 0))
    w_spec = pl.BlockSpec((1, d), lambda i: (0, 0))
    args, in_specs, out_specs, out_shape = [x], [row_spec], [], []
    if has_o:
        args += [o, vec(gate), w_post.reshape(1, d)]
        in_specs += [row_spec, vec_spec, w_spec]
        out_specs.append(row_spec)
        out_shape.append(jax.ShapeDtypeStruct((m, d), F32))
    if has_h:
        args += [vec(shift), vec(scale), w_pre.reshape(1, d)]
        in_specs += [vec_spec, vec_spec, w_spec]
        out_specs.append(row_spec)
        out_shape.append(jax.ShapeDtypeStruct((m, d), BF16))
    outs = pl.pallas_call(
        functools.partial(_post_pre_kernel, has_o=has_o, has_h=has_h),
        grid=(m // tr,), in_specs=in_specs, out_specs=out_specs, out_shape=out_shape,
        compiler_params=_cp("arbitrary"), name="post_pre",
    )(*args)
    outs = list(outs)
    x_new = outs.pop(0) if has_o else x
    h = outs.pop(0) if has_h else None
    return x_new, h


def _mm_kernel(*refs, n_pairs, nk):
    a_refs, b_refs = refs[:n_pairs], refs[n_pairs:2 * n_pairs]
    o_ref = refs[2 * n_pairs]
    part = _dot(a_refs[0][...], b_refs[0][...])
    for a_ref, b_ref in zip(a_refs[1:], b_refs[1:]):
        part = part + _dot(a_ref[...], b_ref[...])
    if nk == 1:
        o_ref[...] = part.astype(o_ref.dtype)
        return
    acc_ref = refs[2 * n_pairs + 1]
    k = pl.program_id(2)

    @pl.when(k == 0)
    def _():
        acc_ref[...] = part

    @pl.when(k > 0)
    def _():
        acc_ref[...] += part

    @pl.when(k == nk - 1)
    def _():
        o_ref[...] = acc_ref[...].astype(o_ref.dtype)


def _matmul(a_list, b_list, out_dtype, tm=1024, tn=1024, tk=None, cols=None, layer=None):
    m, kdim = a_list[0].shape
    c0, n = (0, b_list[0].shape[-1]) if cols is None else cols
    tm = min(tm, m)
    tn = _pick_tile(n, tn)
    tk = kdim if tk is None else _pick_tile(kdim, tk)
    nk = kdim // tk
    assert m % tm == 0 and n % tn == 0 and kdim % tk == 0 and c0 % tn == 0
    jb = c0 // tn
    n_pairs = len(a_list)
    if layer is None:
        b_spec = pl.BlockSpec((tk, tn), lambda i, j, k: (k, j + jb))
    else:
        b_spec = pl.BlockSpec((None, tk, tn), lambda i, j, k: (layer, k, j + jb))
    in_specs = [pl.BlockSpec((tm, tk), lambda i, j, k: (i, k))] * n_pairs + [b_spec] * n_pairs
    scratch = [pltpu.VMEM((tm, tn), F32)] if nk > 1 else []
    return pl.pallas_call(
        functools.partial(_mm_kernel, n_pairs=n_pairs, nk=nk),
        grid=(m // tm, n // tn, nk), in_specs=in_specs,
        out_specs=pl.BlockSpec((tm, tn), lambda i, j, k: (i, j)),
        out_shape=jax.ShapeDtypeStruct((m, n), out_dtype),
        scratch_shapes=scratch,
        compiler_params=_cp("arbitrary", "arbitrary", "arbitrary"), name="matmul",
    )(*a_list, *b_list)


def _mmw_kernel(*refs, n_pairs, ranges):
    n_out = len(ranges)
    a_refs, b_refs = refs[:n_pairs], refs[n_pairs:2 * n_pairs]
    o_refs = refs[2 * n_pairs:2 * n_pairs + n_out]
    wb_refs = refs[2 * n_pairs + n_out:]
    j, i = pl.program_id(0), pl.program_id(1)

    @pl.when(i == 0)
    def _():
        for b_ref, wb_ref in zip(b_refs, wb_refs):
            wb_ref[...] = b_ref[...].astype(BF16)

    acc = _dot(a_refs[0][...], wb_refs[0][...])
    for a_ref, wb_ref in zip(a_refs[1:], wb_refs[1:]):
        acc = acc + _dot(a_ref[...], wb_ref[...])
    if n_out == 1:
        o_refs[0][...] = acc.astype(o_refs[0].dtype)
        return
    for o_ref, (lo, hi) in zip(o_refs, ranges):
        @pl.when((j >= lo) & (j < hi))
        def _(o_ref=o_ref):
            o_ref[...] = acc.astype(o_ref.dtype)


def _matmul_w32(a_list, w_list, out_dtype, out_cols=None, col0=0, tm=512, tn=1024, w_buffers=2):
    m, kdim = a_list[0].shape
    tm = min(tm, m)
    out_cols = [w_list[0].shape[1]] if out_cols is None else out_cols
    n = sum(out_cols)
    tn = _pick_tile(math.gcd(col0, *out_cols), tn)
    assert m % tm == 0
    nm = m // tm
    jb = col0 // tn
    row_blk = [p if w.shape[0] != kdim else 0 for p, w in enumerate(w_list)]
    ranges, lo = [], 0
    for c in out_cols:
        ranges.append((lo, lo + c // tn))
        lo += c // tn
    n_pairs = len(a_list)

    def out_map(lo, hi):
        def index(j, i):
            inside = (j >= lo) & (j < hi)
            ii = jnp.where(inside, i, jnp.where(j < lo, 0, nm - 1))
            return ii, jnp.clip(j - lo, 0, hi - lo - 1)
        return index

    outs = pl.pallas_call(
        functools.partial(_mmw_kernel, n_pairs=n_pairs, ranges=ranges),
        grid=(n // tn, nm),
        in_specs=([pl.BlockSpec((tm, kdim), lambda j, i: (i, 0))] * n_pairs
                  + [pl.BlockSpec((kdim, tn), lambda j, i, rb=rb: (rb, j + jb),
                                  pipeline_mode=pl.Buffered(w_buffers)) for rb in row_blk]),
        out_specs=[pl.BlockSpec((tm, tn), out_map(lo, hi)) for lo, hi in ranges],
        out_shape=[jax.ShapeDtypeStruct((m, c), out_dtype) for c in out_cols],
        scratch_shapes=[pltpu.VMEM((kdim, tn), BF16)] * n_pairs,
        compiler_params=_cp("arbitrary", "arbitrary"), name="matmul_w32",
    )(*a_list, *w_list)
    return outs[0] if len(out_cols) == 1 else outs


def _gates_kernel(h_ref, w_ref, par_ref, g_ref, f_ref, carry_ref, *, ha, hb, seg):
    @pl.when(pl.program_id(1) == 0)
    def _():
        carry_ref[...] = jnp.zeros_like(carry_ref)

    p = _dot(h_ref[...], w_ref[...])
    tr = p.shape[0]
    lane = _iota(p.shape, 1)
    a_log, bias = par_ref[0:1, :], par_ref[1:2, :]
    x = p + bias
    beta = _sigmoid(p)
    g = -jnp.exp(a_log) * _softplus(x)
    logf = -_softplus(-x)
    out = jnp.where(lane < ha, beta,
                    jnp.where(lane < 2 * ha, g, jnp.where(lane < 2 * ha + hb, logf, 0.0)))
    g_ref[...] = out
    rr, cc = _iota((tr, tr), 0), _iota((tr, tr), 1)
    tri = rr >= cc
    if seg is not None:
        tri = tri & ((rr >> _log2(seg)) == (cc >> _log2(seg)))
    cs = _dot(tri.astype(F32), out, HI) + carry_ref[0:1, :]
    f_ref[...] = cs
    carry_ref[0:1, :] = cs[tr - 1:tr, :]


def _gates(h, w_small, par, nb, t, ha, hb):
    m, d = h.shape
    if t >= 128:
        tr, per, seg, nb_grid = min(t, 512), t // min(t, 512), None, nb
    else:
        tr, per, seg, nb_grid = m, 1, t, 1
    spec = pl.BlockSpec((tr, LANE), lambda b, j: (b * per + j, 0))
    return pl.pallas_call(
        functools.partial(_gates_kernel, ha=ha, hb=hb, seg=seg),
        grid=(nb_grid, per),
        in_specs=[pl.BlockSpec((tr, d), lambda b, j: (b * per + j, 0)),
                  pl.BlockSpec((d, LANE), lambda b, j: (0, 0)),
                  pl.BlockSpec((SUBLANE, LANE), lambda b, j: (0, 0))],
        out_specs=[spec, spec],
        out_shape=[jax.ShapeDtypeStruct((m, LANE), F32)] * 2,
        scratch_shapes=[pltpu.VMEM((SUBLANE, LANE), F32)],
        compiler_params=_cp("arbitrary", "arbitrary"), name="gates",
    )(h, w_small, par)


def _each(f, *lists):
    return [f(*args) for args in zip(*lists)]


def _unit_lower_inverse(lows):
    c = lows[0].shape[0]
    assert c == 64
    r, s = _iota((c, c), 0), _iota((c, c), 1)
    eye = (r == s).astype(F32)
    same16 = (r >> 4) == (s >> 4)
    same32 = (r >> 5) == (s >> 5)
    ld = _each(lambda low: jnp.where(same16, low, 0.0), lows)
    x = _each(lambda a: eye - a, ld)
    p = _each(lambda a: _dot3(a, a), ld)
    for _ in range(2):
        xp = _each(lambda a, b: _dot3(jnp.concatenate([a, b], axis=0), b), x, p)
        x = _each(lambda a, b: a + b[:c], x, xp)
        p = _each(lambda b: b[c:], xp)
    x = _each(lambda a, b: a + _dot3(a, b), x, p)
    for keep in (same32 & jnp.logical_not(same16), jnp.logical_not(same32)):
        off = _each(lambda low: jnp.where(keep, low, 0.0), lows)
        y = _each(_dot3, x, off)
        x = _each(lambda a, b: a - _dot3(b, a), x, y)
    return x


def _delta_kernel(q_ref, k_ref, v_ref, z_ref, gt_ref, cwq_ref, cwk_ref, cwv_ref,
                  cbq_ref, cbk_ref, cbv_ref, s0_ref, gn_ref, o_ref, sfin_ref, ext_ref, s_ref,
                  *, hpb, ha, kw, valid_rows, nc):
    c_idx = pl.program_id(2)
    hblk = pl.program_id(1)
    c = q_ref.shape[0]
    hd = LANE

    @pl.when(c_idx == 0)
    def _():
        s_ref[...] = s0_ref[...]
        for i, cb in enumerate((cbq_ref, cbk_ref, cbv_ref)):
            ext_ref[i, 0:SUBLANE, :] = cb[...]

    conv = []
    for i, (r_ref, cw_ref) in enumerate(((q_ref, cwq_ref), (k_ref, cwk_ref), (v_ref, cwv_ref))):
        ext_ref[i, SUBLANE:SUBLANE + c, :] = r_ref[...]
        acc = ext_ref[i, SUBLANE:SUBLANE + c, :] * cw_ref[kw - 1:kw, :]
        for j in range(kw - 1):
            acc = acc + ext_ref[i, pl.ds(SUBLANE - (kw - 1) + j, c), :] * cw_ref[j:j + 1, :]
        ext_ref[i, 0:SUBLANE, :] = ext_ref[i, c:c + SUBLANE, :]
        acc = _silu(acc)
        if valid_rows < c:
            acc = jnp.where(_iota(acc.shape, 0) < valid_rows, acc, 0.0)
        conv.append(acc)
    xq, xk, xv = conv

    gt = gt_ref[...]
    rr, ss = _iota((c, c), 0), _iota((c, c), 1)
    incl, strict = rr >= ss, rr > ss
    gcum = _dot(incl.astype(F32), gt, HI)
    lane = _iota(gt.shape, 1)
    gn = gn_ref[...]
    heads = list(range(hpb))
    sls = [slice(j * hd, (j + 1) * hd) for j in heads]
    beta = [jnp.sum(jnp.where(lane == hblk * hpb + j, gt, 0.0), axis=1, keepdims=True) for j in heads]
    gc = [jnp.sum(jnp.where(lane == ha + hblk * hpb + j, gcum, 0.0), axis=1, keepdims=True) for j in heads]
    qh = [xq[:, sl] for sl in sls]
    qh = _each(lambda a: a * lax.rsqrt(jnp.sum(a * a, axis=-1, keepdims=True) + L2_EPS) * (hd ** -0.5), qh)
    kh = [xk[:, sl] for sl in sls]
    kh = _each(lambda a: a * lax.rsqrt(jnp.sum(a * a, axis=-1, keepdims=True) + L2_EPS), kh)
    decay = _each(lambda g: jnp.where(incl, jnp.exp(g - _col_to_row(g)), 0.0), gc)
    kb = _each(lambda a, b: a * b, kh, beta)
    vb = [xv[:, sl] * b for sl, b in zip(sls, beta)]
    egc = _each(jnp.exp, gc)
    kk = _each(lambda a, q, k: _dot_nt(jnp.concatenate([a, q], axis=0).astype(BF16), k.astype(BF16)),
               kb, qh, kh)
    low = _each(lambda a, d: jnp.where(strict, a[:c] * d, 0.0), kk, decay)
    attn = _each(lambda a, d: (a[c:] * d).astype(BF16), kk, decay)
    tmat = _unit_lower_inverse(low)
    uw = _each(lambda t, v, k, e: _dot(t.astype(BF16), jnp.concatenate([v, k * e], axis=1).astype(BF16)),
               tmat, vb, kb, egc)
    st = [s_ref[j] for j in heads]
    ws = _each(lambda a, q, e, s: _dot(jnp.concatenate([a[:, hd:], q * e], axis=0).astype(BF16),
                                       s.astype(BF16)), uw, qh, egc, st)
    v_new = _each(lambda a, b: (a[:, :hd] - b[:c]).astype(BF16), uw, ws)
    o = _each(lambda b, a, v: b[c:] + _dot(a, v), ws, attn, v_new)
    g_last = _each(lambda g: g[c - 1:c, :], gc)
    k_dec = _each(lambda k, gl, g: (k * jnp.exp(gl - g)).astype(BF16), kh, g_last, gc)
    s_new = _each(lambda s, gl, k, v: s * jnp.exp(gl) + _dot_tn(k, v), st, g_last, k_dec, v_new)
    for j in heads:
        s_ref[j] = s_new[j]
        o_ref[:, sls[j]] = (_rms(o[j], gn) * _silu(z_ref[:, sls[j]])).astype(o_ref.dtype)

    @pl.when(c_idx == nc - 1)
    def _():
        sfin_ref[...] = s_ref[...]


def _delta(proj, gates, conv_w, conv_buf8, s0, gnorm, nb, tp, valid_rows, ha, hpb=8):
    c = DELTA_CHUNK
    hd = LANE
    hpb = min(hpb, ha)
    w = hpb * hd
    nc = tp // c
    nhb = ha // hpb
    kw = conv_w.shape[0]
    cw8 = jnp.pad(conv_w, ((0, SUBLANE - kw), (0, 0)))

    def col(off):
        return pl.BlockSpec((c, w), lambda b, h, i, off=off: (b * nc + i, off * nhb + h))

    def cwspec(off):
        return pl.BlockSpec((SUBLANE, w), lambda b, h, i, off=off: (0, off * nhb + h))

    def cbspec(off):
        return pl.BlockSpec((None, SUBLANE, w), lambda b, h, i, off=off: (b, 0, off * nhb + h))

    state_spec = pl.BlockSpec((None, hpb, hd, hd), lambda b, h, i: (b, h, 0, 0))
    return pl.pallas_call(
        functools.partial(_delta_kernel, hpb=hpb, ha=ha, kw=kw, valid_rows=valid_rows, nc=nc),
        grid=(nb, nhb, nc),
        in_specs=[col(0), col(1), col(2), col(3),
                  pl.BlockSpec((c, LANE), lambda b, h, i: (b * nc + i, 0)),
                  cwspec(0), cwspec(1), cwspec(2), cbspec(0), cbspec(1), cbspec(2),
                  state_spec, pl.BlockSpec((1, hd), lambda b, h, i: (0, 0))],
        out_specs=[pl.BlockSpec((c, w), lambda b, h, i: (b * nc + i, h)), state_spec],
        out_shape=[jax.ShapeDtypeStruct((nb * tp, ha * hd), BF16),
                   jax.ShapeDtypeStruct(s0.shape, F32)],
        scratch_shapes=[pltpu.VMEM((3, c + SUBLANE, w), F32), pltpu.VMEM((hpb, hd, hd), F32)],
        compiler_params=_cp("arbitrary", "arbitrary", "arbitrary"), name="delta",
    )(proj, proj, proj, proj, gates, cw8, cw8, cw8, conv_buf8, conv_buf8, conv_buf8, s0,
      gnorm.reshape(1, hd))


def _softmax_step(carry, s, v, valid=None):
    m, l, acc = carry
    m_new = jnp.maximum(m, jnp.max(s, axis=-1, keepdims=True))
    alpha = jnp.exp(m - m_new)
    p = jnp.exp(s - m_new)
    if valid is not None:
        p = jnp.where(valid, p, 0.0)
    l = alpha * l + jnp.sum(p, axis=-1, keepdims=True)
    acc = alpha * acc + _dot(p.astype(BF16), v)
    return m_new, l, acc


def _softmax_init(rows, hd):
    return (jnp.full((rows, 1), NEG, F32), jnp.zeros((rows, 1), F32), jnp.zeros((rows, hd), F32))


def _fox_prompt_kernel(q_ref, k_ref, v_ref, f_ref, o_ref, kb_ref, vb_ref, *, tq, tk, scale, par):
    seq = q_ref.shape[0]
    kb_ref[...] = k_ref[...].astype(BF16)
    vb_ref[...] = v_ref[...].astype(BF16)
    col_minus_row = _iota((tq, tk), 1) - _iota((tq, tk), 0)
    nqb = seq // tq
    for c0 in range(0, nqb, par):
        chunk = list(range(c0, min(c0 + par, nqb)))
        qs = [(q_ref[pl.ds(qb * tq, tq), :] * scale).astype(BF16) for qb in chunk]
        fqs = [_row_to_col(f_ref[:, pl.ds(qb * tq, tq)]) for qb in chunk]
        carries = [_softmax_init(tq, LANE) for _ in chunk]
        n_steps = [-(-((qb + 1) * tq) // tk) for qb in chunk]
        for step in range(max(n_steps)):
            live = [i for i in range(len(chunk)) if step < n_steps[i]]
            k = kb_ref[pl.ds(step * tk, tk), :]
            v = vb_ref[pl.ds(step * tk, tk), :]
            fk = f_ref[:, pl.ds(step * tk, tk)]
            ss = [_dot_nt(qs[i], k) + (fqs[i] - fk) for i in live]
            for n, i in enumerate(live):
                if step * tk + tk - 1 > chunk[i] * tq:
                    ss[n] = jnp.where(col_minus_row <= chunk[i] * tq - step * tk, ss[n], NEG)
            ms = [jnp.maximum(carries[i][0], jnp.max(s, axis=-1, keepdims=True)) for i, s in zip(live, ss)]
            ps = [jnp.exp(s - m) for s, m in zip(ss, ms)]
            alphas = [jnp.exp(carries[i][0] - m) for i, m in zip(live, ms)]
            pvs = [_dot(p.astype(BF16), v) for p in ps]
            for i, m, p, a, pv in zip(live, ms, ps, alphas, pvs):
                carries[i] = (m, a * carries[i][1] + jnp.sum(p, axis=-1, keepdims=True),
                              a * carries[i][2] + pv)
        for qb, (_, l, acc) in zip(chunk, carries):
            o_ref[pl.ds(qb * tq, tq), :] = (acc / l).astype(o_ref.dtype)


def _fox_prompt(q_arr, q_col, k_arr, v_arr, f_rows, nb, s, nh, tq=128, tk=512, par=4):
    tq, tk = min(tq, s), min(tk, s)
    assert s % tq == 0 and s % tk == 0
    hd = LANE
    return pl.pallas_call(
        functools.partial(_fox_prompt_kernel, tq=tq, tk=tk, scale=hd ** -0.5, par=par),
        grid=(nb, nh),
        in_specs=[pl.BlockSpec((s, hd), lambda b, h: (b, q_col + h)),
                  pl.BlockSpec((s, hd), lambda b, h: (b, h)),
                  pl.BlockSpec((s, hd), lambda b, h: (b, h)),
                  pl.BlockSpec((None, None, 1, s), lambda b, h: (b, h, 0, 0))],
        out_specs=pl.BlockSpec((s, hd), lambda b, h: (b, h)),
        out_shape=jax.ShapeDtypeStruct((nb * s, nh * hd), BF16),
        scratch_shapes=[pltpu.VMEM((s, hd), BF16), pltpu.VMEM((s, hd), BF16)],
        compiler_params=_cp("arbitrary", "arbitrary"), name="fox_prompt",
    )(q_arr, k_arr, v_arr, f_rows)


def _fpast_kernel(pt_ref, *refs, group):
    lp_refs, o_ref, carry_ref = refs[:group], refs[group], refs[group + 1]

    @pl.when(pl.program_id(1) == 0)
    def _():
        carry_ref[...] = jnp.zeros_like(carry_ref)

    p = lp_refs[0].shape[0]
    upper = (_iota((p, p), 1) > _iota((p, p), 0)).astype(F32)
    for i in range(group):
        lp = lp_refs[i][...]
        o_ref[group - 1 - i] = -(_dot(upper, lp, HI) + carry_ref[0:1, :])
        carry_ref[0:1, :] = carry_ref[0:1, :] + jnp.sum(lp, axis=0, keepdims=True)


def _fpast(logf_pool, page_table):
    db, n_pages = page_table.shape
    _, p, nh = logf_pool.shape
    group = 8 if n_pages % 8 == 0 else 1
    nblk = n_pages // group

    def lp_spec(i):
        return pl.BlockSpec((None, p, nh),
                            lambda b, j, pt, i=i: (pt[b, n_pages - 1 - (j * group + i)], 0, 0))

    return pl.pallas_call(
        functools.partial(_fpast_kernel, group=group),
        grid_spec=pltpu.PrefetchScalarGridSpec(
            num_scalar_prefetch=1, grid=(db, nblk),
            in_specs=[lp_spec(i) for i in range(group)],
            out_specs=pl.BlockSpec((None, group, p, nh), lambda b, j, pt: (b, nblk - 1 - j, 0, 0)),
            scratch_shapes=[pltpu.VMEM((SUBLANE, nh), F32)]),
        out_shape=jax.ShapeDtypeStruct((db, n_pages, p, nh), F32),
        compiler_params=_cp("arbitrary", "arbitrary"), name="fpast",
    )(page_table, *([logf_pool] * group))


def _fox_sample_kernel(pt_ref, q_ref, k_ref, v_ref, fp_ref, kn_ref, vn_ref, fn_ref, o_ref,
                       m_ref, l_ref, acc_ref, bias_ref, s0_ref, s1_ref, *, nh, scale, n_pages):
    p = pl.program_id(1)
    r = q_ref.shape[0]
    rows = k_ref.shape[0]
    lognh = _log2(nh)
    fn = fn_ref[...]

    @pl.when(p == 0)
    def _():
        m_ref[...] = jnp.full_like(m_ref, NEG)
        l_ref[...] = jnp.zeros_like(l_ref)
        acc_ref[...] = jnp.zeros_like(acc_ref)
        s1_ref[...] = jnp.zeros_like(s1_ref)
        match = (_iota((r, rows), 0) & (nh - 1)) == (_iota((r, rows), 1) & (nh - 1))
        bias_ref[...] = jnp.where(match, _row_to_col(fn), NEG)

    q = (q_ref[...] * scale).astype(BF16)

    def update(s, v):
        m, l, acc = _softmax_step((m_ref[...], l_ref[...], acc_ref[...]), s, v)
        m_ref[...], l_ref[...], acc_ref[...] = m, l, acc

    def step(s_w, s_r):
        s_w[...] = _dot_nt(q, k_ref[...].astype(BF16)) + (bias_ref[...] - fp_ref[...])
        live = p > 0
        s = s_r[...]
        m_old = m_ref[...]
        m_new = jnp.where(live, jnp.maximum(m_old, jnp.max(s, axis=-1, keepdims=True)), m_old)
        pr = jnp.where(live, jnp.exp(s - m_new), 0.0)
        alpha = jnp.exp(m_old - m_new)
        l_ref[...] = alpha * l_ref[...] + jnp.sum(pr, axis=-1, keepdims=True)
        acc_ref[...] = alpha * acc_ref[...] + _dot(pr.astype(BF16), v_ref[...].astype(BF16))
        m_ref[...] = m_new

    @pl.when(p % 2 == 0)
    def _():
        step(s0_ref, s1_ref)

    @pl.when(p % 2 == 1)
    def _():
        step(s1_ref, s0_ref)

    @pl.when(p == n_pages)
    def _():
        rr, cc = _iota((r, r), 0), _iota((r, r), 1)
        ok = ((rr & (nh - 1)) == (cc & (nh - 1))) & ((cc >> lognh) <= (rr >> lognh))
        s2 = _dot_nt(q, kn_ref[...].astype(BF16)) + _row_to_col(fn) - fn
        update(jnp.where(ok, s2, NEG), vn_ref[...].astype(BF16))
        o_ref[...] = (acc_ref[...] / l_ref[...]).astype(o_ref.dtype)


def _fox_sample(q, k_new, v_new, f_new, k_pool, v_pool, f_past, page_table, nh):
    db, r, hd = q.shape
    n_pages = page_table.shape[1]
    rows = k_pool.shape[1]
    tok = pl.BlockSpec((None, r, hd), lambda b, p, pt: (b, 0, 0))
    last = n_pages - 1
    k_page = pl.BlockSpec((None, rows, hd), lambda b, p, pt: (pt[b, jnp.minimum(p, last)], 0, 0))
    v_page = pl.BlockSpec((None, rows, hd), lambda b, p, pt: (pt[b, jnp.maximum(p - 1, 0)], 0, 0))
    return pl.pallas_call(
        functools.partial(_fox_sample_kernel, nh=nh, scale=hd ** -0.5, n_pages=n_pages),
        grid_spec=pltpu.PrefetchScalarGridSpec(
            num_scalar_prefetch=1, grid=(db, n_pages + 1),
            in_specs=[tok, k_page, v_page,
                      pl.BlockSpec((None, None, 1, rows),
                                   lambda b, p, pt: (b, jnp.minimum(p, last), 0, 0)),
                      tok, tok, pl.BlockSpec((None, 1, r), lambda b, p, pt: (b, 0, 0))],
            out_specs=tok,
            scratch_shapes=[pltpu.VMEM((r, 1), F32), pltpu.VMEM((r, 1), F32),
                            pltpu.VMEM((r, hd), F32), pltpu.VMEM((r, rows), F32),
                            pltpu.VMEM((r, rows), F32), pltpu.VMEM((r, rows), F32)]),
        out_shape=jax.ShapeDtypeStruct((db, r, hd), BF16),
        compiler_params=_cp("arbitrary", "arbitrary"), name="fox_sample",
    )(page_table, q, k_pool, v_pool, f_past, k_new, v_new, f_new)


def _dil_prompt_kernel(*refs, groups, tb, scale, par):
    ng = len(groups)
    q_refs, k_refs, v_refs = refs[:ng], refs[ng:2 * ng], refs[2 * ng:3 * ng]
    o_ref, m_ref, l_ref, acc_ref = refs[3 * ng:]
    seq = q_refs[0].shape[0]
    diff = _iota((tb, tb), 0) - _iota((tb, tb), 1)

    def rows_of(res, blk, dil):
        return pl.ds(res + blk * tb * dil, tb, stride=dil) if dil > 1 else pl.ds(blk * tb, tb)

    for g, (window, dil) in enumerate(groups):
        reach = window // dil
        nqb = seq // dil // tb
        back = -(-reach // tb)
        units = [(res, qb) for res in range(dil) for qb in range(nqb)]
        for c0 in range(0, len(units), par):
            chunk = units[c0:c0 + par]
            rows = [rows_of(res, qb, dil) for res, qb in chunk]
            qs = [(q_refs[g][rw, :] * scale).astype(BF16) for rw in rows]
            if g == 0:
                carries = [_softmax_init(tb, LANE) for _ in chunk]
            else:
                carries = [(m_ref[rw, :], l_ref[rw, :], acc_ref[rw, :]) for rw in rows]
            for step in range(back + 1):
                live = [i for i, (_, qb) in enumerate(chunk) if qb - step >= 0]
                if not live:
                    continue
                krows = [rows_of(chunk[i][0], chunk[i][1] - step, dil) for i in live]
                ks = [k_refs[g][rw, :].astype(BF16) for rw in krows]
                vs = [v_refs[g][rw, :].astype(BF16) for rw in krows]
                ss = [_dot_nt(qs[i], k) for i, k in zip(live, ks)]
                if step * tb - (tb - 1) < 0 or step * tb + (tb - 1) > reach:
                    dist = diff + step * tb
                    valid = (dist >= 0) & (dist <= reach)
                    ss = [jnp.where(valid, s, NEG) for s in ss]
                ms = [jnp.maximum(carries[i][0], jnp.max(s, axis=-1, keepdims=True)) for i, s in zip(live, ss)]
                ps = [jnp.exp(s - m) for s, m in zip(ss, ms)]
                alphas = [jnp.exp(carries[i][0] - m) for i, m in zip(live, ms)]
                pvs = [_dot(p.astype(BF16), v) for p, v in zip(ps, vs)]
                for i, m, p, a, pv in zip(live, ms, ps, alphas, pvs):
                    carries[i] = (m, a * carries[i][1] + jnp.sum(p, axis=-1, keepdims=True),
                                  a * carries[i][2] + pv)
            for rw, (m, l, acc) in zip(rows, carries):
                m_ref[rw, :], l_ref[rw, :], acc_ref[rw, :] = m, l, acc
    o_ref[...] = (acc_ref[...] / l_ref[...]).astype(o_ref.dtype)


def _dil_prompt(srcs, nb, s, nh, groups, tb=128, par=4):
    hd = LANE
    ng = len(groups)
    tb = min(tb, s)
    for window, dil in groups:
        assert s % (dil * tb) == 0 and window % dil == 0
    args, specs = [], []
    for which in range(3):
        for g in range(ng):
            arr, col = srcs[g][which]
            args.append(arr)
            specs.append(pl.BlockSpec((s, hd), lambda b, h, col=col: (b, col + h)))
    return pl.pallas_call(
        functools.partial(_dil_prompt_kernel, groups=groups, tb=tb, scale=hd ** -0.5, par=par),
        grid=(nb, nh),
        in_specs=specs,
        out_specs=pl.BlockSpec((s, hd), lambda b, h: (b, h)),
        out_shape=jax.ShapeDtypeStruct((nb * s, nh * hd), BF16),
        scratch_shapes=[pltpu.VMEM((s, 1), F32), pltpu.VMEM((s, 1), F32), pltpu.VMEM((s, hd), F32)],
        compiler_params=_cp("arbitrary", "arbitrary"), name="dil_prompt",
    )(*args)


def _dil_sample_kernel(*refs, groups, wbs, tb, nh, scale, t):
    ng = len(groups)
    q_refs, kb_refs, vb_refs = refs[:ng], refs[ng:2 * ng], refs[2 * ng:3 * ng]
    kn_refs, vn_refs = refs[3 * ng:4 * ng], refs[4 * ng:5 * ng]
    o_ref, m_ref, l_ref, acc_ref = refs[5 * ng:5 * ng + 4]
    j = pl.program_id(1)
    r = q_refs[0].shape[0]
    rows = tb * nh
    lognh = _log2(nh)
    nblks = [wb // tb for wb in wbs]
    total = sum(nblks)

    @pl.when(j == 0)
    def _():
        m_ref[...] = jnp.full_like(m_ref, NEG)
        l_ref[...] = jnp.zeros_like(l_ref)
        acc_ref[...] = jnp.zeros_like(acc_ref)

    def update(s, v, valid):
        m, l, acc = _softmax_step((m_ref[...], l_ref[...], acc_ref[...]),
                                  jnp.where(valid, s, NEG), v, valid)
        m_ref[...], l_ref[...], acc_ref[...] = m, l, acc

    start = 0
    for g, (window, dil) in enumerate(groups):
        def block(g=g, window=window, dil=dil, start=start):
            rr, cc = _iota((r, rows), 0), _iota((r, rows), 1)
            key_tok = (j - start) * tb + (cc >> lognh)
            delta = wbs[g] + (rr >> lognh) - key_tok
            valid = (((rr & (nh - 1)) == (cc & (nh - 1))) & ((delta & (dil - 1)) == 0)
                     & (delta <= window))
            q = (q_refs[g][...] * scale).astype(BF16)
            update(_dot_nt(q, kb_refs[g][...].astype(BF16)), vb_refs[g][...].astype(BF16), valid)

        pl.when((j >= start) & (j < start + nblks[g]))(block)
        start += nblks[g]

    @pl.when(j == total - 1)
    def _():
        rr, cc = _iota((r, r), 0), _iota((r, r), 1)
        delta = (rr >> lognh) - (cc >> lognh)
        match = (rr & (nh - 1)) == (cc & (nh - 1))
        for g, (window, dil) in enumerate(groups):
            valid = match & (delta >= 0) & ((delta & (dil - 1)) == 0) & (delta <= window)
            q = (q_refs[g][...] * scale).astype(BF16)
            update(_dot_nt(q, kn_refs[g][...].astype(BF16)), vn_refs[g][...].astype(BF16), valid)
        o_ref[...] = (acc_ref[...] / l_ref[...]).astype(o_ref.dtype)


def _dil_sample(qs, kns, vns, kbufs, vbufs, groups, nh, t):
    db, r, hd = qs[0].shape
    ng = len(groups)
    wbs = [kb.shape[1] // nh for kb in kbufs]
    tb = 128
    while any(wb % tb for wb in wbs):
        tb //= 2
    nblks = [wb // tb for wb in wbs]
    starts = [sum(nblks[:g]) for g in range(ng)]
    total = sum(nblks)
    tok = pl.BlockSpec((None, r, hd), lambda b, j: (b, 0, 0))

    def buf_spec(g):
        return pl.BlockSpec(
            (None, tb * nh, hd),
            lambda b, j, g=g: (b, jnp.clip(j - starts[g], 0, nblks[g] - 1), 0))

    bspecs = [buf_spec(g) for g in range(ng)]
    return pl.pallas_call(
        functools.partial(_dil_sample_kernel, groups=groups, wbs=wbs, tb=tb, nh=nh,
                          scale=hd ** -0.5, t=t),
        grid=(db, total),
        in_specs=[tok] * ng + bspecs + bspecs + [tok] * (2 * ng),
        out_specs=tok,
        out_shape=jax.ShapeDtypeStruct((db, r, hd), BF16),
        scratch_shapes=[pltpu.VMEM((r, 1), F32), pltpu.VMEM((r, 1), F32), pltpu.VMEM((r, hd), F32)],
        compiler_params=_cp("arbitrary", "arbitrary"), name="dil_sample",
    )(*qs, *kbufs, *vbufs, *kns, *vns)


def _ffn_up_kernel(*refs, carry_rows, kw, tiles_per_batch, t):
    h_ref, wu_ref, wg_ref, cw_ref = refs[:4]
    nh = kw - 1
    if carry_rows:
        act_ref, tail_ref, ext_ref, wub_ref, wgb_ref = refs[4:]
    else:
        halo_refs = refs[4:4 + nh]
        act_ref, tail_ref, ext_ref, wub_ref, wgb_ref = refs[4 + nh:]
    n, m = pl.program_id(0), pl.program_id(1)

    @pl.when(m == 0)
    def _():
        wub_ref[...] = wu_ref[...].astype(BF16)
        wgb_ref[...] = wg_ref[...].astype(BF16)

    h = h_ref[...]
    tm = h.shape[0]
    up = _dot(h, wub_ref[...])
    gate = _dot(h, wgb_ref[...])
    tn = gate.shape[1]
    if carry_rows:
        @pl.when((m % tiles_per_batch) == 0)
        def _():
            ext_ref[0:SUBLANE, :] = jnp.zeros((SUBLANE, tn), F32)

        @pl.when((m % tiles_per_batch) != 0)
        def _():
            ext_ref[0:SUBLANE, :] = ext_ref[tm:tm + SUBLANE, :]

        ext_ref[SUBLANE:SUBLANE + tm, :] = gate
        tail_ref[...] = gate[tm - SUBLANE:, :]
    else:
        ext_ref[0:SUBLANE, :] = jnp.zeros((SUBLANE, tn), F32)
        ext_ref[SUBLANE:SUBLANE + tm, :] = gate
        tail_ref[...] = gate
        pos = _iota(gate.shape, 0) & (t - 1)
    gc = gate * cw_ref[kw - 1:kw, :]
    for j in range(kw - 1):
        shift = kw - 1 - j
        sh = ext_ref[pl.ds(SUBLANE - shift, tm), :]
        if not carry_rows:
            sh = jnp.where(pos < shift, halo_refs[shift - 1][...], sh)
        gc = gc + sh * cw_ref[j:j + 1, :]
    act_ref[...] = (_gelu_tanh(gc) * up).astype(act_ref.dtype)


def _ffn_up(h, wu, wg, layer, cw8, kw, rows_per_batch, halos=None, tm=512, tn=FFN_TILE):
    m, d = h.shape
    n_pad = wu.shape[-1]
    carry_rows = halos is None
    tm = min(tm, rows_per_batch) if carry_rows else m
    tn = min(tn, -(-n_pad // LANE) * LANE)
    nm, nn = m // tm, -(-n_pad // tn)
    in_specs = [pl.BlockSpec((tm, d), lambda j, i: (i, 0)),
                pl.BlockSpec((None, d, tn), lambda j, i: (layer, 0, j)),
                pl.BlockSpec((None, d, tn), lambda j, i: (layer, 0, j)),
                pl.BlockSpec((SUBLANE, tn), lambda j, i: (0, j))]
    args = [h, wu, wg, cw8]
    if carry_rows:
        tail_spec = pl.BlockSpec((None, SUBLANE, tn), lambda j, i: (i, 0, j))
        tail_shape = jax.ShapeDtypeStruct((nm, SUBLANE, n_pad), F32)
    else:
        _log2(rows_per_batch)
        in_specs += [pl.BlockSpec((tm, tn), lambda j, i: (i, j))] * len(halos)
        args += list(halos)
        tail_spec = pl.BlockSpec((tm, tn), lambda j, i: (i, j))
        tail_shape = jax.ShapeDtypeStruct((m, n_pad), F32)
    return pl.pallas_call(
        functools.partial(_ffn_up_kernel, carry_rows=carry_rows, kw=kw,
                          tiles_per_batch=max(rows_per_batch // tm, 1), t=rows_per_batch),
        grid=(nn, nm), in_specs=in_specs,
        out_specs=[pl.BlockSpec((tm, tn), lambda j, i: (i, j)), tail_spec],
        out_shape=[jax.ShapeDtypeStruct((m, n_pad), BF16), tail_shape],
        scratch_shapes=[pltpu.VMEM((tm + SUBLANE, tn), F32), pltpu.VMEM((d, tn), BF16),
                        pltpu.VMEM((d, tn), BF16)],
        compiler_params=_cp("arbitrary", "arbitrary"), name="ffn_up",
    )(*args)


def _even_weights(w_in, conv_w, a_log, dt_bias, b_f, w_out, ha, hb, hd):
    wa, wb = ha * hd, hb * hd
    o2 = 4 * wa
    o4 = o2 + 2 * ha
    o5 = o4 + 3 * wb
    w_main = jnp.concatenate([w_in[:, :o2], w_in[:, o4:o5]], axis=1).astype(BF16)
    small = jnp.concatenate([w_in[:, o2:o4], w_in[:, o5:]], axis=1)
    assert small.shape[1] <= LANE
    w_small = jnp.pad(small, ((0, 0), (0, LANE - small.shape[1]))).astype(BF16)
    par = jnp.zeros((SUBLANE, LANE), F32)
    par = par.at[0, ha:2 * ha].set(a_log.astype(F32))
    par = par.at[1, ha:2 * ha].set(dt_bias.astype(F32))
    par = par.at[1, 2 * ha:2 * ha + hb].set(b_f.astype(F32))
    return w_main, w_small, par, conv_w, w_out


def _even_mixer(h, nb, t, wts, gnorm, conv_buf, s0, ha, hb, hd, past=None):
    w_main, w_small, par, conv_w, w_out = wts
    wa, wb = ha * hd, hb * hd
    kw = conv_w.shape[0]
    tm = min(1024, nb * t)
    proj = _matmul([h], [w_main], F32, tm=tm, cols=(0, 4 * wa + wb))
    k_new = _matmul([h], [w_main], F32, tm=tm, cols=(4 * wa + wb, wb))
    v_new = _matmul([h], [w_main], F32, tm=tm, cols=(4 * wa + 2 * wb, wb))
    gates, fcum = _gates(h, w_small, par, nb, t, ha, hb)

    c = DELTA_CHUNK
    tp = -(-t // c) * c
    conv_buf8 = jnp.pad(conv_buf.astype(F32), ((0, 0), (SUBLANE - (kw - 1), 0), (0, 0)))
    if tp == t:
        proj_d, gates_d = proj, gates
    else:
        pad = lambda a: jnp.pad(a.reshape(nb, t, -1), ((0, 0), (0, tp - t), (0, 0))).reshape(nb * tp, -1)
        proj_d, gates_d = pad(proj[:, :4 * wa]), pad(gates)
    o_a, s_new = _delta(proj_d, gates_d, conv_w, conv_buf8, s0, gnorm, nb, tp, min(t, c) if tp != t else c,
                        ha)
    if tp != t:
        o_a = o_a.reshape(nb, tp, wa)[:, :t].reshape(nb * t, wa)

    logf = gates[:, 2 * ha:2 * ha + hb].reshape(nb, t, hb)
    f_new = fcum[:, 2 * ha:2 * ha + hb].reshape(nb, t, hb)
    kb, vb = k_new.reshape(nb, t, hb, hd), v_new.reshape(nb, t, hb, hd)
    if past is None:
        f_rows = jnp.swapaxes(f_new, 1, 2).reshape(nb, hb, 1, t)
        o_b = _fox_prompt(proj, 4 * wa // hd, k_new, v_new, f_rows, nb, t, hb)
    else:
        k_pool, v_pool, logf_pool, page_table = past
        n_pool, page = k_pool.shape[0], k_pool.shape[1]
        f_past = _fpast(logf_pool.astype(F32), page_table)
        f_past = f_past.reshape(nb, page_table.shape[1], 1, page * hb)
        o_b = _fox_sample(proj[:, 4 * wa:].reshape(nb, t * hb, hd), k_new.reshape(nb, t * hb, hd),
                          v_new.reshape(nb, t * hb, hd), f_new.reshape(nb, 1, t * hb),
                          k_pool.reshape(n_pool, page * hb, hd), v_pool.reshape(n_pool, page * hb, hd),
                          f_past, page_table, hb)
        o_b = o_b.reshape(nb * t, wb)
    out = _matmul_w32([o_a, o_b], [w_out, w_out], BF16, tm=min(512, nb * t))
    keep = min(kw - 1, t)
    raw_tail = proj.reshape(nb, t, -1)[:, t - keep:, :3 * wa]
    new_conv = jnp.concatenate([conv_buf.astype(F32), raw_tail], axis=1)[:, -(kw - 1):]
    return out, s_new, new_conv, kb, vb, logf


def _odd_mixer(h, nb, t, w_in, w_out, nh, hd, bufs=None):
    ng = len(C_GROUPS)
    wc = nh * hd
    new_bufs = []
    if bufs is None:
        n_main = (3 * ng - 2) * wc
        main = _matmul_w32([h], [w_in], F32, out_cols=[n_main], tm=min(1024, nb * t), w_buffers=1)
        k_last, v_last = _matmul_w32([h], [w_in], F32, out_cols=[wc, wc], col0=n_main,
                                     tm=min(512, nb * t), w_buffers=1)
        srcs = [((main, g * 3 * nh), (main, g * 3 * nh + nh), (main, g * 3 * nh + 2 * nh))
                for g in range(ng - 1)]
        srcs.append(((main, (ng - 1) * 3 * nh), (k_last, 0), (v_last, 0)))
        o = _dil_prompt(srcs, nb, t, nh, C_GROUPS)
        m3 = main.reshape(nb, t, -1)
        for g, (window, _) in enumerate(C_GROUPS):
            keep = min(window, t)
            if g < ng - 1:
                kv = [m3[:, t - keep:, (g * 3 + r) * wc:(g * 3 + r + 1) * wc] for r in (1, 2)]
            else:
                kv = [a.reshape(nb, t, wc)[:, t - keep:] for a in (k_last, v_last)]
            new_bufs.append(tuple(a.reshape(nb, keep, nh, hd) for a in kv))
        return _matmul_w32([o], [w_out], BF16, tm=min(512, nb * t)), new_bufs
    proj = _matmul_w32([h], [w_in], F32, tm=min(512, nb * t), w_buffers=1)
    p6 = proj.reshape(nb, t, ng, 3, nh, hd)
    flat = lambda a: a.reshape(nb, -1, hd)
    qs = [flat(p6[:, :, g, 0]) for g in range(ng)]
    kns = [flat(p6[:, :, g, 1]) for g in range(ng)]
    vns = [flat(p6[:, :, g, 2]) for g in range(ng)]
    o = _dil_sample(qs, kns, vns, [flat(bk) for bk, _ in bufs], [flat(bv) for _, bv in bufs],
                    C_GROUPS, nh, t)
    o = o.reshape(nb * t, nh * hd)
    for g in range(ng):
        bk, bv = bufs[g]
        wb = bk.shape[1]
        new_bufs.append((jnp.concatenate([bk, p6[:, :, g, 1]], axis=1)[:, -wb:],
                         jnp.concatenate([bv, p6[:, :, g, 2]], axis=1)[:, -wb:]))
    return _matmul_w32([o], [w_out], BF16, tm=min(512, nb * t)), new_bufs


def _ffn_weights(w_up, w_gate, conv_w, w_down):
    d_ff = w_up.shape[-1]
    kw = conv_w.shape[1]
    cw8 = jnp.pad(conv_w.astype(F32), ((0, 0), (0, SUBLANE - kw), (0, 0)))
    return w_up, w_gate, cw8, w_down.astype(BF16), kw, d_ff


def _conv_ffn(h, nb, t, wts, layer, buf=None):
    wu, wg, cw8, wd, kw, d_ff = wts
    cw8 = cw8[layer]
    if buf is None:
        act, tail = _ffn_up(h, wu, wg, layer, cw8, kw, t)
        tiles = tail.shape[0] // nb
        gate_tail = tail.reshape(nb, tiles, SUBLANE, d_ff)[:, -1]
        new_buf = gate_tail[:, SUBLANE - (kw - 1):]
    else:
        halos = []
        for shift in range(1, kw):
            hl = jnp.zeros((nb, t, d_ff), F32)
            hl = hl.at[:, :shift].set(buf.astype(F32)[:, (kw - 1) - shift:])
            halos.append(hl.reshape(nb * t, d_ff))
        act, gate = _ffn_up(h, wu, wg, layer, cw8, kw, t, halos=halos)
        new_buf = jnp.concatenate([buf.astype(F32), gate.reshape(nb, t, d_ff)], axis=1)[:, -(kw - 1):]
    out = _matmul([act], [wd], BF16, tm=min(512, nb * t), tn=512, layer=layer)
    return out, new_buf


def kernel(x_prompt, x_sample, c_prompt, c_sample, state_delta, state_conv_qkv, cache_k, cache_v, cache_logf, cache_win_k0, cache_win_v0, cache_win_k1, cache_win_v1, cache_win_k2, cache_win_v2, state_ffn_conv, page_table, w_mod, b_mod, norm_pre_mix, norm_post_mix, norm_pre_ffn, norm_post_ffn, w_in_e, conv_a, a_log, dt_bias, gnorm_a, b_forget, w_out_e, w_in_o, w_out_o, w_up, w_gate, conv_ffn_w, w_down):
    bsz, seq, d = x_prompt.shape
    dbsz, dseq, _ = x_sample.shape
    depth = w_mod.shape[0]
    hd = gnorm_a.shape[-1]
    ha, hb, hc = a_log.shape[-1], b_forget.shape[-1], cache_win_k0.shape[3]
    assert hd == LANE and seq % DELTA_CHUNK == 0
    kw_a = conv_a.shape[1]
    win_caches = ((cache_win_k0, cache_win_v0), (cache_win_k1, cache_win_v1), (cache_win_k2, cache_win_v2))

    c_all = jnp.concatenate([c_prompt, c_sample], axis=0)
    c_all = jnp.pad(c_all, ((0, MOD_ROWS - c_all.shape[0]), (0, 0)))
    mod = _mod(c_all, w_mod, b_mod)
    mods = [(mod[l, :bsz].reshape(bsz, 6, d), mod[l, bsz:bsz + dbsz].reshape(dbsz, 6, d))
            for l in range(depth)]

    groups = [dict(x=x_prompt.reshape(bsz * seq, d), nb=bsz, t=seq, gi=0),
              dict(x=x_sample.reshape(dbsz * dseq, d), nb=dbsz, t=dseq, gi=1)]
    for gr in groups:
        m_0 = mods[0][gr["gi"]]
        _, gr["h"] = _post_pre(gr["x"], gr["nb"], gr["t"], shift=m_0[:, 0], scale=m_0[:, 1],
                               w_pre=norm_pre_mix[0])

    fwts = _ffn_weights(w_up, w_gate, conv_ffn_w, w_down)
    outs = {k: ([], []) for k in ("delta", "conv", "k", "v", "logf", "ffn")}
    wk = [([], []) for _ in C_GROUPS]
    wv = [([], []) for _ in C_GROUPS]
    for l in range(depth):
        if l % 2 == 0:
            e = l // 2
            ewts = _even_weights(w_in_e[e], conv_a[e], a_log[e], dt_bias[e], b_forget[e], w_out_e[e],
                                 ha, hb, hd)
        else:
            o = l // 2
            w_in_o_b, w_out_o_b = w_in_o[o], w_out_o[o]
        for gr in groups:
            gi, nb, t = gr["gi"], gr["nb"], gr["t"]
            m_l = mods[l][gi]
            if l % 2 == 0:
                if gi == 0:
                    conv_buf = jnp.zeros((nb, kw_a - 1, 3 * ha * hd), F32)
                    s0 = jnp.zeros((nb, ha, hd, hd), F32)
                    past = None
                else:
                    conv_buf, s0 = state_conv_qkv[e], state_delta[e]
                    past = (cache_k[e], cache_v[e], cache_logf[e], page_table)
                om, s_new, new_conv, kb, vb, logf = _even_mixer(
                    gr["h"], nb, t, ewts, gnorm_a[e], conv_buf, s0, ha, hb, hd, past)
                for key, val in (("delta", s_new), ("conv", new_conv), ("k", kb), ("v", vb), ("logf", logf)):
                    outs[key][gi].append(val)
            else:
                bufs = None if gi == 0 else tuple((bk[o], bv[o]) for bk, bv in win_caches)
                om, new_bufs = _odd_mixer(gr["h"], nb, t, w_in_o_b, w_out_o_b, hc, hd, bufs)
                for g in range(len(C_GROUPS)):
                    wk[g][gi].append(new_bufs[g][0])
                    wv[g][gi].append(new_bufs[g][1])
            gr["x"], gr["h"] = _post_pre(gr["x"], nb, t, o=om, gate=m_l[:, 2], w_post=norm_post_mix[l],
                                         shift=m_l[:, 3], scale=m_l[:, 4], w_pre=norm_pre_ffn[l])
            of, new_buf = _conv_ffn(gr["h"], nb, t, fwts, l, None if gi == 0 else state_ffn_conv[l])
            outs["ffn"][gi].append(new_buf)
            if l + 1 < depth:
                m_n = mods[l + 1][gi]
                gr["x"], gr["h"] = _post_pre(gr["x"], nb, t, o=of, gate=m_l[:, 5], w_post=norm_post_ffn[l],
                                             shift=m_n[:, 0], scale=m_n[:, 1], w_pre=norm_pre_mix[l + 1])
            else:
                gr["x"], _ = _post_pre(gr["x"], nb, t, o=of, gate=m_l[:, 5], w_post=norm_post_ffn[l])

    stk = lambda lst: jnp.stack(lst, axis=0)
    res = [groups[0]["x"].reshape(bsz, seq, d), groups[1]["x"].reshape(dbsz, dseq, d)]
    for key in ("delta", "conv", "k", "v", "logf"):
        res += [stk(outs[key][0]), stk(outs[key][1])]
    for g in range(len(C_GROUPS)):
        res += [stk(wk[g][0]), stk(wk[g][1]), stk(wv[g][0]), stk(wv[g][1])]
    res += [stk(outs["ffn"][0]), stk(outs["ffn"][1])]
    return tuple(res)
```

```python
import functools
import math

import jax
import jax.numpy as jnp
from jax import lax
from jax.experimental import pallas as pl
from jax.experimental.pallas import tpu as pltpu

F32 = jnp.float32
BF16 = jnp.bfloat16
HI = lax.Precision.HIGHEST

C_GROUPS = ((128, 1), (512, 4), (2048, 16))
DELTA_CHUNK = 64
RMS_EPS = 1e-6
L2_EPS = 1e-6
NEG = -1e30
LANE = 128
SUBLANE = 8
VMEM_LIMIT = 56 * 1024 * 1024
MOD_ROWS = 16
FFN_TILE = 512


def _cp(*sem):
    return pltpu.CompilerParams(dimension_semantics=sem, vmem_limit_bytes=VMEM_LIMIT)


def _dot(a, b, prec=None):
    return jnp.dot(a, b, preferred_element_type=F32, precision=prec)


def _dot_nt(a, b, prec=None):
    return lax.dot_general(a, b, (((1,), (1,)), ((), ())), preferred_element_type=F32, precision=prec)


def _dot_tn(a, b, prec=None):
    return lax.dot_general(a, b, (((0,), (0,)), ((), ())), preferred_element_type=F32, precision=prec)


def _dot3(a, b):
    ah, bh = a.astype(BF16), b.astype(BF16)
    al = (a - ah.astype(F32)).astype(BF16)
    bl = (b - bh.astype(F32)).astype(BF16)
    return _dot(ah, bh) + (_dot(ah, bl) + _dot(al, bh))


def _sigmoid(x):
    return 1.0 / (1.0 + jnp.exp(-x))


def _silu(x):
    return x * _sigmoid(x)


def _softplus(x):
    return jnp.maximum(x, 0.0) + jnp.log1p(jnp.exp(-jnp.abs(x)))


def _gelu_tanh(x):
    return 0.5 * x * (1.0 + jnp.tanh(0.7978845608028654 * (x + 0.044715 * (x * x * x))))


def _rms(x, w):
    return x * lax.rsqrt(jnp.mean(x * x, axis=-1, keepdims=True) + RMS_EPS) * w


def _iota(shape, dim):
    return lax.broadcasted_iota(jnp.int32, shape, dim)


def _row_to_col(row):
    n = row.shape[1]
    eye = _iota((n, n), 0) == _iota((n, n), 1)
    return jnp.sum(jnp.where(eye, jnp.broadcast_to(row, (n, n)), 0.0), axis=1, keepdims=True)


def _col_to_row(col):
    n = col.shape[0]
    eye = _iota((n, n), 0) == _iota((n, n), 1)
    return jnp.sum(jnp.where(eye, jnp.broadcast_to(col, (n, n)), 0.0), axis=0, keepdims=True)


def _pick_tile(n, pref):
    if n <= pref:
        return n
    t = (pref // LANE) * LANE
    while t >= LANE:
        if n % t == 0:
            return t
        t -= LANE
    return n


def _log2(n):
    assert n > 0 and n & (n - 1) == 0, n
    return n.bit_length() - 1


def _mod_kernel(c_ref, w_ref, b_ref, o_ref):
    a = _silu(c_ref[...]).astype(BF16)
    o_ref[...] = _dot(a, w_ref[...].astype(BF16)) + b_ref[...]


def _mod(c_all, w_mod, b_mod):
    depth, d, n = w_mod.shape
    r = c_all.shape[0]
    tn = _pick_tile(n, 512)
    return pl.pallas_call(
        _mod_kernel,
        grid=(depth, n // tn),
        in_specs=[pl.BlockSpec((r, d), lambda l, j: (0, 0)),
                  pl.BlockSpec((None, d, tn), lambda l, j: (l, 0, j)),
                  pl.BlockSpec((None, 1, tn), lambda l, j: (l, 0, j))],
        out_specs=pl.BlockSpec((None, r, tn), lambda l, j: (l, 0, j)),
        out_shape=jax.ShapeDtypeStruct((depth, r, n), F32),
        compiler_params=_cp("arbitrary", "arbitrary"),
        name="mod",
    )(c_all, w_mod, b_mod.reshape(depth, 1, n))


def _post_pre_kernel(*refs, has_o, has_h):
    refs = list(refs)
    x = refs.pop(0)[...]
    if has_o:
        o_ref, gate_ref, wpost_ref = refs.pop(0), refs.pop(0), refs.pop(0)
    if has_h:
        shift_ref, scale_ref, wpre_ref = refs.pop(0), refs.pop(0), refs.pop(0)
    if has_o:
        x = x + gate_ref[...] * _rms(o_ref[...].astype(F32), wpost_ref[...])
        refs.pop(0)[...] = x
    if has_h:
        h = _rms(x, wpre_ref[...]) * (1.0 + scale_ref[...]) + shift_ref[...]
        refs.pop(0)[...] = h.astype(BF16)


def _post_pre(x, nb, t, o=None, gate=None, w_post=None, shift=None, scale=None, w_pre=None):
    m, d = x.shape
    has_o, has_h = o is not None, shift is not None
    if t >= 128:
        tr = 128
        per = t // tr
        vec = lambda v: v.reshape(nb, 1, d)
        vec_spec = pl.BlockSpec((None, 1, d), lambda i: (i // per, 0, 0))
    else:
        tr = m
        vec = lambda v: jnp.repeat(v, t, axis=0).reshape(1, m, d)
        vec_spec = pl.BlockSpec((None, m, d), lambda i: (0, 0, 0))
    row_spec = pl.BlockSpec((tr, d), lambda i: (i, 0))
    w_spec = pl.BlockSpec((1, d), lambda i: (0, 0))
    args, in_specs, out_specs, out_shape = [x], [row_spec], [], []
    if has_o:
        args += [o, vec(gate), w_post.reshape(1, d)]
        in_specs += [row_spec, vec_spec, w_spec]
        out_specs.append(row_spec)
        out_shape.append(jax.ShapeDtypeStruct((m, d), F32))
    if has_h:
        args += [vec(shift), vec(scale), w_pre.reshape(1, d)]
        in_specs += [vec_spec, vec_spec, w_spec]
        out_specs.append(row_spec)
        out_shape.append(jax.ShapeDtypeStruct((m, d), BF16))
    outs = pl.pallas_call(
        functools.partial(_post_pre_kernel, has_o=has_o, has_h=has_h),
        grid=(m // tr,), in_specs=in_specs, out_specs=out_specs, out_shape=out_shape,
        compiler_params=_cp("arbitrary"), name="post_pre",
    )(*args)
    outs = list(outs)
    x_new = outs.pop(0) if has_o else x
    h = outs.pop(0) if has_h else None
    return x_new, h


def _mm_kernel(*refs, n_pairs, nk):
    a_refs, b_refs = refs[:n_pairs], refs[n_pairs:2 * n_pairs]
    o_ref = refs[2 * n_pairs]
    part = _dot(a_refs[0][...], b_refs[0][...])
    for a_ref, b_ref in zip(a_refs[1:], b_refs[1:]):
        part = part + _dot(a_ref[...], b_ref[...])
    if nk == 1:
        o_ref[...] = part.astype(o_ref.dtype)
        return
    acc_ref = refs[2 * n_pairs + 1]
    k = pl.program_id(2)

    @pl.when(k == 0)
    def _():
        acc_ref[...] = part

    @pl.when(k > 0)
    def _():
        acc_ref[...] += part

    @pl.when(k == nk - 1)
    def _():
        o_ref[...] = acc_ref[...].astype(o_ref.dtype)


def _matmul(a_list, b_list, out_dtype, tm=1024, tn=1024, tk=None, cols=None, layer=None):
    m, kdim = a_list[0].shape
    c0, n = (0, b_list[0].shape[-1]) if cols is None else cols
    tm = min(tm, m)
    tn = _pick_tile(n, tn)
    tk = kdim if tk is None else _pick_tile(kdim, tk)
    nk = kdim // tk
    assert m % tm == 0 and n % tn == 0 and kdim % tk == 0 and c0 % tn == 0
    jb = c0 // tn
    n_pairs = len(a_list)
    if layer is None:
        b_spec = pl.BlockSpec((tk, tn), lambda i, j, k: (k, j + jb))
    else:
        b_spec = pl.BlockSpec((None, tk, tn), lambda i, j, k: (layer, k, j + jb))
    in_specs = [pl.BlockSpec((tm, tk), lambda i, j, k: (i, k))] * n_pairs + [b_spec] * n_pairs
    scratch = [pltpu.VMEM((tm, tn), F32)] if nk > 1 else []
    return pl.pallas_call(
        functools.partial(_mm_kernel, n_pairs=n_pairs, nk=nk),
        grid=(m // tm, n // tn, nk), in_specs=in_specs,
        out_specs=pl.BlockSpec((tm, tn), lambda i, j, k: (i, j)),
        out_shape=jax.ShapeDtypeStruct((m, n), out_dtype),
        scratch_shapes=scratch,
        compiler_params=_cp("arbitrary", "arbitrary", "arbitrary"), name="matmul",
    )(*a_list, *b_list)


def _mmw_kernel(*refs, n_pairs, ranges):
    n_out = len(ranges)
    a_refs, b_refs = refs[:n_pairs], refs[n_pairs:2 * n_pairs]
    o_refs = refs[2 * n_pairs:2 * n_pairs + n_out]
    wb_refs = refs[2 * n_pairs + n_out:]
    j, i = pl.program_id(0), pl.program_id(1)

    @pl.when(i == 0)
    def _():
        for b_ref, wb_ref in zip(b_refs, wb_refs):
            wb_ref[...] = b_ref[...].astype(BF16)

    acc = _dot(a_refs[0][...], wb_refs[0][...])
    for a_ref, wb_ref in zip(a_refs[1:], wb_refs[1:]):
        acc = acc + _dot(a_ref[...], wb_ref[...])
    if n_out == 1:
        o_refs[0][...] = acc.astype(o_refs[0].dtype)
        return
    for o_ref, (lo, hi) in zip(o_refs, ranges):
        @pl.when((j >= lo) & (j < hi))
        def _(o_ref=o_ref):
            o_ref[...] = acc.astype(o_ref.dtype)


def _matmul_w32(a_list, w_list, out_dtype, out_cols=None, col0=0, tm=512, tn=1024, w_buffers=2):
    m, kdim = a_list[0].shape
    tm = min(tm, m)
    out_cols = [w_list[0].shape[1]] if out_cols is None else out_cols
    n = sum(out_cols)
    tn = _pick_tile(math.gcd(col0, *out_cols), tn)
    assert m % tm == 0
    nm = m // tm
    jb = col0 // tn
    row_blk = [p if w.shape[0] != kdim else 0 for p, w in enumerate(w_list)]
    ranges, lo = [], 0
    for c in out_cols:
        ranges.append((lo, lo + c // tn))
        lo += c // tn
    n_pairs = len(a_list)

    def out_map(lo, hi):
        def index(j, i):
            inside = (j >= lo) & (j < hi)
            ii = jnp.where(inside, i, jnp.where(j < lo, 0, nm - 1))
            return ii, jnp.clip(j - lo, 0, hi - lo - 1)
        return index

    outs = pl.pallas_call(
        functools.partial(_mmw_kernel, n_pairs=n_pairs, ranges=ranges),
        grid=(n // tn, nm),
        in_specs=([pl.BlockSpec((tm, kdim), lambda j, i: (i, 0))] * n_pairs
                  + [pl.BlockSpec((kdim, tn), lambda j, i, rb=rb: (rb, j + jb),
                                  pipeline_mode=pl.Buffered(w_buffers)) for rb in row_blk]),
        out_specs=[pl.BlockSpec((tm, tn), out_map(lo, hi)) for lo, hi in ranges],
        out_shape=[jax.ShapeDtypeStruct((m, c), out_dtype) for c in out_cols],
        scratch_shapes=[pltpu.VMEM((kdim, tn), BF16)] * n_pairs,
        compiler_params=_cp("arbitrary", "arbitrary"), name="matmul_w32",
    )(*a_list, *w_list)
    return outs[0] if len(out_cols) == 1 else outs


def _gates_kernel(h_ref, w_ref, par_ref, g_ref, f_ref, carry_ref, *, ha, hb, seg):
    @pl.when(pl.program_id(1) == 0)
    def _():
        carry_ref[...] = jnp.zeros_like(carry_ref)

    p = _dot(h_ref[...], w_ref[...])
    tr = p.shape[0]
    lane = _iota(p.shape, 1)
    a_log, bias = par_ref[0:1, :], par_ref[1:2, :]
    x = p + bias
    beta = _sigmoid(p)
    g = -jnp.exp(a_log) * _softplus(x)
    logf = -_softplus(-x)
    out = jnp.where(lane < ha, beta,
                    jnp.where(lane < 2 * ha, g, jnp.where(lane < 2 * ha + hb, logf, 0.0)))
    g_ref[...] = out
    rr, cc = _iota((tr, tr), 0), _iota((tr, tr), 1)
    tri = rr >= cc
    if seg is not None:
        tri = tri & ((rr >> _log2(seg)) == (cc >> _log2(seg)))
    cs = _dot(tri.astype(F32), out, HI) + carry_ref[0:1, :]
    f_ref[...] = cs
    carry_ref[0:1, :] = cs[tr - 1:tr, :]


def _gates(h, w_small, par, nb, t, ha, hb):
    m, d = h.shape
    if t >= 128:
        tr, per, seg, nb_grid = min(t, 512), t // min(t, 512), None, nb
    else:
        tr, per, seg, nb_grid = m, 1, t, 1
    spec = pl.BlockSpec((tr, LANE), lambda b, j: (b * per + j, 0))
    return pl.pallas_call(
        functools.partial(_gates_kernel, ha=ha, hb=hb, seg=seg),
        grid=(nb_grid, per),
        in_specs=[pl.BlockSpec((tr, d), lambda b, j: (b * per + j, 0)),
                  pl.BlockSpec((d, LANE), lambda b, j: (0, 0)),
                  pl.BlockSpec((SUBLANE, LANE), lambda b, j: (0, 0))],
        out_specs=[spec, spec],
        out_shape=[jax.ShapeDtypeStruct((m, LANE), F32)] * 2,
        scratch_shapes=[pltpu.VMEM((SUBLANE, LANE), F32)],
        compiler_params=_cp("arbitrary", "arbitrary"), name="gates",
    )(h, w_small, par)


def _each(f, *lists):
    return [f(*args) for args in zip(*lists)]


def _unit_lower_inverse(lows):
    c = lows[0].shape[0]
    assert c == 64
    r, s = _iota((c, c), 0), _iota((c, c), 1)
    eye = (r == s).astype(F32)
    same16 = (r >> 4) == (s >> 4)
    same32 = (r >> 5) == (s >> 5)
    ld = _each(lambda low: jnp.where(same16, low, 0.0), lows)
    x = _each(lambda a: eye - a, ld)
    p = _each(lambda a: _dot3(a, a), ld)
    for _ in range(2):
        xp = _each(lambda a, b: _dot3(jnp.concatenate([a, b], axis=0), b), x, p)
        x = _each(lambda a, b: a + b[:c], x, xp)
        p = _each(lambda b: b[c:], xp)
    x = _each(lambda a, b: a + _dot3(a, b), x, p)
    for keep in (same32 & jnp.logical_not(same16), jnp.logical_not(same32)):
        off = _each(lambda low: jnp.where(keep, low, 0.0), lows)
        y = _each(_dot3, x, off)
        x = _each(lambda a, b: a - _dot3(b, a), x, y)
    return x


def _delta_kernel(q_ref, k_ref, v_ref, z_ref, gt_ref, cwq_ref, cwk_ref, cwv_ref,
                  cbq_ref, cbk_ref, cbv_ref, s0_ref, gn_ref, o_ref, sfin_ref, ext_ref, s_ref,
                  *, hpb, ha, kw, valid_rows, nc):
    c_idx = pl.program_id(2)
    hblk = pl.program_id(1)
    c = q_ref.shape[0]
    hd = LANE

    @pl.when(c_idx == 0)
    def _():
        s_ref[...] = s0_ref[...]
        for i, cb in enumerate((cbq_ref, cbk_ref, cbv_ref)):
            ext_ref[i, 0:SUBLANE, :] = cb[...]

    conv = []
    for i, (r_ref, cw_ref) in enumerate(((q_ref, cwq_ref), (k_ref, cwk_ref), (v_ref, cwv_ref))):
        ext_ref[i, SUBLANE:SUBLANE + c, :] = r_ref[...]
        acc = ext_ref[i, SUBLANE:SUBLANE + c, :] * cw_ref[kw - 1:kw, :]
        for j in range(kw - 1):
            acc = acc + ext_ref[i, pl.ds(SUBLANE - (kw - 1) + j, c), :] * cw_ref[j:j + 1, :]
        ext_ref[i, 0:SUBLANE, :] = ext_ref[i, c:c + SUBLANE, :]
        acc = _silu(acc)
        if valid_rows < c:
            acc = jnp.where(_iota(acc.shape, 0) < valid_rows, acc, 0.0)
        conv.append(acc)
    xq, xk, xv = conv

    gt = gt_ref[...]
    rr, ss = _iota((c, c), 0), _iota((c, c), 1)
    incl, strict = rr >= ss, rr > ss
    gcum = _dot(incl.astype(F32), gt, HI)
    lane = _iota(gt.shape, 1)
    gn = gn_ref[...]
    heads = list(range(hpb))
    sls = [slice(j * hd, (j + 1) * hd) for j in heads]
    beta = [jnp.sum(jnp.where(lane == hblk * hpb + j, gt, 0.0), axis=1, keepdims=True) for j in heads]
    gc = [jnp.sum(jnp.where(lane == ha + hblk * hpb + j, gcum, 0.0), axis=1, keepdims=True) for j in heads]
    qh = [xq[:, sl] for sl in sls]
    qh = _each(lambda a: a * lax.rsqrt(jnp.sum(a * a, axis=-1, keepdims=True) + L2_EPS) * (hd ** -0.5), qh)
    kh = [xk[:, sl] for sl in sls]
    kh = _each(lambda a: a * lax.rsqrt(jnp.sum(a * a, axis=-1, keepdims=True) + L2_EPS), kh)
    decay = _each(lambda g: jnp.where(incl, jnp.exp(g - _col_to_row(g)), 0.0), gc)
    kb = _each(lambda a, b: a * b, kh, beta)
    vb = [xv[:, sl] * b for sl, b in zip(sls, beta)]
    egc = _each(jnp.exp, gc)
    kk = _each(lambda a, q, k: _dot_nt(jnp.concatenate([a, q], axis=0).astype(BF16), k.astype(BF16)),
               kb, qh, kh)
    low = _each(lambda a, d: jnp.where(strict, a[:c] * d, 0.0), kk, decay)
    attn = _each(lambda a, d: (a[c:] * d).astype(BF16), kk, decay)
    tmat = _unit_lower_inverse(low)
    uw = _each(lambda t, v, k, e: _dot(t.astype(BF16), jnp.concatenate([v, k * e], axis=1).astype(BF16)),
               tmat, vb, kb, egc)
    st = [s_ref[j] for j in heads]
    ws = _each(lambda a, q, e, s: _dot(jnp.concatenate([a[:, hd:], q * e], axis=0).astype(BF16),
                                       s.astype(BF16)), uw, qh, egc, st)
    v_new = _each(lambda a, b: (a[:, :hd] - b[:c]).astype(BF16), uw, ws)
    o = _each(lambda b, a, v: b[c:] + _dot(a, v), ws, attn, v_new)
    g_last = _each(lambda g: g[c - 1:c, :], gc)
    k_dec = _each(lambda k, gl, g: (k * jnp.exp(gl - g)).astype(BF16), kh, g_last, gc)
    s_new = _each(lambda s, gl, k, v: s * jnp.exp(gl) + _dot_tn(k, v), st, g_last, k_dec, v_new)
    for j in heads:
        s_ref[j] = s_new[j]
        o_ref[:, sls[j]] = (_rms(o[j], gn) * _silu(z_ref[:, sls[j]])).astype(o_ref.dtype)

    @pl.when(c_idx == nc - 1)
    def _():
        sfin_ref[...] = s_ref[...]


def _delta(proj, gates, conv_w, conv_buf8, s0, gnorm, nb, tp, valid_rows, ha, hpb=8):
    c = DELTA_CHUNK
    hd = LANE
    hpb = min(hpb, ha)
    w = hpb * hd
    nc = tp // c
    nhb = ha // hpb
    kw = conv_w.shape[0]
    cw8 = jnp.pad(conv_w, ((0, SUBLANE - kw), (0, 0)))

    def col(off):
        return pl.BlockSpec((c, w), lambda b, h, i, off=off: (b * nc + i, off * nhb + h))

    def cwspec(off):
        return pl.BlockSpec((SUBLANE, w), lambda b, h, i, off=off: (0, off * nhb + h))

    def cbspec(off):
        return pl.BlockSpec((None, SUBLANE, w), lambda b, h, i, off=off: (b, 0, off * nhb + h))

    state_spec = pl.BlockSpec((None, hpb, hd, hd), lambda b, h, i: (b, h, 0, 0))
    return pl.pallas_call(
        functools.partial(_delta_kernel, hpb=hpb, ha=ha, kw=kw, valid_rows=valid_rows, nc=nc),
        grid=(nb, nhb, nc),
        in_specs=[col(0), col(1), col(2), col(3),
                  pl.BlockSpec((c, LANE), lambda b, h, i: (b * nc + i, 0)),
                  cwspec(0), cwspec(1), cwspec(2), cbspec(0), cbspec(1), cbspec(2),
                  state_spec, pl.BlockSpec((1, hd), lambda b, h, i: (0, 0))],
        out_specs=[pl.BlockSpec((c, w), lambda b, h, i: (b * nc + i, h)), state_spec],
        out_shape=[jax.ShapeDtypeStruct((nb * tp, ha * hd), BF16),
                   jax.ShapeDtypeStruct(s0.shape, F32)],
        scratch_shapes=[pltpu.VMEM((3, c + SUBLANE, w), F32), pltpu.VMEM((hpb, hd, hd), F32)],
        compiler_params=_cp("arbitrary", "arbitrary", "arbitrary"), name="delta",
    )(proj, proj, proj, proj, gates, cw8, cw8, cw8, conv_buf8, conv_buf8, conv_buf8, s0,
      gnorm.reshape(1, hd))


def _softmax_step(carry, s, v, valid=None):
    m, l, acc = carry
    m_new = jnp.maximum(m, jnp.max(s, axis=-1, keepdims=True))
    alpha = jnp.exp(m - m_new)
    p = jnp.exp(s - m_new)
    if valid is not None:
        p = jnp.where(valid, p, 0.0)
    l = alpha * l + jnp.sum(p, axis=-1, keepdims=True)
    acc = alpha * acc + _dot(p.astype(BF16), v)
    return m_new, l, acc


def _softmax_init(rows, hd):
    return (jnp.full((rows, 1), NEG, F32), jnp.zeros((rows, 1), F32), jnp.zeros((rows, hd), F32))


def _fox_prompt_kernel(q_ref, k_ref, v_ref, f_ref, o_ref, kb_ref, vb_ref, *, tq, tk, scale, par):
    seq = q_ref.shape[0]
    kb_ref[...] = k_ref[...].astype(BF16)
    vb_ref[...] = v_ref[...].astype(BF16)
    col_minus_row = _iota((tq, tk), 1) - _iota((tq, tk), 0)
    nqb = seq // tq
    for c0 in range(0, nqb, par):
        chunk = list(range(c0, min(c0 + par, nqb)))
        qs = [(q_ref[pl.ds(qb * tq, tq), :] * scale).astype(BF16) for qb in chunk]
        fqs = [_row_to_col(f_ref[:, pl.ds(qb * tq, tq)]) for qb in chunk]
        carries = [_softmax_init(tq, LANE) for _ in chunk]
        n_steps = [-(-((qb + 1) * tq) // tk) for qb in chunk]
        for step in range(max(n_steps)):
            live = [i for i in range(len(chunk)) if step < n_steps[i]]
            k = kb_ref[pl.ds(step * tk, tk), :]
            v = vb_ref[pl.ds(step * tk, tk), :]
            fk = f_ref[:, pl.ds(step * tk, tk)]
            ss = [_dot_nt(qs[i], k) + (fqs[i] - fk) for i in live]
            for n, i in enumerate(live):
                if step * tk + tk - 1 > chunk[i] * tq:
                    ss[n] = jnp.where(col_minus_row <= chunk[i] * tq - step * tk, ss[n], NEG)
            ms = [jnp.maximum(carries[i][0], jnp.max(s, axis=-1, keepdims=True)) for i, s in zip(live, ss)]
            ps = [jnp.exp(s - m) for s, m in zip(ss, ms)]
            alphas = [jnp.exp(carries[i][0] - m) for i, m in zip(live, ms)]
            pvs = [_dot(p.astype(BF16), v) for p in ps]
            for i, m, p, a, pv in zip(live, ms, ps, alphas, pvs):
                carries[i] = (m, a * carries[i][1] + jnp.sum(p, axis=-1, keepdims=True),
                              a * carries[i][2] + pv)
        for qb, (_, l, acc) in zip(chunk, carries):
            o_ref[pl.ds(qb * tq, tq), :] = (acc / l).astype(o_ref.dtype)


def _fox_prompt(q_arr, q_col, k_arr, v_arr, f_rows, nb, s, nh, tq=128, tk=512, par=4):
    tq, tk = min(tq, s), min(tk, s)
    assert s % tq == 0 and s % tk == 0
    hd = LANE
    return pl.pallas_call(
        functools.partial(_fox_prompt_kernel, tq=tq, tk=tk, scale=hd ** -0.5, par=par),
        grid=(nb, nh),
        in_specs=[pl.BlockSpec((s, hd), lambda b, h: (b, q_col + h)),
                  pl.BlockSpec((s, hd), lambda b, h: (b, h)),
                  pl.BlockSpec((s, hd), lambda b, h: (b, h)),
                  pl.BlockSpec((None, None, 1, s), lambda b, h: (b, h, 0, 0))],
        out_specs=pl.BlockSpec((s, hd), lambda b, h: (b, h)),
        out_shape=jax.ShapeDtypeStruct((nb * s, nh * hd), BF16),
        scratch_shapes=[pltpu.VMEM((s, hd), BF16), pltpu.VMEM((s, hd), BF16)],
        compiler_params=_cp("arbitrary", "arbitrary"), name="fox_prompt",
    )(q_arr, k_arr, v_arr, f_rows)


def _fpast_kernel(pt_ref, *refs, group):
    lp_refs, o_ref, carry_ref = refs[:group], refs[group], refs[group + 1]

    @pl.when(pl.program_id(1) == 0)
    def _():
        carry_ref[...] = jnp.zeros_like(carry_ref)

    p = lp_refs[0].shape[0]
    upper = (_iota((p, p), 1) > _iota((p, p), 0)).astype(F32)
    for i in range(group):
        lp = lp_refs[i][...]
        o_ref[group - 1 - i] = -(_dot(upper, lp, HI) + carry_ref[0:1, :])
        carry_ref[0:1, :] = carry_ref[0:1, :] + jnp.sum(lp, axis=0, keepdims=True)


def _fpast(logf_pool, page_table):
    db, n_pages = page_table.shape
    _, p, nh = logf_pool.shape
    group = 8 if n_pages % 8 == 0 else 1
    nblk = n_pages // group

    def lp_spec(i):
        return pl.BlockSpec((None, p, nh),
                            lambda b, j, pt, i=i: (pt[b, n_pages - 1 - (j * group + i)], 0, 0))

    return pl.pallas_call(
        functools.partial(_fpast_kernel, group=group),
        grid_spec=pltpu.PrefetchScalarGridSpec(
            num_scalar_prefetch=1, grid=(db, nblk),
            in_specs=[lp_spec(i) for i in range(group)],
            out_specs=pl.BlockSpec((None, group, p, nh), lambda b, j, pt: (b, nblk - 1 - j, 0, 0)),
            scratch_shapes=[pltpu.VMEM((SUBLANE, nh), F32)]),
        out_shape=jax.ShapeDtypeStruct((db, n_pages, p, nh), F32),
        compiler_params=_cp("arbitrary", "arbitrary"), name="fpast",
    )(page_table, *([logf_pool] * group))


def _fox_sample_kernel(pt_ref, *refs, nh, scale, n_steps, pp):
    q_ref, k_refs, v_refs = refs[0], refs[1:1 + pp], refs[1 + pp:1 + 2 * pp]
    (fp_ref, kn_ref, vn_ref, fn_ref, o_ref, m_ref, l_ref, acc_ref, bias_ref, s0_ref,
     s1_ref) = refs[1 + 2 * pp:]
    p = pl.program_id(1)
    r = q_ref.shape[0]
    rows = k_refs[0].shape[0]
    lognh = _log2(nh)
    fn = fn_ref[...]

    @pl.when(p == 0)
    def _():
        m_ref[...] = jnp.full_like(m_ref, NEG)
        l_ref[...] = jnp.zeros_like(l_ref)
        acc_ref[...] = jnp.zeros_like(acc_ref)
        s1_ref[...] = jnp.zeros_like(s1_ref)
        match = (_iota((r, rows), 0) & (nh - 1)) == (_iota((r, rows), 1) & (nh - 1))
        bias_ref[...] = jnp.where(match, _row_to_col(fn), NEG)

    q = (q_ref[...] * scale).astype(BF16)

    def update(s, v):
        m, l, acc = _softmax_step((m_ref[...], l_ref[...], acc_ref[...]), s, v)
        m_ref[...], l_ref[...], acc_ref[...] = m, l, acc

    def step(s_w, s_r):
        for i in range(pp):
            s_w[:, i * rows:(i + 1) * rows] = (_dot_nt(q, k_refs[i][...].astype(BF16))
                                               + (bias_ref[...] - fp_ref[i]))
        live = p > 0
        s = s_r[...]
        m_old = m_ref[...]
        m_new = jnp.where(live, jnp.maximum(m_old, jnp.max(s, axis=-1, keepdims=True)), m_old)
        pr = jnp.where(live, jnp.exp(s - m_new), 0.0)
        alpha = jnp.exp(m_old - m_new)
        l_ref[...] = alpha * l_ref[...] + jnp.sum(pr, axis=-1, keepdims=True)
        pr = pr.astype(BF16)
        pv = _dot(pr[:, :rows], v_refs[0][...].astype(BF16))
        for i in range(1, pp):
            pv = pv + _dot(pr[:, i * rows:(i + 1) * rows], v_refs[i][...].astype(BF16))
        acc_ref[...] = alpha * acc_ref[...] + pv
        m_ref[...] = m_new

    @pl.when(p % 2 == 0)
    def _():
        step(s0_ref, s1_ref)

    @pl.when(p % 2 == 1)
    def _():
        step(s1_ref, s0_ref)

    @pl.when(p == n_steps)
    def _():
        rr, cc = _iota((r, r), 0), _iota((r, r), 1)
        ok = ((rr & (nh - 1)) == (cc & (nh - 1))) & ((cc >> lognh) <= (rr >> lognh))
        s2 = _dot_nt(q, kn_ref[...].astype(BF16)) + _row_to_col(fn) - fn
        update(jnp.where(ok, s2, NEG), vn_ref[...].astype(BF16))
        o_ref[...] = (acc_ref[...] / l_ref[...]).astype(o_ref.dtype)


def _fox_sample(q, k_new, v_new, f_new, k_pool, v_pool, f_past, page_table, nh):
    db, r, hd = q.shape
    n_pages = page_table.shape[1]
    rows = k_pool.shape[1]
    tok = pl.BlockSpec((None, r, hd), lambda b, p, pt: (b, 0, 0))
    pp = 2 if n_pages % 2 == 0 else 1
    n_steps = n_pages // pp
    last = n_steps - 1
    k_pages = [pl.BlockSpec((None, rows, hd),
                            lambda b, p, pt, i=i: (pt[b, jnp.minimum(p, last) * pp + i], 0, 0))
               for i in range(pp)]
    v_pages = [pl.BlockSpec((None, rows, hd),
                            lambda b, p, pt, i=i: (pt[b, jnp.maximum(p - 1, 0) * pp + i], 0, 0))
               for i in range(pp)]
    return pl.pallas_call(
        functools.partial(_fox_sample_kernel, nh=nh, scale=hd ** -0.5, n_steps=n_steps, pp=pp),
        grid_spec=pltpu.PrefetchScalarGridSpec(
            num_scalar_prefetch=1, grid=(db, n_steps + 1),
            in_specs=[tok] + k_pages + v_pages + [
                pl.BlockSpec((None, pp, 1, rows), lambda b, p, pt: (b, jnp.minimum(p, last), 0, 0)),
                tok, tok, pl.BlockSpec((None, 1, r), lambda b, p, pt: (b, 0, 0))],
            out_specs=tok,
            scratch_shapes=[pltpu.VMEM((r, 1), F32), pltpu.VMEM((r, 1), F32),
                            pltpu.VMEM((r, hd), F32), pltpu.VMEM((r, rows), F32),
                            pltpu.VMEM((r, pp * rows), F32), pltpu.VMEM((r, pp * rows), F32)]),
        out_shape=jax.ShapeDtypeStruct((db, r, hd), BF16),
        compiler_params=_cp("arbitrary", "arbitrary"), name="fox_sample",
    )(page_table, q, *([k_pool] * pp), *([v_pool] * pp), f_past, k_new, v_new, f_new)


def _dil_prompt_kernel(*refs, groups, tb, scale, par):
    ng = len(groups)
    q_refs, k_refs, v_refs = refs[:ng], refs[ng:2 * ng], refs[2 * ng:3 * ng]
    o_ref, m_ref, l_ref, acc_ref = refs[3 * ng:]
    seq = q_refs[0].shape[0]
    diff = _iota((tb, tb), 0) - _iota((tb, tb), 1)

    def rows_of(res, blk, dil):
        return pl.ds(res + blk * tb * dil, tb, stride=dil) if dil > 1 else pl.ds(blk * tb, tb)

    for g, (window, dil) in enumerate(groups):
        reach = window // dil
        nqb = seq // dil // tb
        back = -(-reach // tb)
        units = [(res, qb) for res in range(dil) for qb in range(nqb)]
        for c0 in range(0, len(units), par):
            chunk = units[c0:c0 + par]
            rows = [rows_of(res, qb, dil) for res, qb in chunk]
            qs = [(q_refs[g][rw, :] * scale).astype(BF16) for rw in rows]
            if g == 0:
                carries = [_softmax_init(tb, LANE) for _ in chunk]
            else:
                carries = [(m_ref[rw, :], l_ref[rw, :], acc_ref[rw, :]) for rw in rows]
            for step in range(back + 1):
                live = [i for i, (_, qb) in enumerate(chunk) if qb - step >= 0]
                if not live:
                    continue
                krows = [rows_of(chunk[i][0], chunk[i][1] - step, dil) for i in live]
                ks = [k_refs[g][rw, :].astype(BF16) for rw in krows]
                vs = [v_refs[g][rw, :].astype(BF16) for rw in krows]
                ss = [_dot_nt(qs[i], k) for i, k in zip(live, ks)]
                if step * tb - (tb - 1) < 0 or step * tb + (tb - 1) > reach:
                    dist = diff + step * tb
                    valid = (dist >= 0) & (dist <= reach)
                    ss = [jnp.where(valid, s, NEG) for s in ss]
                ms = [jnp.maximum(carries[i][0], jnp.max(s, axis=-1, keepdims=True)) for i, s in zip(live, ss)]
                ps = [jnp.exp(s - m) for s, m in zip(ss, ms)]
                alphas = [jnp.exp(carries[i][0] - m) for i, m in zip(live, ms)]
                pvs = [_dot(p.astype(BF16), v) for p, v in zip(ps, vs)]
                for i, m, p, a, pv in zip(live, ms, ps, alphas, pvs):
                    carries[i] = (m, a * carries[i][1] + jnp.sum(p, axis=-1, keepdims=True),
                                  a * carries[i][2] + pv)
            for rw, (m, l, acc) in zip(rows, carries):
                m_ref[rw, :], l_ref[rw, :], acc_ref[rw, :] = m, l, acc
    o_ref[...] = (acc_ref[...] / l_ref[...]).astype(o_ref.dtype)


def _dil_prompt(srcs, nb, s, nh, groups, tb=128, par=4):
    hd = LANE
    ng = len(groups)
    tb = min(tb, s)
    for window, dil in groups:
        assert s % (dil * tb) == 0 and window % dil == 0
    args, specs = [], []
    for which in range(3):
        for g in range(ng):
            arr, col = srcs[g][which]
            args.append(arr)
            specs.append(pl.BlockSpec((s, hd), lambda b, h, col=col: (b, col + h)))
    return pl.pallas_call(
        functools.partial(_dil_prompt_kernel, groups=groups, tb=tb, scale=hd ** -0.5, par=par),
        grid=(nb, nh),
        in_specs=specs,
        out_specs=pl.BlockSpec((s, hd), lambda b, h: (b, h)),
        out_shape=jax.ShapeDtypeStruct((nb * s, nh * hd), BF16),
        scratch_shapes=[pltpu.VMEM((s, 1), F32), pltpu.VMEM((s, 1), F32), pltpu.VMEM((s, hd), F32)],
        compiler_params=_cp("arbitrary", "arbitrary"), name="dil_prompt",
    )(*args)


def _dil_sample_kernel(*refs, groups, wbs, tb, nh, scale, t):
    ng = len(groups)
    q_refs, kb_refs, vb_refs = refs[:ng], refs[ng:2 * ng], refs[2 * ng:3 * ng]
    kn_refs, vn_refs = refs[3 * ng:4 * ng], refs[4 * ng:5 * ng]
    o_ref, m_ref, l_ref, acc_ref = refs[5 * ng:5 * ng + 4]
    j = pl.program_id(1)
    r = q_refs[0].shape[0]
    rows = tb * nh
    lognh = _log2(nh)
    nblks = [wb // tb for wb in wbs]
    total = sum(nblks)

    @pl.when(j == 0)
    def _():
        m_ref[...] = jnp.full_like(m_ref, NEG)
        l_ref[...] = jnp.zeros_like(l_ref)
        acc_ref[...] = jnp.zeros_like(acc_ref)

    def update(s, v, valid):
        m, l, acc = _softmax_step((m_ref[...], l_ref[...], acc_ref[...]),
                                  jnp.where(valid, s, NEG), v, valid)
        m_ref[...], l_ref[...], acc_ref[...] = m, l, acc

    start = 0
    for g, (window, dil) in enumerate(groups):
        def block(g=g, window=window, dil=dil, start=start):
            rr, cc = _iota((r, rows), 0), _iota((r, rows), 1)
            key_tok = (j - start) * tb + (cc >> lognh)
            delta = wbs[g] + (rr >> lognh) - key_tok
            valid = (((rr & (nh - 1)) == (cc & (nh - 1))) & ((delta & (dil - 1)) == 0)
                     & (delta <= window))
            q = (q_refs[g][...] * scale).astype(BF16)
            update(_dot_nt(q, kb_refs[g][...].astype(BF16)), vb_refs[g][...].astype(BF16), valid)

        pl.when((j >= start) & (j < start + nblks[g]))(block)
        start += nblks[g]

    @pl.when(j == total - 1)
    def _():
        rr, cc = _iota((r, r), 0), _iota((r, r), 1)
        delta = (rr >> lognh) - (cc >> lognh)
        match = (rr & (nh - 1)) == (cc & (nh - 1))
        for g, (window, dil) in enumerate(groups):
            valid = match & (delta >= 0) & ((delta & (dil - 1)) == 0) & (delta <= window)
            q = (q_refs[g][...] * scale).astype(BF16)
            update(_dot_nt(q, kn_refs[g][...].astype(BF16)), vn_refs[g][...].astype(BF16), valid)
        o_ref[...] = (acc_ref[...] / l_ref[...]).astype(o_ref.dtype)


def _dil_sample(qs, kns, vns, kbufs, vbufs, groups, nh, t):
    db, r, hd = qs[0].shape
    ng = len(groups)
    wbs = [kb.shape[1] // nh for kb in kbufs]
    tb = 128
    while any(wb % tb for wb in wbs):
        tb //= 2
    nblks = [wb // tb for wb in wbs]
    starts = [sum(nblks[:g]) for g in range(ng)]
    total = sum(nblks)
    tok = pl.BlockSpec((None, r, hd), lambda b, j: (b, 0, 0))

    def buf_spec(g):
        return pl.BlockSpec(
            (None, tb * nh, hd),
            lambda b, j, g=g: (b, jnp.clip(j - starts[g], 0, nblks[g] - 1), 0))

    bspecs = [buf_spec(g) for g in range(ng)]
    return pl.pallas_call(
        functools.partial(_dil_sample_kernel, groups=groups, wbs=wbs, tb=tb, nh=nh,
                          scale=hd ** -0.5, t=t),
        grid=(db, total),
        in_specs=[tok] * ng + bspecs + bspecs + [tok] * (2 * ng),
        out_specs=tok,
        out_shape=jax.ShapeDtypeStruct((db, r, hd), BF16),
        scratch_shapes=[pltpu.VMEM((r, 1), F32), pltpu.VMEM((r, 1), F32), pltpu.VMEM((r, hd), F32)],
        compiler_params=_cp("arbitrary", "arbitrary"), name="dil_sample",
    )(*qs, *kbufs, *vbufs, *kns, *vns)


def _ffn_up_kernel(*refs, carry_rows, kw, tiles_per_batch, t):
    h_ref, wu_ref, wg_ref, cw_ref = refs[:4]
    nh = kw - 1
    if carry_rows:
        act_ref, tail_ref, ext_ref, wub_ref, wgb_ref = refs[4:]
    else:
        halo_refs = refs[4:4 + nh]
        act_ref, tail_ref, ext_ref, wub_ref, wgb_ref = refs[4 + nh:]
    n, m = pl.program_id(0), pl.program_id(1)

    @pl.when(m == 0)
    def _():
        wub_ref[...] = wu_ref[...].astype(BF16)
        wgb_ref[...] = wg_ref[...].astype(BF16)

    h = h_ref[...]
    tm = h.shape[0]
    up = _dot(h, wub_ref[...])
    gate = _dot(h, wgb_ref[...])
    tn = gate.shape[1]
    if carry_rows:
        @pl.when((m % tiles_per_batch) == 0)
        def _():
            ext_ref[0:SUBLANE, :] = jnp.zeros((SUBLANE, tn), F32)

        @pl.when((m % tiles_per_batch) != 0)
        def _():
            ext_ref[0:SUBLANE, :] = ext_ref[tm:tm + SUBLANE, :]

        ext_ref[SUBLANE:SUBLANE + tm, :] = gate
        tail_ref[...] = gate[tm - SUBLANE:, :]
    else:
        ext_ref[0:SUBLANE, :] = jnp.zeros((SUBLANE, tn), F32)
        ext_ref[SUBLANE:SUBLANE + tm, :] = gate
        tail_ref[...] = gate
        pos = _iota(gate.shape, 0) & (t - 1)
    gc = gate * cw_ref[kw - 1:kw, :]
    for j in range(kw - 1):
        shift = kw - 1 - j
        sh = ext_ref[pl.ds(SUBLANE - shift, tm), :]
        if not carry_rows:
            sh = jnp.where(pos < shift, halo_refs[shift - 1][...], sh)
        gc = gc + sh * cw_ref[j:j + 1, :]
    act_ref[...] = (_gelu_tanh(gc) * up).astype(act_ref.dtype)


def _ffn_up(h, wu, wg, layer, cw8, kw, rows_per_batch, halos=None, tm=512, tn=FFN_TILE):
    m, d = h.shape
    n_pad = wu.shape[-1]
    carry_rows = halos is None
    tm = min(tm, rows_per_batch) if carry_rows else m
    tn = min(tn, -(-n_pad // LANE) * LANE)
    nm, nn = m // tm, -(-n_pad // tn)
    in_specs = [pl.BlockSpec((tm, d), lambda j, i: (i, 0)),
                pl.BlockSpec((None, d, tn), lambda j, i: (layer, 0, j)),
                pl.BlockSpec((None, d, tn), lambda j, i: (layer, 0, j)),
                pl.BlockSpec((SUBLANE, tn), lambda j, i: (0, j))]
    args = [h, wu, wg, cw8]
    if carry_rows:
        tail_spec = pl.BlockSpec((None, SUBLANE, tn), lambda j, i: (i, 0, j))
        tail_shape = jax.ShapeDtypeStruct((nm, SUBLANE, n_pad), F32)
    else:
        _log2(rows_per_batch)
        in_specs += [pl.BlockSpec((tm, tn), lambda j, i: (i, j))] * len(halos)
        args += list(halos)
        tail_spec = pl.BlockSpec((tm, tn), lambda j, i: (i, j))
        tail_shape = jax.ShapeDtypeStruct((m, n_pad), F32)
    return pl.pallas_call(
        functools.partial(_ffn_up_kernel, carry_rows=carry_rows, kw=kw,
                          tiles_per_batch=max(rows_per_batch // tm, 1), t=rows_per_batch),
        grid=(nn, nm), in_specs=in_specs,
        out_specs=[pl.BlockSpec((tm, tn), lambda j, i: (i, j)), tail_spec],
        out_shape=[jax.ShapeDtypeStruct((m, n_pad), BF16), tail_shape],
        scratch_shapes=[pltpu.VMEM((tm + SUBLANE, tn), F32), pltpu.VMEM((d, tn), BF16),
                        pltpu.VMEM((d, tn), BF16)],
        compiler_params=_cp("arbitrary", "arbitrary"), name="ffn_up",
    )(*args)


def _even_weights(w_in, conv_w, a_log, dt_bias, b_f, w_out, ha, hb, hd):
    wa, wb = ha * hd, hb * hd
    o2 = 4 * wa
    o4 = o2 + 2 * ha
    o5 = o4 + 3 * wb
    w_main = (w_in, w_in[:, o4:o5])
    small = jnp.concatenate([w_in[:, o2:o4], w_in[:, o5:]], axis=1)
    assert small.shape[1] <= LANE
    w_small = jnp.pad(small, ((0, 0), (0, LANE - small.shape[1]))).astype(BF16)
    par = jnp.zeros((SUBLANE, LANE), F32)
    par = par.at[0, ha:2 * ha].set(a_log.astype(F32))
    par = par.at[1, ha:2 * ha].set(dt_bias.astype(F32))
    par = par.at[1, 2 * ha:2 * ha + hb].set(b_f.astype(F32))
    return w_main, w_small, par, conv_w, w_out


def _even_mixer(h, nb, t, wts, gnorm, conv_buf, s0, ha, hb, hd, past=None):
    w_main, w_small, par, conv_w, w_out = wts
    wa, wb = ha * hd, hb * hd
    kw = conv_w.shape[0]
    w_in, w_fox = w_main
    proj = _matmul_w32([h], [w_in], F32, out_cols=[4 * wa], tm=min(1024, nb * t), w_buffers=1)
    q_fox, k_new, v_new = _matmul_w32([h], [w_fox], F32, out_cols=[wb, wb, wb], tm=min(512, nb * t),
                                      w_buffers=1)
    gates, fcum = _gates(h, w_small, par, nb, t, ha, hb)

    c = DELTA_CHUNK
    tp = -(-t // c) * c
    conv_buf8 = jnp.pad(conv_buf.astype(F32), ((0, 0), (SUBLANE - (kw - 1), 0), (0, 0)))
    if tp == t:
        proj_d, gates_d = proj, gates
    else:
        pad = lambda a: jnp.pad(a.reshape(nb, t, -1), ((0, 0), (0, tp - t), (0, 0))).reshape(nb * tp, -1)
        proj_d, gates_d = pad(proj), pad(gates)
    o_a, s_new = _delta(proj_d, gates_d, conv_w, conv_buf8, s0, gnorm, nb, tp, min(t, c) if tp != t else c,
                        ha)
    if tp != t:
        o_a = o_a.reshape(nb, tp, wa)[:, :t].reshape(nb * t, wa)

    logf = gates[:, 2 * ha:2 * ha + hb].reshape(nb, t, hb)
    f_new = fcum[:, 2 * ha:2 * ha + hb].reshape(nb, t, hb)
    kb, vb = k_new.reshape(nb, t, hb, hd), v_new.reshape(nb, t, hb, hd)
    if past is None:
        f_rows = jnp.swapaxes(f_new, 1, 2).reshape(nb, hb, 1, t)
        o_b = _fox_prompt(q_fox, 0, k_new, v_new, f_rows, nb, t, hb)
    else:
        k_pool, v_pool, logf_pool, page_table = past
        n_pool, page = k_pool.shape[0], k_pool.shape[1]
        f_past = _fpast(logf_pool.astype(F32), page_table)
        f_past = f_past.reshape(nb, page_table.shape[1], 1, page * hb)
        o_b = _fox_sample(q_fox.reshape(nb, t * hb, hd), k_new.reshape(nb, t * hb, hd),
                          v_new.reshape(nb, t * hb, hd), f_new.reshape(nb, 1, t * hb),
                          k_pool.reshape(n_pool, page * hb, hd), v_pool.reshape(n_pool, page * hb, hd),
                          f_past, page_table, hb)
        o_b = o_b.reshape(nb * t, wb)
    out = _matmul_w32([o_a, o_b], [w_out, w_out], BF16, tm=min(512, nb * t))
    keep = min(kw - 1, t)
    raw_tail = proj.reshape(nb, t, -1)[:, t - keep:, :3 * wa]
    new_conv = jnp.concatenate([conv_buf.astype(F32), raw_tail], axis=1)[:, -(kw - 1):]
    return out, s_new, new_conv, kb, vb, logf


def _odd_mixer(h, nb, t, w_in, w_out, nh, hd, bufs=None):
    ng = len(C_GROUPS)
    wc = nh * hd
    new_bufs = []
    if bufs is None:
        n_main = (3 * ng - 2) * wc
        main = _matmul_w32([h], [w_in], F32, out_cols=[n_main], tm=min(1024, nb * t), w_buffers=1)
        k_last, v_last = _matmul_w32([h], [w_in], F32, out_cols=[wc, wc], col0=n_main,
                                     tm=min(512, nb * t), w_buffers=1)
        srcs = [((main, g * 3 * nh), (main, g * 3 * nh + nh), (main, g * 3 * nh + 2 * nh))
                for g in range(ng - 1)]
        srcs.append(((main, (ng - 1) * 3 * nh), (k_last, 0), (v_last, 0)))
        o = _dil_prompt(srcs, nb, t, nh, C_GROUPS)
        m3 = main.reshape(nb, t, -1)
        for g, (window, _) in enumerate(C_GROUPS):
            keep = min(window, t)
            if g < ng - 1:
                kv = [m3[:, t - keep:, (g * 3 + r) * wc:(g * 3 + r + 1) * wc] for r in (1, 2)]
            else:
                kv = [a.reshape(nb, t, wc)[:, t - keep:] for a in (k_last, v_last)]
            new_bufs.append(tuple(a.reshape(nb, keep, nh, hd) for a in kv))
        return _matmul_w32([o], [w_out], BF16, tm=min(512, nb * t)), new_bufs
    proj = _matmul_w32([h], [w_in], F32, tm=min(512, nb * t), w_buffers=1)
    p6 = proj.reshape(nb, t, ng, 3, nh, hd)
    flat = lambda a: a.reshape(nb, -1, hd)
    qs = [flat(p6[:, :, g, 0]) for g in range(ng)]
    kns = [flat(p6[:, :, g, 1]) for g in range(ng)]
    vns = [flat(p6[:, :, g, 2]) for g in range(ng)]
    o = _dil_sample(qs, kns, vns, [flat(bk) for bk, _ in bufs], [flat(bv) for _, bv in bufs],
                    C_GROUPS, nh, t)
    o = o.reshape(nb * t, nh * hd)
    for g in range(ng):
        bk, bv = bufs[g]
        wb = bk.shape[1]
        new_bufs.append((jnp.concatenate([bk, p6[:, :, g, 1]], axis=1)[:, -wb:],
                         jnp.concatenate([bv, p6[:, :, g, 2]], axis=1)[:, -wb:]))
    return _matmul_w32([o], [w_out], BF16, tm=min(512, nb * t)), new_bufs


def _ffn_weights(w_up, w_gate, conv_w, w_down):
    d_ff = w_up.shape[-1]
    kw = conv_w.shape[1]
    cw8 = jnp.pad(conv_w.astype(F32), ((0, 0), (0, SUBLANE - kw), (0, 0)))
    return w_up, w_gate, cw8, w_down.astype(BF16), kw, d_ff


def _conv_ffn(h, nb, t, wts, layer, buf=None):
    wu, wg, cw8, wd, kw, d_ff = wts
    cw8 = cw8[layer]
    if buf is None:
        act, tail = _ffn_up(h, wu, wg, layer, cw8, kw, t)
        tiles = tail.shape[0] // nb
        gate_tail = tail.reshape(nb, tiles, SUBLANE, d_ff)[:, -1]
        new_buf = gate_tail[:, SUBLANE - (kw - 1):]
    else:
        halos = []
        for shift in range(1, kw):
            hl = jnp.zeros((nb, t, d_ff), F32)
            hl = hl.at[:, :shift].set(buf.astype(F32)[:, (kw - 1) - shift:])
            halos.append(hl.reshape(nb * t, d_ff))
        act, gate = _ffn_up(h, wu, wg, layer, cw8, kw, t, halos=halos)
        new_buf = jnp.concatenate([buf.astype(F32), gate.reshape(nb, t, d_ff)], axis=1)[:, -(kw - 1):]
    out = _matmul([act], [wd], BF16, tm=min(512, nb * t), tn=512, layer=layer)
    return out, new_buf


def kernel(x_prompt, x_sample, c_prompt, c_sample, state_delta, state_conv_qkv, cache_k, cache_v, cache_logf, cache_win_k0, cache_win_v0, cache_win_k1, cache_win_v1, cache_win_k2, cache_win_v2, state_ffn_conv, page_table, w_mod, b_mod, norm_pre_mix, norm_post_mix, norm_pre_ffn, norm_post_ffn, w_in_e, conv_a, a_log, dt_bias, gnorm_a, b_forget, w_out_e, w_in_o, w_out_o, w_up, w_gate, conv_ffn_w, w_down):
    bsz, seq, d = x_prompt.shape
    dbsz, dseq, _ = x_sample.shape
    depth = w_mod.shape[0]
    hd = gnorm_a.shape[-1]
    ha, hb, hc = a_log.shape[-1], b_forget.shape[-1], cache_win_k0.shape[3]
    assert hd == LANE and seq % DELTA_CHUNK == 0
    kw_a = conv_a.shape[1]
    win_caches = ((cache_win_k0, cache_win_v0), (cache_win_k1, cache_win_v1), (cache_win_k2, cache_win_v2))

    c_all = jnp.concatenate([c_prompt, c_sample], axis=0)
    c_all = jnp.pad(c_all, ((0, MOD_ROWS - c_all.shape[0]), (0, 0)))
    mod = _mod(c_all, w_mod, b_mod)
    mods = [(mod[l, :bsz].reshape(bsz, 6, d), mod[l, bsz:bsz + dbsz].reshape(dbsz, 6, d))
            for l in range(depth)]

    groups = [dict(x=x_prompt.reshape(bsz * seq, d), nb=bsz, t=seq, gi=0),
              dict(x=x_sample.reshape(dbsz * dseq, d), nb=dbsz, t=dseq, gi=1)]
    for gr in groups:
        m_0 = mods[0][gr["gi"]]
        _, gr["h"] = _post_pre(gr["x"], gr["nb"], gr["t"], shift=m_0[:, 0], scale=m_0[:, 1],
                               w_pre=norm_pre_mix[0])

    fwts = _ffn_weights(w_up, w_gate, conv_ffn_w, w_down)
    outs = {k: ([], []) for k in ("delta", "conv", "k", "v", "logf", "ffn")}
    wk = [([], []) for _ in C_GROUPS]
    wv = [([], []) for _ in C_GROUPS]
    for l in range(depth):
        if l % 2 == 0:
            e = l // 2
            ewts = _even_weights(w_in_e[e], conv_a[e], a_log[e], dt_bias[e], b_forget[e], w_out_e[e],
                                 ha, hb, hd)
        else:
            o = l // 2
            w_in_o_b, w_out_o_b = w_in_o[o], w_out_o[o]
        for gr in groups:
            gi, nb, t = gr["gi"], gr["nb"], gr["t"]
            m_l = mods[l][gi]
            if l % 2 == 0:
                if gi == 0:
                    conv_buf = jnp.zeros((nb, kw_a - 1, 3 * ha * hd), F32)
                    s0 = jnp.zeros((nb, ha, hd, hd), F32)
                    past = None
                else:
                    conv_buf, s0 = state_conv_qkv[e], state_delta[e]
                    past = (cache_k[e], cache_v[e], cache_logf[e], page_table)
                om, s_new, new_conv, kb, vb, logf = _even_mixer(
                    gr["h"], nb, t, ewts, gnorm_a[e], conv_buf, s0, ha, hb, hd, past)
                for key, val in (("delta", s_new), ("conv", new_conv), ("k", kb), ("v", vb), ("logf", logf)):
                    outs[key][gi].append(val)
            else:
                bufs = None if gi == 0 else tuple((bk[o], bv[o]) for bk, bv in win_caches)
                om, new_bufs = _odd_mixer(gr["h"], nb, t, w_in_o_b, w_out_o_b, hc, hd, bufs)
                for g in range(len(C_GROUPS)):
                    wk[g][gi].append(new_bufs[g][0])
                    wv[g][gi].append(new_bufs[g][1])
            gr["x"], gr["h"] = _post_pre(gr["x"], nb, t, o=om, gate=m_l[:, 2], w_post=norm_post_mix[l],
                                         shift=m_l[:, 3], scale=m_l[:, 4], w_pre=norm_pre_ffn[l])
            of, new_buf = _conv_ffn(gr["h"], nb, t, fwts, l, None if gi == 0 else state_ffn_conv[l])
            outs["ffn"][gi].append(new_buf)
            if l + 1 < depth:
                m_n = mods[l + 1][gi]
                gr["x"], gr["h"] = _post_pre(gr["x"], nb, t, o=of, gate=m_l[:, 5], w_post=norm_post_ffn[l],
                                             shift=m_n[:, 0], scale=m_n[:, 1], w_pre=norm_pre_mix[l + 1])
            else:
                gr["x"], _ = _post_pre(gr["x"], nb, t, o=of, gate=m_l[:, 5], w_post=norm_post_ffn[l])

    stk = lambda lst: jnp.stack(lst, axis=0)
    res = [groups[0]["x"].reshape(bsz, seq, d), groups[1]["x"].reshape(dbsz, dseq, d)]
    for key in ("delta", "conv", "k", "v", "logf"):
        res += [stk(outs[key][0]), stk(outs[key][1])]
    for g in range(len(C_GROUPS)):
        res += [stk(wk[g][0]), stk(wk[g][1]), stk(wv[g][0]), stk(wv[g][1])]
    res += [stk(outs["ffn"][0]), stk(outs["ffn"][1])]
    return tuple(res)
```

```python
import functools
import math

import jax
import jax.numpy as jnp
from jax import lax
from jax.experimental import pallas as pl
from jax.experimental.pallas import tpu as pltpu

F32 = jnp.float32
BF16 = jnp.bfloat16
HI = lax.Precision.HIGHEST

C_GROUPS = ((128, 1), (512, 4), (2048, 16))
DELTA_CHUNK = 64
RMS_EPS = 1e-6
L2_EPS = 1e-6
NEG = -1e30
LANE = 128
SUBLANE = 8
VMEM_LIMIT = 58 * 1024 * 1024
MOD_ROWS = 16
FFN_TILE = 512


def _cp(*sem):
    return pltpu.CompilerParams(dimension_semantics=sem, vmem_limit_bytes=VMEM_LIMIT)


def _dot(a, b, prec=None):
    return jnp.dot(a, b, preferred_element_type=F32, precision=prec)


def _dot_nt(a, b, prec=None):
    return lax.dot_general(a, b, (((1,), (1,)), ((), ())), preferred_element_type=F32, precision=prec)


def _dot_tn(a, b, prec=None):
    return lax.dot_general(a, b, (((0,), (0,)), ((), ())), preferred_element_type=F32, precision=prec)


def _dot3(a, b):
    ah, bh = a.astype(BF16), b.astype(BF16)
    al = (a - ah.astype(F32)).astype(BF16)
    bl = (b - bh.astype(F32)).astype(BF16)
    return _dot(ah, bh) + (_dot(ah, bl) + _dot(al, bh))


def _sigmoid(x):
    return 1.0 / (1.0 + jnp.exp(-x))


def _silu(x):
    return x * _sigmoid(x)


def _softplus(x):
    return jnp.maximum(x, 0.0) + jnp.log1p(jnp.exp(-jnp.abs(x)))


def _gelu_tanh(x):
    return 0.5 * x * (1.0 + jnp.tanh(0.7978845608028654 * (x + 0.044715 * (x * x * x))))


def _rms(x, w):
    return x * lax.rsqrt(jnp.mean(x * x, axis=-1, keepdims=True) + RMS_EPS) * w


def _iota(shape, dim):
    return lax.broadcasted_iota(jnp.int32, shape, dim)


def _row_to_col(row):
    n = row.shape[1]
    eye = _iota((n, n), 0) == _iota((n, n), 1)
    return jnp.sum(jnp.where(eye, jnp.broadcast_to(row, (n, n)), 0.0), axis=1, keepdims=True)


def _col_to_row(col):
    n = col.shape[0]
    eye = _iota((n, n), 0) == _iota((n, n), 1)
    return jnp.sum(jnp.where(eye, jnp.broadcast_to(col, (n, n)), 0.0), axis=0, keepdims=True)


def _pick_tile(n, pref):
    if n <= pref:
        return n
    t = (pref // LANE) * LANE
    while t >= LANE:
        if n % t == 0:
            return t
        t -= LANE
    return n


def _log2(n):
    assert n > 0 and n & (n - 1) == 0, n
    return n.bit_length() - 1


def _mod_kernel(c_ref, w_ref, b_ref, o_ref):
    a = _silu(c_ref[...]).astype(BF16)
    o_ref[...] = _dot(a, w_ref[...].astype(BF16)) + b_ref[...]


def _mod(c_all, w_mod, b_mod):
    depth, d, n = w_mod.shape
    r = c_all.shape[0]
    tn = _pick_tile(n, 512)
    return pl.pallas_call(
        _mod_kernel,
        grid=(depth, n // tn),
        in_specs=[pl.BlockSpec((r, d), lambda l, j: (0, 0)),
                  pl.BlockSpec((None, d, tn), lambda l, j: (l, 0, j)),
                  pl.BlockSpec((None, 1, tn), lambda l, j: (l, 0, j))],
        out_specs=pl.BlockSpec((None, r, tn), lambda l, j: (l, 0, j)),
        out_shape=jax.ShapeDtypeStruct((depth, r, n), F32),
        compiler_params=_cp("arbitrary", "arbitrary"),
        name="mod",
    )(c_all, w_mod, b_mod.reshape(depth, 1, n))


def _post_pre_kernel(*refs, has_o, has_h):
    refs = list(refs)
    x = refs.pop(0)[...]
    if has_o:
        o_ref, gate_ref, wpost_ref = refs.pop(0), refs.pop(0), refs.pop(0)
    if has_h:
        shift_ref, scale_ref, wpre_ref = refs.pop(0), refs.pop(0), refs.pop(0)
    if has_o:
        x = x + gate_ref[...] * _rms(o_ref[...].astype(F32), wpost_ref[...])
        refs.pop(0)[...] = x
    if has_h:
        h = _rms(x, wpre_ref[...]) * (1.0 + scale_ref[...]) + shift_ref[...]
        refs.pop(0)[...] = h.astype(BF16)


def _post_pre(x, nb, t, o=None, gate=None, w_post=None, shift=None, scale=None, w_pre=None):
    m, d = x.shape
    has_o, has_h = o is not None, shift is not None
    if t >= 128:
        tr = 128
        per = t // tr
        vec = lambda v: v.reshape(nb, 1, d)
        vec_spec = pl.BlockSpec((None, 1, d), lambda i: (i // per, 0, 0))
    else:
        tr = m
        vec = lambda v: jnp.repeat(v, t, axis=0).reshape(1, m, d)
        vec_spec = pl.BlockSpec((None, m, d), lambda i: (0, 0, 0))
    row_spec = pl.BlockSpec((tr, d), lambda i: (i, 0))
    w_spec = pl.BlockSpec((1, d), lambda i: (0, 0))
    args, in_specs, out_specs, out_shape = [x], [row_spec], [], []
    if has_o:
        args += [o, vec(gate), w_post.reshape(1, d)]
        in_specs += [row_spec, vec_spec, w_spec]
        out_specs.append(row_spec)
        out_shape.append(jax.ShapeDtypeStruct((m, d), F32))
    if has_h:
        args += [vec(shift), vec(scale), w_pre.reshape(1, d)]
        in_specs += [vec_spec, vec_spec, w_spec]
        out_specs.append(row_spec)
        out_shape.append(jax.ShapeDtypeStruct((m, d), BF16))
    outs = pl.pallas_call(
        functools.partial(_post_pre_kernel, has_o=has_o, has_h=has_h),
        grid=(m // tr,), in_specs=in_specs, out_specs=out_specs, out_shape=out_shape,
        compiler_params=_cp("arbitrary"), name="post_pre",
    )(*args)
    outs = list(outs)
    x_new = outs.pop(0) if has_o else x
    h = outs.pop(0) if has_h else None
    return x_new, h


def _mm_kernel(*refs, n_pairs, nk):
    a_refs, b_refs = refs[:n_pairs], refs[n_pairs:2 * n_pairs]
    o_ref = refs[2 * n_pairs]
    part = _dot(a_refs[0][...], b_refs[0][...])
    for a_ref, b_ref in zip(a_refs[1:], b_refs[1:]):
        part = part + _dot(a_ref[...], b_ref[...])
    if nk == 1:
        o_ref[...] = part.astype(o_ref.dtype)
        return
    acc_ref = refs[2 * n_pairs + 1]
    k = pl.program_id(2)

    @pl.when(k == 0)
    def _():
        acc_ref[...] = part

    @pl.when(k > 0)
    def _():
        acc_ref[...] += part

    @pl.when(k == nk - 1)
    def _():
        o_ref[...] = acc_ref[...].astype(o_ref.dtype)


def _matmul(a_list, b_list, out_dtype, tm=1024, tn=1024, tk=None, cols=None, layer=None):
    m, kdim = a_list[0].shape
    c0, n = (0, b_list[0].shape[-1]) if cols is None else cols
    tm = min(tm, m)
    tn = _pick_tile(n, tn)
    tk = kdim if tk is None else _pick_tile(kdim, tk)
    nk = kdim // tk
    assert m % tm == 0 and n % tn == 0 and kdim % tk == 0 and c0 % tn == 0
    jb = c0 // tn
    n_pairs = len(a_list)
    if layer is None:
        b_spec = pl.BlockSpec((tk, tn), lambda i, j, k: (k, j + jb))
    else:
        b_spec = pl.BlockSpec((None, tk, tn), lambda i, j, k: (layer, k, j + jb))
    in_specs = [pl.BlockSpec((tm, tk), lambda i, j, k: (i, k))] * n_pairs + [b_spec] * n_pairs
    scratch = [pltpu.VMEM((tm, tn), F32)] if nk > 1 else []
    return pl.pallas_call(
        functools.partial(_mm_kernel, n_pairs=n_pairs, nk=nk),
        grid=(m // tm, n // tn, nk), in_specs=in_specs,
        out_specs=pl.BlockSpec((tm, tn), lambda i, j, k: (i, j)),
        out_shape=jax.ShapeDtypeStruct((m, n), out_dtype),
        scratch_shapes=scratch,
        compiler_params=_cp("arbitrary", "arbitrary", "arbitrary"), name="matmul",
    )(*a_list, *b_list)


def _mmw_kernel(*refs, n_pairs, ranges, nm, extra):
    n_out = len(ranges)
    a_refs, refs = refs[:n_pairs], refs[n_pairs:]
    if extra:
        a2_refs, refs = refs[:n_pairs], refs[n_pairs:]
    b_refs, refs = refs[:n_pairs], refs[n_pairs:]
    o_refs, refs = refs[:n_out], refs[n_out:]
    if extra:
        o2_refs, refs = refs[:n_out], refs[n_out:]
    wb_refs = refs
    j, i = pl.program_id(0), pl.program_id(1)

    @pl.when(i == 0)
    def _():
        for b_ref, wb_ref in zip(b_refs, wb_refs):
            wb_ref[...] = b_ref[...].astype(BF16)

    def product(lhs_refs, out_refs):
        acc = _dot(lhs_refs[0][...], wb_refs[0][...])
        for a_ref, wb_ref in zip(lhs_refs[1:], wb_refs[1:]):
            acc = acc + _dot(a_ref[...], wb_ref[...])
        if n_out == 1:
            out_refs[0][...] = acc.astype(out_refs[0].dtype)
            return
        for o_ref, (lo, hi) in zip(out_refs, ranges):
            @pl.when((j >= lo) & (j < hi))
            def _(o_ref=o_ref):
                o_ref[...] = acc.astype(o_ref.dtype)

    if not extra:
        product(a_refs, o_refs)
        return

    @pl.when(i < nm)
    def _():
        product(a_refs, o_refs)

    @pl.when(i == nm)
    def _():
        product(a2_refs, o2_refs)


def _matmul_w32(a_list, w_list, out_dtype, out_cols=None, col0=0, tm=512, tn=1024, w_buffers=2,
                a2_list=None):
    m, kdim = a_list[0].shape
    tm = min(tm, m)
    out_cols = [w_list[0].shape[1]] if out_cols is None else out_cols
    n = sum(out_cols)
    tn = _pick_tile(math.gcd(col0, *out_cols), tn)
    assert m % tm == 0
    nm = m // tm
    jb = col0 // tn
    row_blk = [p if w.shape[0] != kdim else 0 for p, w in enumerate(w_list)]
    ranges, lo = [], 0
    for c in out_cols:
        ranges.append((lo, lo + c // tn))
        lo += c // tn
    n_pairs = len(a_list)
    extra = a2_list is not None
    m2 = a2_list[0].shape[0] if extra else 0

    def out_map(lo, hi, second=False):
        def index(j, i):
            inside = (j >= lo) & (j < hi)
            ii = jnp.where(inside, jnp.minimum(i, nm - 1), jnp.where(j < lo, 0, nm - 1))
            return (0 if second else ii), jnp.clip(j - lo, 0, hi - lo - 1)
        return index

    in_specs = [pl.BlockSpec((tm, kdim), lambda j, i: (jnp.minimum(i, nm - 1), 0))] * n_pairs
    out_specs = [pl.BlockSpec((tm, tn), out_map(lo, hi)) for lo, hi in ranges]
    out_shape = [jax.ShapeDtypeStruct((m, c), out_dtype) for c in out_cols]
    args = list(a_list)
    if extra:
        in_specs += [pl.BlockSpec((m2, kdim), lambda j, i: (0, 0))] * n_pairs
        out_specs += [pl.BlockSpec((m2, tn), out_map(lo, hi, True)) for lo, hi in ranges]
        out_shape += [jax.ShapeDtypeStruct((m2, c), out_dtype) for c in out_cols]
        args += list(a2_list)
    in_specs += [pl.BlockSpec((kdim, tn), lambda j, i, rb=rb: (rb, j + jb),
                              pipeline_mode=pl.Buffered(w_buffers)) for rb in row_blk]
    outs = pl.pallas_call(
        functools.partial(_mmw_kernel, n_pairs=n_pairs, ranges=ranges, nm=nm, extra=extra),
        grid=(n // tn, nm + (1 if extra else 0)),
        in_specs=in_specs, out_specs=out_specs, out_shape=out_shape,
        scratch_shapes=[pltpu.VMEM((kdim, tn), BF16)] * n_pairs,
        compiler_params=_cp("arbitrary", "arbitrary"), name="matmul_w32",
    )(*args, *w_list)
    k = len(out_cols)
    first = outs[0] if k == 1 else list(outs[:k])
    if not extra:
        return first
    return first, (outs[k] if k == 1 else list(outs[k:]))


def _gates_kernel(h_ref, w_ref, par_ref, g_ref, f_ref, carry_ref, *, ha, hb, seg):
    @pl.when(pl.program_id(1) == 0)
    def _():
        carry_ref[...] = jnp.zeros_like(carry_ref)

    p = _dot(h_ref[...], w_ref[...].astype(BF16))
    tr = p.shape[0]
    lane = _iota(p.shape, 1)
    a_log, bias = par_ref[0:1, :], par_ref[1:2, :]
    x = p + bias
    beta = _sigmoid(p)
    g = -jnp.exp(a_log) * _softplus(x)
    logf = -_softplus(-x)
    out = jnp.where(lane < ha, beta,
                    jnp.where(lane < 2 * ha, g, jnp.where(lane < 2 * ha + hb, logf, 0.0)))
    g_ref[...] = out
    rr, cc = _iota((tr, tr), 0), _iota((tr, tr), 1)
    tri = rr >= cc
    if seg is not None:
        tri = tri & ((rr >> _log2(seg)) == (cc >> _log2(seg)))
    cs = _dot(tri.astype(F32), out, HI) + carry_ref[0:1, :]
    f_ref[...] = cs
    carry_ref[0:1, :] = cs[tr - 1:tr, :]


def _gates(h, w_small, par, nb, t, ha, hb):
    m, d = h.shape
    if t >= 128:
        tr, per, seg, nb_grid = min(t, 512), t // min(t, 512), None, nb
    else:
        tr, per, seg, nb_grid = m, 1, t, 1
    spec = pl.BlockSpec((tr, LANE), lambda b, j: (b * per + j, 0))
    return pl.pallas_call(
        functools.partial(_gates_kernel, ha=ha, hb=hb, seg=seg),
        grid=(nb_grid, per),
        in_specs=[pl.BlockSpec((tr, d), lambda b, j: (b * per + j, 0)),
                  pl.BlockSpec((d, LANE), lambda b, j: (0, 0)),
                  pl.BlockSpec((SUBLANE, LANE), lambda b, j: (0, 0))],
        out_specs=[spec, spec],
        out_shape=[jax.ShapeDtypeStruct((m, LANE), F32)] * 2,
        scratch_shapes=[pltpu.VMEM((SUBLANE, LANE), F32)],
        compiler_params=_cp("arbitrary", "arbitrary"), name="gates",
    )(h, w_small, par)


def _each(f, *lists):
    return [f(*args) for args in zip(*lists)]


def _unit_lower_inverse(lows):
    c = lows[0].shape[0]
    assert c == 64
    r, s = _iota((c, c), 0), _iota((c, c), 1)
    eye = (r == s).astype(F32)
    same16 = (r >> 4) == (s >> 4)
    same32 = (r >> 5) == (s >> 5)
    ld = _each(lambda low: jnp.where(same16, low, 0.0), lows)
    x = _each(lambda a: eye - a, ld)
    p = _each(lambda a: _dot3(a, a), ld)
    for _ in range(2):
        xp = _each(lambda a, b: _dot3(jnp.concatenate([a, b], axis=0), b), x, p)
        x = _each(lambda a, b: a + b[:c], x, xp)
        p = _each(lambda b: b[c:], xp)
    x = _each(lambda a, b: a + _dot3(a, b), x, p)
    for keep in (same32 & jnp.logical_not(same16), jnp.logical_not(same32)):
        off = _each(lambda low: jnp.where(keep, low, 0.0), lows)
        y = _each(_dot3, x, off)
        x = _each(lambda a, b: a - _dot3(b, a), x, y)
    return x


def _delta_kernel(q_ref, k_ref, v_ref, z_ref, gt_ref, cwq_ref, cwk_ref, cwv_ref,
                  cbq_ref, cbk_ref, cbv_ref, s0_ref, gn_ref, o_ref, sfin_ref, ext_ref, s_ref,
                  *, hpb, ha, kw, valid_rows, nc):
    c_idx = pl.program_id(2)
    hblk = pl.program_id(1)
    c = q_ref.shape[0]
    hd = LANE

    @pl.when(c_idx == 0)
    def _():
        s_ref[...] = s0_ref[...]
        for i, cb in enumerate((cbq_ref, cbk_ref, cbv_ref)):
            ext_ref[i, 0:SUBLANE, :] = cb[...]

    conv = []
    for i, (r_ref, cw_ref) in enumerate(((q_ref, cwq_ref), (k_ref, cwk_ref), (v_ref, cwv_ref))):
        ext_ref[i, SUBLANE:SUBLANE + c, :] = r_ref[...]
        acc = ext_ref[i, SUBLANE:SUBLANE + c, :] * cw_ref[kw - 1:kw, :]
        for j in range(kw - 1):
            acc = acc + ext_ref[i, pl.ds(SUBLANE - (kw - 1) + j, c), :] * cw_ref[j:j + 1, :]
        ext_ref[i, 0:SUBLANE, :] = ext_ref[i, c:c + SUBLANE, :]
        acc = _silu(acc)
        if valid_rows < c:
            acc = jnp.where(_iota(acc.shape, 0) < valid_rows, acc, 0.0)
        conv.append(acc)
    xq, xk, xv = conv

    gt = gt_ref[...]
    rr, ss = _iota((c, c), 0), _iota((c, c), 1)
    incl, strict = rr >= ss, rr > ss
    gcum = _dot(incl.astype(F32), gt, HI)
    lane = _iota(gt.shape, 1)
    gn = gn_ref[...]
    heads = list(range(hpb))
    sls = [slice(j * hd, (j + 1) * hd) for j in heads]
    beta = [jnp.sum(jnp.where(lane == hblk * hpb + j, gt, 0.0), axis=1, keepdims=True) for j in heads]
    gc = [jnp.sum(jnp.where(lane == ha + hblk * hpb + j, gcum, 0.0), axis=1, keepdims=True) for j in heads]
    qh = [xq[:, sl] for sl in sls]
    qh = _each(lambda a: a * lax.rsqrt(jnp.sum(a * a, axis=-1, keepdims=True) + L2_EPS) * (hd ** -0.5), qh)
    kh = [xk[:, sl] for sl in sls]
    kh = _each(lambda a: a * lax.rsqrt(jnp.sum(a * a, axis=-1, keepdims=True) + L2_EPS), kh)
    decay = _each(lambda g: jnp.where(incl, jnp.exp(g - _col_to_row(g)), 0.0), gc)
    kb = _each(lambda a, b: a * b, kh, beta)
    vb = [xv[:, sl] * b for sl, b in zip(sls, beta)]
    egc = _each(jnp.exp, gc)
    kk = _each(lambda a, q, k: _dot_nt(jnp.concatenate([a, q], axis=0).astype(BF16), k.astype(BF16)),
               kb, qh, kh)
    low = _each(lambda a, d: jnp.where(strict, a[:c] * d, 0.0), kk, decay)
    attn = _each(lambda a, d: (a[c:] * d).astype(BF16), kk, decay)
    tmat = _unit_lower_inverse(low)
    uw = _each(lambda t, v, k, e: _dot(t.astype(BF16), jnp.concatenate([v, k * e], axis=1).astype(BF16)),
               tmat, vb, kb, egc)
    st = [s_ref[j] for j in heads]
    ws = _each(lambda a, q, e, s: _dot(jnp.concatenate([a[:, hd:], q * e], axis=0).astype(BF16),
                                       s.astype(BF16)), uw, qh, egc, st)
    v_new = _each(lambda a, b: (a[:, :hd] - b[:c]).astype(BF16), uw, ws)
    o = _each(lambda b, a, v: b[c:] + _dot(a, v), ws, attn, v_new)
    g_last = _each(lambda g: g[c - 1:c, :], gc)
    k_dec = _each(lambda k, gl, g: (k * jnp.exp(gl - g)).astype(BF16), kh, g_last, gc)
    s_new = _each(lambda s, gl, k, v: s * jnp.exp(gl) + _dot_tn(k, v), st, g_last, k_dec, v_new)
    for j in heads:
        s_ref[j] = s_new[j]
        o_ref[:, sls[j]] = (_rms(o[j], gn) * _silu(z_ref[:, sls[j]])).astype(o_ref.dtype)

    @pl.when(c_idx == nc - 1)
    def _():
        sfin_ref[...] = s_ref[...]


def _delta(proj, gates, conv_w, conv_buf8, s0, gnorm, nb, tp, valid_rows, ha, hpb=8):
    c = DELTA_CHUNK
    hd = LANE
    hpb = min(hpb, ha)
    w = hpb * hd
    nc = tp // c
    nhb = ha // hpb
    kw = conv_w.shape[0]
    cw8 = jnp.pad(conv_w, ((0, SUBLANE - kw), (0, 0)))

    def col(off):
        return pl.BlockSpec((c, w), lambda b, h, i, off=off: (b * nc + i, off * nhb + h))

    def cwspec(off):
        return pl.BlockSpec((SUBLANE, w), lambda b, h, i, off=off: (0, off * nhb + h))

    def cbspec(off):
        return pl.BlockSpec((None, SUBLANE, w), lambda b, h, i, off=off: (b, 0, off * nhb + h))

    state_spec = pl.BlockSpec((None, hpb, hd, hd), lambda b, h, i: (b, h, 0, 0))
    return pl.pallas_call(
        functools.partial(_delta_kernel, hpb=hpb, ha=ha, kw=kw, valid_rows=valid_rows, nc=nc),
        grid=(nb, nhb, nc),
        in_specs=[col(0), col(1), col(2), col(3),
                  pl.BlockSpec((c, LANE), lambda b, h, i: (b * nc + i, 0)),
                  cwspec(0), cwspec(1), cwspec(2), cbspec(0), cbspec(1), cbspec(2),
                  state_spec, pl.BlockSpec((1, hd), lambda b, h, i: (0, 0))],
        out_specs=[pl.BlockSpec((c, w), lambda b, h, i: (b * nc + i, h)), state_spec],
        out_shape=[jax.ShapeDtypeStruct((nb * tp, ha * hd), BF16),
                   jax.ShapeDtypeStruct(s0.shape, F32)],
        scratch_shapes=[pltpu.VMEM((3, c + SUBLANE, w), F32), pltpu.VMEM((hpb, hd, hd), F32)],
        compiler_params=_cp("arbitrary", "arbitrary", "arbitrary"), name="delta",
    )(proj, proj, proj, proj, gates, cw8, cw8, cw8, conv_buf8, conv_buf8, conv_buf8, s0,
      gnorm.reshape(1, hd))


def _softmax_step(carry, s, v, valid=None):
    m, l, acc = carry
    m_new = jnp.maximum(m, jnp.max(s, axis=-1, keepdims=True))
    alpha = jnp.exp(m - m_new)
    p = jnp.exp(s - m_new)
    if valid is not None:
        p = jnp.where(valid, p, 0.0)
    l = alpha * l + jnp.sum(p, axis=-1, keepdims=True)
    acc = alpha * acc + _dot(p.astype(BF16), v)
    return m_new, l, acc


def _softmax_init(rows, hd):
    return (jnp.full((rows, 1), NEG, F32), jnp.zeros((rows, 1), F32), jnp.zeros((rows, hd), F32))


def _fox_prompt_kernel(q_ref, k_ref, v_ref, f_ref, o_ref, kb_ref, vb_ref, *, tq, tk, scale, par):
    seq = q_ref.shape[0]
    kb_ref[...] = k_ref[...].astype(BF16)
    vb_ref[...] = v_ref[...].astype(BF16)
    col_minus_row = _iota((tq, tk), 1) - _iota((tq, tk), 0)
    nqb = seq // tq
    for c0 in range(0, nqb, par):
        chunk = list(range(c0, min(c0 + par, nqb)))
        qs = [(q_ref[pl.ds(qb * tq, tq), :] * scale).astype(BF16) for qb in chunk]
        fqs = [_row_to_col(f_ref[:, pl.ds(qb * tq, tq)]) for qb in chunk]
        carries = [_softmax_init(tq, LANE) for _ in chunk]
        n_steps = [-(-((qb + 1) * tq) // tk) for qb in chunk]
        for step in range(max(n_steps)):
            live = [i for i in range(len(chunk)) if step < n_steps[i]]
            k = kb_ref[pl.ds(step * tk, tk), :]
            v = vb_ref[pl.ds(step * tk, tk), :]
            fk = f_ref[:, pl.ds(step * tk, tk)]
            ss = [_dot_nt(qs[i], k) + (fqs[i] - fk) for i in live]
            for n, i in enumerate(live):
                if step * tk + tk - 1 > chunk[i] * tq:
                    ss[n] = jnp.where(col_minus_row <= chunk[i] * tq - step * tk, ss[n], NEG)
            ms = [jnp.maximum(carries[i][0], jnp.max(s, axis=-1, keepdims=True)) for i, s in zip(live, ss)]
            ps = [jnp.exp(s - m) for s, m in zip(ss, ms)]
            alphas = [jnp.exp(carries[i][0] - m) for i, m in zip(live, ms)]
            pvs = [_dot(p.astype(BF16), v) for p in ps]
            for i, m, p, a, pv in zip(live, ms, ps, alphas, pvs):
                carries[i] = (m, a * carries[i][1] + jnp.sum(p, axis=-1, keepdims=True),
                              a * carries[i][2] + pv)
        for qb, (_, l, acc) in zip(chunk, carries):
            o_ref[pl.ds(qb * tq, tq), :] = (acc / l).astype(o_ref.dtype)


def _fox_prompt(q_arr, q_col, k_arr, v_arr, f_rows, nb, s, nh, tq=128, tk=512, par=4):
    tq, tk = min(tq, s), min(tk, s)
    assert s % tq == 0 and s % tk == 0
    hd = LANE
    return pl.pallas_call(
        functools.partial(_fox_prompt_kernel, tq=tq, tk=tk, scale=hd ** -0.5, par=par),
        grid=(nb, nh),
        in_specs=[pl.BlockSpec((s, hd), lambda b, h: (b, q_col + h)),
                  pl.BlockSpec((s, hd), lambda b, h: (b, h)),
                  pl.BlockSpec((s, hd), lambda b, h: (b, h)),
                  pl.BlockSpec((None, None, 1, s), lambda b, h: (b, h, 0, 0))],
        out_specs=pl.BlockSpec((s, hd), lambda b, h: (b, h)),
        out_shape=jax.ShapeDtypeStruct((nb * s, nh * hd), BF16),
        scratch_shapes=[pltpu.VMEM((s, hd), BF16), pltpu.VMEM((s, hd), BF16)],
        compiler_params=_cp("arbitrary", "arbitrary"), name="fox_prompt",
    )(q_arr, k_arr, v_arr, f_rows)


def _fpast_kernel(pt_ref, *refs, group):
    lp_refs, o_ref, carry_ref = refs[:group], refs[group], refs[group + 1]

    @pl.when(pl.program_id(1) == 0)
    def _():
        carry_ref[...] = jnp.zeros_like(carry_ref)

    p = lp_refs[0].shape[0]
    upper = (_iota((p, p), 1) > _iota((p, p), 0)).astype(F32)
    for i in range(group):
        lp = lp_refs[i][...]
        o_ref[group - 1 - i] = -(_dot(upper, lp, HI) + carry_ref[0:1, :])
        carry_ref[0:1, :] = carry_ref[0:1, :] + jnp.sum(lp, axis=0, keepdims=True)


def _fpast(logf_pool, page_table):
    db, n_pages = page_table.shape
    _, p, nh = logf_pool.shape
    group = 8 if n_pages % 8 == 0 else 1
    nblk = n_pages // group

    def lp_spec(i):
        return pl.BlockSpec((None, p, nh),
                            lambda b, j, pt, i=i: (pt[b, n_pages - 1 - (j * group + i)], 0, 0))

    return pl.pallas_call(
        functools.partial(_fpast_kernel, group=group),
        grid_spec=pltpu.PrefetchScalarGridSpec(
            num_scalar_prefetch=1, grid=(db, nblk),
            in_specs=[lp_spec(i) for i in range(group)],
            out_specs=pl.BlockSpec((None, group, p, nh), lambda b, j, pt: (b, nblk - 1 - j, 0, 0)),
            scratch_shapes=[pltpu.VMEM((SUBLANE, nh), F32)]),
        out_shape=jax.ShapeDtypeStruct((db, n_pages, p, nh), F32),
        compiler_params=_cp("arbitrary", "arbitrary"), name="fpast",
    )(page_table, *([logf_pool] * group))


def _fox_sample_kernel(pt_ref, *refs, nh, scale, n_steps, pp):
    q_ref, k_refs, v_refs = refs[0], refs[1:1 + pp], refs[1 + pp:1 + 2 * pp]
    (fp_ref, kn_ref, vn_ref, fn_ref, o_ref, m_ref, l_ref, acc_ref, bias_ref, s0_ref,
     s1_ref) = refs[1 + 2 * pp:]
    p = pl.program_id(1)
    r = q_ref.shape[0]
    rows = k_refs[0].shape[0]
    lognh = _log2(nh)
    fn = fn_ref[...]

    @pl.when(p == 0)
    def _():
        m_ref[...] = jnp.full_like(m_ref, NEG)
        l_ref[...] = jnp.zeros_like(l_ref)
        acc_ref[...] = jnp.zeros_like(acc_ref)
        s1_ref[...] = jnp.zeros_like(s1_ref)
        match = (_iota((r, rows), 0) & (nh - 1)) == (_iota((r, rows), 1) & (nh - 1))
        bias_ref[...] = jnp.where(match, _row_to_col(fn), NEG)

    q = (q_ref[...] * scale).astype(BF16)

    def update(s, v):
        m, l, acc = _softmax_step((m_ref[...], l_ref[...], acc_ref[...]), s, v)
        m_ref[...], l_ref[...], acc_ref[...] = m, l, acc

    def step(s_w, s_r):
        for i in range(pp):
            s_w[:, i * rows:(i + 1) * rows] = (_dot_nt(q, k_refs[i][...].astype(BF16))
                                               + (bias_ref[...] - fp_ref[i]))
        live = p > 0
        s = s_r[...]
        m_old = m_ref[...]
        m_new = jnp.where(live, jnp.maximum(m_old, jnp.max(s, axis=-1, keepdims=True)), m_old)
        pr = jnp.where(live, jnp.exp(s - m_new), 0.0)
        alpha = jnp.exp(m_old - m_new)
        l_ref[...] = alpha * l_ref[...] + jnp.sum(pr, axis=-1, keepdims=True)
        pr = pr.astype(BF16)
        pv = _dot(pr[:, :rows], v_refs[0][...].astype(BF16))
        for i in range(1, pp):
            pv = pv + _dot(pr[:, i * rows:(i + 1) * rows], v_refs[i][...].astype(BF16))
        acc_ref[...] = alpha * acc_ref[...] + pv
        m_ref[...] = m_new

    @pl.when(p % 2 == 0)
    def _():
        step(s0_ref, s1_ref)

    @pl.when(p % 2 == 1)
    def _():
        step(s1_ref, s0_ref)

    @pl.when(p == n_steps)
    def _():
        rr, cc = _iota((r, r), 0), _iota((r, r), 1)
        ok = ((rr & (nh - 1)) == (cc & (nh - 1))) & ((cc >> lognh) <= (rr >> lognh))
        s2 = _dot_nt(q, kn_ref[...].astype(BF16)) + _row_to_col(fn) - fn
        update(jnp.where(ok, s2, NEG), vn_ref[...].astype(BF16))
        o_ref[...] = (acc_ref[...] / l_ref[...]).astype(o_ref.dtype)


def _fox_sample(q, k_new, v_new, f_new, k_pool, v_pool, f_past, page_table, nh):
    db, r, hd = q.shape
    n_pages = page_table.shape[1]
    rows = k_pool.shape[1]
    tok = pl.BlockSpec((None, r, hd), lambda b, p, pt: (b, 0, 0))
    pp = 2 if n_pages % 2 == 0 else 1
    n_steps = n_pages // pp
    last = n_steps - 1
    k_pages = [pl.BlockSpec((None, rows, hd),
                            lambda b, p, pt, i=i: (pt[b, jnp.minimum(p, last) * pp + i], 0, 0))
               for i in range(pp)]
    v_pages = [pl.BlockSpec((None, rows, hd),
                            lambda b, p, pt, i=i: (pt[b, jnp.maximum(p - 1, 0) * pp + i], 0, 0))
               for i in range(pp)]
    return pl.pallas_call(
        functools.partial(_fox_sample_kernel, nh=nh, scale=hd ** -0.5, n_steps=n_steps, pp=pp),
        grid_spec=pltpu.PrefetchScalarGridSpec(
            num_scalar_prefetch=1, grid=(db, n_steps + 1),
            in_specs=[tok] + k_pages + v_pages + [
                pl.BlockSpec((None, pp, 1, rows), lambda b, p, pt: (b, jnp.minimum(p, last), 0, 0)),
                tok, tok, pl.BlockSpec((None, 1, r), lambda b, p, pt: (b, 0, 0))],
            out_specs=tok,
            scratch_shapes=[pltpu.VMEM((r, 1), F32), pltpu.VMEM((r, 1), F32),
                            pltpu.VMEM((r, hd), F32), pltpu.VMEM((r, rows), F32),
                            pltpu.VMEM((r, pp * rows), F32), pltpu.VMEM((r, pp * rows), F32)]),
        out_shape=jax.ShapeDtypeStruct((db, r, hd), BF16),
        compiler_params=_cp("arbitrary", "arbitrary"), name="fox_sample",
    )(page_table, q, *([k_pool] * pp), *([v_pool] * pp), f_past, k_new, v_new, f_new)


def _dil_prompt_kernel(*refs, groups, tb, scale, par):
    ng = len(groups)
    q_refs, k_refs, v_refs = refs[:ng], refs[ng:2 * ng], refs[2 * ng:3 * ng]
    o_ref, m_ref, l_ref, acc_ref = refs[3 * ng:]
    seq = q_refs[0].shape[0]
    diff = _iota((tb, tb), 0) - _iota((tb, tb), 1)

    def rows_of(res, blk, dil):
        return pl.ds(res + blk * tb * dil, tb, stride=dil) if dil > 1 else pl.ds(blk * tb, tb)

    for g, (window, dil) in enumerate(groups):
        reach = window // dil
        nqb = seq // dil // tb
        back = -(-reach // tb)
        units = [(res, qb) for res in range(dil) for qb in range(nqb)]
        for c0 in range(0, len(units), par):
            chunk = units[c0:c0 + par]
            rows = [rows_of(res, qb, dil) for res, qb in chunk]
            qs = [(q_refs[g][rw, :] * scale).astype(BF16) for rw in rows]
            if g == 0:
                carries = [_softmax_init(tb, LANE) for _ in chunk]
            else:
                carries = [(m_ref[rw, :], l_ref[rw, :], acc_ref[rw, :]) for rw in rows]
            for step in range(back + 1):
                live = [i for i, (_, qb) in enumerate(chunk) if qb - step >= 0]
                if not live:
                    continue
                krows = [rows_of(chunk[i][0], chunk[i][1] - step, dil) for i in live]
                ks = [k_refs[g][rw, :].astype(BF16) for rw in krows]
                vs = [v_refs[g][rw, :].astype(BF16) for rw in krows]
                ss = [_dot_nt(qs[i], k) for i, k in zip(live, ks)]
                if step * tb - (tb - 1) < 0 or step * tb + (tb - 1) > reach:
                    dist = diff + step * tb
                    valid = (dist >= 0) & (dist <= reach)
                    ss = [jnp.where(valid, s, NEG) for s in ss]
                ms = [jnp.maximum(carries[i][0], jnp.max(s, axis=-1, keepdims=True)) for i, s in zip(live, ss)]
                ps = [jnp.exp(s - m) for s, m in zip(ss, ms)]
                alphas = [jnp.exp(carries[i][0] - m) for i, m in zip(live, ms)]
                pvs = [_dot(p.astype(BF16), v) for p, v in zip(ps, vs)]
                for i, m, p, a, pv in zip(live, ms, ps, alphas, pvs):
                    carries[i] = (m, a * carries[i][1] + jnp.sum(p, axis=-1, keepdims=True),
                                  a * carries[i][2] + pv)
            for rw, (m, l, acc) in zip(rows, carries):
                m_ref[rw, :], l_ref[rw, :], acc_ref[rw, :] = m, l, acc
    o_ref[...] = (acc_ref[...] / l_ref[...]).astype(o_ref.dtype)


def _dil_prompt(srcs, nb, s, nh, groups, tb=128, par=4):
    hd = LANE
    ng = len(groups)
    tb = min(tb, s)
    for window, dil in groups:
        assert s % (dil * tb) == 0 and window % dil == 0
    args, specs = [], []
    for which in range(3):
        for g in range(ng):
            arr, col = srcs[g][which]
            args.append(arr)
            specs.append(pl.BlockSpec((s, hd), lambda b, h, col=col: (b, col + h)))
    return pl.pallas_call(
        functools.partial(_dil_prompt_kernel, groups=groups, tb=tb, scale=hd ** -0.5, par=par),
        grid=(nb, nh),
        in_specs=specs,
        out_specs=pl.BlockSpec((s, hd), lambda b, h: (b, h)),
        out_shape=jax.ShapeDtypeStruct((nb * s, nh * hd), BF16),
        scratch_shapes=[pltpu.VMEM((s, 1), F32), pltpu.VMEM((s, 1), F32), pltpu.VMEM((s, hd), F32)],
        compiler_params=_cp("arbitrary", "arbitrary"), name="dil_prompt",
    )(*args)


def _dil_sample_kernel(*refs, groups, wbs, tb, nh, scale, t):
    ng = len(groups)
    q_refs, kb_refs, vb_refs = refs[:ng], refs[ng:2 * ng], refs[2 * ng:3 * ng]
    kn_refs, vn_refs = refs[3 * ng:4 * ng], refs[4 * ng:5 * ng]
    o_ref, m_ref, l_ref, acc_ref = refs[5 * ng:5 * ng + 4]
    j = pl.program_id(1)
    r = q_refs[0].shape[0]
    rows = tb * nh
    lognh = _log2(nh)
    nblks = [wb // tb for wb in wbs]
    total = sum(nblks)

    @pl.when(j == 0)
    def _():
        m_ref[...] = jnp.full_like(m_ref, NEG)
        l_ref[...] = jnp.zeros_like(l_ref)
        acc_ref[...] = jnp.zeros_like(acc_ref)

    def update(s, v, valid):
        m, l, acc = _softmax_step((m_ref[...], l_ref[...], acc_ref[...]),
                                  jnp.where(valid, s, NEG), v, valid)
        m_ref[...], l_ref[...], acc_ref[...] = m, l, acc

    start = 0
    for g, (window, dil) in enumerate(groups):
        def block(g=g, window=window, dil=dil, start=start):
            rr, cc = _iota((r, rows), 0), _iota((r, rows), 1)
            key_tok = (j - start) * tb + (cc >> lognh)
            delta = wbs[g] + (rr >> lognh) - key_tok
            valid = (((rr & (nh - 1)) == (cc & (nh - 1))) & ((delta & (dil - 1)) == 0)
                     & (delta <= window))
            q = (q_refs[g][...] * scale).astype(BF16)
            update(_dot_nt(q, kb_refs[g][...].astype(BF16)), vb_refs[g][...].astype(BF16), valid)

        pl.when((j >= start) & (j < start + nblks[g]))(block)
        start += nblks[g]

    @pl.when(j == total - 1)
    def _():
        rr, cc = _iota((r, r), 0), _iota((r, r), 1)
        delta = (rr >> lognh) - (cc >> lognh)
        match = (rr & (nh - 1)) == (cc & (nh - 1))
        for g, (window, dil) in enumerate(groups):
            valid = match & (delta >= 0) & ((delta & (dil - 1)) == 0) & (delta <= window)
            q = (q_refs[g][...] * scale).astype(BF16)
            update(_dot_nt(q, kn_refs[g][...].astype(BF16)), vn_refs[g][...].astype(BF16), valid)
        o_ref[...] = (acc_ref[...] / l_ref[...]).astype(o_ref.dtype)


def _dil_sample(qs, kns, vns, kbufs, vbufs, groups, nh, t):
    db, r, hd = qs[0].shape
    ng = len(groups)
    wbs = [kb.shape[1] // nh for kb in kbufs]
    tb = 128
    while any(wb % tb for wb in wbs):
        tb //= 2
    nblks = [wb // tb for wb in wbs]
    starts = [sum(nblks[:g]) for g in range(ng)]
    total = sum(nblks)
    tok = pl.BlockSpec((None, r, hd), lambda b, j: (b, 0, 0))

    def buf_spec(g):
        return pl.BlockSpec(
            (None, tb * nh, hd),
            lambda b, j, g=g: (b, jnp.clip(j - starts[g], 0, nblks[g] - 1), 0))

    bspecs = [buf_spec(g) for g in range(ng)]
    return pl.pallas_call(
        functools.partial(_dil_sample_kernel, groups=groups, wbs=wbs, tb=tb, nh=nh,
                          scale=hd ** -0.5, t=t),
        grid=(db, total),
        in_specs=[tok] * ng + bspecs + bspecs + [tok] * (2 * ng),
        out_specs=tok,
        out_shape=jax.ShapeDtypeStruct((db, r, hd), BF16),
        scratch_shapes=[pltpu.VMEM((r, 1), F32), pltpu.VMEM((r, 1), F32), pltpu.VMEM((r, hd), F32)],
        compiler_params=_cp("arbitrary", "arbitrary"), name="dil_sample",
    )(*qs, *kbufs, *vbufs, *kns, *vns)


def _ffn_up_kernel(*refs, kw, tiles_per_batch, t_s, nm):
    nh = kw - 1
    hp_ref, hs_ref, wu_ref, wg_ref, cw_ref = refs[:5]
    halo_refs = refs[5:5 + nh]
    actp_ref, tailp_ref, acts_ref, gates_ref, ext_ref, wub_ref, wgb_ref = refs[5 + nh:]
    m = pl.program_id(1)
    tn = wub_ref.shape[1]

    @pl.when(m == 0)
    def _():
        wub_ref[...] = wu_ref[...].astype(BF16)
        wgb_ref[...] = wg_ref[...].astype(BF16)

    def conv_gelu(gate, up, fix):
        rows = gate.shape[0]
        ext_ref[SUBLANE:SUBLANE + rows, :] = gate
        gc = gate * cw_ref[kw - 1:kw, :]
        for j in range(kw - 1):
            shift = kw - 1 - j
            gc = gc + fix(ext_ref[pl.ds(SUBLANE - shift, rows), :], shift) * cw_ref[j:j + 1, :]
        return _gelu_tanh(gc) * up

    @pl.when(m < nm)
    def _():
        h = hp_ref[...]
        tm = h.shape[0]
        up = _dot(h, wub_ref[...])
        gate = _dot(h, wgb_ref[...])
        prev = jnp.where((m % tiles_per_batch) == 0, 0.0, ext_ref[tm:tm + SUBLANE, :])
        ext_ref[0:SUBLANE, :] = prev
        tailp_ref[...] = gate[tm - SUBLANE:, :]
        actp_ref[...] = conv_gelu(gate, up, lambda sh, shift: sh).astype(actp_ref.dtype)

    @pl.when(m == nm)
    def _():
        h = hs_ref[...]
        up = _dot(h, wub_ref[...])
        gate = _dot(h, wgb_ref[...])
        ext_ref[0:SUBLANE, :] = jnp.zeros((SUBLANE, tn), F32)
        gates_ref[...] = gate
        pos = _iota(gate.shape, 0) & (t_s - 1)
        fix = lambda sh, shift: jnp.where(pos < shift, halo_refs[shift - 1][...], sh)
        acts_ref[...] = conv_gelu(gate, up, fix).astype(acts_ref.dtype)


def _ffn_up(h_p, h_s, wu, wg, layer, cw8, kw, t_p, t_s, halos, tm=512, tn=FFN_TILE):
    m, d = h_p.shape
    ms = h_s.shape[0]
    n_pad = wu.shape[-1]
    tm = min(tm, t_p)
    tn = min(tn, -(-n_pad // LANE) * LANE)
    nm, nn = m // tm, -(-n_pad // tn)
    _log2(t_s)
    last = nm - 1
    s_spec = pl.BlockSpec((ms, tn), lambda j, i: (0, j))
    in_specs = [pl.BlockSpec((tm, d), lambda j, i: (jnp.minimum(i, last), 0)),
                pl.BlockSpec((ms, d), lambda j, i: (0, 0)),
                pl.BlockSpec((None, d, tn), lambda j, i: (layer, 0, j)),
                pl.BlockSpec((None, d, tn), lambda j, i: (layer, 0, j)),
                pl.BlockSpec((SUBLANE, tn), lambda j, i: (0, j))] + [s_spec] * len(halos)
    return pl.pallas_call(
        functools.partial(_ffn_up_kernel, kw=kw, tiles_per_batch=max(t_p // tm, 1), t_s=t_s, nm=nm),
        grid=(nn, nm + 1), in_specs=in_specs,
        out_specs=[pl.BlockSpec((tm, tn), lambda j, i: (jnp.minimum(i, last), j)),
                   pl.BlockSpec((None, SUBLANE, tn), lambda j, i: (jnp.minimum(i, last), 0, j)),
                   s_spec, s_spec],
        out_shape=[jax.ShapeDtypeStruct((m, n_pad), BF16),
                   jax.ShapeDtypeStruct((nm, SUBLANE, n_pad), F32),
                   jax.ShapeDtypeStruct((ms, n_pad), BF16), jax.ShapeDtypeStruct((ms, n_pad), F32)],
        scratch_shapes=[pltpu.VMEM((max(tm, ms) + SUBLANE, tn), F32), pltpu.VMEM((d, tn), BF16),
                        pltpu.VMEM((d, tn), BF16)],
        compiler_params=_cp("arbitrary", "arbitrary"), name="ffn_up",
    )(h_p, h_s, wu, wg, cw8, *halos)


def _even_weights(w_in, conv_w, a_log, dt_bias, b_f, w_out, ha, hb, hd):
    wa, wb = ha * hd, hb * hd
    o2 = 4 * wa
    o4 = o2 + 2 * ha
    o5 = o4 + 3 * wb
    w_main = (w_in, w_in[:, o4:o5])
    small = jnp.concatenate([w_in[:, o2:o4], w_in[:, o5:]], axis=1)
    assert small.shape[1] <= LANE
    w_small = jnp.pad(small, ((0, 0), (0, LANE - small.shape[1])))
    par = jnp.zeros((SUBLANE, LANE), F32)
    par = par.at[0, ha:2 * ha].set(a_log.astype(F32))
    par = par.at[1, ha:2 * ha].set(dt_bias.astype(F32))
    par = par.at[1, 2 * ha:2 * ha + hb].set(b_f.astype(F32))
    return w_main, w_small, par, conv_w, w_out


def _even_mixer(hs, nbs, ts, wts, gnorm, conv_bufs, s0s, ha, hb, hd, past):
    (w_in, w_fox), _, _, _, w_out = wts
    wa, wb = ha * hd, hb * hd
    projs = _matmul_w32([hs[0]], [w_in], F32, out_cols=[4 * wa], tm=1024, w_buffers=1, a2_list=[hs[1]])
    foxes = _matmul_w32([hs[0]], [w_fox], F32, out_cols=[wb, wb, wb], tm=512, w_buffers=1,
                        a2_list=[hs[1]])
    cores = [_even_core(hs[g], nbs[g], ts[g], projs[g], foxes[g], wts, gnorm, conv_bufs[g], s0s[g], ha, hb,
                        hd, past if g == 1 else None) for g in (0, 1)]
    outs = _matmul_w32([cores[0][0], cores[0][1]], [w_out, w_out], BF16, tm=512,
                       a2_list=[cores[1][0], cores[1][1]])
    return [(outs[g],) + tuple(cores[g][2:]) for g in (0, 1)]


def _even_core(h, nb, t, proj, fox, wts, gnorm, conv_buf, s0, ha, hb, hd, past):
    _, w_small, par, conv_w, _ = wts
    wa, wb = ha * hd, hb * hd
    kw = conv_w.shape[0]
    q_fox, k_new, v_new = fox
    gates, fcum = _gates(h, w_small, par, nb, t, ha, hb)

    c = DELTA_CHUNK
    tp = -(-t // c) * c
    conv_buf8 = jnp.pad(conv_buf.astype(F32), ((0, 0), (SUBLANE - (kw - 1), 0), (0, 0)))
    if tp == t:
        proj_d, gates_d = proj, gates
    else:
        pad = lambda a: jnp.pad(a.reshape(nb, t, -1), ((0, 0), (0, tp - t), (0, 0))).reshape(nb * tp, -1)
        proj_d, gates_d = pad(proj), pad(gates)
    o_a, s_new = _delta(proj_d, gates_d, conv_w, conv_buf8, s0, gnorm, nb, tp, min(t, c) if tp != t else c,
                        ha)
    if tp != t:
        o_a = o_a.reshape(nb, tp, wa)[:, :t].reshape(nb * t, wa)

    logf = gates[:, 2 * ha:2 * ha + hb].reshape(nb, t, hb)
    f_new = fcum[:, 2 * ha:2 * ha + hb].reshape(nb, t, hb)
    kb, vb = k_new.reshape(nb, t, hb, hd), v_new.reshape(nb, t, hb, hd)
    if past is None:
        f_rows = jnp.swapaxes(f_new, 1, 2).reshape(nb, hb, 1, t)
        o_b = _fox_prompt(q_fox, 0, k_new, v_new, f_rows, nb, t, hb)
    else:
        k_pool, v_pool, logf_pool, page_table = past
        n_pool, page = k_pool.shape[0], k_pool.shape[1]
        f_past = _fpast(logf_pool.astype(F32), page_table)
        f_past = f_past.reshape(nb, page_table.shape[1], 1, page * hb)
        o_b = _fox_sample(q_fox.reshape(nb, t * hb, hd), k_new.reshape(nb, t * hb, hd),
                          v_new.reshape(nb, t * hb, hd), f_new.reshape(nb, 1, t * hb),
                          k_pool.reshape(n_pool, page * hb, hd), v_pool.reshape(n_pool, page * hb, hd),
                          f_past, page_table, hb)
        o_b = o_b.reshape(nb * t, wb)
    keep = min(kw - 1, t)
    raw_tail = proj.reshape(nb, t, -1)[:, t - keep:, :3 * wa]
    new_conv = jnp.concatenate([conv_buf.astype(F32), raw_tail], axis=1)[:, -(kw - 1):]
    return o_a, o_b, s_new, new_conv, kb, vb, logf


def _odd_mixer(hs, nbs, ts, w_in, w_out, nh, hd, bufs):
    ng = len(C_GROUPS)
    wc = nh * hd
    n_main = (3 * ng - 2) * wc
    main, main_s = _matmul_w32([hs[0]], [w_in], F32, out_cols=[n_main], tm=1024, w_buffers=1,
                               a2_list=[hs[1]])
    (k_last, v_last), kv_s = _matmul_w32([hs[0]], [w_in], F32, out_cols=[wc, wc], col0=n_main, tm=512,
                                         w_buffers=1, a2_list=[hs[1]])
    nb, t = nbs[0], ts[0]
    srcs = [((main, g * 3 * nh), (main, g * 3 * nh + nh), (main, g * 3 * nh + 2 * nh))
            for g in range(ng - 1)]
    srcs.append(((main, (ng - 1) * 3 * nh), (k_last, 0), (v_last, 0)))
    o_p = _dil_prompt(srcs, nb, t, nh, C_GROUPS)
    m3 = main.reshape(nb, t, -1)
    bufs_p = []
    for g, (window, _) in enumerate(C_GROUPS):
        keep = min(window, t)
        if g < ng - 1:
            kv = [m3[:, t - keep:, (g * 3 + r) * wc:(g * 3 + r + 1) * wc] for r in (1, 2)]
        else:
            kv = [a.reshape(nb, t, wc)[:, t - keep:] for a in (k_last, v_last)]
        bufs_p.append(tuple(a.reshape(nb, keep, nh, hd) for a in kv))
    o_s, bufs_s = _odd_sample_core(jnp.concatenate([main_s] + list(kv_s), axis=1), nbs[1], ts[1], nh, hd, bufs)
    out_p, out_s = _matmul_w32([o_p], [w_out], BF16, tm=512, a2_list=[o_s])
    return (out_p, bufs_p), (out_s, bufs_s)


def _odd_sample_core(proj, nb, t, nh, hd, bufs):
    ng = len(C_GROUPS)
    new_bufs = []
    p6 = proj.reshape(nb, t, ng, 3, nh, hd)
    flat = lambda a: a.reshape(nb, -1, hd)
    qs = [flat(p6[:, :, g, 0]) for g in range(ng)]
    kns = [flat(p6[:, :, g, 1]) for g in range(ng)]
    vns = [flat(p6[:, :, g, 2]) for g in range(ng)]
    o = _dil_sample(qs, kns, vns, [flat(bk) for bk, _ in bufs], [flat(bv) for _, bv in bufs],
                    C_GROUPS, nh, t)
    o = o.reshape(nb * t, nh * hd)
    for g in range(ng):
        bk, bv = bufs[g]
        wb = bk.shape[1]
        new_bufs.append((jnp.concatenate([bk, p6[:, :, g, 1]], axis=1)[:, -wb:],
                         jnp.concatenate([bv, p6[:, :, g, 2]], axis=1)[:, -wb:]))
    return o, new_bufs


def _ffn_weights(w_up, w_gate, conv_w, w_down):
    d_ff = w_up.shape[-1]
    kw = conv_w.shape[1]
    cw8 = jnp.pad(conv_w.astype(F32), ((0, 0), (0, SUBLANE - kw), (0, 0)))
    return w_up, w_gate, cw8, w_down.astype(BF16), kw, d_ff


def _conv_ffn(hs, nbs, ts, wts, layer, buf_s):
    wu, wg, cw8, wd, kw, d_ff = wts
    (nb, nbs_), (t, t_s) = nbs, ts
    halos = []
    for shift in range(1, kw):
        hl = jnp.zeros((nbs_, t_s, d_ff), F32)
        hl = hl.at[:, :shift].set(buf_s.astype(F32)[:, (kw - 1) - shift:])
        halos.append(hl.reshape(nbs_ * t_s, d_ff))
    act_p, tail, act_s, gate_s = _ffn_up(hs[0], hs[1], wu, wg, layer, cw8[layer], kw, t, t_s, halos)
    tiles = tail.shape[0] // nb
    gate_tail = tail.reshape(nb, tiles, SUBLANE, d_ff)[:, -1]
    new_buf_p = gate_tail[:, SUBLANE - (kw - 1):]
    new_buf_s = jnp.concatenate([buf_s.astype(F32), gate_s.reshape(nbs_, t_s, d_ff)], axis=1)[:, -(kw - 1):]
    out_p = _matmul([act_p], [wd], BF16, tm=512, tn=512, layer=layer)
    out_s = _matmul([act_s], [wd], BF16, tm=act_s.shape[0], tn=512, layer=layer)
    return (out_p, new_buf_p), (out_s, new_buf_s)


def kernel(x_prompt, x_sample, c_prompt, c_sample, state_delta, state_conv_qkv, cache_k, cache_v, cache_logf, cache_win_k0, cache_win_v0, cache_win_k1, cache_win_v1, cache_win_k2, cache_win_v2, state_ffn_conv, page_table, w_mod, b_mod, norm_pre_mix, norm_post_mix, norm_pre_ffn, norm_post_ffn, w_in_e, conv_a, a_log, dt_bias, gnorm_a, b_forget, w_out_e, w_in_o, w_out_o, w_up, w_gate, conv_ffn_w, w_down):
    bsz, seq, d = x_prompt.shape
    dbsz, dseq, _ = x_sample.shape
    depth = w_mod.shape[0]
    hd = gnorm_a.shape[-1]
    ha, hb, hc = a_log.shape[-1], b_forget.shape[-1], cache_win_k0.shape[3]
    assert hd == LANE and seq % DELTA_CHUNK == 0
    kw_a = conv_a.shape[1]
    win_caches = ((cache_win_k0, cache_win_v0), (cache_win_k1, cache_win_v1), (cache_win_k2, cache_win_v2))

    c_all = jnp.concatenate([c_prompt, c_sample], axis=0)
    c_all = jnp.pad(c_all, ((0, MOD_ROWS - c_all.shape[0]), (0, 0)))
    mod = _mod(c_all, w_mod, b_mod)
    mods = [(mod[l, :bsz].reshape(bsz, 6, d), mod[l, bsz:bsz + dbsz].reshape(dbsz, 6, d))
            for l in range(depth)]

    groups = [dict(x=x_prompt.reshape(bsz * seq, d), nb=bsz, t=seq, gi=0),
              dict(x=x_sample.reshape(dbsz * dseq, d), nb=dbsz, t=dseq, gi=1)]
    for gr in groups:
        m_0 = mods[0][gr["gi"]]
        _, gr["h"] = _post_pre(gr["x"], gr["nb"], gr["t"], shift=m_0[:, 0], scale=m_0[:, 1],
                               w_pre=norm_pre_mix[0])

    fwts = _ffn_weights(w_up, w_gate, conv_ffn_w, w_down)
    outs = {k: ([], []) for k in ("delta", "conv", "k", "v", "logf", "ffn")}
    wk = [([], []) for _ in C_GROUPS]
    wv = [([], []) for _ in C_GROUPS]
    for l in range(depth):
        if l % 2 == 0:
            e = l // 2
            ewts = _even_weights(w_in_e[e], conv_a[e], a_log[e], dt_bias[e], b_forget[e], w_out_e[e],
                                 ha, hb, hd)
        else:
            o = l // 2
            w_in_o_b, w_out_o_b = w_in_o[o], w_out_o[o]
        hs, nbs, ts = [[gr[key] for gr in groups] for key in ("h", "nb", "t")]
        if l % 2 == 0:
            conv_bufs = (jnp.zeros((bsz, kw_a - 1, 3 * ha * hd), F32), state_conv_qkv[e])
            s0s = (jnp.zeros((bsz, ha, hd, hd), F32), state_delta[e])
            past = (cache_k[e], cache_v[e], cache_logf[e], page_table)
            mixed = _even_mixer(hs, nbs, ts, ewts, gnorm_a[e], conv_bufs, s0s, ha, hb, hd, past)
            for gi, (_, s_new, new_conv, kb, vb, logf) in enumerate(mixed):
                for key, val in (("delta", s_new), ("conv", new_conv), ("k", kb), ("v", vb), ("logf", logf)):
                    outs[key][gi].append(val)
        else:
            bufs = tuple((bk[o], bv[o]) for bk, bv in win_caches)
            mixed = _odd_mixer(hs, nbs, ts, w_in_o_b, w_out_o_b, hc, hd, bufs)
            for gi, (_, new_bufs) in enumerate(mixed):
                for g in range(len(C_GROUPS)):
                    wk[g][gi].append(new_bufs[g][0])
                    wv[g][gi].append(new_bufs[g][1])
        for gr in groups:
            gi, nb, t = gr["gi"], gr["nb"], gr["t"]
            m_l = mods[l][gi]
            om = mixed[gi][0]
            gr["x"], gr["h"] = _post_pre(gr["x"], nb, t, o=om, gate=m_l[:, 2], w_post=norm_post_mix[l],
                                         shift=m_l[:, 3], scale=m_l[:, 4], w_pre=norm_pre_ffn[l])
        ffn = _conv_ffn([gr["h"] for gr in groups], nbs, ts, fwts, l, state_ffn_conv[l])
        for gr in groups:
            gi, nb, t = gr["gi"], gr["nb"], gr["t"]
            m_l = mods[l][gi]
            of, new_buf = ffn[gi]
            outs["ffn"][gi].append(new_buf)
            if l + 1 < depth:
                m_n = mods[l + 1][gi]
                gr["x"], gr["h"] = _post_pre(gr["x"], nb, t, o=of, gate=m_l[:, 5], w_post=norm_post_ffn[l],
                                             shift=m_n[:, 0], scale=m_n[:, 1], w_pre=norm_pre_mix[l + 1])
            else:
                gr["x"], _ = _post_pre(gr["x"], nb, t, o=of, gate=m_l[:, 5], w_post=norm_post_ffn[l])

    stk = lambda lst: jnp.stack(lst, axis=0)
    res = [groups[0]["x"].reshape(bsz, seq, d), groups[1]["x"].reshape(dbsz, dseq, d)]
    for key in ("delta", "conv", "k", "v", "logf"):
        res += [stk(outs[key][0]), stk(outs[key][1])]
    for g in range(len(C_GROUPS)):
        res += [stk(wk[g][0]), stk(wk[g][1]), stk(wv[g][0]), stk(wv[g][1])]
    res += [stk(outs["ffn"][0]), stk(outs["ffn"][1])]
    return tuple(res)
```

```python
import functools
import math

import jax
import jax.numpy as jnp
from jax import lax
from jax.experimental import pallas as pl
from jax.experimental.pallas import tpu as pltpu

F32 = jnp.float32
BF16 = jnp.bfloat16
HI = lax.Precision.HIGHEST

C_GROUPS = ((128, 1), (512, 4), (2048, 16))
DELTA_CHUNK = 64
RMS_EPS = 1e-6
L2_EPS = 1e-6
NEG = -1e30
LANE = 128
SUBLANE = 8
VMEM_LIMIT = 58 * 1024 * 1024
MOD_ROWS = 16
FFN_TILE = 512


def _cp(*sem):
    return pltpu.CompilerParams(dimension_semantics=sem, vmem_limit_bytes=VMEM_LIMIT)


def _dot(a, b, prec=None):
    return jnp.dot(a, b, preferred_element_type=F32, precision=prec)


def _dot_nt(a, b, prec=None):
    return lax.dot_general(a, b, (((1,), (1,)), ((), ())), preferred_element_type=F32, precision=prec)


def _dot_tn(a, b, prec=None):
    return lax.dot_general(a, b, (((0,), (0,)), ((), ())), preferred_element_type=F32, precision=prec)


def _dot3(a, b):
    ah, bh = a.astype(BF16), b.astype(BF16)
    al = (a - ah.astype(F32)).astype(BF16)
    bl = (b - bh.astype(F32)).astype(BF16)
    return _dot(ah, bh) + (_dot(ah, bl) + _dot(al, bh))


def _sigmoid(x):
    return 1.0 / (1.0 + jnp.exp(-x))


def _silu(x):
    return x * _sigmoid(x)


def _softplus(x):
    return jnp.maximum(x, 0.0) + jnp.log1p(jnp.exp(-jnp.abs(x)))


def _gelu_tanh(x):
    return 0.5 * x * (1.0 + jnp.tanh(0.7978845608028654 * (x + 0.044715 * (x * x * x))))


def _rms(x, w):
    return x * lax.rsqrt(jnp.mean(x * x, axis=-1, keepdims=True) + RMS_EPS) * w


def _iota(shape, dim):
    return lax.broadcasted_iota(jnp.int32, shape, dim)


def _row_to_col(row):
    n = row.shape[1]
    eye = _iota((n, n), 0) == _iota((n, n), 1)
    return jnp.sum(jnp.where(eye, jnp.broadcast_to(row, (n, n)), 0.0), axis=1, keepdims=True)


def _col_to_row(col):
    n = col.shape[0]
    eye = _iota((n, n), 0) == _iota((n, n), 1)
    return jnp.sum(jnp.where(eye, jnp.broadcast_to(col, (n, n)), 0.0), axis=0, keepdims=True)


def _pick_tile(n, pref):
    if n <= pref:
        return n
    t = (pref // LANE) * LANE
    while t >= LANE:
        if n % t == 0:
            return t
        t -= LANE
    return n


def _log2(n):
    assert n > 0 and n & (n - 1) == 0, n
    return n.bit_length() - 1


def _mod_kernel(c_ref, w_ref, b_ref, o_ref):
    a = _silu(c_ref[...]).astype(BF16)
    o_ref[...] = _dot(a, w_ref[...].astype(BF16)) + b_ref[...]


def _mod(c_all, w_mod, b_mod):
    depth, d, n = w_mod.shape
    r = c_all.shape[0]
    tn = _pick_tile(n, 512)
    return pl.pallas_call(
        _mod_kernel,
        grid=(depth, n // tn),
        in_specs=[pl.BlockSpec((r, d), lambda l, j: (0, 0)),
                  pl.BlockSpec((None, d, tn), lambda l, j: (l, 0, j)),
                  pl.BlockSpec((None, 1, tn), lambda l, j: (l, 0, j))],
        out_specs=pl.BlockSpec((None, r, tn), lambda l, j: (l, 0, j)),
        out_shape=jax.ShapeDtypeStruct((depth, r, n), F32),
        compiler_params=_cp("arbitrary", "arbitrary"),
        name="mod",
    )(c_all, w_mod, b_mod.reshape(depth, 1, n))


def _post_pre_kernel(*refs, has_o, has_h):
    refs = list(refs)
    x = refs.pop(0)[...]
    if has_o:
        o_ref, gate_ref, wpost_ref = refs.pop(0), refs.pop(0), refs.pop(0)
    if has_h:
        shift_ref, scale_ref, wpre_ref = refs.pop(0), refs.pop(0), refs.pop(0)
    if has_o:
        x = x + gate_ref[...] * _rms(o_ref[...].astype(F32), wpost_ref[...])
        refs.pop(0)[...] = x
    if has_h:
        h = _rms(x, wpre_ref[...]) * (1.0 + scale_ref[...]) + shift_ref[...]
        refs.pop(0)[...] = h.astype(BF16)


def _post_pre(x, nb, t, o=None, gate=None, w_post=None, shift=None, scale=None, w_pre=None):
    m, d = x.shape
    has_o, has_h = o is not None, shift is not None
    if t >= 128:
        tr = 128
        per = t // tr
        vec = lambda v: v.reshape(nb, 1, d)
        vec_spec = pl.BlockSpec((None, 1, d), lambda i: (i // per, 0, 0))
    else:
        tr = m
        vec = lambda v: jnp.repeat(v, t, axis=0).reshape(1, m, d)
        vec_spec = pl.BlockSpec((None, m, d), lambda i: (0, 0, 0))
    row_spec = pl.BlockSpec((tr, d), lambda i: (i, 0))
    w_spec = pl.BlockSpec((1, d), lambda i: (0, 0))
    args, in_specs, out_specs, out_shape = [x], [row_spec], [], []
    if has_o:
        args += [o, vec(gate), w_post.reshape(1, d)]
        in_specs += [row_spec, vec_spec, w_spec]
        out_specs.append(row_spec)
        out_shape.append(jax.ShapeDtypeStruct((m, d), F32))
    if has_h:
        args += [vec(shift), vec(scale), w_pre.reshape(1, d)]
        in_specs += [vec_spec, vec_spec, w_spec]
        out_specs.append(row_spec)
        out_shape.append(jax.ShapeDtypeStruct((m, d), BF16))
    outs = pl.pallas_call(
        functools.partial(_post_pre_kernel, has_o=has_o, has_h=has_h),
        grid=(m // tr,), in_specs=in_specs, out_specs=out_specs, out_shape=out_shape,
        compiler_params=_cp("arbitrary"), name="post_pre",
    )(*args)
    outs = list(outs)
    x_new = outs.pop(0) if has_o else x
    h = outs.pop(0) if has_h else None
    return x_new, h


def _mm_kernel(*refs, n_pairs, nk):
    a_refs, b_refs = refs[:n_pairs], refs[n_pairs:2 * n_pairs]
    o_ref = refs[2 * n_pairs]
    part = _dot(a_refs[0][...], b_refs[0][...])
    for a_ref, b_ref in zip(a_refs[1:], b_refs[1:]):
        part = part + _dot(a_ref[...], b_ref[...])
    if nk == 1:
        o_ref[...] = part.astype(o_ref.dtype)
        return
    acc_ref = refs[2 * n_pairs + 1]
    k = pl.program_id(2)

    @pl.when(k == 0)
    def _():
        acc_ref[...] = part

    @pl.when(k > 0)
    def _():
        acc_ref[...] += part

    @pl.when(k == nk - 1)
    def _():
        o_ref[...] = acc_ref[...].astype(o_ref.dtype)


def _matmul(a_list, b_list, out_dtype, tm=1024, tn=1024, tk=None, cols=None, layer=None):
    m, kdim = a_list[0].shape
    c0, n = (0, b_list[0].shape[-1]) if cols is None else cols
    tm = min(tm, m)
    tn = _pick_tile(n, tn)
    tk = kdim if tk is None else _pick_tile(kdim, tk)
    nk = kdim // tk
    assert m % tm == 0 and n % tn == 0 and kdim % tk == 0 and c0 % tn == 0
    jb = c0 // tn
    n_pairs = len(a_list)
    if layer is None:
        b_spec = pl.BlockSpec((tk, tn), lambda i, j, k: (k, j + jb))
    else:
        b_spec = pl.BlockSpec((None, tk, tn), lambda i, j, k: (layer, k, j + jb))
    in_specs = [pl.BlockSpec((tm, tk), lambda i, j, k: (i, k))] * n_pairs + [b_spec] * n_pairs
    scratch = [pltpu.VMEM((tm, tn), F32)] if nk > 1 else []
    return pl.pallas_call(
        functools.partial(_mm_kernel, n_pairs=n_pairs, nk=nk),
        grid=(m // tm, n // tn, nk), in_specs=in_specs,
        out_specs=pl.BlockSpec((tm, tn), lambda i, j, k: (i, j)),
        out_shape=jax.ShapeDtypeStruct((m, n), out_dtype),
        scratch_shapes=scratch,
        compiler_params=_cp("arbitrary", "arbitrary", "arbitrary"), name="matmul",
    )(*a_list, *b_list)


def _mmw_kernel(*refs, n_pairs, ranges, nm, extra):
    n_out = len(ranges)
    a_refs, refs = refs[:n_pairs], refs[n_pairs:]
    if extra:
        a2_refs, refs = refs[:n_pairs], refs[n_pairs:]
    b_refs, refs = refs[:n_pairs], refs[n_pairs:]
    o_refs, refs = refs[:n_out], refs[n_out:]
    if extra:
        o2_refs, refs = refs[:n_out], refs[n_out:]
    wb_refs = refs
    j, i = pl.program_id(0), pl.program_id(1)

    @pl.when(i == 0)
    def _():
        for b_ref, wb_ref in zip(b_refs, wb_refs):
            wb_ref[...] = b_ref[...].astype(BF16)

    def product(lhs_refs, out_refs):
        acc = _dot(lhs_refs[0][...], wb_refs[0][...])
        for a_ref, wb_ref in zip(lhs_refs[1:], wb_refs[1:]):
            acc = acc + _dot(a_ref[...], wb_ref[...])
        if n_out == 1:
            out_refs[0][...] = acc.astype(out_refs[0].dtype)
            return
        for o_ref, (lo, hi) in zip(out_refs, ranges):
            @pl.when((j >= lo) & (j < hi))
            def _(o_ref=o_ref):
                o_ref[...] = acc.astype(o_ref.dtype)

    if not extra:
        product(a_refs, o_refs)
        return

    @pl.when(i < nm)
    def _():
        product(a_refs, o_refs)

    @pl.when(i == nm)
    def _():
        product(a2_refs, o2_refs)


def _matmul_w32(a_list, w_list, out_dtype, out_cols=None, col0=0, tm=512, tn=1024, w_buffers=2,
                a2_list=None):
    m, kdim = a_list[0].shape
    tm = min(tm, m)
    out_cols = [w_list[0].shape[1]] if out_cols is None else out_cols
    n = sum(out_cols)
    tn = _pick_tile(math.gcd(col0, *out_cols), tn)
    assert m % tm == 0
    nm = m // tm
    jb = col0 // tn
    row_blk = [p if w.shape[0] != kdim else 0 for p, w in enumerate(w_list)]
    ranges, lo = [], 0
    for c in out_cols:
        ranges.append((lo, lo + c // tn))
        lo += c // tn
    n_pairs = len(a_list)
    extra = a2_list is not None
    m2 = a2_list[0].shape[0] if extra else 0

    def out_map(lo, hi, second=False):
        def index(j, i):
            inside = (j >= lo) & (j < hi)
            ii = jnp.where(inside, jnp.minimum(i, nm - 1), jnp.where(j < lo, 0, nm - 1))
            return (0 if second else ii), jnp.clip(j - lo, 0, hi - lo - 1)
        return index

    in_specs = [pl.BlockSpec((tm, kdim), lambda j, i: (jnp.minimum(i, nm - 1), 0))] * n_pairs
    out_specs = [pl.BlockSpec((tm, tn), out_map(lo, hi)) for lo, hi in ranges]
    out_shape = [jax.ShapeDtypeStruct((m, c), out_dtype) for c in out_cols]
    args = list(a_list)
    if extra:
        in_specs += [pl.BlockSpec((m2, kdim), lambda j, i: (0, 0))] * n_pairs
        out_specs += [pl.BlockSpec((m2, tn), out_map(lo, hi, True)) for lo, hi in ranges]
        out_shape += [jax.ShapeDtypeStruct((m2, c), out_dtype) for c in out_cols]
        args += list(a2_list)
    in_specs += [pl.BlockSpec((kdim, tn), lambda j, i, rb=rb: (rb, j + jb),
                              pipeline_mode=pl.Buffered(w_buffers)) for rb in row_blk]
    outs = pl.pallas_call(
        functools.partial(_mmw_kernel, n_pairs=n_pairs, ranges=ranges, nm=nm, extra=extra),
        grid=(n // tn, nm + (1 if extra else 0)),
        in_specs=in_specs, out_specs=out_specs, out_shape=out_shape,
        scratch_shapes=[pltpu.VMEM((kdim, tn), BF16)] * n_pairs,
        compiler_params=_cp("arbitrary", "arbitrary"), name="matmul_w32",
    )(*args, *w_list)
    k = len(out_cols)
    first = outs[0] if k == 1 else list(outs[:k])
    if not extra:
        return first
    return first, (outs[k] if k == 1 else list(outs[k:]))


def _gates_kernel(h_ref, w_ref, par_ref, g_ref, f_ref, carry_ref, *, ha, hb, seg):
    @pl.when(pl.program_id(1) == 0)
    def _():
        carry_ref[...] = jnp.zeros_like(carry_ref)

    p = _dot(h_ref[...], w_ref[...].astype(BF16))
    tr = p.shape[0]
    lane = _iota(p.shape, 1)
    a_log, bias = par_ref[0:1, :], par_ref[1:2, :]
    x = p + bias
    beta = _sigmoid(p)
    g = -jnp.exp(a_log) * _softplus(x)
    logf = -_softplus(-x)
    out = jnp.where(lane < ha, beta,
                    jnp.where(lane < 2 * ha, g, jnp.where(lane < 2 * ha + hb, logf, 0.0)))
    g_ref[...] = out
    rr, cc = _iota((tr, tr), 0), _iota((tr, tr), 1)
    tri = rr >= cc
    if seg is not None:
        tri = tri & ((rr >> _log2(seg)) == (cc >> _log2(seg)))
    cs = _dot(tri.astype(F32), out, HI) + carry_ref[0:1, :]
    f_ref[...] = cs
    carry_ref[0:1, :] = cs[tr - 1:tr, :]


def _gates(h, w_small, par, nb, t, ha, hb):
    m, d = h.shape
    if t >= 128:
        tr, per, seg, nb_grid = min(t, 512), t // min(t, 512), None, nb
    else:
        tr, per, seg, nb_grid = m, 1, t, 1
    spec = pl.BlockSpec((tr, LANE), lambda b, j: (b * per + j, 0))
    return pl.pallas_call(
        functools.partial(_gates_kernel, ha=ha, hb=hb, seg=seg),
        grid=(nb_grid, per),
        in_specs=[pl.BlockSpec((tr, d), lambda b, j: (b * per + j, 0)),
                  pl.BlockSpec((d, LANE), lambda b, j: (0, 0)),
                  pl.BlockSpec((SUBLANE, LANE), lambda b, j: (0, 0))],
        out_specs=[spec, spec],
        out_shape=[jax.ShapeDtypeStruct((m, LANE), F32)] * 2,
        scratch_shapes=[pltpu.VMEM((SUBLANE, LANE), F32)],
        compiler_params=_cp("arbitrary", "arbitrary"), name="gates",
    )(h, w_small, par)


def _each(f, *lists):
    return [f(*args) for args in zip(*lists)]


def _unit_lower_inverse(lows):
    c = lows[0].shape[0]
    assert c == 64
    r, s = _iota((c, c), 0), _iota((c, c), 1)
    eye = (r == s).astype(F32)
    same16 = (r >> 4) == (s >> 4)
    same32 = (r >> 5) == (s >> 5)
    ld = _each(lambda low: jnp.where(same16, low, 0.0), lows)
    x = _each(lambda a: eye - a, ld)
    p = _each(lambda a: _dot3(a, a), ld)
    for _ in range(2):
        xp = _each(lambda a, b: _dot3(jnp.concatenate([a, b], axis=0), b), x, p)
        x = _each(lambda a, b: a + b[:c], x, xp)
        p = _each(lambda b: b[c:], xp)
    x = _each(lambda a, b: a + _dot3(a, b), x, p)
    for keep in (same32 & jnp.logical_not(same16), jnp.logical_not(same32)):
        off = _each(lambda low: jnp.where(keep, low, 0.0), lows)
        y = _each(_dot3, x, off)
        x = _each(lambda a, b: a - _dot3(b, a), x, y)
    return x


def _delta_kernel(q_ref, k_ref, v_ref, z_ref, gt_ref, cwq_ref, cwk_ref, cwv_ref,
                  cbq_ref, cbk_ref, cbv_ref, s0_ref, gn_ref, o_ref, sfin_ref, ext_ref, s_ref,
                  *, hpb, ha, kw, valid_rows, nc):
    c_idx = pl.program_id(2)
    hblk = pl.program_id(1)
    c = q_ref.shape[0]
    hd = LANE

    @pl.when(c_idx == 0)
    def _():
        s_ref[...] = s0_ref[...]
        for i, cb in enumerate((cbq_ref, cbk_ref, cbv_ref)):
            ext_ref[i, 0:SUBLANE, :] = cb[...]

    conv = []
    for i, (r_ref, cw_ref) in enumerate(((q_ref, cwq_ref), (k_ref, cwk_ref), (v_ref, cwv_ref))):
        ext_ref[i, SUBLANE:SUBLANE + c, :] = r_ref[...]
        acc = ext_ref[i, SUBLANE:SUBLANE + c, :] * cw_ref[kw - 1:kw, :]
        for j in range(kw - 1):
            acc = acc + ext_ref[i, pl.ds(SUBLANE - (kw - 1) + j, c), :] * cw_ref[j:j + 1, :]
        ext_ref[i, 0:SUBLANE, :] = ext_ref[i, c:c + SUBLANE, :]
        acc = _silu(acc)
        if valid_rows < c:
            acc = jnp.where(_iota(acc.shape, 0) < valid_rows, acc, 0.0)
        conv.append(acc)
    xq, xk, xv = conv

    gt = gt_ref[...]
    rr, ss = _iota((c, c), 0), _iota((c, c), 1)
    incl, strict = rr >= ss, rr > ss
    gcum = _dot(incl.astype(F32), gt, HI)
    lane = _iota(gt.shape, 1)
    gn = gn_ref[...]
    heads = list(range(hpb))
    sls = [slice(j * hd, (j + 1) * hd) for j in heads]
    beta = [jnp.sum(jnp.where(lane == hblk * hpb + j, gt, 0.0), axis=1, keepdims=True) for j in heads]
    gc = [jnp.sum(jnp.where(lane == ha + hblk * hpb + j, gcum, 0.0), axis=1, keepdims=True) for j in heads]
    qh = [xq[:, sl] for sl in sls]
    qh = _each(lambda a: a * lax.rsqrt(jnp.sum(a * a, axis=-1, keepdims=True) + L2_EPS) * (hd ** -0.5), qh)
    kh = [xk[:, sl] for sl in sls]
    kh = _each(lambda a: a * lax.rsqrt(jnp.sum(a * a, axis=-1, keepdims=True) + L2_EPS), kh)
    decay = _each(lambda g: jnp.where(incl, jnp.exp(g - _col_to_row(g)), 0.0), gc)
    kb = _each(lambda a, b: a * b, kh, beta)
    vb = [xv[:, sl] * b for sl, b in zip(sls, beta)]
    egc = _each(jnp.exp, gc)
    kk = _each(lambda a, q, k: _dot_nt(jnp.concatenate([a, q], axis=0).astype(BF16), k.astype(BF16)),
               kb, qh, kh)
    low = _each(lambda a, d: jnp.where(strict, a[:c] * d, 0.0), kk, decay)
    attn = _each(lambda a, d: (a[c:] * d).astype(BF16), kk, decay)
    tmat = _unit_lower_inverse(low)
    uw = _each(lambda t, v, k, e: _dot(t.astype(BF16), jnp.concatenate([v, k * e], axis=1).astype(BF16)),
               tmat, vb, kb, egc)
    st = [s_ref[j] for j in heads]
    ws = _each(lambda a, q, e, s: _dot(jnp.concatenate([a[:, hd:], q * e], axis=0).astype(BF16),
                                       s.astype(BF16)), uw, qh, egc, st)
    v_new = _each(lambda a, b: (a[:, :hd] - b[:c]).astype(BF16), uw, ws)
    o = _each(lambda b, a, v: b[c:] + _dot(a, v), ws, attn, v_new)
    g_last = _each(lambda g: g[c - 1:c, :], gc)
    k_dec = _each(lambda k, gl, g: (k * jnp.exp(gl - g)).astype(BF16), kh, g_last, gc)
    s_new = _each(lambda s, gl, k, v: s * jnp.exp(gl) + _dot_tn(k, v), st, g_last, k_dec, v_new)
    for j in heads:
        s_ref[j] = s_new[j]
        o_ref[:, sls[j]] = (_rms(o[j], gn) * _silu(z_ref[:, sls[j]])).astype(o_ref.dtype)

    @pl.when(c_idx == nc - 1)
    def _():
        sfin_ref[...] = s_ref[...]


def _delta(proj, gates, conv_w, conv_buf8, s0, gnorm, nb, tp, valid_rows, ha, hpb=16):
    c = DELTA_CHUNK
    hd = LANE
    hpb = min(hpb, ha)
    w = hpb * hd
    nc = tp // c
    nhb = ha // hpb
    kw = conv_w.shape[0]
    cw8 = jnp.pad(conv_w, ((0, SUBLANE - kw), (0, 0)))

    def col(off):
        return pl.BlockSpec((c, w), lambda b, h, i, off=off: (b * nc + i, off * nhb + h))

    def cwspec(off):
        return pl.BlockSpec((SUBLANE, w), lambda b, h, i, off=off: (0, off * nhb + h))

    def cbspec(off):
        return pl.BlockSpec((None, SUBLANE, w), lambda b, h, i, off=off: (b, 0, off * nhb + h))

    state_spec = pl.BlockSpec((None, hpb, hd, hd), lambda b, h, i: (b, h, 0, 0))
    return pl.pallas_call(
        functools.partial(_delta_kernel, hpb=hpb, ha=ha, kw=kw, valid_rows=valid_rows, nc=nc),
        grid=(nb, nhb, nc),
        in_specs=[col(0), col(1), col(2), col(3),
                  pl.BlockSpec((c, LANE), lambda b, h, i: (b * nc + i, 0)),
                  cwspec(0), cwspec(1), cwspec(2), cbspec(0), cbspec(1), cbspec(2),
                  state_spec, pl.BlockSpec((1, hd), lambda b, h, i: (0, 0))],
        out_specs=[pl.BlockSpec((c, w), lambda b, h, i: (b * nc + i, h)), state_spec],
        out_shape=[jax.ShapeDtypeStruct((nb * tp, ha * hd), BF16),
                   jax.ShapeDtypeStruct(s0.shape, F32)],
        scratch_shapes=[pltpu.VMEM((3, c + SUBLANE, w), F32), pltpu.VMEM((hpb, hd, hd), F32)],
        compiler_params=_cp("arbitrary", "arbitrary", "arbitrary"), name="delta",
    )(proj, proj, proj, proj, gates, cw8, cw8, cw8, conv_buf8, conv_buf8, conv_buf8, s0,
      gnorm.reshape(1, hd))


def _softmax_step(carry, s, v, valid=None):
    m, l, acc = carry
    m_new = jnp.maximum(m, jnp.max(s, axis=-1, keepdims=True))
    alpha = jnp.exp(m - m_new)
    p = jnp.exp(s - m_new)
    if valid is not None:
        p = jnp.where(valid, p, 0.0)
    l = alpha * l + jnp.sum(p, axis=-1, keepdims=True)
    acc = alpha * acc + _dot(p.astype(BF16), v)
    return m_new, l, acc


def _softmax_init(rows, hd):
    return (jnp.full((rows, 1), NEG, F32), jnp.zeros((rows, 1), F32), jnp.zeros((rows, hd), F32))


def _fox_prompt_kernel(q_ref, k_ref, v_ref, f_ref, o_ref, kb_ref, vb_ref, *, tq, tk, scale, par):
    seq = q_ref.shape[0]
    kb_ref[...] = k_ref[...].astype(BF16)
    vb_ref[...] = v_ref[...].astype(BF16)
    col_minus_row = _iota((tq, tk), 1) - _iota((tq, tk), 0)
    nqb = seq // tq
    for c0 in range(0, nqb, par):
        chunk = list(range(c0, min(c0 + par, nqb)))
        qs = [(q_ref[pl.ds(qb * tq, tq), :] * scale).astype(BF16) for qb in chunk]
        fqs = [_row_to_col(f_ref[:, pl.ds(qb * tq, tq)]) for qb in chunk]
        carries = [_softmax_init(tq, LANE) for _ in chunk]
        n_steps = [-(-((qb + 1) * tq) // tk) for qb in chunk]
        for step in range(max(n_steps)):
            live = [i for i in range(len(chunk)) if step < n_steps[i]]
            k = kb_ref[pl.ds(step * tk, tk), :]
            v = vb_ref[pl.ds(step * tk, tk), :]
            fk = f_ref[:, pl.ds(step * tk, tk)]
            ss = [_dot_nt(qs[i], k) + (fqs[i] - fk) for i in live]
            for n, i in enumerate(live):
                if step * tk + tk - 1 > chunk[i] * tq:
                    ss[n] = jnp.where(col_minus_row <= chunk[i] * tq - step * tk, ss[n], NEG)
            ms = [jnp.maximum(carries[i][0], jnp.max(s, axis=-1, keepdims=True)) for i, s in zip(live, ss)]
            ps = [jnp.exp(s - m) for s, m in zip(ss, ms)]
            alphas = [jnp.exp(carries[i][0] - m) for i, m in zip(live, ms)]
            pvs = [_dot(p.astype(BF16), v) for p in ps]
            for i, m, p, a, pv in zip(live, ms, ps, alphas, pvs):
                carries[i] = (m, a * carries[i][1] + jnp.sum(p, axis=-1, keepdims=True),
                              a * carries[i][2] + pv)
        for qb, (_, l, acc) in zip(chunk, carries):
            o_ref[pl.ds(qb * tq, tq), :] = (acc / l).astype(o_ref.dtype)


def _fox_prompt(q_arr, q_col, k_arr, v_arr, f_rows, nb, s, nh, tq=128, tk=512, par=8):
    tq, tk = min(tq, s), min(tk, s)
    assert s % tq == 0 and s % tk == 0
    hd = LANE
    return pl.pallas_call(
        functools.partial(_fox_prompt_kernel, tq=tq, tk=tk, scale=hd ** -0.5, par=par),
        grid=(nb, nh),
        in_specs=[pl.BlockSpec((s, hd), lambda b, h: (b, q_col + h)),
                  pl.BlockSpec((s, hd), lambda b, h: (b, h)),
                  pl.BlockSpec((s, hd), lambda b, h: (b, h)),
                  pl.BlockSpec((None, None, 1, s), lambda b, h: (b, h, 0, 0))],
        out_specs=pl.BlockSpec((s, hd), lambda b, h: (b, h)),
        out_shape=jax.ShapeDtypeStruct((nb * s, nh * hd), BF16),
        scratch_shapes=[pltpu.VMEM((s, hd), BF16), pltpu.VMEM((s, hd), BF16)],
        compiler_params=_cp("arbitrary", "arbitrary"), name="fox_prompt",
    )(q_arr, k_arr, v_arr, f_rows)


def _fpast_kernel(pt_ref, *refs, group):
    lp_refs, o_ref, carry_ref = refs[:group], refs[group], refs[group + 1]

    @pl.when(pl.program_id(1) == 0)
    def _():
        carry_ref[...] = jnp.zeros_like(carry_ref)

    p = lp_refs[0].shape[0]
    upper = (_iota((p, p), 1) > _iota((p, p), 0)).astype(F32)
    for i in range(group):
        lp = lp_refs[i][...]
        o_ref[group - 1 - i] = -(_dot(upper, lp, HI) + carry_ref[0:1, :])
        carry_ref[0:1, :] = carry_ref[0:1, :] + jnp.sum(lp, axis=0, keepdims=True)


def _fpast(logf_pool, page_table):
    db, n_pages = page_table.shape
    _, p, nh = logf_pool.shape
    group = 8 if n_pages % 8 == 0 else 1
    nblk = n_pages // group

    def lp_spec(i):
        return pl.BlockSpec((None, p, nh),
                            lambda b, j, pt, i=i: (pt[b, n_pages - 1 - (j * group + i)], 0, 0))

    return pl.pallas_call(
        functools.partial(_fpast_kernel, group=group),
        grid_spec=pltpu.PrefetchScalarGridSpec(
            num_scalar_prefetch=1, grid=(db, nblk),
            in_specs=[lp_spec(i) for i in range(group)],
            out_specs=pl.BlockSpec((None, group, p, nh), lambda b, j, pt: (b, nblk - 1 - j, 0, 0)),
            scratch_shapes=[pltpu.VMEM((SUBLANE, nh), F32)]),
        out_shape=jax.ShapeDtypeStruct((db, n_pages, p, nh), F32),
        compiler_params=_cp("arbitrary", "arbitrary"), name="fpast",
    )(page_table, *([logf_pool] * group))


def _fox_sample_kernel(pt_ref, *refs, nh, scale, n_steps, pp):
    q_ref, k_refs, v_refs = refs[0], refs[1:1 + pp], refs[1 + pp:1 + 2 * pp]
    (fp_ref, kn_ref, vn_ref, fn_ref, o_ref, m_ref, l_ref, acc_ref, bias_ref, s0_ref,
     s1_ref) = refs[1 + 2 * pp:]
    p = pl.program_id(1)
    r = q_ref.shape[0]
    rows = k_refs[0].shape[0]
    lognh = _log2(nh)
    fn = fn_ref[...]

    @pl.when(p == 0)
    def _():
        m_ref[...] = jnp.full_like(m_ref, NEG)
        l_ref[...] = jnp.zeros_like(l_ref)
        acc_ref[...] = jnp.zeros_like(acc_ref)
        s1_ref[...] = jnp.zeros_like(s1_ref)
        match = (_iota((r, rows), 0) & (nh - 1)) == (_iota((r, rows), 1) & (nh - 1))
        bias_ref[...] = jnp.where(match, _row_to_col(fn), NEG)

    q = (q_ref[...] * scale).astype(BF16)

    def update(s, v):
        m, l, acc = _softmax_step((m_ref[...], l_ref[...], acc_ref[...]), s, v)
        m_ref[...], l_ref[...], acc_ref[...] = m, l, acc

    def step(s_w, s_r):
        for i in range(pp):
            s_w[:, i * rows:(i + 1) * rows] = (_dot_nt(q, k_refs[i][...].astype(BF16))
                                               + (bias_ref[...] - fp_ref[i]))
        live = p > 0
        s = s_r[...]
        m_old = m_ref[...]
        m_new = jnp.where(live, jnp.maximum(m_old, jnp.max(s, axis=-1, keepdims=True)), m_old)
        pr = jnp.where(live, jnp.exp(s - m_new), 0.0)
        alpha = jnp.exp(m_old - m_new)
        l_ref[...] = alpha * l_ref[...] + jnp.sum(pr, axis=-1, keepdims=True)
        pr = pr.astype(BF16)
        pv = _dot(pr[:, :rows], v_refs[0][...].astype(BF16))
        for i in range(1, pp):
            pv = pv + _dot(pr[:, i * rows:(i + 1) * rows], v_refs[i][...].astype(BF16))
        acc_ref[...] = alpha * acc_ref[...] + pv
        m_ref[...] = m_new

    @pl.when(p % 2 == 0)
    def _():
        step(s0_ref, s1_ref)

    @pl.when(p % 2 == 1)
    def _():
        step(s1_ref, s0_ref)

    @pl.when(p == n_steps)
    def _():
        rr, cc = _iota((r, r), 0), _iota((r, r), 1)
        ok = ((rr & (nh - 1)) == (cc & (nh - 1))) & ((cc >> lognh) <= (rr >> lognh))
        s2 = _dot_nt(q, kn_ref[...].astype(BF16)) + _row_to_col(fn) - fn
        update(jnp.where(ok, s2, NEG), vn_ref[...].astype(BF16))
        o_ref[...] = (acc_ref[...] / l_ref[...]).astype(o_ref.dtype)


def _fox_sample(q, k_new, v_new, f_new, k_pool, v_pool, f_past, page_table, nh):
    db, r, hd = q.shape
    n_pages = page_table.shape[1]
    rows = k_pool.shape[1]
    tok = pl.BlockSpec((None, r, hd), lambda b, p, pt: (b, 0, 0))
    pp = max(k for k in (4, 2, 1) if n_pages % k == 0)
    n_steps = n_pages // pp
    last = n_steps - 1
    k_pages = [pl.BlockSpec((None, rows, hd),
                            lambda b, p, pt, i=i: (pt[b, jnp.minimum(p, last) * pp + i], 0, 0))
               for i in range(pp)]
    v_pages = [pl.BlockSpec((None, rows, hd),
                            lambda b, p, pt, i=i: (pt[b, jnp.maximum(p - 1, 0) * pp + i], 0, 0))
               for i in range(pp)]
    return pl.pallas_call(
        functools.partial(_fox_sample_kernel, nh=nh, scale=hd ** -0.5, n_steps=n_steps, pp=pp),
        grid_spec=pltpu.PrefetchScalarGridSpec(
            num_scalar_prefetch=1, grid=(db, n_steps + 1),
            in_specs=[tok] + k_pages + v_pages + [
                pl.BlockSpec((None, pp, 1, rows), lambda b, p, pt: (b, jnp.minimum(p, last), 0, 0)),
                tok, tok, pl.BlockSpec((None, 1, r), lambda b, p, pt: (b, 0, 0))],
            out_specs=tok,
            scratch_shapes=[pltpu.VMEM((r, 1), F32), pltpu.VMEM((r, 1), F32),
                            pltpu.VMEM((r, hd), F32), pltpu.VMEM((r, rows), F32),
                            pltpu.VMEM((r, pp * rows), F32), pltpu.VMEM((r, pp * rows), F32)]),
        out_shape=jax.ShapeDtypeStruct((db, r, hd), BF16),
        compiler_params=_cp("arbitrary", "arbitrary"), name="fox_sample",
    )(page_table, q, *([k_pool] * pp), *([v_pool] * pp), f_past, k_new, v_new, f_new)


def _dil_prompt_kernel(*refs, groups, tb, scale, par):
    ng = len(groups)
    q_refs, k_refs, v_refs = refs[:ng], refs[ng:2 * ng], refs[2 * ng:3 * ng]
    o_ref, m_ref, l_ref, acc_ref = refs[3 * ng:]
    seq = q_refs[0].shape[0]
    diff = _iota((tb, tb), 0) - _iota((tb, tb), 1)

    def rows_of(res, blk, dil):
        return pl.ds(res + blk * tb * dil, tb, stride=dil) if dil > 1 else pl.ds(blk * tb, tb)

    for g, (window, dil) in enumerate(groups):
        reach = window // dil
        nqb = seq // dil // tb
        back = -(-reach // tb)
        units = [(res, qb) for res in range(dil) for qb in range(nqb)]
        for c0 in range(0, len(units), par):
            chunk = units[c0:c0 + par]
            rows = [rows_of(res, qb, dil) for res, qb in chunk]
            qs = [(q_refs[g][rw, :] * scale).astype(BF16) for rw in rows]
            if g == 0:
                carries = [_softmax_init(tb, LANE) for _ in chunk]
            else:
                carries = [(m_ref[rw, :], l_ref[rw, :], acc_ref[rw, :]) for rw in rows]
            for step in range(back + 1):
                live = [i for i, (_, qb) in enumerate(chunk) if qb - step >= 0]
                if not live:
                    continue
                krows = [rows_of(chunk[i][0], chunk[i][1] - step, dil) for i in live]
                ks = [k_refs[g][rw, :].astype(BF16) for rw in krows]
                vs = [v_refs[g][rw, :].astype(BF16) for rw in krows]
                ss = [_dot_nt(qs[i], k) for i, k in zip(live, ks)]
                if step * tb - (tb - 1) < 0 or step * tb + (tb - 1) > reach:
                    dist = diff + step * tb
                    valid = (dist >= 0) & (dist <= reach)
                    ss = [jnp.where(valid, s, NEG) for s in ss]
                ms = [jnp.maximum(carries[i][0], jnp.max(s, axis=-1, keepdims=True)) for i, s in zip(live, ss)]
                ps = [jnp.exp(s - m) for s, m in zip(ss, ms)]
                alphas = [jnp.exp(carries[i][0] - m) for i, m in zip(live, ms)]
                pvs = [_dot(p.astype(BF16), v) for p, v in zip(ps, vs)]
                for i, m, p, a, pv in zip(live, ms, ps, alphas, pvs):
                    carries[i] = (m, a * carries[i][1] + jnp.sum(p, axis=-1, keepdims=True),
                                  a * carries[i][2] + pv)
            for rw, (m, l, acc) in zip(rows, carries):
                m_ref[rw, :], l_ref[rw, :], acc_ref[rw, :] = m, l, acc
    o_ref[...] = (acc_ref[...] / l_ref[...]).astype(o_ref.dtype)


def _dil_prompt(srcs, nb, s, nh, groups, tb=128, par=8):
    hd = LANE
    ng = len(groups)
    tb = min(tb, s)
    for window, dil in groups:
        assert s % (dil * tb) == 0 and window % dil == 0
    args, specs = [], []
    for which in range(3):
        for g in range(ng):
            arr, col = srcs[g][which]
            args.append(arr)
            specs.append(pl.BlockSpec((s, hd), lambda b, h, col=col: (b, col + h)))
    return pl.pallas_call(
        functools.partial(_dil_prompt_kernel, groups=groups, tb=tb, scale=hd ** -0.5, par=par),
        grid=(nb, nh),
        in_specs=specs,
        out_specs=pl.BlockSpec((s, hd), lambda b, h: (b, h)),
        out_shape=jax.ShapeDtypeStruct((nb * s, nh * hd), BF16),
        scratch_shapes=[pltpu.VMEM((s, 1), F32), pltpu.VMEM((s, 1), F32), pltpu.VMEM((s, hd), F32)],
        compiler_params=_cp("arbitrary", "arbitrary"), name="dil_prompt",
    )(*args)


def _dil_sample_kernel(*refs, groups, wbs, tb, nh, scale, t):
    ng = len(groups)
    q_refs, kb_refs, vb_refs = refs[:ng], refs[ng:2 * ng], refs[2 * ng:3 * ng]
    kn_refs, vn_refs = refs[3 * ng:4 * ng], refs[4 * ng:5 * ng]
    o_ref, m_ref, l_ref, acc_ref = refs[5 * ng:5 * ng + 4]
    j = pl.program_id(1)
    r = q_refs[0].shape[0]
    rows = tb * nh
    lognh = _log2(nh)
    nblks = [wb // tb for wb in wbs]
    total = sum(nblks)

    @pl.when(j == 0)
    def _():
        m_ref[...] = jnp.full_like(m_ref, NEG)
        l_ref[...] = jnp.zeros_like(l_ref)
        acc_ref[...] = jnp.zeros_like(acc_ref)

    def update(s, v, valid):
        m, l, acc = _softmax_step((m_ref[...], l_ref[...], acc_ref[...]),
                                  jnp.where(valid, s, NEG), v, valid)
        m_ref[...], l_ref[...], acc_ref[...] = m, l, acc

    start = 0
    for g, (window, dil) in enumerate(groups):
        def block(g=g, window=window, dil=dil, start=start):
            rr, cc = _iota((r, rows), 0), _iota((r, rows), 1)
            key_tok = (j - start) * tb + (cc >> lognh)
            delta = wbs[g] + (rr >> lognh) - key_tok
            valid = (((rr & (nh - 1)) == (cc & (nh - 1))) & ((delta & (dil - 1)) == 0)
                     & (delta <= window))
            q = (q_refs[g][...] * scale).astype(BF16)
            update(_dot_nt(q, kb_refs[g][...].astype(BF16)), vb_refs[g][...].astype(BF16), valid)

        pl.when((j >= start) & (j < start + nblks[g]))(block)
        start += nblks[g]

    @pl.when(j == total - 1)
    def _():
        rr, cc = _iota((r, r), 0), _iota((r, r), 1)
        delta = (rr >> lognh) - (cc >> lognh)
        match = (rr & (nh - 1)) == (cc & (nh - 1))
        for g, (window, dil) in enumerate(groups):
            valid = match & (delta >= 0) & ((delta & (dil - 1)) == 0) & (delta <= window)
            q = (q_refs[g][...] * scale).astype(BF16)
            update(_dot_nt(q, kn_refs[g][...].astype(BF16)), vn_refs[g][...].astype(BF16), valid)
        o_ref[...] = (acc_ref[...] / l_ref[...]).astype(o_ref.dtype)


def _dil_sample(qs, kns, vns, kbufs, vbufs, groups, nh, t):
    db, r, hd = qs[0].shape
    ng = len(groups)
    wbs = [kb.shape[1] // nh for kb in kbufs]
    tb = 128
    while any(wb % tb for wb in wbs):
        tb //= 2
    nblks = [wb // tb for wb in wbs]
    starts = [sum(nblks[:g]) for g in range(ng)]
    total = sum(nblks)
    tok = pl.BlockSpec((None, r, hd), lambda b, j: (b, 0, 0))

    def buf_spec(g):
        return pl.BlockSpec(
            (None, tb * nh, hd),
            lambda b, j, g=g: (b, jnp.clip(j - starts[g], 0, nblks[g] - 1), 0))

    bspecs = [buf_spec(g) for g in range(ng)]
    return pl.pallas_call(
        functools.partial(_dil_sample_kernel, groups=groups, wbs=wbs, tb=tb, nh=nh,
                          scale=hd ** -0.5, t=t),
        grid=(db, total),
        in_specs=[tok] * ng + bspecs + bspecs + [tok] * (2 * ng),
        out_specs=tok,
        out_shape=jax.ShapeDtypeStruct((db, r, hd), BF16),
        scratch_shapes=[pltpu.VMEM((r, 1), F32), pltpu.VMEM((r, 1), F32), pltpu.VMEM((r, hd), F32)],
        compiler_params=_cp("arbitrary", "arbitrary"), name="dil_sample",
    )(*qs, *kbufs, *vbufs, *kns, *vns)


def _ffn_up_kernel(*refs, kw, tiles_per_batch, t_s, nm):
    nh = kw - 1
    hp_ref, hs_ref, wu_ref, wg_ref, cw_ref = refs[:5]
    halo_refs = refs[5:5 + nh]
    actp_ref, tailp_ref, acts_ref, gates_ref, ext_ref, wub_ref, wgb_ref = refs[5 + nh:]
    m = pl.program_id(1)
    tn = wub_ref.shape[1]

    @pl.when(m == 0)
    def _():
        wub_ref[...] = wu_ref[...].astype(BF16)
        wgb_ref[...] = wg_ref[...].astype(BF16)

    def conv_gelu(gate, up, fix):
        rows = gate.shape[0]
        ext_ref[SUBLANE:SUBLANE + rows, :] = gate
        gc = gate * cw_ref[kw - 1:kw, :]
        for j in range(kw - 1):
            shift = kw - 1 - j
            gc = gc + fix(ext_ref[pl.ds(SUBLANE - shift, rows), :], shift) * cw_ref[j:j + 1, :]
        return _gelu_tanh(gc) * up

    @pl.when(m < nm)
    def _():
        h = hp_ref[...]
        tm = h.shape[0]
        up = _dot(h, wub_ref[...])
        gate = _dot(h, wgb_ref[...])
        prev = jnp.where((m % tiles_per_batch) == 0, 0.0, ext_ref[tm:tm + SUBLANE, :])
        ext_ref[0:SUBLANE, :] = prev
        tailp_ref[...] = gate[tm - SUBLANE:, :]
        actp_ref[...] = conv_gelu(gate, up, lambda sh, shift: sh).astype(actp_ref.dtype)

    @pl.when(m == nm)
    def _():
        h = hs_ref[...]
        up = _dot(h, wub_ref[...])
        gate = _dot(h, wgb_ref[...])
        ext_ref[0:SUBLANE, :] = jnp.zeros((SUBLANE, tn), F32)
        gates_ref[...] = gate
        pos = _iota(gate.shape, 0) & (t_s - 1)
        fix = lambda sh, shift: jnp.where(pos < shift, halo_refs[shift - 1][...], sh)
        acts_ref[...] = conv_gelu(gate, up, fix).astype(acts_ref.dtype)


def _ffn_up(h_p, h_s, wu, wg, layer, cw8, kw, t_p, t_s, halos, tm=512, tn=FFN_TILE):
    m, d = h_p.shape
    ms = h_s.shape[0]
    n_pad = wu.shape[-1]
    tm = min(tm, t_p)
    tn = min(tn, -(-n_pad // LANE) * LANE)
    nm, nn = m // tm, -(-n_pad // tn)
    _log2(t_s)
    last = nm - 1
    s_spec = pl.BlockSpec((ms, tn), lambda j, i: (0, j))
    in_specs = [pl.BlockSpec((tm, d), lambda j, i: (jnp.minimum(i, last), 0)),
                pl.BlockSpec((ms, d), lambda j, i: (0, 0)),
                pl.BlockSpec((None, d, tn), lambda j, i: (layer, 0, j)),
                pl.BlockSpec((None, d, tn), lambda j, i: (layer, 0, j)),
                pl.BlockSpec((SUBLANE, tn), lambda j, i: (0, j))] + [s_spec] * len(halos)
    return pl.pallas_call(
        functools.partial(_ffn_up_kernel, kw=kw, tiles_per_batch=max(t_p // tm, 1), t_s=t_s, nm=nm),
        grid=(nn, nm + 1), in_specs=in_specs,
        out_specs=[pl.BlockSpec((tm, tn), lambda j, i: (jnp.minimum(i, last), j)),
                   pl.BlockSpec((None, SUBLANE, tn), lambda j, i: (jnp.minimum(i, last), 0, j)),
                   s_spec, s_spec],
        out_shape=[jax.ShapeDtypeStruct((m, n_pad), BF16),
                   jax.ShapeDtypeStruct((nm, SUBLANE, n_pad), F32),
                   jax.ShapeDtypeStruct((ms, n_pad), BF16), jax.ShapeDtypeStruct((ms, n_pad), F32)],
        scratch_shapes=[pltpu.VMEM((max(tm, ms) + SUBLANE, tn), F32), pltpu.VMEM((d, tn), BF16),
                        pltpu.VMEM((d, tn), BF16)],
        compiler_params=_cp("arbitrary", "arbitrary"), name="ffn_up",
    )(h_p, h_s, wu, wg, cw8, *halos)


def _even_weights(w_in, conv_w, a_log, dt_bias, b_f, w_out, ha, hb, hd):
    wa, wb = ha * hd, hb * hd
    o2 = 4 * wa
    o4 = o2 + 2 * ha
    o5 = o4 + 3 * wb
    w_main = (w_in, w_in[:, o4:o5])
    small = jnp.concatenate([w_in[:, o2:o4], w_in[:, o5:]], axis=1)
    assert small.shape[1] <= LANE
    w_small = jnp.pad(small, ((0, 0), (0, LANE - small.shape[1])))
    par = jnp.zeros((SUBLANE, LANE), F32)
    par = par.at[0, ha:2 * ha].set(a_log.astype(F32))
    par = par.at[1, ha:2 * ha].set(dt_bias.astype(F32))
    par = par.at[1, 2 * ha:2 * ha + hb].set(b_f.astype(F32))
    return w_main, w_small, par, conv_w, w_out


def _even_mixer(hs, nbs, ts, wts, gnorm, conv_bufs, s0s, ha, hb, hd, past):
    (w_in, w_fox), _, _, _, w_out = wts
    wa, wb = ha * hd, hb * hd
    projs = _matmul_w32([hs[0]], [w_in], F32, out_cols=[4 * wa], tm=1024, w_buffers=1, a2_list=[hs[1]])
    foxes = _matmul_w32([hs[0]], [w_fox], F32, out_cols=[wb, wb, wb], tm=512, w_buffers=1,
                        a2_list=[hs[1]])
    cores = [_even_core(hs[g], nbs[g], ts[g], projs[g], foxes[g], wts, gnorm, conv_bufs[g], s0s[g], ha, hb,
                        hd, past if g == 1 else None) for g in (0, 1)]
    outs = _matmul_w32([cores[0][0], cores[0][1]], [w_out, w_out], BF16, tm=512,
                       a2_list=[cores[1][0], cores[1][1]])
    return [(outs[g],) + tuple(cores[g][2:]) for g in (0, 1)]


def _even_core(h, nb, t, proj, fox, wts, gnorm, conv_buf, s0, ha, hb, hd, past):
    _, w_small, par, conv_w, _ = wts
    wa, wb = ha * hd, hb * hd
    kw = conv_w.shape[0]
    q_fox, k_new, v_new = fox
    gates, fcum = _gates(h, w_small, par, nb, t, ha, hb)

    c = DELTA_CHUNK
    tp = -(-t // c) * c
    conv_buf8 = jnp.pad(conv_buf.astype(F32), ((0, 0), (SUBLANE - (kw - 1), 0), (0, 0)))
    if tp == t:
        proj_d, gates_d = proj, gates
    else:
        pad = lambda a: jnp.pad(a.reshape(nb, t, -1), ((0, 0), (0, tp - t), (0, 0))).reshape(nb * tp, -1)
        proj_d, gates_d = pad(proj), pad(gates)
    o_a, s_new = _delta(proj_d, gates_d, conv_w, conv_buf8, s0, gnorm, nb, tp, min(t, c) if tp != t else c,
                        ha)
    if tp != t:
        o_a = o_a.reshape(nb, tp, wa)[:, :t].reshape(nb * t, wa)

    logf = gates[:, 2 * ha:2 * ha + hb].reshape(nb, t, hb)
    f_new = fcum[:, 2 * ha:2 * ha + hb].reshape(nb, t, hb)
    kb, vb = k_new.reshape(nb, t, hb, hd), v_new.reshape(nb, t, hb, hd)
    if past is None:
        f_rows = jnp.swapaxes(f_new, 1, 2).reshape(nb, hb, 1, t)
        o_b = _fox_prompt(q_fox, 0, k_new, v_new, f_rows, nb, t, hb)
    else:
        k_pool, v_pool, logf_pool, page_table = past
        n_pool, page = k_pool.shape[0], k_pool.shape[1]
        f_past = _fpast(logf_pool.astype(F32), page_table)
        f_past = f_past.reshape(nb, page_table.shape[1], 1, page * hb)
        o_b = _fox_sample(q_fox.reshape(nb, t * hb, hd), k_new.reshape(nb, t * hb, hd),
                          v_new.reshape(nb, t * hb, hd), f_new.reshape(nb, 1, t * hb),
                          k_pool.reshape(n_pool, page * hb, hd), v_pool.reshape(n_pool, page * hb, hd),
                          f_past, page_table, hb)
        o_b = o_b.reshape(nb * t, wb)
    keep = min(kw - 1, t)
    raw_tail = proj.reshape(nb, t, -1)[:, t - keep:, :3 * wa]
    new_conv = jnp.concatenate([conv_buf.astype(F32), raw_tail], axis=1)[:, -(kw - 1):]
    return o_a, o_b, s_new, new_conv, kb, vb, logf


def _odd_mixer(hs, nbs, ts, w_in, w_out, nh, hd, bufs):
    ng = len(C_GROUPS)
    wc = nh * hd
    n_main = (3 * ng - 2) * wc
    main, main_s = _matmul_w32([hs[0]], [w_in], F32, out_cols=[n_main], tm=1024, w_buffers=1,
                               a2_list=[hs[1]])
    (k_last, v_last), kv_s = _matmul_w32([hs[0]], [w_in], F32, out_cols=[wc, wc], col0=n_main, tm=512,
                                         w_buffers=1, a2_list=[hs[1]])
    nb, t = nbs[0], ts[0]
    srcs = [((main, g * 3 * nh), (main, g * 3 * nh + nh), (main, g * 3 * nh + 2 * nh))
            for g in range(ng - 1)]
    srcs.append(((main, (ng - 1) * 3 * nh), (k_last, 0), (v_last, 0)))
    o_p = _dil_prompt(srcs, nb, t, nh, C_GROUPS)
    m3 = main.reshape(nb, t, -1)
    bufs_p = []
    for g, (window, _) in enumerate(C_GROUPS):
        keep = min(window, t)
        if g < ng - 1:
            kv = [m3[:, t - keep:, (g * 3 + r) * wc:(g * 3 + r + 1) * wc] for r in (1, 2)]
        else:
            kv = [a.reshape(nb, t, wc)[:, t - keep:] for a in (k_last, v_last)]
        bufs_p.append(tuple(a.reshape(nb, keep, nh, hd) for a in kv))
    o_s, bufs_s = _odd_sample_core(jnp.concatenate([main_s] + list(kv_s), axis=1), nbs[1], ts[1], nh, hd, bufs)
    out_p, out_s = _matmul_w32([o_p], [w_out], BF16, tm=512, a2_list=[o_s])
    return (out_p, bufs_p), (out_s, bufs_s)


def _odd_sample_core(proj, nb, t, nh, hd, bufs):
    ng = len(C_GROUPS)
    new_bufs = []
    p6 = proj.reshape(nb, t, ng, 3, nh, hd)
    flat = lambda a: a.reshape(nb, -1, hd)
    qs = [flat(p6[:, :, g, 0]) for g in range(ng)]
    kns = [flat(p6[:, :, g, 1]) for g in range(ng)]
    vns = [flat(p6[:, :, g, 2]) for g in range(ng)]
    o = _dil_sample(qs, kns, vns, [flat(bk) for bk, _ in bufs], [flat(bv) for _, bv in bufs],
                    C_GROUPS, nh, t)
    o = o.reshape(nb * t, nh * hd)
    for g in range(ng):
        bk, bv = bufs[g]
        wb = bk.shape[1]
        new_bufs.append((jnp.concatenate([bk, p6[:, :, g, 1]], axis=1)[:, -wb:],
                         jnp.concatenate([bv, p6[:, :, g, 2]], axis=1)[:, -wb:]))
    return o, new_bufs


def _ffn_weights(w_up, w_gate, conv_w, w_down):
    d_ff = w_up.shape[-1]
    kw = conv_w.shape[1]
    cw8 = jnp.pad(conv_w.astype(F32), ((0, 0), (0, SUBLANE - kw), (0, 0)))
    return w_up, w_gate, cw8, w_down.astype(BF16), kw, d_ff


def _conv_ffn(hs, nbs, ts, wts, layer, buf_s):
    wu, wg, cw8, wd, kw, d_ff = wts
    (nb, nbs_), (t, t_s) = nbs, ts
    halos = []
    for shift in range(1, kw):
        hl = jnp.zeros((nbs_, t_s, d_ff), F32)
        hl = hl.at[:, :shift].set(buf_s.astype(F32)[:, (kw - 1) - shift:])
        halos.append(hl.reshape(nbs_ * t_s, d_ff))
    act_p, tail, act_s, gate_s = _ffn_up(hs[0], hs[1], wu, wg, layer, cw8[layer], kw, t, t_s, halos)
    tiles = tail.shape[0] // nb
    gate_tail = tail.reshape(nb, tiles, SUBLANE, d_ff)[:, -1]
    new_buf_p = gate_tail[:, SUBLANE - (kw - 1):]
    new_buf_s = jnp.concatenate([buf_s.astype(F32), gate_s.reshape(nbs_, t_s, d_ff)], axis=1)[:, -(kw - 1):]
    out_p = _matmul([act_p], [wd], BF16, tm=512, tn=512, layer=layer)
    out_s = _matmul([act_s], [wd], BF16, tm=act_s.shape[0], tn=512, layer=layer)
    return (out_p, new_buf_p), (out_s, new_buf_s)


def kernel(x_prompt, x_sample, c_prompt, c_sample, state_delta, state_conv_qkv, cache_k, cache_v, cache_logf, cache_win_k0, cache_win_v0, cache_win_k1, cache_win_v1, cache_win_k2, cache_win_v2, state_ffn_conv, page_table, w_mod, b_mod, norm_pre_mix, norm_post_mix, norm_pre_ffn, norm_post_ffn, w_in_e, conv_a, a_log, dt_bias, gnorm_a, b_forget, w_out_e, w_in_o, w_out_o, w_up, w_gate, conv_ffn_w, w_down):
    bsz, seq, d = x_prompt.shape
    dbsz, dseq, _ = x_sample.shape
    depth = w_mod.shape[0]
    hd = gnorm_a.shape[-1]
    ha, hb, hc = a_log.shape[-1], b_forget.shape[-1], cache_win_k0.shape[3]
    assert hd == LANE and seq % DELTA_CHUNK == 0
    kw_a = conv_a.shape[1]
    win_caches = ((cache_win_k0, cache_win_v0), (cache_win_k1, cache_win_v1), (cache_win_k2, cache_win_v2))

    c_all = jnp.concatenate([c_prompt, c_sample], axis=0)
    c_all = jnp.pad(c_all, ((0, MOD_ROWS - c_all.shape[0]), (0, 0)))
    mod = _mod(c_all, w_mod, b_mod)
    mods = [(mod[l, :bsz].reshape(bsz, 6, d), mod[l, bsz:bsz + dbsz].reshape(dbsz, 6, d))
            for l in range(depth)]

    groups = [dict(x=x_prompt.reshape(bsz * seq, d), nb=bsz, t=seq, gi=0),
              dict(x=x_sample.reshape(dbsz * dseq, d), nb=dbsz, t=dseq, gi=1)]
    for gr in groups:
        m_0 = mods[0][gr["gi"]]
        _, gr["h"] = _post_pre(gr["x"], gr["nb"], gr["t"], shift=m_0[:, 0], scale=m_0[:, 1],
                               w_pre=norm_pre_mix[0])

    fwts = _ffn_weights(w_up, w_gate, conv_ffn_w, w_down)
    outs = {k: ([], []) for k in ("delta", "conv", "k", "v", "logf", "ffn")}
    wk = [([], []) for _ in C_GROUPS]
    wv = [([], []) for _ in C_GROUPS]
    for l in range(depth):
        if l % 2 == 0:
            e = l // 2
            ewts = _even_weights(w_in_e[e], conv_a[e], a_log[e], dt_bias[e], b_forget[e], w_out_e[e],
                                 ha, hb, hd)
        else:
            o = l // 2
            w_in_o_b, w_out_o_b = w_in_o[o], w_out_o[o]
        hs, nbs, ts = [[gr[key] for gr in groups] for key in ("h", "nb", "t")]
        if l % 2 == 0:
            conv_bufs = (jnp.zeros((bsz, kw_a - 1, 3 * ha * hd), F32), state_conv_qkv[e])
            s0s = (jnp.zeros((bsz, ha, hd, hd), F32), state_delta[e])
            past = (cache_k[e], cache_v[e], cache_logf[e], page_table)
            mixed = _even_mixer(hs, nbs, ts, ewts, gnorm_a[e], conv_bufs, s0s, ha, hb, hd, past)
            for gi, (_, s_new, new_conv, kb, vb, logf) in enumerate(mixed):
                for key, val in (("delta", s_new), ("conv", new_conv), ("k", kb), ("v", vb), ("logf", logf)):
                    outs[key][gi].append(val)
        else:
            bufs = tuple((bk[o], bv[o]) for bk, bv in win_caches)
            mixed = _odd_mixer(hs, nbs, ts, w_in_o_b, w_out_o_b, hc, hd, bufs)
            for gi, (_, new_bufs) in enumerate(mixed):
                for g in range(len(C_GROUPS)):
                    wk[g][gi].append(new_bufs[g][0])
                    wv[g][gi].append(new_bufs[g][1])
        for gr in groups:
            gi, nb, t = gr["gi"], gr["nb"], gr["t"]
            m_l = mods[l][gi]
            om = mixed[gi][0]
            gr["x"], gr["h"] = _post_pre(gr["x"], nb, t, o=om, gate=m_l[:, 2], w_post=norm_post_mix[l],
                                         shift=m_l[:, 3], scale=m_l[:, 4], w_pre=norm_pre_ffn[l])
        ffn = _conv_ffn([gr["h"] for gr in groups], nbs, ts, fwts, l, state_ffn_conv[l])
        for gr in groups:
            gi, nb, t = gr["gi"], gr["nb"], gr["t"]
            m_l = mods[l][gi]
            of, new_buf = ffn[gi]
            outs["ffn"][gi].append(new_buf)
            if l + 1 < depth:
                m_n = mods[l + 1][gi]
                gr["x"], gr["h"] = _post_pre(gr["x"], nb, t, o=of, gate=m_l[:, 5], w_post=norm_post_ffn[l],
                                             shift=m_n[:, 0], scale=m_n[:, 1], w_pre=norm_pre_mix[l + 1])
            else:
                gr["x"], _ = _post_pre(gr["x"], nb, t, o=of, gate=m_l[:, 5], w_post=norm_post_ffn[l])

    stk = lambda lst: jnp.stack(lst, axis=0)
    res = [groups[0]["x"].reshape(bsz, seq, d), groups[1]["x"].reshape(dbsz, dseq, d)]
    for key in ("delta", "conv", "k", "v", "logf"):
        res += [stk(outs[key][0]), stk(outs[key][1])]
    for g in range(len(C_GROUPS)):
        res += [stk(wk[g][0]), stk(wk[g][1]), stk(wv[g][0]), stk(wv[g][1])]
    res += [stk(outs["ffn"][0]), stk(outs["ffn"][1])]
    return tuple(res)
```

```python
import functools
import math

import jax
import jax.numpy as jnp
from jax import lax
from jax.experimental import pallas as pl
from jax.experimental.pallas import tpu as pltpu

F32 = jnp.float32
BF16 = jnp.bfloat16
HI = lax.Precision.HIGHEST

C_GROUPS = ((128, 1), (512, 4), (2048, 16))
DELTA_CHUNK = 64
RMS_EPS = 1e-6
L2_EPS = 1e-6
NEG = -1e30
LANE = 128
SUBLANE = 8
VMEM_LIMIT = 58 * 1024 * 1024
MOD_ROWS = 16
FFN_TILE = 512


def _cp(*sem):
    return pltpu.CompilerParams(dimension_semantics=sem, vmem_limit_bytes=VMEM_LIMIT)


def _dot(a, b, prec=None):
    return jnp.dot(a, b, preferred_element_type=F32, precision=prec)


def _dot_nt(a, b, prec=None):
    return lax.dot_general(a, b, (((1,), (1,)), ((), ())), preferred_element_type=F32, precision=prec)


def _dot_tn(a, b, prec=None):
    return lax.dot_general(a, b, (((0,), (0,)), ((), ())), preferred_element_type=F32, precision=prec)


def _dot3(a, b):
    ah, bh = a.astype(BF16), b.astype(BF16)
    al = (a - ah.astype(F32)).astype(BF16)
    bl = (b - bh.astype(F32)).astype(BF16)
    return _dot(ah, bh) + (_dot(ah, bl) + _dot(al, bh))


def _sigmoid(x):
    return 1.0 / (1.0 + jnp.exp(-x))


def _silu(x):
    return x * _sigmoid(x)


def _softplus(x):
    return jnp.maximum(x, 0.0) + jnp.log1p(jnp.exp(-jnp.abs(x)))


def _gelu_tanh(x):
    return 0.5 * x * (1.0 + jnp.tanh(0.7978845608028654 * (x + 0.044715 * (x * x * x))))


def _rms(x, w):
    return x * lax.rsqrt(jnp.mean(x * x, axis=-1, keepdims=True) + RMS_EPS) * w


def _iota(shape, dim):
    return lax.broadcasted_iota(jnp.int32, shape, dim)


def _row_to_col(row):
    n = row.shape[1]
    eye = _iota((n, n), 0) == _iota((n, n), 1)
    return jnp.sum(jnp.where(eye, jnp.broadcast_to(row, (n, n)), 0.0), axis=1, keepdims=True)


def _col_to_row(col):
    n = col.shape[0]
    eye = _iota((n, n), 0) == _iota((n, n), 1)
    return jnp.sum(jnp.where(eye, jnp.broadcast_to(col, (n, n)), 0.0), axis=0, keepdims=True)


def _pick_tile(n, pref):
    if n <= pref:
        return n
    t = (pref // LANE) * LANE
    while t >= LANE:
        if n % t == 0:
            return t
        t -= LANE
    return n


def _log2(n):
    assert n > 0 and n & (n - 1) == 0, n
    return n.bit_length() - 1


def _mod_kernel(c_ref, w_ref, b_ref, o_ref):
    a = _silu(c_ref[...]).astype(BF16)
    o_ref[...] = _dot(a, w_ref[...].astype(BF16)) + b_ref[...]


def _mod(c_all, w_mod, b_mod):
    depth, d, n = w_mod.shape
    r = c_all.shape[0]
    tn = _pick_tile(n, 512)
    return pl.pallas_call(
        _mod_kernel,
        grid=(depth, n // tn),
        in_specs=[pl.BlockSpec((r, d), lambda l, j: (0, 0)),
                  pl.BlockSpec((None, d, tn), lambda l, j: (l, 0, j)),
                  pl.BlockSpec((None, 1, tn), lambda l, j: (l, 0, j))],
        out_specs=pl.BlockSpec((None, r, tn), lambda l, j: (l, 0, j)),
        out_shape=jax.ShapeDtypeStruct((depth, r, n), F32),
        compiler_params=_cp("arbitrary", "arbitrary"),
        name="mod",
    )(c_all, w_mod, b_mod.reshape(depth, 1, n))


def _post_pre_kernel(*refs, has_o, has_h):
    refs = list(refs)
    x = refs.pop(0)[...]
    if has_o:
        o_ref, gate_ref, wpost_ref = refs.pop(0), refs.pop(0), refs.pop(0)
    if has_h:
        shift_ref, scale_ref, wpre_ref = refs.pop(0), refs.pop(0), refs.pop(0)
    if has_o:
        x = x + gate_ref[...] * _rms(o_ref[...].astype(F32), wpost_ref[...])
        refs.pop(0)[...] = x
    if has_h:
        h = _rms(x, wpre_ref[...]) * (1.0 + scale_ref[...]) + shift_ref[...]
        refs.pop(0)[...] = h.astype(BF16)


def _post_pre(x, nb, t, o=None, gate=None, w_post=None, shift=None, scale=None, w_pre=None):
    m, d = x.shape
    has_o, has_h = o is not None, shift is not None
    if t >= 128:
        tr = 256 if t % 256 == 0 else 128
        per = t // tr
        vec = lambda v: v.reshape(nb, 1, d)
        vec_spec = pl.BlockSpec((None, 1, d), lambda i: (i // per, 0, 0))
    else:
        tr = m
        vec = lambda v: jnp.repeat(v, t, axis=0).reshape(1, m, d)
        vec_spec = pl.BlockSpec((None, m, d), lambda i: (0, 0, 0))
    row_spec = pl.BlockSpec((tr, d), lambda i: (i, 0))
    w_spec = pl.BlockSpec((1, d), lambda i: (0, 0))
    args, in_specs, out_specs, out_shape = [x], [row_spec], [], []
    if has_o:
        args += [o, vec(gate), w_post.reshape(1, d)]
        in_specs += [row_spec, vec_spec, w_spec]
        out_specs.append(row_spec)
        out_shape.append(jax.ShapeDtypeStruct((m, d), F32))
    if has_h:
        args += [vec(shift), vec(scale), w_pre.reshape(1, d)]
        in_specs += [vec_spec, vec_spec, w_spec]
        out_specs.append(row_spec)
        out_shape.append(jax.ShapeDtypeStruct((m, d), BF16))
    outs = pl.pallas_call(
        functools.partial(_post_pre_kernel, has_o=has_o, has_h=has_h),
        grid=(m // tr,), in_specs=in_specs, out_specs=out_specs, out_shape=out_shape,
        compiler_params=_cp("arbitrary"), name="post_pre",
    )(*args)
    outs = list(outs)
    x_new = outs.pop(0) if has_o else x
    h = outs.pop(0) if has_h else None
    return x_new, h


def _mm_kernel(*refs, n_pairs, nk):
    a_refs, b_refs = refs[:n_pairs], refs[n_pairs:2 * n_pairs]
    o_ref = refs[2 * n_pairs]
    part = _dot(a_refs[0][...], b_refs[0][...])
    for a_ref, b_ref in zip(a_refs[1:], b_refs[1:]):
        part = part + _dot(a_ref[...], b_ref[...])
    if nk == 1:
        o_ref[...] = part.astype(o_ref.dtype)
        return
    acc_ref = refs[2 * n_pairs + 1]
    k = pl.program_id(2)

    @pl.when(k == 0)
    def _():
        acc_ref[...] = part

    @pl.when(k > 0)
    def _():
        acc_ref[...] += part

    @pl.when(k == nk - 1)
    def _():
        o_ref[...] = acc_ref[...].astype(o_ref.dtype)


def _matmul(a_list, b_list, out_dtype, tm=1024, tn=1024, tk=None, cols=None, layer=None):
    m, kdim = a_list[0].shape
    c0, n = (0, b_list[0].shape[-1]) if cols is None else cols
    tm = min(tm, m)
    tn = _pick_tile(n, tn)
    tk = kdim if tk is None else _pick_tile(kdim, tk)
    nk = kdim // tk
    assert m % tm == 0 and n % tn == 0 and kdim % tk == 0 and c0 % tn == 0
    jb = c0 // tn
    n_pairs = len(a_list)
    if layer is None:
        b_spec = pl.BlockSpec((tk, tn), lambda i, j, k: (k, j + jb))
    else:
        b_spec = pl.BlockSpec((None, tk, tn), lambda i, j, k: (layer, k, j + jb))
    in_specs = [pl.BlockSpec((tm, tk), lambda i, j, k: (i, k))] * n_pairs + [b_spec] * n_pairs
    scratch = [pltpu.VMEM((tm, tn), F32)] if nk > 1 else []
    return pl.pallas_call(
        functools.partial(_mm_kernel, n_pairs=n_pairs, nk=nk),
        grid=(m // tm, n // tn, nk), in_specs=in_specs,
        out_specs=pl.BlockSpec((tm, tn), lambda i, j, k: (i, j)),
        out_shape=jax.ShapeDtypeStruct((m, n), out_dtype),
        scratch_shapes=scratch,
        compiler_params=_cp("arbitrary", "arbitrary", "arbitrary"), name="matmul",
    )(*a_list, *b_list)


def _mmw_kernel(*refs, n_pairs, ranges, nm, extra):
    n_out = len(ranges)
    a_refs, refs = refs[:n_pairs], refs[n_pairs:]
    if extra:
        a2_refs, refs = refs[:n_pairs], refs[n_pairs:]
    b_refs, refs = refs[:n_pairs], refs[n_pairs:]
    o_refs, refs = refs[:n_out], refs[n_out:]
    if extra:
        o2_refs, refs = refs[:n_out], refs[n_out:]
    wb_refs = refs
    j, i = pl.program_id(0), pl.program_id(1)

    @pl.when(i == 0)
    def _():
        for b_ref, wb_ref in zip(b_refs, wb_refs):
            wb_ref[...] = b_ref[...].astype(BF16)

    def product(lhs_refs, out_refs):
        acc = _dot(lhs_refs[0][...], wb_refs[0][...])
        for a_ref, wb_ref in zip(lhs_refs[1:], wb_refs[1:]):
            acc = acc + _dot(a_ref[...], wb_ref[...])
        if n_out == 1:
            out_refs[0][...] = acc.astype(out_refs[0].dtype)
            return
        for o_ref, (lo, hi) in zip(out_refs, ranges):
            @pl.when((j >= lo) & (j < hi))
            def _(o_ref=o_ref):
                o_ref[...] = acc.astype(o_ref.dtype)

    if not extra:
        product(a_refs, o_refs)
        return

    @pl.when(i < nm)
    def _():
        product(a_refs, o_refs)

    @pl.when(i == nm)
    def _():
        product(a2_refs, o2_refs)


def _matmul_w32(a_list, w_list, out_dtype, out_cols=None, col0=0, tm=512, tn=1024, w_buffers=2,
                a2_list=None):
    m, kdim = a_list[0].shape
    tm = min(tm, m)
    out_cols = [w_list[0].shape[1]] if out_cols is None else out_cols
    n = sum(out_cols)
    tn = _pick_tile(math.gcd(col0, *out_cols), tn)
    assert m % tm == 0
    nm = m // tm
    jb = col0 // tn
    row_blk = [p if w.shape[0] != kdim else 0 for p, w in enumerate(w_list)]
    ranges, lo = [], 0
    for c in out_cols:
        ranges.append((lo, lo + c // tn))
        lo += c // tn
    n_pairs = len(a_list)
    extra = a2_list is not None
    m2 = a2_list[0].shape[0] if extra else 0

    def out_map(lo, hi, second=False):
        def index(j, i):
            inside = (j >= lo) & (j < hi)
            ii = jnp.where(inside, jnp.minimum(i, nm - 1), jnp.where(j < lo, 0, nm - 1))
            return (0 if second else ii), jnp.clip(j - lo, 0, hi - lo - 1)
        return index

    in_specs = [pl.BlockSpec((tm, kdim), lambda j, i: (jnp.minimum(i, nm - 1), 0))] * n_pairs
    out_specs = [pl.BlockSpec((tm, tn), out_map(lo, hi)) for lo, hi in ranges]
    out_shape = [jax.ShapeDtypeStruct((m, c), out_dtype) for c in out_cols]
    args = list(a_list)
    if extra:
        in_specs += [pl.BlockSpec((m2, kdim), lambda j, i: (0, 0))] * n_pairs
        out_specs += [pl.BlockSpec((m2, tn), out_map(lo, hi, True)) for lo, hi in ranges]
        out_shape += [jax.ShapeDtypeStruct((m2, c), out_dtype) for c in out_cols]
        args += list(a2_list)
    in_specs += [pl.BlockSpec((kdim, tn), lambda j, i, rb=rb: (rb, j + jb),
                              pipeline_mode=pl.Buffered(w_buffers)) for rb in row_blk]
    outs = pl.pallas_call(
        functools.partial(_mmw_kernel, n_pairs=n_pairs, ranges=ranges, nm=nm, extra=extra),
        grid=(n // tn, nm + (1 if extra else 0)),
        in_specs=in_specs, out_specs=out_specs, out_shape=out_shape,
        scratch_shapes=[pltpu.VMEM((kdim, tn), BF16)] * n_pairs,
        compiler_params=_cp("arbitrary", "arbitrary"), name="matmul_w32",
    )(*args, *w_list)
    k = len(out_cols)
    first = outs[0] if k == 1 else list(outs[:k])
    if not extra:
        return first
    return first, (outs[k] if k == 1 else list(outs[k:]))


def _gates_kernel(h_ref, w_ref, par_ref, g_ref, f_ref, carry_ref, *, ha, hb, seg):
    @pl.when(pl.program_id(1) == 0)
    def _():
        carry_ref[...] = jnp.zeros_like(carry_ref)

    p = _dot(h_ref[...], w_ref[...].astype(BF16))
    tr = p.shape[0]
    lane = _iota(p.shape, 1)
    a_log, bias = par_ref[0:1, :], par_ref[1:2, :]
    x = p + bias
    beta = _sigmoid(p)
    g = -jnp.exp(a_log) * _softplus(x)
    logf = -_softplus(-x)
    out = jnp.where(lane < ha, beta,
                    jnp.where(lane < 2 * ha, g, jnp.where(lane < 2 * ha + hb, logf, 0.0)))
    g_ref[...] = out
    rr, cc = _iota((tr, tr), 0), _iota((tr, tr), 1)
    tri = rr >= cc
    if seg is not None:
        tri = tri & ((rr >> _log2(seg)) == (cc >> _log2(seg)))
    cs = _dot(tri.astype(F32), out, HI) + carry_ref[0:1, :]
    f_ref[...] = cs
    carry_ref[0:1, :] = cs[tr - 1:tr, :]


def _gates(h, w_small, par, nb, t, ha, hb):
    m, d = h.shape
    if t >= 128:
        tr, per, seg, nb_grid = min(t, 512), t // min(t, 512), None, nb
    else:
        tr, per, seg, nb_grid = m, 1, t, 1
    spec = pl.BlockSpec((tr, LANE), lambda b, j: (b * per + j, 0))
    return pl.pallas_call(
        functools.partial(_gates_kernel, ha=ha, hb=hb, seg=seg),
        grid=(nb_grid, per),
        in_specs=[pl.BlockSpec((tr, d), lambda b, j: (b * per + j, 0)),
                  pl.BlockSpec((d, LANE), lambda b, j: (0, 0)),
                  pl.BlockSpec((SUBLANE, LANE), lambda b, j: (0, 0))],
        out_specs=[spec, spec],
        out_shape=[jax.ShapeDtypeStruct((m, LANE), F32)] * 2,
        scratch_shapes=[pltpu.VMEM((SUBLANE, LANE), F32)],
        compiler_params=_cp("arbitrary", "arbitrary"), name="gates",
    )(h, w_small, par)


def _each(f, *lists):
    return [f(*args) for args in zip(*lists)]


def _unit_lower_inverse(lows):
    c = lows[0].shape[0]
    assert c == 64
    r, s = _iota((c, c), 0), _iota((c, c), 1)
    eye = (r == s).astype(F32)
    same16 = (r >> 4) == (s >> 4)
    same32 = (r >> 5) == (s >> 5)
    ld = _each(lambda low: jnp.where(same16, low, 0.0), lows)
    x = _each(lambda a: eye - a, ld)
    p = _each(lambda a: _dot3(a, a), ld)
    for _ in range(2):
        xp = _each(lambda a, b: _dot3(jnp.concatenate([a, b], axis=0), b), x, p)
        x = _each(lambda a, b: a + b[:c], x, xp)
        p = _each(lambda b: b[c:], xp)
    x = _each(lambda a, b: a + _dot3(a, b), x, p)
    for keep in (same32 & jnp.logical_not(same16), jnp.logical_not(same32)):
        off = _each(lambda low: jnp.where(keep, low, 0.0), lows)
        y = _each(_dot3, x, off)
        x = _each(lambda a, b: a - _dot3(b, a), x, y)
    return x


def _delta_kernel(q_ref, k_ref, v_ref, z_ref, gt_ref, cwq_ref, cwk_ref, cwv_ref,
                  cbq_ref, cbk_ref, cbv_ref, s0_ref, gn_ref, o_ref, sfin_ref, ext_ref, s_ref,
                  *, hpb, ha, kw, valid_rows, nc):
    c_idx = pl.program_id(2)
    hblk = pl.program_id(1)
    c = q_ref.shape[0]
    hd = LANE

    @pl.when(c_idx == 0)
    def _():
        s_ref[...] = s0_ref[...]
        for i, cb in enumerate((cbq_ref, cbk_ref, cbv_ref)):
            ext_ref[i, 0:SUBLANE, :] = cb[...]

    conv = []
    for i, (r_ref, cw_ref) in enumerate(((q_ref, cwq_ref), (k_ref, cwk_ref), (v_ref, cwv_ref))):
        ext_ref[i, SUBLANE:SUBLANE + c, :] = r_ref[...]
        acc = ext_ref[i, SUBLANE:SUBLANE + c, :] * cw_ref[kw - 1:kw, :]
        for j in range(kw - 1):
            acc = acc + ext_ref[i, pl.ds(SUBLANE - (kw - 1) + j, c), :] * cw_ref[j:j + 1, :]
        ext_ref[i, 0:SUBLANE, :] = ext_ref[i, c:c + SUBLANE, :]
        acc = _silu(acc)
        if valid_rows < c:
            acc = jnp.where(_iota(acc.shape, 0) < valid_rows, acc, 0.0)
        conv.append(acc)
    xq, xk, xv = conv

    gt = gt_ref[...]
    rr, ss = _iota((c, c), 0), _iota((c, c), 1)
    incl, strict = rr >= ss, rr > ss
    gcum = _dot(incl.astype(F32), gt, HI)
    lane = _iota(gt.shape, 1)
    gn = gn_ref[...]
    heads = list(range(hpb))
    sls = [slice(j * hd, (j + 1) * hd) for j in heads]
    beta = [jnp.sum(jnp.where(lane == hblk * hpb + j, gt, 0.0), axis=1, keepdims=True) for j in heads]
    gc = [jnp.sum(jnp.where(lane == ha + hblk * hpb + j, gcum, 0.0), axis=1, keepdims=True) for j in heads]
    qh = [xq[:, sl] for sl in sls]
    qh = _each(lambda a: a * lax.rsqrt(jnp.sum(a * a, axis=-1, keepdims=True) + L2_EPS) * (hd ** -0.5), qh)
    kh = [xk[:, sl] for sl in sls]
    kh = _each(lambda a: a * lax.rsqrt(jnp.sum(a * a, axis=-1, keepdims=True) + L2_EPS), kh)
    decay = _each(lambda g: jnp.where(incl, jnp.exp(g - _col_to_row(g)), 0.0), gc)
    kb = _each(lambda a, b: a * b, kh, beta)
    vb = [xv[:, sl] * b for sl, b in zip(sls, beta)]
    egc = _each(jnp.exp, gc)
    kk = _each(lambda a, q, k: _dot_nt(jnp.concatenate([a, q], axis=0).astype(BF16), k.astype(BF16)),
               kb, qh, kh)
    low = _each(lambda a, d: jnp.where(strict, a[:c] * d, 0.0), kk, decay)
    attn = _each(lambda a, d: (a[c:] * d).astype(BF16), kk, decay)
    tmat = _unit_lower_inverse(low)
    uw = _each(lambda t, v, k, e: _dot(t.astype(BF16), jnp.concatenate([v, k * e], axis=1).astype(BF16)),
               tmat, vb, kb, egc)
    st = [s_ref[j] for j in heads]
    ws = _each(lambda a, q, e, s: _dot(jnp.concatenate([a[:, hd:], q * e], axis=0).astype(BF16),
                                       s.astype(BF16)), uw, qh, egc, st)
    v_new = _each(lambda a, b: (a[:, :hd] - b[:c]).astype(BF16), uw, ws)
    o = _each(lambda b, a, v: b[c:] + _dot(a, v), ws, attn, v_new)
    g_last = _each(lambda g: g[c - 1:c, :], gc)
    k_dec = _each(lambda k, gl, g: (k * jnp.exp(gl - g)).astype(BF16), kh, g_last, gc)
    s_new = _each(lambda s, gl, k, v: s * jnp.exp(gl) + _dot_tn(k, v), st, g_last, k_dec, v_new)
    for j in heads:
        s_ref[j] = s_new[j]
        o_ref[:, sls[j]] = (_rms(o[j], gn) * _silu(z_ref[:, sls[j]])).astype(o_ref.dtype)

    @pl.when(c_idx == nc - 1)
    def _():
        sfin_ref[...] = s_ref[...]


def _delta(proj, gates, conv_w, conv_buf8, s0, gnorm, nb, tp, valid_rows, ha, hpb=16):
    c = DELTA_CHUNK
    hd = LANE
    hpb = min(hpb, ha)
    w = hpb * hd
    nc = tp // c
    nhb = ha // hpb
    kw = conv_w.shape[0]
    cw8 = jnp.pad(conv_w, ((0, SUBLANE - kw), (0, 0)))

    def col(off):
        return pl.BlockSpec((c, w), lambda b, h, i, off=off: (b * nc + i, off * nhb + h))

    def cwspec(off):
        return pl.BlockSpec((SUBLANE, w), lambda b, h, i, off=off: (0, off * nhb + h))

    def cbspec(off):
        return pl.BlockSpec((None, SUBLANE, w), lambda b, h, i, off=off: (b, 0, off * nhb + h))

    state_spec = pl.BlockSpec((None, hpb, hd, hd), lambda b, h, i: (b, h, 0, 0))
    return pl.pallas_call(
        functools.partial(_delta_kernel, hpb=hpb, ha=ha, kw=kw, valid_rows=valid_rows, nc=nc),
        grid=(nb, nhb, nc),
        in_specs=[col(0), col(1), col(2), col(3),
                  pl.BlockSpec((c, LANE), lambda b, h, i: (b * nc + i, 0)),
                  cwspec(0), cwspec(1), cwspec(2), cbspec(0), cbspec(1), cbspec(2),
                  state_spec, pl.BlockSpec((1, hd), lambda b, h, i: (0, 0))],
        out_specs=[pl.BlockSpec((c, w), lambda b, h, i: (b * nc + i, h)), state_spec],
        out_shape=[jax.ShapeDtypeStruct((nb * tp, ha * hd), BF16),
                   jax.ShapeDtypeStruct(s0.shape, F32)],
        scratch_shapes=[pltpu.VMEM((3, c + SUBLANE, w), F32), pltpu.VMEM((hpb, hd, hd), F32)],
        compiler_params=_cp("arbitrary", "arbitrary", "arbitrary"), name="delta",
    )(proj, proj, proj, proj, gates, cw8, cw8, cw8, conv_buf8, conv_buf8, conv_buf8, s0,
      gnorm.reshape(1, hd))


def _softmax_step(carry, s, v, valid=None):
    m, l, acc = carry
    m_new = jnp.maximum(m, jnp.max(s, axis=-1, keepdims=True))
    alpha = jnp.exp(m - m_new)
    p = jnp.exp(s - m_new)
    if valid is not None:
        p = jnp.where(valid, p, 0.0)
    l = alpha * l + jnp.sum(p, axis=-1, keepdims=True)
    acc = alpha * acc + _dot(p.astype(BF16), v)
    return m_new, l, acc


def _softmax_init(rows, hd):
    return (jnp.full((rows, 1), NEG, F32), jnp.zeros((rows, 1), F32), jnp.zeros((rows, hd), F32))


def _fox_prompt_kernel(q_ref, k_ref, v_ref, f_ref, o_ref, kb_ref, vb_ref, *, tq, tk, scale, par):
    seq = q_ref.shape[0]
    kb_ref[...] = k_ref[...].astype(BF16)
    vb_ref[...] = v_ref[...].astype(BF16)
    col_minus_row = _iota((tq, tk), 1) - _iota((tq, tk), 0)
    nqb = seq // tq
    for c0 in range(0, nqb, par):
        chunk = list(range(c0, min(c0 + par, nqb)))
        qs = [(q_ref[pl.ds(qb * tq, tq), :] * scale).astype(BF16) for qb in chunk]
        fqs = [_row_to_col(f_ref[:, pl.ds(qb * tq, tq)]) for qb in chunk]
        carries = [_softmax_init(tq, LANE) for _ in chunk]
        n_steps = [-(-((qb + 1) * tq) // tk) for qb in chunk]
        for step in range(max(n_steps)):
            live = [i for i in range(len(chunk)) if step < n_steps[i]]
            k = kb_ref[pl.ds(step * tk, tk), :]
            v = vb_ref[pl.ds(step * tk, tk), :]
            fk = f_ref[:, pl.ds(step * tk, tk)]
            ss = [_dot_nt(qs[i], k) + (fqs[i] - fk) for i in live]
            for n, i in enumerate(live):
                if step * tk + tk - 1 > chunk[i] * tq:
                    ss[n] = jnp.where(col_minus_row <= chunk[i] * tq - step * tk, ss[n], NEG)
            ms = [jnp.maximum(carries[i][0], jnp.max(s, axis=-1, keepdims=True)) for i, s in zip(live, ss)]
            ps = [jnp.exp(s - m) for s, m in zip(ss, ms)]
            alphas = [jnp.exp(carries[i][0] - m) for i, m in zip(live, ms)]
            pvs = [_dot(p.astype(BF16), v) for p in ps]
            for i, m, p, a, pv in zip(live, ms, ps, alphas, pvs):
                carries[i] = (m, a * carries[i][1] + jnp.sum(p, axis=-1, keepdims=True),
                              a * carries[i][2] + pv)
        for qb, (_, l, acc) in zip(chunk, carries):
            o_ref[pl.ds(qb * tq, tq), :] = (acc / l).astype(o_ref.dtype)


def _fox_prompt(q_arr, q_col, k_arr, v_arr, f_rows, nb, s, nh, tq=128, tk=512, par=8):
    tq, tk = min(tq, s), min(tk, s)
    assert s % tq == 0 and s % tk == 0
    hd = LANE
    return pl.pallas_call(
        functools.partial(_fox_prompt_kernel, tq=tq, tk=tk, scale=hd ** -0.5, par=par),
        grid=(nb, nh),
        in_specs=[pl.BlockSpec((s, hd), lambda b, h: (b, q_col + h)),
                  pl.BlockSpec((s, hd), lambda b, h: (b, h)),
                  pl.BlockSpec((s, hd), lambda b, h: (b, h)),
                  pl.BlockSpec((None, None, 1, s), lambda b, h: (b, h, 0, 0))],
        out_specs=pl.BlockSpec((s, hd), lambda b, h: (b, h)),
        out_shape=jax.ShapeDtypeStruct((nb * s, nh * hd), BF16),
        scratch_shapes=[pltpu.VMEM((s, hd), BF16), pltpu.VMEM((s, hd), BF16)],
        compiler_params=_cp("arbitrary", "arbitrary"), name="fox_prompt",
    )(q_arr, k_arr, v_arr, f_rows)


def _fpast_kernel(pt_ref, *refs, group):
    lp_refs, o_ref, carry_ref = refs[:group], refs[group], refs[group + 1]

    @pl.when(pl.program_id(1) == 0)
    def _():
        carry_ref[...] = jnp.zeros_like(carry_ref)

    p = lp_refs[0].shape[0]
    upper = (_iota((p, p), 1) > _iota((p, p), 0)).astype(F32)
    for i in range(group):
        lp = lp_refs[i][...]
        o_ref[group - 1 - i] = -(_dot(upper, lp, HI) + carry_ref[0:1, :])
        carry_ref[0:1, :] = carry_ref[0:1, :] + jnp.sum(lp, axis=0, keepdims=True)


def _fpast(logf_pool, page_table):
    db, n_pages = page_table.shape
    _, p, nh = logf_pool.shape
    group = max(k for k in (16, 8, 4, 2, 1) if n_pages % k == 0)
    nblk = n_pages // group

    def lp_spec(i):
        return pl.BlockSpec((None, p, nh),
                            lambda b, j, pt, i=i: (pt[b, n_pages - 1 - (j * group + i)], 0, 0))

    return pl.pallas_call(
        functools.partial(_fpast_kernel, group=group),
        grid_spec=pltpu.PrefetchScalarGridSpec(
            num_scalar_prefetch=1, grid=(db, nblk),
            in_specs=[lp_spec(i) for i in range(group)],
            out_specs=pl.BlockSpec((None, group, p, nh), lambda b, j, pt: (b, nblk - 1 - j, 0, 0)),
            scratch_shapes=[pltpu.VMEM((SUBLANE, nh), F32)]),
        out_shape=jax.ShapeDtypeStruct((db, n_pages, p, nh), F32),
        compiler_params=_cp("arbitrary", "arbitrary"), name="fpast",
    )(page_table, *([logf_pool] * group))


def _fox_sample_kernel(pt_ref, *refs, nh, scale, n_steps, pp):
    q_ref, k_refs, v_refs = refs[0], refs[1:1 + pp], refs[1 + pp:1 + 2 * pp]
    (fp_ref, kn_ref, vn_ref, fn_ref, o_ref, m_ref, l_ref, acc_ref, bias_ref, s0_ref,
     s1_ref) = refs[1 + 2 * pp:]
    p = pl.program_id(1)
    r = q_ref.shape[0]
    rows = k_refs[0].shape[0]
    lognh = _log2(nh)
    fn = fn_ref[...]

    @pl.when(p == 0)
    def _():
        m_ref[...] = jnp.full_like(m_ref, NEG)
        l_ref[...] = jnp.zeros_like(l_ref)
        acc_ref[...] = jnp.zeros_like(acc_ref)
        s1_ref[...] = jnp.zeros_like(s1_ref)
        match = (_iota((r, rows), 0) & (nh - 1)) == (_iota((r, rows), 1) & (nh - 1))
        bias_ref[...] = jnp.where(match, _row_to_col(fn), NEG)

    q = (q_ref[...] * scale).astype(BF16)

    def update(s, v):
        m, l, acc = _softmax_step((m_ref[...], l_ref[...], acc_ref[...]), s, v)
        m_ref[...], l_ref[...], acc_ref[...] = m, l, acc

    def step(s_w, s_r):
        for i in range(pp):
            s_w[:, i * rows:(i + 1) * rows] = (_dot_nt(q, k_refs[i][...].astype(BF16))
                                               + (bias_ref[...] - fp_ref[i]))
        live = p > 0
        s = s_r[...]
        m_old = m_ref[...]
        m_new = jnp.where(live, jnp.maximum(m_old, jnp.max(s, axis=-1, keepdims=True)), m_old)
        pr = jnp.where(live, jnp.exp(s - m_new), 0.0)
        alpha = jnp.exp(m_old - m_new)
        l_ref[...] = alpha * l_ref[...] + jnp.sum(pr, axis=-1, keepdims=True)
        pr = pr.astype(BF16)
        pv = _dot(pr[:, :rows], v_refs[0][...].astype(BF16))
        for i in range(1, pp):
            pv = pv + _dot(pr[:, i * rows:(i + 1) * rows], v_refs[i][...].astype(BF16))
        acc_ref[...] = alpha * acc_ref[...] + pv
        m_ref[...] = m_new

    @pl.when(p % 2 == 0)
    def _():
        step(s0_ref, s1_ref)

    @pl.when(p % 2 == 1)
    def _():
        step(s1_ref, s0_ref)

    @pl.when(p == n_steps)
    def _():
        rr, cc = _iota((r, r), 0), _iota((r, r), 1)
        ok = ((rr & (nh - 1)) == (cc & (nh - 1))) & ((cc >> lognh) <= (rr >> lognh))
        s2 = _dot_nt(q, kn_ref[...].astype(BF16)) + _row_to_col(fn) - fn
        update(jnp.where(ok, s2, NEG), vn_ref[...].astype(BF16))
        o_ref[...] = (acc_ref[...] / l_ref[...]).astype(o_ref.dtype)


def _fox_sample(q, k_new, v_new, f_new, k_pool, v_pool, f_past, page_table, nh):
    db, r, hd = q.shape
    n_pages = page_table.shape[1]
    rows = k_pool.shape[1]
    tok = pl.BlockSpec((None, r, hd), lambda b, p, pt: (b, 0, 0))
    pp = max(k for k in (4, 2, 1) if n_pages % k == 0)
    n_steps = n_pages // pp
    last = n_steps - 1
    k_pages = [pl.BlockSpec((None, rows, hd),
                            lambda b, p, pt, i=i: (pt[b, jnp.minimum(p, last) * pp + i], 0, 0))
               for i in range(pp)]
    v_pages = [pl.BlockSpec((None, rows, hd),
                            lambda b, p, pt, i=i: (pt[b, jnp.maximum(p - 1, 0) * pp + i], 0, 0))
               for i in range(pp)]
    return pl.pallas_call(
        functools.partial(_fox_sample_kernel, nh=nh, scale=hd ** -0.5, n_steps=n_steps, pp=pp),
        grid_spec=pltpu.PrefetchScalarGridSpec(
            num_scalar_prefetch=1, grid=(db, n_steps + 1),
            in_specs=[tok] + k_pages + v_pages + [
                pl.BlockSpec((None, pp, 1, rows), lambda b, p, pt: (b, jnp.minimum(p, last), 0, 0)),
                tok, tok, pl.BlockSpec((None, 1, r), lambda b, p, pt: (b, 0, 0))],
            out_specs=tok,
            scratch_shapes=[pltpu.VMEM((r, 1), F32), pltpu.VMEM((r, 1), F32),
                            pltpu.VMEM((r, hd), F32), pltpu.VMEM((r, rows), F32),
                            pltpu.VMEM((r, pp * rows), F32), pltpu.VMEM((r, pp * rows), F32)]),
        out_shape=jax.ShapeDtypeStruct((db, r, hd), BF16),
        compiler_params=_cp("arbitrary", "arbitrary"), name="fox_sample",
    )(page_table, q, *([k_pool] * pp), *([v_pool] * pp), f_past, k_new, v_new, f_new)


def _dil_prompt_kernel(*refs, groups, tb, scale, par):
    ng = len(groups)
    q_refs, k_refs, v_refs = refs[:ng], refs[ng:2 * ng], refs[2 * ng:3 * ng]
    o_ref, m_ref, l_ref, acc_ref = refs[3 * ng:]
    seq = q_refs[0].shape[0]
    diff = _iota((tb, tb), 0) - _iota((tb, tb), 1)

    def rows_of(res, blk, dil):
        return pl.ds(res + blk * tb * dil, tb, stride=dil) if dil > 1 else pl.ds(blk * tb, tb)

    for g, (window, dil) in enumerate(groups):
        reach = window // dil
        nqb = seq // dil // tb
        back = -(-reach // tb)
        units = [(res, qb) for res in range(dil) for qb in range(nqb)]
        for c0 in range(0, len(units), par):
            chunk = units[c0:c0 + par]
            rows = [rows_of(res, qb, dil) for res, qb in chunk]
            qs = [(q_refs[g][rw, :] * scale).astype(BF16) for rw in rows]
            if g == 0:
                carries = [_softmax_init(tb, LANE) for _ in chunk]
            else:
                carries = [(m_ref[rw, :], l_ref[rw, :], acc_ref[rw, :]) for rw in rows]
            for step in range(back + 1):
                live = [i for i, (_, qb) in enumerate(chunk) if qb - step >= 0]
                if not live:
                    continue
                krows = [rows_of(chunk[i][0], chunk[i][1] - step, dil) for i in live]
                ks = [k_refs[g][rw, :].astype(BF16) for rw in krows]
                vs = [v_refs[g][rw, :].astype(BF16) for rw in krows]
                ss = [_dot_nt(qs[i], k) for i, k in zip(live, ks)]
                if step * tb - (tb - 1) < 0 or step * tb + (tb - 1) > reach:
                    dist = diff + step * tb
                    valid = (dist >= 0) & (dist <= reach)
                    ss = [jnp.where(valid, s, NEG) for s in ss]
                ms = [jnp.maximum(carries[i][0], jnp.max(s, axis=-1, keepdims=True)) for i, s in zip(live, ss)]
                ps = [jnp.exp(s - m) for s, m in zip(ss, ms)]
                alphas = [jnp.exp(carries[i][0] - m) for i, m in zip(live, ms)]
                pvs = [_dot(p.astype(BF16), v) for p, v in zip(ps, vs)]
                for i, m, p, a, pv in zip(live, ms, ps, alphas, pvs):
                    carries[i] = (m, a * carries[i][1] + jnp.sum(p, axis=-1, keepdims=True),
                                  a * carries[i][2] + pv)
            for rw, (m, l, acc) in zip(rows, carries):
                m_ref[rw, :], l_ref[rw, :], acc_ref[rw, :] = m, l, acc
    o_ref[...] = (acc_ref[...] / l_ref[...]).astype(o_ref.dtype)


def _dil_prompt(srcs, nb, s, nh, groups, tb=128, par=8):
    hd = LANE
    ng = len(groups)
    tb = min(tb, s)
    for window, dil in groups:
        assert s % (dil * tb) == 0 and window % dil == 0
    args, specs = [], []
    for which in range(3):
        for g in range(ng):
            arr, col = srcs[g][which]
            args.append(arr)
            specs.append(pl.BlockSpec((s, hd), lambda b, h, col=col: (b, col + h)))
    return pl.pallas_call(
        functools.partial(_dil_prompt_kernel, groups=groups, tb=tb, scale=hd ** -0.5, par=par),
        grid=(nb, nh),
        in_specs=specs,
        out_specs=pl.BlockSpec((s, hd), lambda b, h: (b, h)),
        out_shape=jax.ShapeDtypeStruct((nb * s, nh * hd), BF16),
        scratch_shapes=[pltpu.VMEM((s, 1), F32), pltpu.VMEM((s, 1), F32), pltpu.VMEM((s, hd), F32)],
        compiler_params=_cp("arbitrary", "arbitrary"), name="dil_prompt",
    )(*args)


def _dil_sample_kernel(*refs, groups, wbs, tb, nh, scale, plan):
    ng = len(groups)
    q_refs, kb_refs, vb_refs = refs[:ng], refs[ng:2 * ng], refs[2 * ng:3 * ng]
    kn_refs, vn_refs = refs[3 * ng:4 * ng], refs[4 * ng:5 * ng]
    o_ref, m_ref, l_ref, acc_ref = refs[5 * ng:5 * ng + 4]
    j = pl.program_id(1)
    r = q_refs[0].shape[0]
    rows = tb * nh
    lognh = _log2(nh)
    total = sum(steps for _, steps, _ in plan)

    @pl.when(j == 0)
    def _():
        m_ref[...] = jnp.full_like(m_ref, NEG)
        l_ref[...] = jnp.zeros_like(l_ref)
        acc_ref[...] = jnp.zeros_like(acc_ref)

    def update(s, v, valid):
        m, l, acc = _softmax_step((m_ref[...], l_ref[...], acc_ref[...]),
                                  jnp.where(valid, s, NEG), v, valid)
        m_ref[...], l_ref[...], acc_ref[...] = m, l, acc

    for g, (window, dil) in enumerate(groups):
        start, steps, nmb = plan[g]

        def block(g=g, window=window, dil=dil, start=start, nmb=nmb):
            rr, cc = _iota((r, rows), 0), _iota((r, rows), 1)
            res = (wbs[g] + (j - start) // nmb) % dil
            key_tok = (((j - start) % nmb) * tb + (cc >> lognh)) * dil + res
            delta = wbs[g] + (rr >> lognh) - key_tok
            valid = (((rr & (nh - 1)) == (cc & (nh - 1))) & ((delta & (dil - 1)) == 0)
                     & (delta <= window))
            q = (q_refs[g][...] * scale).astype(BF16)
            k = kb_refs[g][...].reshape(rows, LANE).astype(BF16)
            v = vb_refs[g][...].reshape(rows, LANE).astype(BF16)
            update(_dot_nt(q, k), v, valid)

        pl.when((j >= start) & (j < start + steps))(block)

    @pl.when(j == total - 1)
    def _():
        rr, cc = _iota((r, r), 0), _iota((r, r), 1)
        delta = (rr >> lognh) - (cc >> lognh)
        match = (rr & (nh - 1)) == (cc & (nh - 1))
        for g, (window, dil) in enumerate(groups):
            valid = match & (delta >= 0) & ((delta & (dil - 1)) == 0) & (delta <= window)
            q = (q_refs[g][...] * scale).astype(BF16)
            update(_dot_nt(q, kn_refs[g][...].astype(BF16)), vn_refs[g][...].astype(BF16), valid)
        o_ref[...] = (acc_ref[...] / l_ref[...]).astype(o_ref.dtype)


def _dil_sample(qs, kns, vns, kbufs, vbufs, groups, nh, t):
    db, r, hd = qs[0].shape
    ng = len(groups)
    wbs = [kb.shape[1] for kb in kbufs]
    assert all(wb % dil == 0 for wb, (_, dil) in zip(wbs, groups))
    per_res = [wb // dil for wb, (_, dil) in zip(wbs, groups)]
    tb = 128
    while any(n % tb for n in per_res):
        tb //= 2
    plan, start = [], 0
    for n, (_, dil) in zip(per_res, groups):
        steps = min(dil, t) * (n // tb)
        plan.append((start, steps, n // tb))
        start += steps
    total = start
    tok = pl.BlockSpec((None, r, hd), lambda b, j: (b, 0, 0))

    def buf_spec(g):
        first, steps, nmb = plan[g]
        dil, wb = groups[g][1], wbs[g]

        def index(b, j):
            sg = jnp.clip(j - first, 0, steps - 1)
            return b, sg % nmb, (wb + sg // nmb) % dil, 0, 0

        return pl.BlockSpec((None, tb, None, nh, hd), index)

    split = lambda a, g: a.reshape(db, per_res[g], groups[g][1], nh, hd)
    kbufs = [split(a, g) for g, a in enumerate(kbufs)]
    vbufs = [split(a, g) for g, a in enumerate(vbufs)]
    bspecs = [buf_spec(g) for g in range(ng)]
    return pl.pallas_call(
        functools.partial(_dil_sample_kernel, groups=groups, wbs=wbs, tb=tb, nh=nh,
                          scale=hd ** -0.5, plan=plan),
        grid=(db, total),
        in_specs=[tok] * ng + bspecs + bspecs + [tok] * (2 * ng),
        out_specs=tok,
        out_shape=jax.ShapeDtypeStruct((db, r, hd), BF16),
        scratch_shapes=[pltpu.VMEM((r, 1), F32), pltpu.VMEM((r, 1), F32), pltpu.VMEM((r, hd), F32)],
        compiler_params=_cp("arbitrary", "arbitrary"), name="dil_sample",
    )(*qs, *kbufs, *vbufs, *kns, *vns)


def _ffn_up_kernel(*refs, kw, tiles_per_batch, t_s, nm):
    nh = kw - 1
    hp_ref, hs_ref, wu_ref, wg_ref, cw_ref = refs[:5]
    halo_refs = refs[5:5 + nh]
    actp_ref, tailp_ref, acts_ref, gates_ref, ext_ref, wub_ref, wgb_ref = refs[5 + nh:]
    m = pl.program_id(1)
    tn = wub_ref.shape[1]

    @pl.when(m == 0)
    def _():
        wub_ref[...] = wu_ref[...].astype(BF16)
        wgb_ref[...] = wg_ref[...].astype(BF16)

    def conv_gelu(gate, up, fix):
        rows = gate.shape[0]
        ext_ref[SUBLANE:SUBLANE + rows, :] = gate
        gc = gate * cw_ref[kw - 1:kw, :]
        for j in range(kw - 1):
            shift = kw - 1 - j
            gc = gc + fix(ext_ref[pl.ds(SUBLANE - shift, rows), :], shift) * cw_ref[j:j + 1, :]
        return _gelu_tanh(gc) * up

    @pl.when(m < nm)
    def _():
        h = hp_ref[...]
        tm = h.shape[0]
        up = _dot(h, wub_ref[...])
        gate = _dot(h, wgb_ref[...])
        prev = jnp.where((m % tiles_per_batch) == 0, 0.0, ext_ref[tm:tm + SUBLANE, :])
        ext_ref[0:SUBLANE, :] = prev
        tailp_ref[...] = gate[tm - SUBLANE:, :]
        actp_ref[...] = conv_gelu(gate, up, lambda sh, shift: sh).astype(actp_ref.dtype)

    @pl.when(m == nm)
    def _():
        h = hs_ref[...]
        up = _dot(h, wub_ref[...])
        gate = _dot(h, wgb_ref[...])
        ext_ref[0:SUBLANE, :] = jnp.zeros((SUBLANE, tn), F32)
        gates_ref[...] = gate
        pos = _iota(gate.shape, 0) & (t_s - 1)
        fix = lambda sh, shift: jnp.where(pos < shift, halo_refs[shift - 1][...], sh)
        acts_ref[...] = conv_gelu(gate, up, fix).astype(acts_ref.dtype)


def _ffn_up(h_p, h_s, wu, wg, layer, cw8, kw, t_p, t_s, halos, tm=512, tn=FFN_TILE):
    m, d = h_p.shape
    ms = h_s.shape[0]
    n_pad = wu.shape[-1]
    tm = min(tm, t_p)
    tn = min(tn, -(-n_pad // LANE) * LANE)
    nm, nn = m // tm, -(-n_pad // tn)
    _log2(t_s)
    last = nm - 1
    s_spec = pl.BlockSpec((ms, tn), lambda j, i: (0, j))
    in_specs = [pl.BlockSpec((tm, d), lambda j, i: (jnp.minimum(i, last), 0)),
                pl.BlockSpec((ms, d), lambda j, i: (0, 0)),
                pl.BlockSpec((None, d, tn), lambda j, i: (layer, 0, j)),
                pl.BlockSpec((None, d, tn), lambda j, i: (layer, 0, j)),
                pl.BlockSpec((SUBLANE, tn), lambda j, i: (0, j))] + [s_spec] * len(halos)
    return pl.pallas_call(
        functools.partial(_ffn_up_kernel, kw=kw, tiles_per_batch=max(t_p // tm, 1), t_s=t_s, nm=nm),
        grid=(nn, nm + 1), in_specs=in_specs,
        out_specs=[pl.BlockSpec((tm, tn), lambda j, i: (jnp.minimum(i, last), j)),
                   pl.BlockSpec((None, SUBLANE, tn), lambda j, i: (jnp.minimum(i, last), 0, j)),
                   s_spec, s_spec],
        out_shape=[jax.ShapeDtypeStruct((m, n_pad), BF16),
                   jax.ShapeDtypeStruct((nm, SUBLANE, n_pad), F32),
                   jax.ShapeDtypeStruct((ms, n_pad), BF16), jax.ShapeDtypeStruct((ms, n_pad), F32)],
        scratch_shapes=[pltpu.VMEM((max(tm, ms) + SUBLANE, tn), F32), pltpu.VMEM((d, tn), BF16),
                        pltpu.VMEM((d, tn), BF16)],
        compiler_params=_cp("arbitrary", "arbitrary"), name="ffn_up",
    )(h_p, h_s, wu, wg, cw8, *halos)


def _even_weights(w_in, conv_w, a_log, dt_bias, b_f, w_out, ha, hb, hd):
    wa, wb = ha * hd, hb * hd
    o2 = 4 * wa
    o4 = o2 + 2 * ha
    o5 = o4 + 3 * wb
    w_main = (w_in, w_in[:, o4:o5])
    small = jnp.concatenate([w_in[:, o2:o4], w_in[:, o5:]], axis=1)
    assert small.shape[1] <= LANE
    w_small = jnp.pad(small, ((0, 0), (0, LANE - small.shape[1])))
    par = jnp.zeros((SUBLANE, LANE), F32)
    par = par.at[0, ha:2 * ha].set(a_log.astype(F32))
    par = par.at[1, ha:2 * ha].set(dt_bias.astype(F32))
    par = par.at[1, 2 * ha:2 * ha + hb].set(b_f.astype(F32))
    return w_main, w_small, par, conv_w, w_out


def _even_mixer(hs, nbs, ts, wts, gnorm, conv_bufs, s0s, ha, hb, hd, past):
    (w_in, w_fox), _, _, _, w_out = wts
    wa, wb = ha * hd, hb * hd
    projs = _matmul_w32([hs[0]], [w_in], F32, out_cols=[4 * wa], tm=1024, w_buffers=1, a2_list=[hs[1]])
    foxes = _matmul_w32([hs[0]], [w_fox], F32, out_cols=[wb, wb, wb], tm=512, w_buffers=1,
                        a2_list=[hs[1]])
    cores = [_even_core(hs[g], nbs[g], ts[g], projs[g], foxes[g], wts, gnorm, conv_bufs[g], s0s[g], ha, hb,
                        hd, past if g == 1 else None) for g in (0, 1)]
    outs = _matmul_w32([cores[0][0], cores[0][1]], [w_out, w_out], BF16, tm=512,
                       a2_list=[cores[1][0], cores[1][1]])
    return [(outs[g],) + tuple(cores[g][2:]) for g in (0, 1)]


def _even_core(h, nb, t, proj, fox, wts, gnorm, conv_buf, s0, ha, hb, hd, past):
    _, w_small, par, conv_w, _ = wts
    wa, wb = ha * hd, hb * hd
    kw = conv_w.shape[0]
    q_fox, k_new, v_new = fox
    gates, fcum = _gates(h, w_small, par, nb, t, ha, hb)

    c = DELTA_CHUNK
    tp = -(-t // c) * c
    conv_buf8 = jnp.pad(conv_buf.astype(F32), ((0, 0), (SUBLANE - (kw - 1), 0), (0, 0)))
    if tp == t:
        proj_d, gates_d = proj, gates
    else:
        pad = lambda a: jnp.pad(a.reshape(nb, t, -1), ((0, 0), (0, tp - t), (0, 0))).reshape(nb * tp, -1)
        proj_d, gates_d = pad(proj), pad(gates)
    o_a, s_new = _delta(proj_d, gates_d, conv_w, conv_buf8, s0, gnorm, nb, tp, min(t, c) if tp != t else c,
                        ha)
    if tp != t:
        o_a = o_a.reshape(nb, tp, wa)[:, :t].reshape(nb * t, wa)

    logf = gates[:, 2 * ha:2 * ha + hb].reshape(nb, t, hb)
    f_new = fcum[:, 2 * ha:2 * ha + hb].reshape(nb, t, hb)
    kb, vb = k_new.reshape(nb, t, hb, hd), v_new.reshape(nb, t, hb, hd)
    if past is None:
        f_rows = jnp.swapaxes(f_new, 1, 2).reshape(nb, hb, 1, t)
        o_b = _fox_prompt(q_fox, 0, k_new, v_new, f_rows, nb, t, hb)
    else:
        k_pool, v_pool, logf_pool, page_table = past
        n_pool, page = k_pool.shape[0], k_pool.shape[1]
        f_past = _fpast(logf_pool.astype(F32), page_table)
        f_past = f_past.reshape(nb, page_table.shape[1], 1, page * hb)
        o_b = _fox_sample(q_fox.reshape(nb, t * hb, hd), k_new.reshape(nb, t * hb, hd),
                          v_new.reshape(nb, t * hb, hd), f_new.reshape(nb, 1, t * hb),
                          k_pool.reshape(n_pool, page * hb, hd), v_pool.reshape(n_pool, page * hb, hd),
                          f_past, page_table, hb)
        o_b = o_b.reshape(nb * t, wb)
    keep = min(kw - 1, t)
    raw_tail = proj.reshape(nb, t, -1)[:, t - keep:, :3 * wa]
    new_conv = jnp.concatenate([conv_buf.astype(F32), raw_tail], axis=1)[:, -(kw - 1):]
    return o_a, o_b, s_new, new_conv, kb, vb, logf


def _odd_mixer(hs, nbs, ts, w_in, w_out, nh, hd, bufs):
    ng = len(C_GROUPS)
    wc = nh * hd
    n_main = (3 * ng - 2) * wc
    main, main_s = _matmul_w32([hs[0]], [w_in], F32, out_cols=[n_main], tm=1024, w_buffers=1,
                               a2_list=[hs[1]])
    (k_last, v_last), kv_s = _matmul_w32([hs[0]], [w_in], F32, out_cols=[wc, wc], col0=n_main, tm=512,
                                         w_buffers=1, a2_list=[hs[1]])
    nb, t = nbs[0], ts[0]
    srcs = [((main, g * 3 * nh), (main, g * 3 * nh + nh), (main, g * 3 * nh + 2 * nh))
            for g in range(ng - 1)]
    srcs.append(((main, (ng - 1) * 3 * nh), (k_last, 0), (v_last, 0)))
    o_p = _dil_prompt(srcs, nb, t, nh, C_GROUPS)
    m3 = main.reshape(nb, t, -1)
    bufs_p = []
    for g, (window, _) in enumerate(C_GROUPS):
        keep = min(window, t)
        if g < ng - 1:
            kv = [m3[:, t - keep:, (g * 3 + r) * wc:(g * 3 + r + 1) * wc] for r in (1, 2)]
        else:
            kv = [a.reshape(nb, t, wc)[:, t - keep:] for a in (k_last, v_last)]
        bufs_p.append(tuple(a.reshape(nb, keep, nh, hd) for a in kv))
    o_s, bufs_s = _odd_sample_core(jnp.concatenate([main_s] + list(kv_s), axis=1), nbs[1], ts[1], nh, hd, bufs)
    out_p, out_s = _matmul_w32([o_p], [w_out], BF16, tm=512, a2_list=[o_s])
    return (out_p, bufs_p), (out_s, bufs_s)


def _odd_sample_core(proj, nb, t, nh, hd, bufs):
    ng = len(C_GROUPS)
    new_bufs = []
    p6 = proj.reshape(nb, t, ng, 3, nh, hd)
    flat = lambda a: a.reshape(nb, -1, hd)
    qs = [flat(p6[:, :, g, 0]) for g in range(ng)]
    kns = [flat(p6[:, :, g, 1]) for g in range(ng)]
    vns = [flat(p6[:, :, g, 2]) for g in range(ng)]
    o = _dil_sample(qs, kns, vns, [bk for bk, _ in bufs], [bv for _, bv in bufs], C_GROUPS, nh, t)
    o = o.reshape(nb * t, nh * hd)
    for g in range(ng):
        bk, bv = bufs[g]
        wb = bk.shape[1]
        new_bufs.append((jnp.concatenate([bk, p6[:, :, g, 1]], axis=1)[:, -wb:],
                         jnp.concatenate([bv, p6[:, :, g, 2]], axis=1)[:, -wb:]))
    return o, new_bufs


def _ffn_weights(w_up, w_gate, conv_w, w_down):
    d_ff = w_up.shape[-1]
    kw = conv_w.shape[1]
    cw8 = jnp.pad(conv_w.astype(F32), ((0, 0), (0, SUBLANE - kw), (0, 0)))
    return w_up, w_gate, cw8, w_down.astype(BF16), kw, d_ff


def _conv_ffn(hs, nbs, ts, wts, layer, buf_s):
    wu, wg, cw8, wd, kw, d_ff = wts
    (nb, nbs_), (t, t_s) = nbs, ts
    halos = []
    for shift in range(1, kw):
        hl = jnp.zeros((nbs_, t_s, d_ff), F32)
        hl = hl.at[:, :shift].set(buf_s.astype(F32)[:, (kw - 1) - shift:])
        halos.append(hl.reshape(nbs_ * t_s, d_ff))
    act_p, tail, act_s, gate_s = _ffn_up(hs[0], hs[1], wu, wg, layer, cw8[layer], kw, t, t_s, halos)
    tiles = tail.shape[0] // nb
    gate_tail = tail.reshape(nb, tiles, SUBLANE, d_ff)[:, -1]
    new_buf_p = gate_tail[:, SUBLANE - (kw - 1):]
    new_buf_s = jnp.concatenate([buf_s.astype(F32), gate_s.reshape(nbs_, t_s, d_ff)], axis=1)[:, -(kw - 1):]
    out_p = _matmul([act_p], [wd], BF16, tm=512, tn=512, layer=layer)
    out_s = _matmul([act_s], [wd], BF16, tm=act_s.shape[0], tn=512, layer=layer)
    return (out_p, new_buf_p), (out_s, new_buf_s)


def kernel(x_prompt, x_sample, c_prompt, c_sample, state_delta, state_conv_qkv, cache_k, cache_v, cache_logf, cache_win_k0, cache_win_v0, cache_win_k1, cache_win_v1, cache_win_k2, cache_win_v2, state_ffn_conv, page_table, w_mod, b_mod, norm_pre_mix, norm_post_mix, norm_pre_ffn, norm_post_ffn, w_in_e, conv_a, a_log, dt_bias, gnorm_a, b_forget, w_out_e, w_in_o, w_out_o, w_up, w_gate, conv_ffn_w, w_down):
    bsz, seq, d = x_prompt.shape
    dbsz, dseq, _ = x_sample.shape
    depth = w_mod.shape[0]
    hd = gnorm_a.shape[-1]
    ha, hb, hc = a_log.shape[-1], b_forget.shape[-1], cache_win_k0.shape[3]
    assert hd == LANE and seq % DELTA_CHUNK == 0
    kw_a = conv_a.shape[1]
    win_caches = ((cache_win_k0, cache_win_v0), (cache_win_k1, cache_win_v1), (cache_win_k2, cache_win_v2))

    c_all = jnp.concatenate([c_prompt, c_sample], axis=0)
    c_all = jnp.pad(c_all, ((0, MOD_ROWS - c_all.shape[0]), (0, 0)))
    mod = _mod(c_all, w_mod, b_mod)
    mods = [(mod[l, :bsz].reshape(bsz, 6, d), mod[l, bsz:bsz + dbsz].reshape(dbsz, 6, d))
            for l in range(depth)]

    groups = [dict(x=x_prompt.reshape(bsz * seq, d), nb=bsz, t=seq, gi=0),
              dict(x=x_sample.reshape(dbsz * dseq, d), nb=dbsz, t=dseq, gi=1)]
    for gr in groups:
        m_0 = mods[0][gr["gi"]]
        _, gr["h"] = _post_pre(gr["x"], gr["nb"], gr["t"], shift=m_0[:, 0], scale=m_0[:, 1],
                               w_pre=norm_pre_mix[0])

    fwts = _ffn_weights(w_up, w_gate, conv_ffn_w, w_down)
    outs = {k: ([], []) for k in ("delta", "conv", "k", "v", "logf", "ffn")}
    wk = [([], []) for _ in C_GROUPS]
    wv = [([], []) for _ in C_GROUPS]
    for l in range(depth):
        if l % 2 == 0:
            e = l // 2
            ewts = _even_weights(w_in_e[e], conv_a[e], a_log[e], dt_bias[e], b_forget[e], w_out_e[e],
                                 ha, hb, hd)
        else:
            o = l // 2
            w_in_o_b, w_out_o_b = w_in_o[o], w_out_o[o]
        hs, nbs, ts = [[gr[key] for gr in groups] for key in ("h", "nb", "t")]
        if l % 2 == 0:
            conv_bufs = (jnp.zeros((bsz, kw_a - 1, 3 * ha * hd), F32), state_conv_qkv[e])
            s0s = (jnp.zeros((bsz, ha, hd, hd), F32), state_delta[e])
            past = (cache_k[e], cache_v[e], cache_logf[e], page_table)
            mixed = _even_mixer(hs, nbs, ts, ewts, gnorm_a[e], conv_bufs, s0s, ha, hb, hd, past)
            for gi, (_, s_new, new_conv, kb, vb, logf) in enumerate(mixed):
                for key, val in (("delta", s_new), ("conv", new_conv), ("k", kb), ("v", vb), ("logf", logf)):
                    outs[key][gi].append(val)
        else:
            bufs = tuple((bk[o], bv[o]) for bk, bv in win_caches)
            mixed = _odd_mixer(hs, nbs, ts, w_in_o_b, w_out_o_b, hc, hd, bufs)
            for gi, (_, new_bufs) in enumerate(mixed):
                for g in range(len(C_GROUPS)):
                    wk[g][gi].append(new_bufs[g][0])
                    wv[g][gi].append(new_bufs[g][1])
        for gr in groups:
            gi, nb, t = gr["gi"], gr["nb"], gr["t"]
            m_l = mods[l][gi]
            om = mixed[gi][0]
            gr["x"], gr["h"] = _post_pre(gr["x"], nb, t, o=om, gate=m_l[:, 2], w_post=norm_post_mix[l],
                                         shift=m_l[:, 3], scale=m_l[:, 4], w_pre=norm_pre_ffn[l])
        ffn = _conv_ffn([gr["h"] for gr in groups], nbs, ts, fwts, l, state_ffn_conv[l])
        for gr in groups:
            gi, nb, t = gr["gi"], gr["nb"], gr["t"]
            m_l = mods[l][gi]
            of, new_buf = ffn[gi]
            outs["ffn"][gi].append(new_buf)
            if l + 1 < depth:
                m_n = mods[l + 1][gi]
                gr["x"], gr["h"] = _post_pre(gr["x"], nb, t, o=of, gate=m_l[:, 5], w_post=norm_post_ffn[l],
                                             shift=m_n[:, 0], scale=m_n[:, 1], w_pre=norm_pre_mix[l + 1])
            else:
                gr["x"], _ = _post_pre(gr["x"], nb, t, o=of, gate=m_l[:, 5], w_post=norm_post_ffn[l])

    stk = lambda lst: jnp.stack(lst, axis=0)
    res = [groups[0]["x"].reshape(bsz, seq, d), groups[1]["x"].reshape(dbsz, dseq, d)]
    for key in ("delta", "conv", "k", "v", "logf"):
        res += [stk(outs[key][0]), stk(outs[key][1])]
    for g in range(len(C_GROUPS)):
        res += [stk(wk[g][0]), stk(wk[g][1]), stk(wv[g][0]), stk(wv[g][1])]
    res += [stk(outs["ffn"][0]), stk(outs["ffn"][1])]
    return tuple(res)
```

```python
import functools
import math

import jax
import jax.numpy as jnp
from jax import lax
from jax.experimental import pallas as pl
from jax.experimental.pallas import tpu as pltpu

F32 = jnp.float32
BF16 = jnp.bfloat16
HI = lax.Precision.HIGHEST

C_GROUPS = ((128, 1), (512, 4), (2048, 16))
DELTA_CHUNK = 64
RMS_EPS = 1e-6
L2_EPS = 1e-6
NEG = -1e30
LANE = 128
SUBLANE = 8
VMEM_LIMIT = 58 * 1024 * 1024
MOD_ROWS = 16
FFN_TILE = 512


def _cp(*sem):
    return pltpu.CompilerParams(dimension_semantics=sem, vmem_limit_bytes=VMEM_LIMIT)


def _dot(a, b, prec=None):
    return jnp.dot(a, b, preferred_element_type=F32, precision=prec)


def _dot_nt(a, b, prec=None):
    return lax.dot_general(a, b, (((1,), (1,)), ((), ())), preferred_element_type=F32, precision=prec)


def _dot_tn(a, b, prec=None):
    return lax.dot_general(a, b, (((0,), (0,)), ((), ())), preferred_element_type=F32, precision=prec)


def _dot3(a, b):
    ah, bh = a.astype(BF16), b.astype(BF16)
    al = (a - ah.astype(F32)).astype(BF16)
    bl = (b - bh.astype(F32)).astype(BF16)
    return _dot(ah, bh) + (_dot(ah, bl) + _dot(al, bh))


def _sigmoid(x):
    return 1.0 / (1.0 + jnp.exp(-x))


def _silu(x):
    return x * _sigmoid(x)


def _softplus(x):
    return jnp.maximum(x, 0.0) + jnp.log1p(jnp.exp(-jnp.abs(x)))


def _gelu_tanh(x):
    return 0.5 * x * (1.0 + jnp.tanh(0.7978845608028654 * (x + 0.044715 * (x * x * x))))


def _rms(x, w):
    return x * lax.rsqrt(jnp.mean(x * x, axis=-1, keepdims=True) + RMS_EPS) * w


def _iota(shape, dim):
    return lax.broadcasted_iota(jnp.int32, shape, dim)


def _row_to_col(row):
    n = row.shape[1]
    eye = _iota((n, n), 0) == _iota((n, n), 1)
    return jnp.sum(jnp.where(eye, jnp.broadcast_to(row, (n, n)), 0.0), axis=1, keepdims=True)


def _col_to_row(col):
    n = col.shape[0]
    eye = _iota((n, n), 0) == _iota((n, n), 1)
    return jnp.sum(jnp.where(eye, jnp.broadcast_to(col, (n, n)), 0.0), axis=0, keepdims=True)


def _pick_tile(n, pref):
    if n <= pref:
        return n
    t = (pref // LANE) * LANE
    while t >= LANE:
        if n % t == 0:
            return t
        t -= LANE
    return n


def _log2(n):
    assert n > 0 and n & (n - 1) == 0, n
    return n.bit_length() - 1


def _mod_kernel(c_ref, w_ref, b_ref, o_ref):
    a = _silu(c_ref[...]).astype(BF16)
    o_ref[...] = _dot(a, w_ref[...].astype(BF16)) + b_ref[...]


def _mod(c_all, w_mod, b_mod):
    depth, d, n = w_mod.shape
    r = c_all.shape[0]
    tn = _pick_tile(n, 512)
    return pl.pallas_call(
        _mod_kernel,
        grid=(depth, n // tn),
        in_specs=[pl.BlockSpec((r, d), lambda l, j: (0, 0)),
                  pl.BlockSpec((None, d, tn), lambda l, j: (l, 0, j)),
                  pl.BlockSpec((None, 1, tn), lambda l, j: (l, 0, j))],
        out_specs=pl.BlockSpec((None, r, tn), lambda l, j: (l, 0, j)),
        out_shape=jax.ShapeDtypeStruct((depth, r, n), F32),
        compiler_params=_cp("arbitrary", "arbitrary"),
        name="mod",
    )(c_all, w_mod, b_mod.reshape(depth, 1, n))


def _post_pre_kernel(*refs, has_o, has_h):
    refs = list(refs)
    x = refs.pop(0)[...]
    if has_o:
        o_ref, gate_ref, wpost_ref = refs.pop(0), refs.pop(0), refs.pop(0)
    if has_h:
        shift_ref, scale_ref, wpre_ref = refs.pop(0), refs.pop(0), refs.pop(0)
    if has_o:
        x = x + gate_ref[...] * _rms(o_ref[...].astype(F32), wpost_ref[...])
        refs.pop(0)[...] = x
    if has_h:
        h = _rms(x, wpre_ref[...]) * (1.0 + scale_ref[...]) + shift_ref[...]
        refs.pop(0)[...] = h.astype(BF16)


def _post_pre(x, nb, t, o=None, gate=None, w_post=None, shift=None, scale=None, w_pre=None):
    m, d = x.shape
    has_o, has_h = o is not None, shift is not None
    if t >= 128:
        tr = 256 if t % 256 == 0 else 128
        per = t // tr
        vec = lambda v: v.reshape(nb, 1, d)
        vec_spec = pl.BlockSpec((None, 1, d), lambda i: (i // per, 0, 0))
    else:
        tr = m
        vec = lambda v: jnp.repeat(v, t, axis=0).reshape(1, m, d)
        vec_spec = pl.BlockSpec((None, m, d), lambda i: (0, 0, 0))
    row_spec = pl.BlockSpec((tr, d), lambda i: (i, 0))
    w_spec = pl.BlockSpec((1, d), lambda i: (0, 0))
    args, in_specs, out_specs, out_shape = [x], [row_spec], [], []
    if has_o:
        args += [o, vec(gate), w_post.reshape(1, d)]
        in_specs += [row_spec, vec_spec, w_spec]
        out_specs.append(row_spec)
        out_shape.append(jax.ShapeDtypeStruct((m, d), F32))
    if has_h:
        args += [vec(shift), vec(scale), w_pre.reshape(1, d)]
        in_specs += [vec_spec, vec_spec, w_spec]
        out_specs.append(row_spec)
        out_shape.append(jax.ShapeDtypeStruct((m, d), BF16))
    outs = pl.pallas_call(
        functools.partial(_post_pre_kernel, has_o=has_o, has_h=has_h),
        grid=(m // tr,), in_specs=in_specs, out_specs=out_specs, out_shape=out_shape,
        compiler_params=_cp("arbitrary"), name="post_pre",
    )(*args)
    outs = list(outs)
    x_new = outs.pop(0) if has_o else x
    h = outs.pop(0) if has_h else None
    return x_new, h


def _mm_kernel(*refs, n_pairs, nk):
    a_refs, b_refs = refs[:n_pairs], refs[n_pairs:2 * n_pairs]
    o_ref = refs[2 * n_pairs]
    part = _dot(a_refs[0][...], b_refs[0][...])
    for a_ref, b_ref in zip(a_refs[1:], b_refs[1:]):
        part = part + _dot(a_ref[...], b_ref[...])
    if nk == 1:
        o_ref[...] = part.astype(o_ref.dtype)
        return
    acc_ref = refs[2 * n_pairs + 1]
    k = pl.program_id(2)

    @pl.when(k == 0)
    def _():
        acc_ref[...] = part

    @pl.when(k > 0)
    def _():
        acc_ref[...] += part

    @pl.when(k == nk - 1)
    def _():
        o_ref[...] = acc_ref[...].astype(o_ref.dtype)


def _matmul(a_list, b_list, out_dtype, tm=1024, tn=1024, tk=None, cols=None, layer=None):
    m, kdim = a_list[0].shape
    c0, n = (0, b_list[0].shape[-1]) if cols is None else cols
    tm = min(tm, m)
    tn = _pick_tile(n, tn)
    tk = kdim if tk is None else _pick_tile(kdim, tk)
    nk = kdim // tk
    assert m % tm == 0 and n % tn == 0 and kdim % tk == 0 and c0 % tn == 0
    jb = c0 // tn
    n_pairs = len(a_list)
    if layer is None:
        b_spec = pl.BlockSpec((tk, tn), lambda i, j, k: (k, j + jb))
    else:
        b_spec = pl.BlockSpec((None, tk, tn), lambda i, j, k: (layer, k, j + jb))
    in_specs = [pl.BlockSpec((tm, tk), lambda i, j, k: (i, k))] * n_pairs + [b_spec] * n_pairs
    scratch = [pltpu.VMEM((tm, tn), F32)] if nk > 1 else []
    return pl.pallas_call(
        functools.partial(_mm_kernel, n_pairs=n_pairs, nk=nk),
        grid=(m // tm, n // tn, nk), in_specs=in_specs,
        out_specs=pl.BlockSpec((tm, tn), lambda i, j, k: (i, j)),
        out_shape=jax.ShapeDtypeStruct((m, n), out_dtype),
        scratch_shapes=scratch,
        compiler_params=_cp("arbitrary", "arbitrary", "arbitrary"), name="matmul",
    )(*a_list, *b_list)


def _mmw_kernel(*refs, n_pairs, ranges, nm, extra, w_rows):
    n_out = len(ranges)
    mul = _dot_nt if w_rows else _dot
    a_refs, refs = refs[:n_pairs], refs[n_pairs:]
    if extra:
        a2_refs, refs = refs[:n_pairs], refs[n_pairs:]
    b_refs, refs = refs[:n_pairs], refs[n_pairs:]
    o_refs, refs = refs[:n_out], refs[n_out:]
    if extra:
        o2_refs, refs = refs[:n_out], refs[n_out:]
    wb_refs = refs
    j, i = pl.program_id(0), pl.program_id(1)

    @pl.when(i == 0)
    def _():
        for b_ref, wb_ref in zip(b_refs, wb_refs):
            wb_ref[...] = b_ref[...].astype(BF16)

    def product(lhs_refs, out_refs):
        acc = mul(lhs_refs[0][...], wb_refs[0][...])
        for a_ref, wb_ref in zip(lhs_refs[1:], wb_refs[1:]):
            acc = acc + mul(a_ref[...], wb_ref[...])
        if n_out == 1:
            out_refs[0][...] = acc.astype(out_refs[0].dtype)
            return
        for o_ref, (lo, hi) in zip(out_refs, ranges):
            @pl.when((j >= lo) & (j < hi))
            def _(o_ref=o_ref):
                o_ref[...] = acc.astype(o_ref.dtype)

    if not extra:
        product(a_refs, o_refs)
        return

    @pl.when(i < nm)
    def _():
        product(a_refs, o_refs)

    @pl.when(i == nm)
    def _():
        product(a2_refs, o2_refs)


def _matmul_w32(a_list, w_list, out_dtype, out_cols=None, col0=0, tm=512, tn=1024, w_buffers=2,
                a2_list=None, w_rows=False):
    m, kdim = a_list[0].shape
    tm = min(tm, m)
    out_cols = [w_list[0].shape[1]] if out_cols is None else out_cols
    n = sum(out_cols)
    tn = _pick_tile(math.gcd(col0, *out_cols), tn)
    assert m % tm == 0
    nm = m // tm
    jb = col0 // tn
    row_blk = [p if w.shape[0] != kdim else 0 for p, w in enumerate(w_list)]
    ranges, lo = [], 0
    for c in out_cols:
        ranges.append((lo, lo + c // tn))
        lo += c // tn
    n_pairs = len(a_list)
    extra = a2_list is not None
    m2 = a2_list[0].shape[0] if extra else 0

    def out_map(lo, hi, second=False):
        def index(j, i):
            inside = (j >= lo) & (j < hi)
            ii = jnp.where(inside, jnp.minimum(i, nm - 1), jnp.where(j < lo, 0, nm - 1))
            return (0 if second else ii), jnp.clip(j - lo, 0, hi - lo - 1)
        return index

    in_specs = [pl.BlockSpec((tm, kdim), lambda j, i: (jnp.minimum(i, nm - 1), 0))] * n_pairs
    out_specs = [pl.BlockSpec((tm, tn), out_map(lo, hi)) for lo, hi in ranges]
    out_shape = [jax.ShapeDtypeStruct((m, c), out_dtype) for c in out_cols]
    args = list(a_list)
    if extra:
        in_specs += [pl.BlockSpec((m2, kdim), lambda j, i: (0, 0))] * n_pairs
        out_specs += [pl.BlockSpec((m2, tn), out_map(lo, hi, True)) for lo, hi in ranges]
        out_shape += [jax.ShapeDtypeStruct((m2, c), out_dtype) for c in out_cols]
        args += list(a2_list)
    if w_rows:
        assert n_pairs == 1 and w_list[0].shape[1] == kdim
        w_block = (tn, kdim)
        in_specs += [pl.BlockSpec(w_block, lambda j, i: (j + jb, 0), pipeline_mode=pl.Buffered(w_buffers))]
    else:
        w_block = (kdim, tn)
        in_specs += [pl.BlockSpec(w_block, lambda j, i, rb=rb: (rb, j + jb),
                                  pipeline_mode=pl.Buffered(w_buffers)) for rb in row_blk]
    outs = pl.pallas_call(
        functools.partial(_mmw_kernel, n_pairs=n_pairs, ranges=ranges, nm=nm, extra=extra, w_rows=w_rows),
        grid=(n // tn, nm + (1 if extra else 0)),
        in_specs=in_specs, out_specs=out_specs, out_shape=out_shape,
        scratch_shapes=[pltpu.VMEM(w_block, BF16)] * n_pairs,
        compiler_params=_cp("arbitrary", "arbitrary"), name="matmul_w32",
    )(*args, *w_list)
    k = len(out_cols)
    first = outs[0] if k == 1 else list(outs[:k])
    if not extra:
        return first
    return first, (outs[k] if k == 1 else list(outs[k:]))


def _gates_kernel(h_ref, w_ref, par_ref, g_ref, f_ref, carry_ref, *, ha, hb, seg):
    @pl.when(pl.program_id(1) == 0)
    def _():
        carry_ref[...] = jnp.zeros_like(carry_ref)

    p = _dot(h_ref[...], w_ref[...].astype(BF16))
    tr = p.shape[0]
    lane = _iota(p.shape, 1)
    a_log, bias = par_ref[0:1, :], par_ref[1:2, :]
    x = p + bias
    beta = _sigmoid(p)
    g = -jnp.exp(a_log) * _softplus(x)
    logf = -_softplus(-x)
    out = jnp.where(lane < ha, beta,
                    jnp.where(lane < 2 * ha, g, jnp.where(lane < 2 * ha + hb, logf, 0.0)))
    g_ref[...] = out
    rr, cc = _iota((tr, tr), 0), _iota((tr, tr), 1)
    tri = rr >= cc
    if seg is not None:
        tri = tri & ((rr >> _log2(seg)) == (cc >> _log2(seg)))
    cs = _dot(tri.astype(F32), out, HI) + carry_ref[0:1, :]
    f_ref[...] = cs
    carry_ref[0:1, :] = cs[tr - 1:tr, :]


def _gates(h, w_small, par, nb, t, ha, hb):
    m, d = h.shape
    if t >= 128:
        tr, per, seg, nb_grid = min(t, 512), t // min(t, 512), None, nb
    else:
        tr, per, seg, nb_grid = m, 1, t, 1
    spec = pl.BlockSpec((tr, LANE), lambda b, j: (b * per + j, 0))
    return pl.pallas_call(
        functools.partial(_gates_kernel, ha=ha, hb=hb, seg=seg),
        grid=(nb_grid, per),
        in_specs=[pl.BlockSpec((tr, d), lambda b, j: (b * per + j, 0)),
                  pl.BlockSpec((d, LANE), lambda b, j: (0, 0)),
                  pl.BlockSpec((SUBLANE, LANE), lambda b, j: (0, 0))],
        out_specs=[spec, spec],
        out_shape=[jax.ShapeDtypeStruct((m, LANE), F32)] * 2,
        scratch_shapes=[pltpu.VMEM((SUBLANE, LANE), F32)],
        compiler_params=_cp("arbitrary", "arbitrary"), name="gates",
    )(h, w_small, par)


def _each(f, *lists):
    return [f(*args) for args in zip(*lists)]


def _unit_lower_inverse(lows):
    c = lows[0].shape[0]
    assert c == 64
    r, s = _iota((c, c), 0), _iota((c, c), 1)
    eye = (r == s).astype(F32)
    same16 = (r >> 4) == (s >> 4)
    same32 = (r >> 5) == (s >> 5)
    ld = _each(lambda low: jnp.where(same16, low, 0.0), lows)
    x = _each(lambda a: eye - a, ld)
    p = _each(lambda a: _dot3(a, a), ld)
    for _ in range(2):
        xp = _each(lambda a, b: _dot3(jnp.concatenate([a, b], axis=0), b), x, p)
        x = _each(lambda a, b: a + b[:c], x, xp)
        p = _each(lambda b: b[c:], xp)
    x = _each(lambda a, b: a + _dot3(a, b), x, p)
    for keep in (same32 & jnp.logical_not(same16), jnp.logical_not(same32)):
        off = _each(lambda low: jnp.where(keep, low, 0.0), lows)
        y = _each(_dot3, x, off)
        x = _each(lambda a, b: a - _dot3(b, a), x, y)
    return x


def _delta_kernel(q_ref, k_ref, v_ref, z_ref, gt_ref, cwq_ref, cwk_ref, cwv_ref,
                  cbq_ref, cbk_ref, cbv_ref, s0_ref, gn_ref, o_ref, sfin_ref, ext_ref, s_ref,
                  *, hpb, ha, kw, valid_rows, nc):
    c_idx = pl.program_id(2)
    hblk = pl.program_id(1)
    c = q_ref.shape[0]
    hd = LANE

    @pl.when(c_idx == 0)
    def _():
        s_ref[...] = s0_ref[...]
        for i, cb in enumerate((cbq_ref, cbk_ref, cbv_ref)):
            ext_ref[i, 0:SUBLANE, :] = cb[...]

    conv = []
    for i, (r_ref, cw_ref) in enumerate(((q_ref, cwq_ref), (k_ref, cwk_ref), (v_ref, cwv_ref))):
        ext_ref[i, SUBLANE:SUBLANE + c, :] = r_ref[...]
        acc = ext_ref[i, SUBLANE:SUBLANE + c, :] * cw_ref[kw - 1:kw, :]
        for j in range(kw - 1):
            acc = acc + ext_ref[i, pl.ds(SUBLANE - (kw - 1) + j, c), :] * cw_ref[j:j + 1, :]
        ext_ref[i, 0:SUBLANE, :] = ext_ref[i, c:c + SUBLANE, :]
        acc = _silu(acc)
        if valid_rows < c:
            acc = jnp.where(_iota(acc.shape, 0) < valid_rows, acc, 0.0)
        conv.append(acc)
    xq, xk, xv = conv

    gt = gt_ref[...]
    rr, ss = _iota((c, c), 0), _iota((c, c), 1)
    incl, strict = rr >= ss, rr > ss
    gcum = _dot(incl.astype(F32), gt, HI)
    lane = _iota(gt.shape, 1)
    gn = gn_ref[...]
    heads = list(range(hpb))
    sls = [slice(j * hd, (j + 1) * hd) for j in heads]
    beta = [jnp.sum(jnp.where(lane == hblk * hpb + j, gt, 0.0), axis=1, keepdims=True) for j in heads]
    gc = [jnp.sum(jnp.where(lane == ha + hblk * hpb + j, gcum, 0.0), axis=1, keepdims=True) for j in heads]
    qh = [xq[:, sl] for sl in sls]
    qh = _each(lambda a: a * lax.rsqrt(jnp.sum(a * a, axis=-1, keepdims=True) + L2_EPS) * (hd ** -0.5), qh)
    kh = [xk[:, sl] for sl in sls]
    kh = _each(lambda a: a * lax.rsqrt(jnp.sum(a * a, axis=-1, keepdims=True) + L2_EPS), kh)
    decay = _each(lambda g: jnp.where(incl, jnp.exp(g - _col_to_row(g)), 0.0), gc)
    kb = _each(lambda a, b: a * b, kh, beta)
    vb = [xv[:, sl] * b for sl, b in zip(sls, beta)]
    egc = _each(jnp.exp, gc)
    kk = _each(lambda a, q, k: _dot_nt(jnp.concatenate([a, q], axis=0).astype(BF16), k.astype(BF16)),
               kb, qh, kh)
    low = _each(lambda a, d: jnp.where(strict, a[:c] * d, 0.0), kk, decay)
    attn = _each(lambda a, d: (a[c:] * d).astype(BF16), kk, decay)
    tmat = _unit_lower_inverse(low)
    uw = _each(lambda t, v, k, e: _dot(t.astype(BF16), jnp.concatenate([v, k * e], axis=1).astype(BF16)),
               tmat, vb, kb, egc)
    st = [s_ref[j] for j in heads]
    ws = _each(lambda a, q, e, s: _dot(jnp.concatenate([a[:, hd:], q * e], axis=0).astype(BF16),
                                       s.astype(BF16)), uw, qh, egc, st)
    v_new = _each(lambda a, b: (a[:, :hd] - b[:c]).astype(BF16), uw, ws)
    o = _each(lambda b, a, v: b[c:] + _dot(a, v), ws, attn, v_new)
    g_last = _each(lambda g: g[c - 1:c, :], gc)
    k_dec = _each(lambda k, gl, g: (k * jnp.exp(gl - g)).astype(BF16), kh, g_last, gc)
    s_new = _each(lambda s, gl, k, v: s * jnp.exp(gl) + _dot_tn(k, v), st, g_last, k_dec, v_new)
    for j in heads:
        s_ref[j] = s_new[j]
        o_ref[:, sls[j]] = (_rms(o[j], gn) * _silu(z_ref[:, sls[j]])).astype(o_ref.dtype)

    @pl.when(c_idx == nc - 1)
    def _():
        sfin_ref[...] = s_ref[...]


def _delta(proj, gates, conv_w, conv_buf8, s0, gnorm, nb, tp, valid_rows, ha, hpb=16):
    c = DELTA_CHUNK
    hd = LANE
    hpb = min(hpb, ha)
    w = hpb * hd
    nc = tp // c
    nhb = ha // hpb
    kw = conv_w.shape[0]
    cw8 = jnp.pad(conv_w, ((0, SUBLANE - kw), (0, 0)))

    def col(off):
        return pl.BlockSpec((c, w), lambda b, h, i, off=off: (b * nc + i, off * nhb + h))

    def cwspec(off):
        return pl.BlockSpec((SUBLANE, w), lambda b, h, i, off=off: (0, off * nhb + h))

    def cbspec(off):
        return pl.BlockSpec((None, SUBLANE, w), lambda b, h, i, off=off: (b, 0, off * nhb + h))

    state_spec = pl.BlockSpec((None, hpb, hd, hd), lambda b, h, i: (b, h, 0, 0))
    return pl.pallas_call(
        functools.partial(_delta_kernel, hpb=hpb, ha=ha, kw=kw, valid_rows=valid_rows, nc=nc),
        grid=(nb, nhb, nc),
        in_specs=[col(0), col(1), col(2), col(3),
                  pl.BlockSpec((c, LANE), lambda b, h, i: (b * nc + i, 0)),
                  cwspec(0), cwspec(1), cwspec(2), cbspec(0), cbspec(1), cbspec(2),
                  state_spec, pl.BlockSpec((1, hd), lambda b, h, i: (0, 0))],
        out_specs=[pl.BlockSpec((c, w), lambda b, h, i: (b * nc + i, h)), state_spec],
        out_shape=[jax.ShapeDtypeStruct((nb * tp, ha * hd), BF16),
                   jax.ShapeDtypeStruct(s0.shape, F32)],
        scratch_shapes=[pltpu.VMEM((3, c + SUBLANE, w), F32), pltpu.VMEM((hpb, hd, hd), F32)],
        compiler_params=_cp("arbitrary", "arbitrary", "arbitrary"), name="delta",
    )(proj, proj, proj, proj, gates, cw8, cw8, cw8, conv_buf8, conv_buf8, conv_buf8, s0,
      gnorm.reshape(1, hd))


def _softmax_step(carry, s, v, valid=None):
    m, l, acc = carry
    m_new = jnp.maximum(m, jnp.max(s, axis=-1, keepdims=True))
    alpha = jnp.exp(m - m_new)
    p = jnp.exp(s - m_new)
    if valid is not None:
        p = jnp.where(valid, p, 0.0)
    l = alpha * l + jnp.sum(p, axis=-1, keepdims=True)
    acc = alpha * acc + _dot(p.astype(BF16), v)
    return m_new, l, acc


def _softmax_init(rows, hd):
    return (jnp.full((rows, 1), NEG, F32), jnp.zeros((rows, 1), F32), jnp.zeros((rows, hd), F32))


def _fox_prompt_kernel(q_ref, k_ref, v_ref, f_ref, o_ref, kb_ref, vb_ref, *, tq, tk, scale, par):
    seq = q_ref.shape[0]
    kb_ref[...] = k_ref[...].astype(BF16)
    vb_ref[...] = v_ref[...].astype(BF16)
    col_minus_row = _iota((tq, tk), 1) - _iota((tq, tk), 0)
    nqb = seq // tq
    for c0 in range(0, nqb, par):
        chunk = list(range(c0, min(c0 + par, nqb)))
        qs = [(q_ref[pl.ds(qb * tq, tq), :] * scale).astype(BF16) for qb in chunk]
        fqs = [_row_to_col(f_ref[:, pl.ds(qb * tq, tq)]) for qb in chunk]
        carries = [_softmax_init(tq, LANE) for _ in chunk]
        n_steps = [-(-((qb + 1) * tq) // tk) for qb in chunk]
        for step in range(max(n_steps)):
            live = [i for i in range(len(chunk)) if step < n_steps[i]]
            k = kb_ref[pl.ds(step * tk, tk), :]
            v = vb_ref[pl.ds(step * tk, tk), :]
            fk = f_ref[:, pl.ds(step * tk, tk)]
            ss = [_dot_nt(qs[i], k) + (fqs[i] - fk) for i in live]
            for n, i in enumerate(live):
                if step * tk + tk - 1 > chunk[i] * tq:
                    ss[n] = jnp.where(col_minus_row <= chunk[i] * tq - step * tk, ss[n], NEG)
            ms = [jnp.maximum(carries[i][0], jnp.max(s, axis=-1, keepdims=True)) for i, s in zip(live, ss)]
            ps = [jnp.exp(s - m) for s, m in zip(ss, ms)]
            alphas = [jnp.exp(carries[i][0] - m) for i, m in zip(live, ms)]
            pvs = [_dot(p.astype(BF16), v) for p in ps]
            for i, m, p, a, pv in zip(live, ms, ps, alphas, pvs):
                carries[i] = (m, a * carries[i][1] + jnp.sum(p, axis=-1, keepdims=True),
                              a * carries[i][2] + pv)
        for qb, (_, l, acc) in zip(chunk, carries):
            o_ref[pl.ds(qb * tq, tq), :] = (acc / l).astype(o_ref.dtype)


def _fox_prompt(q_arr, q_col, k_arr, v_arr, f_rows, nb, s, nh, tq=128, tk=512, par=8):
    tq, tk = min(tq, s), min(tk, s)
    assert s % tq == 0 and s % tk == 0
    hd = LANE
    return pl.pallas_call(
        functools.partial(_fox_prompt_kernel, tq=tq, tk=tk, scale=hd ** -0.5, par=par),
        grid=(nb, nh),
        in_specs=[pl.BlockSpec((s, hd), lambda b, h: (b, q_col + h)),
                  pl.BlockSpec((s, hd), lambda b, h: (b, h)),
                  pl.BlockSpec((s, hd), lambda b, h: (b, h)),
                  pl.BlockSpec((None, None, 1, s), lambda b, h: (b, h, 0, 0))],
        out_specs=pl.BlockSpec((s, hd), lambda b, h: (b, h)),
        out_shape=jax.ShapeDtypeStruct((nb * s, nh * hd), BF16),
        scratch_shapes=[pltpu.VMEM((s, hd), BF16), pltpu.VMEM((s, hd), BF16)],
        compiler_params=_cp("arbitrary", "arbitrary"), name="fox_prompt",
    )(q_arr, k_arr, v_arr, f_rows)


def _fpast_kernel(pt_ref, *refs, group):
    lp_refs, o_ref, carry_ref = refs[:group], refs[group], refs[group + 1]

    @pl.when(pl.program_id(1) == 0)
    def _():
        carry_ref[...] = jnp.zeros_like(carry_ref)

    p = lp_refs[0].shape[0]
    upper = (_iota((p, p), 1) > _iota((p, p), 0)).astype(F32)
    for i in range(group):
        lp = lp_refs[i][...]
        o_ref[group - 1 - i] = -(_dot(upper, lp, HI) + carry_ref[0:1, :])
        carry_ref[0:1, :] = carry_ref[0:1, :] + jnp.sum(lp, axis=0, keepdims=True)


def _fpast(logf_pool, page_table):
    db, n_pages = page_table.shape
    _, p, nh = logf_pool.shape
    group = max(k for k in (16, 8, 4, 2, 1) if n_pages % k == 0)
    nblk = n_pages // group

    def lp_spec(i):
        return pl.BlockSpec((None, p, nh),
                            lambda b, j, pt, i=i: (pt[b, n_pages - 1 - (j * group + i)], 0, 0))

    return pl.pallas_call(
        functools.partial(_fpast_kernel, group=group),
        grid_spec=pltpu.PrefetchScalarGridSpec(
            num_scalar_prefetch=1, grid=(db, nblk),
            in_specs=[lp_spec(i) for i in range(group)],
            out_specs=pl.BlockSpec((None, group, p, nh), lambda b, j, pt: (b, nblk - 1 - j, 0, 0)),
            scratch_shapes=[pltpu.VMEM((SUBLANE, nh), F32)]),
        out_shape=jax.ShapeDtypeStruct((db, n_pages, p, nh), F32),
        compiler_params=_cp("arbitrary", "arbitrary"), name="fpast",
    )(page_table, *([logf_pool] * group))


def _fox_sample_kernel(pt_ref, *refs, nh, scale, n_steps, pp):
    q_ref, k_refs, v_refs = refs[0], refs[1:1 + pp], refs[1 + pp:1 + 2 * pp]
    (fp_ref, kn_ref, vn_ref, fn_ref, o_ref, m_ref, l_ref, acc_ref, bias_ref, s0_ref,
     s1_ref) = refs[1 + 2 * pp:]
    p = pl.program_id(1)
    r = q_ref.shape[0]
    rows = k_refs[0].shape[0]
    lognh = _log2(nh)
    fn = fn_ref[...]

    @pl.when(p == 0)
    def _():
        m_ref[...] = jnp.full_like(m_ref, NEG)
        l_ref[...] = jnp.zeros_like(l_ref)
        acc_ref[...] = jnp.zeros_like(acc_ref)
        s1_ref[...] = jnp.zeros_like(s1_ref)
        match = (_iota((r, rows), 0) & (nh - 1)) == (_iota((r, rows), 1) & (nh - 1))
        bias_ref[...] = jnp.where(match, _row_to_col(fn), NEG)

    q = (q_ref[...] * scale).astype(BF16)

    def update(s, v):
        m, l, acc = _softmax_step((m_ref[...], l_ref[...], acc_ref[...]), s, v)
        m_ref[...], l_ref[...], acc_ref[...] = m, l, acc

    def step(s_w, s_r):
        for i in range(pp):
            s_w[:, i * rows:(i + 1) * rows] = (_dot_nt(q, k_refs[i][...].astype(BF16))
                                               + (bias_ref[...] - fp_ref[i]))
        live = p > 0
        s = s_r[...]
        m_old = m_ref[...]
        m_new = jnp.where(live, jnp.maximum(m_old, jnp.max(s, axis=-1, keepdims=True)), m_old)
        pr = jnp.where(live, jnp.exp(s - m_new), 0.0)
        alpha = jnp.exp(m_old - m_new)
        l_ref[...] = alpha * l_ref[...] + jnp.sum(pr, axis=-1, keepdims=True)
        pr = pr.astype(BF16)
        pv = _dot(pr[:, :rows], v_refs[0][...].astype(BF16))
        for i in range(1, pp):
            pv = pv + _dot(pr[:, i * rows:(i + 1) * rows], v_refs[i][...].astype(BF16))
        acc_ref[...] = alpha * acc_ref[...] + pv
        m_ref[...] = m_new

    @pl.when(p % 2 == 0)
    def _():
        step(s0_ref, s1_ref)

    @pl.when(p % 2 == 1)
    def _():
        step(s1_ref, s0_ref)

    @pl.when(p == n_steps)
    def _():
        rr, cc = _iota((r, r), 0), _iota((r, r), 1)
        ok = ((rr & (nh - 1)) == (cc & (nh - 1))) & ((cc >> lognh) <= (rr >> lognh))
        s2 = _dot_nt(q, kn_ref[...].astype(BF16)) + _row_to_col(fn) - fn
        update(jnp.where(ok, s2, NEG), vn_ref[...].astype(BF16))
        o_ref[...] = (acc_ref[...] / l_ref[...]).astype(o_ref.dtype)


def _fox_sample(q, k_new, v_new, f_new, k_pool, v_pool, f_past, page_table, nh):
    db, r, hd = q.shape
    n_pages = page_table.shape[1]
    rows = k_pool.shape[1]
    tok = pl.BlockSpec((None, r, hd), lambda b, p, pt: (b, 0, 0))
    pp = max(k for k in (4, 2, 1) if n_pages % k == 0)
    n_steps = n_pages // pp
    last = n_steps - 1
    k_pages = [pl.BlockSpec((None, rows, hd),
                            lambda b, p, pt, i=i: (pt[b, jnp.minimum(p, last) * pp + i], 0, 0))
               for i in range(pp)]
    v_pages = [pl.BlockSpec((None, rows, hd),
                            lambda b, p, pt, i=i: (pt[b, jnp.maximum(p - 1, 0) * pp + i], 0, 0))
               for i in range(pp)]
    return pl.pallas_call(
        functools.partial(_fox_sample_kernel, nh=nh, scale=hd ** -0.5, n_steps=n_steps, pp=pp),
        grid_spec=pltpu.PrefetchScalarGridSpec(
            num_scalar_prefetch=1, grid=(db, n_steps + 1),
            in_specs=[tok] + k_pages + v_pages + [
                pl.BlockSpec((None, pp, 1, rows), lambda b, p, pt: (b, jnp.minimum(p, last), 0, 0)),
                tok, tok, pl.BlockSpec((None, 1, r), lambda b, p, pt: (b, 0, 0))],
            out_specs=tok,
            scratch_shapes=[pltpu.VMEM((r, 1), F32), pltpu.VMEM((r, 1), F32),
                            pltpu.VMEM((r, hd), F32), pltpu.VMEM((r, rows), F32),
                            pltpu.VMEM((r, pp * rows), F32), pltpu.VMEM((r, pp * rows), F32)]),
        out_shape=jax.ShapeDtypeStruct((db, r, hd), BF16),
        compiler_params=_cp("arbitrary", "arbitrary"), name="fox_sample",
    )(page_table, q, *([k_pool] * pp), *([v_pool] * pp), f_past, k_new, v_new, f_new)


def _dil_prompt_kernel(*refs, groups, tb, scale, par):
    ng = len(groups)
    q_refs, k_refs, v_refs = refs[:ng], refs[ng:2 * ng], refs[2 * ng:3 * ng]
    o_ref, m_ref, l_ref, acc_ref = refs[3 * ng:]
    seq = q_refs[0].shape[0]
    diff = _iota((tb, tb), 0) - _iota((tb, tb), 1)

    def rows_of(res, blk, dil):
        return pl.ds(res + blk * tb * dil, tb, stride=dil) if dil > 1 else pl.ds(blk * tb, tb)

    for g, (window, dil) in enumerate(groups):
        reach = window // dil
        nqb = seq // dil // tb
        back = -(-reach // tb)
        units = [(res, qb) for res in range(dil) for qb in range(nqb)]
        for c0 in range(0, len(units), par):
            chunk = units[c0:c0 + par]
            rows = [rows_of(res, qb, dil) for res, qb in chunk]
            qs = [(q_refs[g][rw, :] * scale).astype(BF16) for rw in rows]
            if g == 0:
                carries = [_softmax_init(tb, LANE) for _ in chunk]
            else:
                carries = [(m_ref[rw, :], l_ref[rw, :], acc_ref[rw, :]) for rw in rows]
            for step in range(back + 1):
                live = [i for i, (_, qb) in enumerate(chunk) if qb - step >= 0]
                if not live:
                    continue
                krows = [rows_of(chunk[i][0], chunk[i][1] - step, dil) for i in live]
                ks = [k_refs[g][rw, :].astype(BF16) for rw in krows]
                vs = [v_refs[g][rw, :].astype(BF16) for rw in krows]
                ss = [_dot_nt(qs[i], k) for i, k in zip(live, ks)]
                if step * tb - (tb - 1) < 0 or step * tb + (tb - 1) > reach:
                    dist = diff + step * tb
                    valid = (dist >= 0) & (dist <= reach)
                    ss = [jnp.where(valid, s, NEG) for s in ss]
                ms = [jnp.maximum(carries[i][0], jnp.max(s, axis=-1, keepdims=True)) for i, s in zip(live, ss)]
                ps = [jnp.exp(s - m) for s, m in zip(ss, ms)]
                alphas = [jnp.exp(carries[i][0] - m) for i, m in zip(live, ms)]
                pvs = [_dot(p.astype(BF16), v) for p, v in zip(ps, vs)]
                for i, m, p, a, pv in zip(live, ms, ps, alphas, pvs):
                    carries[i] = (m, a * carries[i][1] + jnp.sum(p, axis=-1, keepdims=True),
                                  a * carries[i][2] + pv)
            for rw, (m, l, acc) in zip(rows, carries):
                m_ref[rw, :], l_ref[rw, :], acc_ref[rw, :] = m, l, acc
    o_ref[...] = (acc_ref[...] / l_ref[...]).astype(o_ref.dtype)


def _dil_prompt(srcs, nb, s, nh, groups, tb=128, par=8):
    hd = LANE
    ng = len(groups)
    tb = min(tb, s)
    for window, dil in groups:
        assert s % (dil * tb) == 0 and window % dil == 0
    args, specs = [], []
    for which in range(3):
        for g in range(ng):
            arr, col = srcs[g][which]
            args.append(arr)
            specs.append(pl.BlockSpec((s, hd), lambda b, h, col=col: (b, col + h)))
    return pl.pallas_call(
        functools.partial(_dil_prompt_kernel, groups=groups, tb=tb, scale=hd ** -0.5, par=par),
        grid=(nb, nh),
        in_specs=specs,
        out_specs=pl.BlockSpec((s, hd), lambda b, h: (b, h)),
        out_shape=jax.ShapeDtypeStruct((nb * s, nh * hd), BF16),
        scratch_shapes=[pltpu.VMEM((s, 1), F32), pltpu.VMEM((s, 1), F32), pltpu.VMEM((s, hd), F32)],
        compiler_params=_cp("arbitrary", "arbitrary"), name="dil_prompt",
    )(*args)


def _dil_sample_kernel(*refs, groups, wbs, tb, nh, scale, plan):
    ng = len(groups)
    q_refs, kb_refs, vb_refs = refs[:ng], refs[ng:2 * ng], refs[2 * ng:3 * ng]
    kn_refs, vn_refs = refs[3 * ng:4 * ng], refs[4 * ng:5 * ng]
    o_ref, m_ref, l_ref, acc_ref = refs[5 * ng:5 * ng + 4]
    j = pl.program_id(1)
    r = q_refs[0].shape[0]
    rows = tb * nh
    lognh = _log2(nh)
    total = sum(steps for _, steps, _ in plan)

    @pl.when(j == 0)
    def _():
        m_ref[...] = jnp.full_like(m_ref, NEG)
        l_ref[...] = jnp.zeros_like(l_ref)
        acc_ref[...] = jnp.zeros_like(acc_ref)

    def update(s, v, valid):
        m, l, acc = _softmax_step((m_ref[...], l_ref[...], acc_ref[...]),
                                  jnp.where(valid, s, NEG), v, valid)
        m_ref[...], l_ref[...], acc_ref[...] = m, l, acc

    for g, (window, dil) in enumerate(groups):
        start, steps, nmb = plan[g]

        def block(g=g, window=window, dil=dil, start=start, nmb=nmb):
            rr, cc = _iota((r, rows), 0), _iota((r, rows), 1)
            res = (wbs[g] + (j - start) // nmb) % dil
            key_tok = (((j - start) % nmb) * tb + (cc >> lognh)) * dil + res
            delta = wbs[g] + (rr >> lognh) - key_tok
            valid = (((rr & (nh - 1)) == (cc & (nh - 1))) & ((delta & (dil - 1)) == 0)
                     & (delta <= window))
            q = (q_refs[g][...] * scale).astype(BF16)
            k = kb_refs[g][...].reshape(rows, LANE).astype(BF16)
            v = vb_refs[g][...].reshape(rows, LANE).astype(BF16)
            update(_dot_nt(q, k), v, valid)

        pl.when((j >= start) & (j < start + steps))(block)

    @pl.when(j == total - 1)
    def _():
        rr, cc = _iota((r, r), 0), _iota((r, r), 1)
        delta = (rr >> lognh) - (cc >> lognh)
        match = (rr & (nh - 1)) == (cc & (nh - 1))
        for g, (window, dil) in enumerate(groups):
            valid = match & (delta >= 0) & ((delta & (dil - 1)) == 0) & (delta <= window)
            q = (q_refs[g][...] * scale).astype(BF16)
            update(_dot_nt(q, kn_refs[g][...].astype(BF16)), vn_refs[g][...].astype(BF16), valid)
        o_ref[...] = (acc_ref[...] / l_ref[...]).astype(o_ref.dtype)


def _dil_sample(qs, kns, vns, kbufs, vbufs, groups, nh, t):
    db, r, hd = qs[0].shape
    ng = len(groups)
    wbs = [kb.shape[1] for kb in kbufs]
    assert all(wb % dil == 0 for wb, (_, dil) in zip(wbs, groups))
    per_res = [wb // dil for wb, (_, dil) in zip(wbs, groups)]
    tb = 128
    while any(n % tb for n in per_res):
        tb //= 2
    plan, start = [], 0
    for n, (_, dil) in zip(per_res, groups):
        steps = min(dil, t) * (n // tb)
        plan.append((start, steps, n // tb))
        start += steps
    total = start
    tok = pl.BlockSpec((None, r, hd), lambda b, j: (b, 0, 0))

    def buf_spec(g):
        first, steps, nmb = plan[g]
        dil, wb = groups[g][1], wbs[g]

        def index(b, j):
            sg = jnp.clip(j - first, 0, steps - 1)
            return b, sg % nmb, (wb + sg // nmb) % dil, 0, 0

        return pl.BlockSpec((None, tb, None, nh, hd), index)

    split = lambda a, g: a.reshape(db, per_res[g], groups[g][1], nh, hd)
    kbufs = [split(a, g) for g, a in enumerate(kbufs)]
    vbufs = [split(a, g) for g, a in enumerate(vbufs)]
    bspecs = [buf_spec(g) for g in range(ng)]
    return pl.pallas_call(
        functools.partial(_dil_sample_kernel, groups=groups, wbs=wbs, tb=tb, nh=nh,
                          scale=hd ** -0.5, plan=plan),
        grid=(db, total),
        in_specs=[tok] * ng + bspecs + bspecs + [tok] * (2 * ng),
        out_specs=tok,
        out_shape=jax.ShapeDtypeStruct((db, r, hd), BF16),
        scratch_shapes=[pltpu.VMEM((r, 1), F32), pltpu.VMEM((r, 1), F32), pltpu.VMEM((r, hd), F32)],
        compiler_params=_cp("arbitrary", "arbitrary"), name="dil_sample",
    )(*qs, *kbufs, *vbufs, *kns, *vns)


def _ffn_up_kernel(*refs, kw, tiles_per_batch, t_s, nm):
    nh = kw - 1
    hp_ref, hs_ref, wu_ref, wg_ref, cw_ref = refs[:5]
    halo_refs = refs[5:5 + nh]
    actp_ref, tailp_ref, acts_ref, gates_ref, ext_ref, wub_ref, wgb_ref = refs[5 + nh:]
    m = pl.program_id(1)
    tn = wub_ref.shape[1]

    @pl.when(m == 0)
    def _():
        wub_ref[...] = wu_ref[...].astype(BF16)
        wgb_ref[...] = wg_ref[...].astype(BF16)

    def conv_gelu(gate, up, fix):
        rows = gate.shape[0]
        ext_ref[SUBLANE:SUBLANE + rows, :] = gate
        gc = gate * cw_ref[kw - 1:kw, :]
        for j in range(kw - 1):
            shift = kw - 1 - j
            gc = gc + fix(ext_ref[pl.ds(SUBLANE - shift, rows), :], shift) * cw_ref[j:j + 1, :]
        return _gelu_tanh(gc) * up

    @pl.when(m < nm)
    def _():
        h = hp_ref[...]
        tm = h.shape[0]
        up = _dot(h, wub_ref[...])
        gate = _dot(h, wgb_ref[...])
        prev = jnp.where((m % tiles_per_batch) == 0, 0.0, ext_ref[tm:tm + SUBLANE, :])
        ext_ref[0:SUBLANE, :] = prev
        tailp_ref[...] = gate[tm - SUBLANE:, :]
        actp_ref[...] = conv_gelu(gate, up, lambda sh, shift: sh).astype(actp_ref.dtype)

    @pl.when(m == nm)
    def _():
        h = hs_ref[...]
        up = _dot(h, wub_ref[...])
        gate = _dot(h, wgb_ref[...])
        ext_ref[0:SUBLANE, :] = jnp.zeros((SUBLANE, tn), F32)
        gates_ref[...] = gate
        pos = _iota(gate.shape, 0) & (t_s - 1)
        fix = lambda sh, shift: jnp.where(pos < shift, halo_refs[shift - 1][...], sh)
        acts_ref[...] = conv_gelu(gate, up, fix).astype(acts_ref.dtype)


def _ffn_up(h_p, h_s, wu, wg, layer, cw8, kw, t_p, t_s, halos, tm=512, tn=FFN_TILE):
    m, d = h_p.shape
    ms = h_s.shape[0]
    n_pad = wu.shape[-1]
    tm = min(tm, t_p)
    tn = min(tn, -(-n_pad // LANE) * LANE)
    nm, nn = m // tm, -(-n_pad // tn)
    _log2(t_s)
    last = nm - 1
    s_spec = pl.BlockSpec((ms, tn), lambda j, i: (0, j))
    in_specs = [pl.BlockSpec((tm, d), lambda j, i: (jnp.minimum(i, last), 0)),
                pl.BlockSpec((ms, d), lambda j, i: (0, 0)),
                pl.BlockSpec((None, d, tn), lambda j, i: (layer, 0, j)),
                pl.BlockSpec((None, d, tn), lambda j, i: (layer, 0, j)),
                pl.BlockSpec((SUBLANE, tn), lambda j, i: (0, j))] + [s_spec] * len(halos)
    return pl.pallas_call(
        functools.partial(_ffn_up_kernel, kw=kw, tiles_per_batch=max(t_p // tm, 1), t_s=t_s, nm=nm),
        grid=(nn, nm + 1), in_specs=in_specs,
        out_specs=[pl.BlockSpec((tm, tn), lambda j, i: (jnp.minimum(i, last), j)),
                   pl.BlockSpec((None, SUBLANE, tn), lambda j, i: (jnp.minimum(i, last), 0, j)),
                   s_spec, s_spec],
        out_shape=[jax.ShapeDtypeStruct((m, n_pad), BF16),
                   jax.ShapeDtypeStruct((nm, SUBLANE, n_pad), F32),
                   jax.ShapeDtypeStruct((ms, n_pad), BF16), jax.ShapeDtypeStruct((ms, n_pad), F32)],
        scratch_shapes=[pltpu.VMEM((max(tm, ms) + SUBLANE, tn), F32), pltpu.VMEM((d, tn), BF16),
                        pltpu.VMEM((d, tn), BF16)],
        compiler_params=_cp("arbitrary", "arbitrary"), name="ffn_up",
    )(h_p, h_s, wu, wg, cw8, *halos)


def _even_weights(w_in, conv_w, a_log, dt_bias, b_f, w_out, ha, hb, hd):
    wa, wb = ha * hd, hb * hd
    o2 = 4 * wa
    o4 = o2 + 2 * ha
    o5 = o4 + 3 * wb
    w_t = jnp.swapaxes(w_in, 0, 1)
    w_main = (w_t, w_t[o4:o5])
    small = jnp.swapaxes(jnp.concatenate([w_t[o2:o4], w_t[o5:]], axis=0), 0, 1)
    assert small.shape[1] <= LANE
    w_small = jnp.pad(small, ((0, 0), (0, LANE - small.shape[1])))
    par = jnp.zeros((SUBLANE, LANE), F32)
    par = par.at[0, ha:2 * ha].set(a_log.astype(F32))
    par = par.at[1, ha:2 * ha].set(dt_bias.astype(F32))
    par = par.at[1, 2 * ha:2 * ha + hb].set(b_f.astype(F32))
    return w_main, w_small, par, conv_w, w_out


def _even_mixer(hs, nbs, ts, wts, gnorm, conv_bufs, s0s, ha, hb, hd, past):
    (w_in, w_fox), _, _, _, w_out = wts
    wa, wb = ha * hd, hb * hd
    projs = _matmul_w32([hs[0]], [w_in], F32, out_cols=[4 * wa], tm=1024, w_buffers=1, a2_list=[hs[1]],
                        w_rows=True)
    foxes = _matmul_w32([hs[0]], [w_fox], F32, out_cols=[wb, wb, wb], tm=512, w_buffers=1,
                        a2_list=[hs[1]], w_rows=True)
    cores = [_even_core(hs[g], nbs[g], ts[g], projs[g], foxes[g], wts, gnorm, conv_bufs[g], s0s[g], ha, hb,
                        hd, past if g == 1 else None) for g in (0, 1)]
    outs = _matmul_w32([cores[0][0], cores[0][1]], [w_out, w_out], BF16, tm=512,
                       a2_list=[cores[1][0], cores[1][1]])
    return [(outs[g],) + tuple(cores[g][2:]) for g in (0, 1)]


def _even_core(h, nb, t, proj, fox, wts, gnorm, conv_buf, s0, ha, hb, hd, past):
    _, w_small, par, conv_w, _ = wts
    wa, wb = ha * hd, hb * hd
    kw = conv_w.shape[0]
    q_fox, k_new, v_new = fox
    gates, fcum = _gates(h, w_small, par, nb, t, ha, hb)

    c = DELTA_CHUNK
    tp = -(-t // c) * c
    conv_buf8 = jnp.pad(conv_buf.astype(F32), ((0, 0), (SUBLANE - (kw - 1), 0), (0, 0)))
    if tp == t:
        proj_d, gates_d = proj, gates
    else:
        pad = lambda a: jnp.pad(a.reshape(nb, t, -1), ((0, 0), (0, tp - t), (0, 0))).reshape(nb * tp, -1)
        proj_d, gates_d = pad(proj), pad(gates)
    o_a, s_new = _delta(proj_d, gates_d, conv_w, conv_buf8, s0, gnorm, nb, tp, min(t, c) if tp != t else c,
                        ha)
    if tp != t:
        o_a = o_a.reshape(nb, tp, wa)[:, :t].reshape(nb * t, wa)

    logf = gates[:, 2 * ha:2 * ha + hb].reshape(nb, t, hb)
    f_new = fcum[:, 2 * ha:2 * ha + hb].reshape(nb, t, hb)
    kb, vb = k_new.reshape(nb, t, hb, hd), v_new.reshape(nb, t, hb, hd)
    if past is None:
        f_rows = jnp.swapaxes(f_new, 1, 2).reshape(nb, hb, 1, t)
        o_b = _fox_prompt(q_fox, 0, k_new, v_new, f_rows, nb, t, hb)
    else:
        k_pool, v_pool, logf_pool, page_table = past
        n_pool, page = k_pool.shape[0], k_pool.shape[1]
        f_past = _fpast(logf_pool.astype(F32), page_table)
        f_past = f_past.reshape(nb, page_table.shape[1], 1, page * hb)
        o_b = _fox_sample(q_fox.reshape(nb, t * hb, hd), k_new.reshape(nb, t * hb, hd),
                          v_new.reshape(nb, t * hb, hd), f_new.reshape(nb, 1, t * hb),
                          k_pool.reshape(n_pool, page * hb, hd), v_pool.reshape(n_pool, page * hb, hd),
                          f_past, page_table, hb)
        o_b = o_b.reshape(nb * t, wb)
    keep = min(kw - 1, t)
    raw_tail = proj.reshape(nb, t, -1)[:, t - keep:, :3 * wa]
    new_conv = jnp.concatenate([conv_buf.astype(F32), raw_tail], axis=1)[:, -(kw - 1):]
    return o_a, o_b, s_new, new_conv, kb, vb, logf


def _odd_mixer(hs, nbs, ts, w_in, w_out, nh, hd, bufs):
    ng = len(C_GROUPS)
    wc = nh * hd
    n_main = (3 * ng - 2) * wc
    main, main_s = _matmul_w32([hs[0]], [w_in], F32, out_cols=[n_main], tm=1024, w_buffers=1,
                               a2_list=[hs[1]])
    (k_last, v_last), kv_s = _matmul_w32([hs[0]], [w_in], F32, out_cols=[wc, wc], col0=n_main, tm=512,
                                         w_buffers=1, a2_list=[hs[1]])
    nb, t = nbs[0], ts[0]
    srcs = [((main, g * 3 * nh), (main, g * 3 * nh + nh), (main, g * 3 * nh + 2 * nh))
            for g in range(ng - 1)]
    srcs.append(((main, (ng - 1) * 3 * nh), (k_last, 0), (v_last, 0)))
    o_p = _dil_prompt(srcs, nb, t, nh, C_GROUPS)
    m3 = main.reshape(nb, t, -1)
    bufs_p = []
    for g, (window, _) in enumerate(C_GROUPS):
        keep = min(window, t)
        if g < ng - 1:
            kv = [m3[:, t - keep:, (g * 3 + r) * wc:(g * 3 + r + 1) * wc] for r in (1, 2)]
        else:
            kv = [a.reshape(nb, t, wc)[:, t - keep:] for a in (k_last, v_last)]
        bufs_p.append(tuple(a.reshape(nb, keep, nh, hd) for a in kv))
    o_s, bufs_s = _odd_sample_core(jnp.concatenate([main_s] + list(kv_s), axis=1), nbs[1], ts[1], nh, hd, bufs)
    out_p, out_s = _matmul_w32([o_p], [w_out], BF16, tm=512, a2_list=[o_s])
    return (out_p, bufs_p), (out_s, bufs_s)


def _odd_sample_core(proj, nb, t, nh, hd, bufs):
    ng = len(C_GROUPS)
    new_bufs = []
    p6 = proj.reshape(nb, t, ng, 3, nh, hd)
    flat = lambda a: a.reshape(nb, -1, hd)
    qs = [flat(p6[:, :, g, 0]) for g in range(ng)]
    kns = [flat(p6[:, :, g, 1]) for g in range(ng)]
    vns = [flat(p6[:, :, g, 2]) for g in range(ng)]
    o = _dil_sample(qs, kns, vns, [bk for bk, _ in bufs], [bv for _, bv in bufs], C_GROUPS, nh, t)
    o = o.reshape(nb * t, nh * hd)
    for g in range(ng):
        bk, bv = bufs[g]
        wb = bk.shape[1]
        new_bufs.append((jnp.concatenate([bk, p6[:, :, g, 1]], axis=1)[:, -wb:],
                         jnp.concatenate([bv, p6[:, :, g, 2]], axis=1)[:, -wb:]))
    return o, new_bufs


def _ffn_weights(w_up, w_gate, conv_w, w_down):
    d_ff = w_up.shape[-1]
    kw = conv_w.shape[1]
    cw8 = jnp.pad(conv_w.astype(F32), ((0, 0), (0, SUBLANE - kw), (0, 0)))
    return w_up, w_gate, cw8, w_down.astype(BF16), kw, d_ff


def _conv_ffn(hs, nbs, ts, wts, layer, buf_s):
    wu, wg, cw8, wd, kw, d_ff = wts
    (nb, nbs_), (t, t_s) = nbs, ts
    halos = []
    for shift in range(1, kw):
        hl = jnp.zeros((nbs_, t_s, d_ff), F32)
        hl = hl.at[:, :shift].set(buf_s.astype(F32)[:, (kw - 1) - shift:])
        halos.append(hl.reshape(nbs_ * t_s, d_ff))
    act_p, tail, act_s, gate_s = _ffn_up(hs[0], hs[1], wu, wg, layer, cw8[layer], kw, t, t_s, halos)
    tiles = tail.shape[0] // nb
    gate_tail = tail.reshape(nb, tiles, SUBLANE, d_ff)[:, -1]
    new_buf_p = gate_tail[:, SUBLANE - (kw - 1):]
    new_buf_s = jnp.concatenate([buf_s.astype(F32), gate_s.reshape(nbs_, t_s, d_ff)], axis=1)[:, -(kw - 1):]
    out_p = _matmul([act_p], [wd], BF16, tm=512, tn=512, layer=layer)
    out_s = _matmul([act_s], [wd], BF16, tm=act_s.shape[0], tn=512, layer=layer)
    return (out_p, new_buf_p), (out_s, new_buf_s)


def kernel(x_prompt, x_sample, c_prompt, c_sample, state_delta, state_conv_qkv, cache_k, cache_v, cache_logf, cache_win_k0, cache_win_v0, cache_win_k1, cache_win_v1, cache_win_k2, cache_win_v2, state_ffn_conv, page_table, w_mod, b_mod, norm_pre_mix, norm_post_mix, norm_pre_ffn, norm_post_ffn, w_in_e, conv_a, a_log, dt_bias, gnorm_a, b_forget, w_out_e, w_in_o, w_out_o, w_up, w_gate, conv_ffn_w, w_down):
    bsz, seq, d = x_prompt.shape
    dbsz, dseq, _ = x_sample.shape
    depth = w_mod.shape[0]
    hd = gnorm_a.shape[-1]
    ha, hb, hc = a_log.shape[-1], b_forget.shape[-1], cache_win_k0.shape[3]
    assert hd == LANE and seq % DELTA_CHUNK == 0
    kw_a = conv_a.shape[1]
    win_caches = ((cache_win_k0, cache_win_v0), (cache_win_k1, cache_win_v1), (cache_win_k2, cache_win_v2))

    c_all = jnp.concatenate([c_prompt, c_sample], axis=0)
    c_all = jnp.pad(c_all, ((0, MOD_ROWS - c_all.shape[0]), (0, 0)))
    mod = _mod(c_all, w_mod, b_mod)
    mods = [(mod[l, :bsz].reshape(bsz, 6, d), mod[l, bsz:bsz + dbsz].reshape(dbsz, 6, d))
            for l in range(depth)]

    groups = [dict(x=x_prompt.reshape(bsz * seq, d), nb=bsz, t=seq, gi=0),
              dict(x=x_sample.reshape(dbsz * dseq, d), nb=dbsz, t=dseq, gi=1)]
    for gr in groups:
        m_0 = mods[0][gr["gi"]]
        _, gr["h"] = _post_pre(gr["x"], gr["nb"], gr["t"], shift=m_0[:, 0], scale=m_0[:, 1],
                               w_pre=norm_pre_mix[0])

    fwts = _ffn_weights(w_up, w_gate, conv_ffn_w, w_down)
    outs = {k: ([], []) for k in ("delta", "conv", "k", "v", "logf", "ffn")}
    wk = [([], []) for _ in C_GROUPS]
    wv = [([], []) for _ in C_GROUPS]
    for l in range(depth):
        if l % 2 == 0:
            e = l // 2
            ewts = _even_weights(w_in_e[e], conv_a[e], a_log[e], dt_bias[e], b_forget[e], w_out_e[e],
                                 ha, hb, hd)
        else:
            o = l // 2
            w_in_o_b, w_out_o_b = w_in_o[o], w_out_o[o]
        hs, nbs, ts = [[gr[key] for gr in groups] for key in ("h", "nb", "t")]
        if l % 2 == 0:
            conv_bufs = (jnp.zeros((bsz, kw_a - 1, 3 * ha * hd), F32), state_conv_qkv[e])
            s0s = (jnp.zeros((bsz, ha, hd, hd), F32), state_delta[e])
            past = (cache_k[e], cache_v[e], cache_logf[e], page_table)
            mixed = _even_mixer(hs, nbs, ts, ewts, gnorm_a[e], conv_bufs, s0s, ha, hb, hd, past)
            for gi, (_, s_new, new_conv, kb, vb, logf) in enumerate(mixed):
                for key, val in (("delta", s_new), ("conv", new_conv), ("k", kb), ("v", vb), ("logf", logf)):
                    outs[key][gi].append(val)
        else:
            bufs = tuple((bk[o], bv[o]) for bk, bv in win_caches)
            mixed = _odd_mixer(hs, nbs, ts, w_in_o_b, w_out_o_b, hc, hd, bufs)
            for gi, (_, new_bufs) in enumerate(mixed):
                for g in range(len(C_GROUPS)):
                    wk[g][gi].append(new_bufs[g][0])
                    wv[g][gi].append(new_bufs[g][1])
        for gr in groups:
            gi, nb, t = gr["gi"], gr["nb"], gr["t"]
            m_l = mods[l][gi]
            om = mixed[gi][0]
            gr["x"], gr["h"] = _post_pre(gr["x"], nb, t, o=om, gate=m_l[:, 2], w_post=norm_post_mix[l],
                                         shift=m_l[:, 3], scale=m_l[:, 4], w_pre=norm_pre_ffn[l])
        ffn = _conv_ffn([gr["h"] for gr in groups], nbs, ts, fwts, l, state_ffn_conv[l])
        for gr in groups:
            gi, nb, t = gr["gi"], gr["nb"], gr["t"]
            m_l = mods[l][gi]
            of, new_buf = ffn[gi]
            outs["ffn"][gi].append(new_buf)
            if l + 1 < depth:
                m_n = mods[l + 1][gi]
                gr["x"], gr["h"] = _post_pre(gr["x"], nb, t, o=of, gate=m_l[:, 5], w_post=norm_post_ffn[l],
                                             shift=m_n[:, 0], scale=m_n[:, 1], w_pre=norm_pre_mix[l + 1])
            else:
                gr["x"], _ = _post_pre(gr["x"], nb, t, o=of, gate=m_l[:, 5], w_post=norm_post_ffn[l])

    stk = lambda lst: jnp.stack(lst, axis=0)
    res = [groups[0]["x"].reshape(bsz, seq, d), groups[1]["x"].reshape(dbsz, dseq, d)]
    for key in ("delta", "conv", "k", "v", "logf"):
        res += [stk(outs[key][0]), stk(outs[key][1])]
    for g in range(len(C_GROUPS)):
        res += [stk(wk[g][0]), stk(wk[g][1]), stk(wv[g][0]), stk(wv[g][1])]
    res += [stk(outs["ffn"][0]), stk(outs["ffn"][1])]
    return tuple(res)
```

```python
import functools
import math

import jax
import jax.numpy as jnp
from jax import lax
from jax.experimental import pallas as pl
from jax.experimental.pallas import tpu as pltpu

F32 = jnp.float32
BF16 = jnp.bfloat16
HI = lax.Precision.HIGHEST

C_GROUPS = ((128, 1), (512, 4), (2048, 16))
DELTA_CHUNK = 64
RMS_EPS = 1e-6
L2_EPS = 1e-6
NEG = -1e30
LANE = 128
SUBLANE = 8
VMEM_LIMIT = 58 * 1024 * 1024
MOD_ROWS = 16
FFN_TILE = 512


def _cp(*sem):
    return pltpu.CompilerParams(dimension_semantics=sem, vmem_limit_bytes=VMEM_LIMIT)


def _dot(a, b, prec=None):
    return jnp.dot(a, b, preferred_element_type=F32, precision=prec)


def _dot_nt(a, b, prec=None):
    return lax.dot_general(a, b, (((1,), (1,)), ((), ())), preferred_element_type=F32, precision=prec)


def _dot_tn(a, b, prec=None):
    return lax.dot_general(a, b, (((0,), (0,)), ((), ())), preferred_element_type=F32, precision=prec)


def _dot3(a, b):
    ah, bh = a.astype(BF16), b.astype(BF16)
    al = (a - ah.astype(F32)).astype(BF16)
    bl = (b - bh.astype(F32)).astype(BF16)
    return _dot(ah, bh) + (_dot(ah, bl) + _dot(al, bh))


def _sigmoid(x):
    return 1.0 / (1.0 + jnp.exp(-x))


def _silu(x):
    return x * _sigmoid(x)


def _softplus(x):
    return jnp.maximum(x, 0.0) + jnp.log1p(jnp.exp(-jnp.abs(x)))


def _gelu_tanh(x):
    return 0.5 * x * (1.0 + jnp.tanh(0.7978845608028654 * (x + 0.044715 * (x * x * x))))


def _rms(x, w):
    return x * lax.rsqrt(jnp.mean(x * x, axis=-1, keepdims=True) + RMS_EPS) * w


def _iota(shape, dim):
    return lax.broadcasted_iota(jnp.int32, shape, dim)


def _row_to_col(row):
    n = row.shape[1]
    eye = _iota((n, n), 0) == _iota((n, n), 1)
    return jnp.sum(jnp.where(eye, jnp.broadcast_to(row, (n, n)), 0.0), axis=1, keepdims=True)


def _col_to_row(col):
    n = col.shape[0]
    eye = _iota((n, n), 0) == _iota((n, n), 1)
    return jnp.sum(jnp.where(eye, jnp.broadcast_to(col, (n, n)), 0.0), axis=0, keepdims=True)


def _pick_tile(n, pref):
    if n <= pref:
        return n
    t = (pref // LANE) * LANE
    while t >= LANE:
        if n % t == 0:
            return t
        t -= LANE
    return n


def _log2(n):
    assert n > 0 and n & (n - 1) == 0, n
    return n.bit_length() - 1


def _mod_kernel(c_ref, w_ref, b_ref, o_ref):
    a = _silu(c_ref[...]).astype(BF16)
    o_ref[...] = _dot(a, w_ref[...].astype(BF16)) + b_ref[...]


def _mod(c_all, w_mod, b_mod):
    depth, d, n = w_mod.shape
    r = c_all.shape[0]
    tn = _pick_tile(n, 512)
    return pl.pallas_call(
        _mod_kernel,
        grid=(depth, n // tn),
        in_specs=[pl.BlockSpec((r, d), lambda l, j: (0, 0)),
                  pl.BlockSpec((None, d, tn), lambda l, j: (l, 0, j)),
                  pl.BlockSpec((None, 1, tn), lambda l, j: (l, 0, j))],
        out_specs=pl.BlockSpec((None, r, tn), lambda l, j: (l, 0, j)),
        out_shape=jax.ShapeDtypeStruct((depth, r, n), F32),
        compiler_params=_cp("arbitrary", "arbitrary"),
        name="mod",
    )(c_all, w_mod, b_mod.reshape(depth, 1, n))


def _post_pre_kernel(*refs, has_o, has_h):
    refs = list(refs)
    x = refs.pop(0)[...]
    if has_o:
        o_ref, gate_ref, wpost_ref = refs.pop(0), refs.pop(0), refs.pop(0)
    if has_h:
        shift_ref, scale_ref, wpre_ref = refs.pop(0), refs.pop(0), refs.pop(0)
    if has_o:
        x = x + gate_ref[...] * _rms(o_ref[...].astype(F32), wpost_ref[...])
        refs.pop(0)[...] = x
    if has_h:
        h = _rms(x, wpre_ref[...]) * (1.0 + scale_ref[...]) + shift_ref[...]
        refs.pop(0)[...] = h.astype(BF16)


def _post_pre(x, nb, t, o=None, gate=None, w_post=None, shift=None, scale=None, w_pre=None):
    m, d = x.shape
    has_o, has_h = o is not None, shift is not None
    if t >= 128:
        tr = 256 if t % 256 == 0 else 128
        per = t // tr
        vec = lambda v: v.reshape(nb, 1, d)
        vec_spec = pl.BlockSpec((None, 1, d), lambda i: (i // per, 0, 0))
    else:
        tr = m
        vec = lambda v: jnp.repeat(v, t, axis=0).reshape(1, m, d)
        vec_spec = pl.BlockSpec((None, m, d), lambda i: (0, 0, 0))
    row_spec = pl.BlockSpec((tr, d), lambda i: (i, 0))
    w_spec = pl.BlockSpec((1, d), lambda i: (0, 0))
    args, in_specs, out_specs, out_shape = [x], [row_spec], [], []
    if has_o:
        args += [o, vec(gate), w_post.reshape(1, d)]
        in_specs += [row_spec, vec_spec, w_spec]
        out_specs.append(row_spec)
        out_shape.append(jax.ShapeDtypeStruct((m, d), F32))
    if has_h:
        args += [vec(shift), vec(scale), w_pre.reshape(1, d)]
        in_specs += [vec_spec, vec_spec, w_spec]
        out_specs.append(row_spec)
        out_shape.append(jax.ShapeDtypeStruct((m, d), BF16))
    outs = pl.pallas_call(
        functools.partial(_post_pre_kernel, has_o=has_o, has_h=has_h),
        grid=(m // tr,), in_specs=in_specs, out_specs=out_specs, out_shape=out_shape,
        compiler_params=_cp("arbitrary"), name="post_pre",
    )(*args)
    outs = list(outs)
    x_new = outs.pop(0) if has_o else x
    h = outs.pop(0) if has_h else None
    return x_new, h


def _mm_kernel(*refs, n_pairs, nk):
    a_refs, b_refs = refs[:n_pairs], refs[n_pairs:2 * n_pairs]
    o_ref = refs[2 * n_pairs]
    part = _dot(a_refs[0][...], b_refs[0][...])
    for a_ref, b_ref in zip(a_refs[1:], b_refs[1:]):
        part = part + _dot(a_ref[...], b_ref[...])
    if nk == 1:
        o_ref[...] = part.astype(o_ref.dtype)
        return
    acc_ref = refs[2 * n_pairs + 1]
    k = pl.program_id(2)

    @pl.when(k == 0)
    def _():
        acc_ref[...] = part

    @pl.when(k > 0)
    def _():
        acc_ref[...] += part

    @pl.when(k == nk - 1)
    def _():
        o_ref[...] = acc_ref[...].astype(o_ref.dtype)


def _matmul(a_list, b_list, out_dtype, tm=1024, tn=1024, tk=None, cols=None, layer=None):
    m, kdim = a_list[0].shape
    c0, n = (0, b_list[0].shape[-1]) if cols is None else cols
    tm = min(tm, m)
    tn = _pick_tile(n, tn)
    tk = kdim if tk is None else _pick_tile(kdim, tk)
    nk = kdim // tk
    assert m % tm == 0 and n % tn == 0 and kdim % tk == 0 and c0 % tn == 0
    jb = c0 // tn
    n_pairs = len(a_list)
    if layer is None:
        b_spec = pl.BlockSpec((tk, tn), lambda i, j, k: (k, j + jb))
    else:
        b_spec = pl.BlockSpec((None, tk, tn), lambda i, j, k: (layer, k, j + jb))
    in_specs = [pl.BlockSpec((tm, tk), lambda i, j, k: (i, k))] * n_pairs + [b_spec] * n_pairs
    scratch = [pltpu.VMEM((tm, tn), F32)] if nk > 1 else []
    return pl.pallas_call(
        functools.partial(_mm_kernel, n_pairs=n_pairs, nk=nk),
        grid=(m // tm, n // tn, nk), in_specs=in_specs,
        out_specs=pl.BlockSpec((tm, tn), lambda i, j, k: (i, j)),
        out_shape=jax.ShapeDtypeStruct((m, n), out_dtype),
        scratch_shapes=scratch,
        compiler_params=_cp("arbitrary", "arbitrary", "arbitrary"), name="matmul",
    )(*a_list, *b_list)


def _mmw_kernel(*refs, n_pairs, ranges, nm, extra, w_rows):
    n_out = len(ranges)
    mul = _dot_nt if w_rows else _dot
    a_refs, refs = refs[:n_pairs], refs[n_pairs:]
    if extra:
        a2_refs, refs = refs[:n_pairs], refs[n_pairs:]
    b_refs, refs = refs[:n_pairs], refs[n_pairs:]
    o_refs, refs = refs[:n_out], refs[n_out:]
    if extra:
        o2_refs, refs = refs[:n_out], refs[n_out:]
    wb_refs = refs
    j, i = pl.program_id(0), pl.program_id(1)

    @pl.when(i == 0)
    def _():
        for b_ref, wb_ref in zip(b_refs, wb_refs):
            wb_ref[...] = b_ref[...].astype(BF16)

    def product(lhs_refs, out_refs):
        acc = mul(lhs_refs[0][...], wb_refs[0][...])
        for a_ref, wb_ref in zip(lhs_refs[1:], wb_refs[1:]):
            acc = acc + mul(a_ref[...], wb_ref[...])
        if n_out == 1:
            out_refs[0][...] = acc.astype(out_refs[0].dtype)
            return
        for o_ref, (lo, hi) in zip(out_refs, ranges):
            @pl.when((j >= lo) & (j < hi))
            def _(o_ref=o_ref):
                o_ref[...] = acc.astype(o_ref.dtype)

    if not extra:
        product(a_refs, o_refs)
        return

    @pl.when(i < nm)
    def _():
        product(a_refs, o_refs)

    @pl.when(i == nm)
    def _():
        product(a2_refs, o2_refs)


def _matmul_w32(a_list, w_list, out_dtype, out_cols=None, col0=0, tm=512, tn=1024, w_buffers=2,
                a2_list=None, w_rows=False):
    m, kdim = a_list[0].shape
    tm = min(tm, m)
    out_cols = [w_list[0].shape[1]] if out_cols is None else out_cols
    n = sum(out_cols)
    tn = _pick_tile(math.gcd(col0, *out_cols), tn)
    assert m % tm == 0
    nm = m // tm
    jb = col0 // tn
    row_blk = [p if w.shape[0] != kdim else 0 for p, w in enumerate(w_list)]
    ranges, lo = [], 0
    for c in out_cols:
        ranges.append((lo, lo + c // tn))
        lo += c // tn
    n_pairs = len(a_list)
    extra = a2_list is not None
    m2 = a2_list[0].shape[0] if extra else 0

    def out_map(lo, hi, second=False):
        def index(j, i):
            inside = (j >= lo) & (j < hi)
            ii = jnp.where(inside, jnp.minimum(i, nm - 1), jnp.where(j < lo, 0, nm - 1))
            return (0 if second else ii), jnp.clip(j - lo, 0, hi - lo - 1)
        return index

    in_specs = [pl.BlockSpec((tm, kdim), lambda j, i: (jnp.minimum(i, nm - 1), 0))] * n_pairs
    out_specs = [pl.BlockSpec((tm, tn), out_map(lo, hi)) for lo, hi in ranges]
    out_shape = [jax.ShapeDtypeStruct((m, c), out_dtype) for c in out_cols]
    args = list(a_list)
    if extra:
        in_specs += [pl.BlockSpec((m2, kdim), lambda j, i: (0, 0))] * n_pairs
        out_specs += [pl.BlockSpec((m2, tn), out_map(lo, hi, True)) for lo, hi in ranges]
        out_shape += [jax.ShapeDtypeStruct((m2, c), out_dtype) for c in out_cols]
        args += list(a2_list)
    if w_rows:
        assert n_pairs == 1 and w_list[0].shape[1] == kdim
        w_block = (tn, kdim)
        in_specs += [pl.BlockSpec(w_block, lambda j, i: (j + jb, 0), pipeline_mode=pl.Buffered(w_buffers))]
    else:
        w_block = (kdim, tn)
        in_specs += [pl.BlockSpec(w_block, lambda j, i, rb=rb: (rb, j + jb),
                                  pipeline_mode=pl.Buffered(w_buffers)) for rb in row_blk]
    outs = pl.pallas_call(
        functools.partial(_mmw_kernel, n_pairs=n_pairs, ranges=ranges, nm=nm, extra=extra, w_rows=w_rows),
        grid=(n // tn, nm + (1 if extra else 0)),
        in_specs=in_specs, out_specs=out_specs, out_shape=out_shape,
        scratch_shapes=[pltpu.VMEM(w_block, BF16)] * n_pairs,
        compiler_params=_cp("arbitrary", "arbitrary"), name="matmul_w32",
    )(*args, *w_list)
    k = len(out_cols)
    first = outs[0] if k == 1 else list(outs[:k])
    if not extra:
        return first
    return first, (outs[k] if k == 1 else list(outs[k:]))


def _gates_kernel(h_ref, w_ref, par_ref, g_ref, f_ref, carry_ref, *, ha, hb, seg):
    @pl.when(pl.program_id(1) == 0)
    def _():
        carry_ref[...] = jnp.zeros_like(carry_ref)

    p = _dot(h_ref[...], w_ref[...].astype(BF16))
    tr = p.shape[0]
    lane = _iota(p.shape, 1)
    a_log, bias = par_ref[0:1, :], par_ref[1:2, :]
    x = p + bias
    beta = _sigmoid(p)
    g = -jnp.exp(a_log) * _softplus(x)
    logf = -_softplus(-x)
    out = jnp.where(lane < ha, beta,
                    jnp.where(lane < 2 * ha, g, jnp.where(lane < 2 * ha + hb, logf, 0.0)))
    g_ref[...] = out
    rr, cc = _iota((tr, tr), 0), _iota((tr, tr), 1)
    tri = rr >= cc
    if seg is not None:
        tri = tri & ((rr >> _log2(seg)) == (cc >> _log2(seg)))
    cs = _dot(tri.astype(F32), out, HI) + carry_ref[0:1, :]
    f_ref[...] = cs
    carry_ref[0:1, :] = cs[tr - 1:tr, :]


def _gates(h, w_small, par, nb, t, ha, hb):
    m, d = h.shape
    if t >= 128:
        tr, per, seg, nb_grid = min(t, 512), t // min(t, 512), None, nb
    else:
        tr, per, seg, nb_grid = m, 1, t, 1
    spec = pl.BlockSpec((tr, LANE), lambda b, j: (b * per + j, 0))
    return pl.pallas_call(
        functools.partial(_gates_kernel, ha=ha, hb=hb, seg=seg),
        grid=(nb_grid, per),
        in_specs=[pl.BlockSpec((tr, d), lambda b, j: (b * per + j, 0)),
                  pl.BlockSpec((d, LANE), lambda b, j: (0, 0)),
                  pl.BlockSpec((SUBLANE, LANE), lambda b, j: (0, 0))],
        out_specs=[spec, spec],
        out_shape=[jax.ShapeDtypeStruct((m, LANE), F32)] * 2,
        scratch_shapes=[pltpu.VMEM((SUBLANE, LANE), F32)],
        compiler_params=_cp("arbitrary", "arbitrary"), name="gates",
    )(h, w_small, par)


def _each(f, *lists):
    return [f(*args) for args in zip(*lists)]


def _unit_lower_inverse(lows):
    c = lows[0].shape[0]
    assert c == 64
    r, s = _iota((c, c), 0), _iota((c, c), 1)
    eye = (r == s).astype(F32)
    same16 = (r >> 4) == (s >> 4)
    same32 = (r >> 5) == (s >> 5)
    ld = _each(lambda low: jnp.where(same16, low, 0.0), lows)
    x = _each(lambda a: eye - a, ld)
    p = _each(lambda a: _dot3(a, a), ld)
    for _ in range(2):
        xp = _each(lambda a, b: _dot3(jnp.concatenate([a, b], axis=0), b), x, p)
        x = _each(lambda a, b: a + b[:c], x, xp)
        p = _each(lambda b: b[c:], xp)
    x = _each(lambda a, b: a + _dot3(a, b), x, p)
    for keep in (same32 & jnp.logical_not(same16), jnp.logical_not(same32)):
        off = _each(lambda low: jnp.where(keep, low, 0.0), lows)
        y = _each(_dot3, x, off)
        x = _each(lambda a, b: a - _dot3(b, a), x, y)
    return x


def _delta_kernel(q_ref, k_ref, v_ref, z_ref, gt_ref, cwq_ref, cwk_ref, cwv_ref,
                  cbq_ref, cbk_ref, cbv_ref, s0_ref, gn_ref, o_ref, sfin_ref, ext_ref, s_ref,
                  *, hpb, ha, kw, valid_rows, nc):
    c_idx = pl.program_id(2)
    hblk = pl.program_id(1)
    c = q_ref.shape[0]
    hd = LANE

    @pl.when(c_idx == 0)
    def _():
        s_ref[...] = s0_ref[...]
        for i, cb in enumerate((cbq_ref, cbk_ref, cbv_ref)):
            ext_ref[i, 0:SUBLANE, :] = cb[...]

    conv = []
    for i, (r_ref, cw_ref) in enumerate(((q_ref, cwq_ref), (k_ref, cwk_ref), (v_ref, cwv_ref))):
        ext_ref[i, SUBLANE:SUBLANE + c, :] = r_ref[...]
        acc = ext_ref[i, SUBLANE:SUBLANE + c, :] * cw_ref[kw - 1:kw, :]
        for j in range(kw - 1):
            acc = acc + ext_ref[i, pl.ds(SUBLANE - (kw - 1) + j, c), :] * cw_ref[j:j + 1, :]
        ext_ref[i, 0:SUBLANE, :] = ext_ref[i, c:c + SUBLANE, :]
        acc = _silu(acc)
        if valid_rows < c:
            acc = jnp.where(_iota(acc.shape, 0) < valid_rows, acc, 0.0)
        conv.append(acc)
    xq, xk, xv = conv

    gt = gt_ref[...]
    rr, ss = _iota((c, c), 0), _iota((c, c), 1)
    incl, strict = rr >= ss, rr > ss
    gcum = _dot(incl.astype(F32), gt, HI)
    lane = _iota(gt.shape, 1)
    gn = gn_ref[...]
    heads = list(range(hpb))
    sls = [slice(j * hd, (j + 1) * hd) for j in heads]
    beta = [jnp.sum(jnp.where(lane == hblk * hpb + j, gt, 0.0), axis=1, keepdims=True) for j in heads]
    gc = [jnp.sum(jnp.where(lane == ha + hblk * hpb + j, gcum, 0.0), axis=1, keepdims=True) for j in heads]
    qh = [xq[:, sl] for sl in sls]
    qh = _each(lambda a: a * lax.rsqrt(jnp.sum(a * a, axis=-1, keepdims=True) + L2_EPS) * (hd ** -0.5), qh)
    kh = [xk[:, sl] for sl in sls]
    kh = _each(lambda a: a * lax.rsqrt(jnp.sum(a * a, axis=-1, keepdims=True) + L2_EPS), kh)
    decay = _each(lambda g: jnp.where(incl, jnp.exp(g - _col_to_row(g)), 0.0), gc)
    kb = _each(lambda a, b: a * b, kh, beta)
    vb = [xv[:, sl] * b for sl, b in zip(sls, beta)]
    egc = _each(jnp.exp, gc)
    kk = _each(lambda a, q, k: _dot_nt(jnp.concatenate([a, q], axis=0).astype(BF16), k.astype(BF16)),
               kb, qh, kh)
    low = _each(lambda a, d: jnp.where(strict, a[:c] * d, 0.0), kk, decay)
    attn = _each(lambda a, d: (a[c:] * d).astype(BF16), kk, decay)
    tmat = _unit_lower_inverse(low)
    uw = _each(lambda t, v, k, e: _dot(t.astype(BF16), jnp.concatenate([v, k * e], axis=1).astype(BF16)),
               tmat, vb, kb, egc)
    st = [s_ref[j] for j in heads]
    ws = _each(lambda a, q, e, s: _dot(jnp.concatenate([a[:, hd:], q * e], axis=0).astype(BF16),
                                       s.astype(BF16)), uw, qh, egc, st)
    v_new = _each(lambda a, b: (a[:, :hd] - b[:c]).astype(BF16), uw, ws)
    o = _each(lambda b, a, v: b[c:] + _dot(a, v), ws, attn, v_new)
    g_last = _each(lambda g: g[c - 1:c, :], gc)
    k_dec = _each(lambda k, gl, g: (k * jnp.exp(gl - g)).astype(BF16), kh, g_last, gc)
    s_new = _each(lambda s, gl, k, v: s * jnp.exp(gl) + _dot_tn(k, v), st, g_last, k_dec, v_new)
    for j in heads:
        s_ref[j] = s_new[j]
        o_ref[:, sls[j]] = (_rms(o[j], gn) * _silu(z_ref[:, sls[j]])).astype(o_ref.dtype)

    @pl.when(c_idx == nc - 1)
    def _():
        sfin_ref[...] = s_ref[...]


def _delta(proj, gates, conv_w, conv_buf8, s0, gnorm, nb, tp, valid_rows, ha, hpb=16):
    c = DELTA_CHUNK
    hd = LANE
    hpb = min(hpb, ha)
    w = hpb * hd
    nc = tp // c
    nhb = ha // hpb
    kw = conv_w.shape[0]
    cw8 = jnp.pad(conv_w, ((0, SUBLANE - kw), (0, 0)))

    def col(off):
        return pl.BlockSpec((c, w), lambda b, h, i, off=off: (b * nc + i, off * nhb + h))

    def cwspec(off):
        return pl.BlockSpec((SUBLANE, w), lambda b, h, i, off=off: (0, off * nhb + h))

    def cbspec(off):
        return pl.BlockSpec((None, SUBLANE, w), lambda b, h, i, off=off: (b, 0, off * nhb + h))

    state_spec = pl.BlockSpec((None, hpb, hd, hd), lambda b, h, i: (b, h, 0, 0))
    return pl.pallas_call(
        functools.partial(_delta_kernel, hpb=hpb, ha=ha, kw=kw, valid_rows=valid_rows, nc=nc),
        grid=(nb, nhb, nc),
        in_specs=[col(0), col(1), col(2), col(3),
                  pl.BlockSpec((c, LANE), lambda b, h, i: (b * nc + i, 0)),
                  cwspec(0), cwspec(1), cwspec(2), cbspec(0), cbspec(1), cbspec(2),
                  state_spec, pl.BlockSpec((1, hd), lambda b, h, i: (0, 0))],
        out_specs=[pl.BlockSpec((c, w), lambda b, h, i: (b * nc + i, h)), state_spec],
        out_shape=[jax.ShapeDtypeStruct((nb * tp, ha * hd), BF16),
                   jax.ShapeDtypeStruct(s0.shape, F32)],
        scratch_shapes=[pltpu.VMEM((3, c + SUBLANE, w), F32), pltpu.VMEM((hpb, hd, hd), F32)],
        compiler_params=_cp("arbitrary", "arbitrary", "arbitrary"), name="delta",
    )(proj, proj, proj, proj, gates, cw8, cw8, cw8, conv_buf8, conv_buf8, conv_buf8, s0,
      gnorm.reshape(1, hd))


def _softmax_step(carry, s, v, valid=None):
    m, l, acc = carry
    m_new = jnp.maximum(m, jnp.max(s, axis=-1, keepdims=True))
    alpha = jnp.exp(m - m_new)
    p = jnp.exp(s - m_new)
    if valid is not None:
        p = jnp.where(valid, p, 0.0)
    l = alpha * l + jnp.sum(p, axis=-1, keepdims=True)
    acc = alpha * acc + _dot(p.astype(BF16), v)
    return m_new, l, acc


def _softmax_init(rows, hd):
    return (jnp.full((rows, 1), NEG, F32), jnp.zeros((rows, 1), F32), jnp.zeros((rows, hd), F32))


def _fox_prompt_kernel(q_ref, k_ref, v_ref, f_ref, o_ref, kb_ref, vb_ref, *, tq, tk, scale, par):
    seq = q_ref.shape[0]
    kb_ref[...] = k_ref[...].astype(BF16)
    vb_ref[...] = v_ref[...].astype(BF16)
    col_minus_row = _iota((tq, tk), 1) - _iota((tq, tk), 0)
    nqb = seq // tq
    for c0 in range(0, nqb, par):
        chunk = list(range(c0, min(c0 + par, nqb)))
        qs = [(q_ref[pl.ds(qb * tq, tq), :] * scale).astype(BF16) for qb in chunk]
        fqs = [_row_to_col(f_ref[:, pl.ds(qb * tq, tq)]) for qb in chunk]
        carries = [_softmax_init(tq, LANE) for _ in chunk]
        n_steps = [-(-((qb + 1) * tq) // tk) for qb in chunk]
        for step in range(max(n_steps)):
            live = [i for i in range(len(chunk)) if step < n_steps[i]]
            k = kb_ref[pl.ds(step * tk, tk), :]
            v = vb_ref[pl.ds(step * tk, tk), :]
            fk = f_ref[:, pl.ds(step * tk, tk)]
            ss = [_dot_nt(qs[i], k) + (fqs[i] - fk) for i in live]
            for n, i in enumerate(live):
                if step * tk + tk - 1 > chunk[i] * tq:
                    ss[n] = jnp.where(col_minus_row <= chunk[i] * tq - step * tk, ss[n], NEG)
            ms = [jnp.maximum(carries[i][0], jnp.max(s, axis=-1, keepdims=True)) for i, s in zip(live, ss)]
            ps = [jnp.exp(s - m) for s, m in zip(ss, ms)]
            alphas = [jnp.exp(carries[i][0] - m) for i, m in zip(live, ms)]
            pvs = [_dot(p.astype(BF16), v) for p in ps]
            for i, m, p, a, pv in zip(live, ms, ps, alphas, pvs):
                carries[i] = (m, a * carries[i][1] + jnp.sum(p, axis=-1, keepdims=True),
                              a * carries[i][2] + pv)
        for qb, (_, l, acc) in zip(chunk, carries):
            o_ref[pl.ds(qb * tq, tq), :] = (acc / l).astype(o_ref.dtype)


def _fox_prompt(q_arr, q_col, k_arr, v_arr, f_rows, nb, s, nh, tq=128, tk=512, par=8):
    tq, tk = min(tq, s), min(tk, s)
    assert s % tq == 0 and s % tk == 0
    hd = LANE
    return pl.pallas_call(
        functools.partial(_fox_prompt_kernel, tq=tq, tk=tk, scale=hd ** -0.5, par=par),
        grid=(nb, nh),
        in_specs=[pl.BlockSpec((s, hd), lambda b, h: (b, q_col + h)),
                  pl.BlockSpec((s, hd), lambda b, h: (b, h)),
                  pl.BlockSpec((s, hd), lambda b, h: (b, h)),
                  pl.BlockSpec((None, None, 1, s), lambda b, h: (b, h, 0, 0))],
        out_specs=pl.BlockSpec((s, hd), lambda b, h: (b, h)),
        out_shape=jax.ShapeDtypeStruct((nb * s, nh * hd), BF16),
        scratch_shapes=[pltpu.VMEM((s, hd), BF16), pltpu.VMEM((s, hd), BF16)],
        compiler_params=_cp("arbitrary", "arbitrary"), name="fox_prompt",
    )(q_arr, k_arr, v_arr, f_rows)


def _fpast_kernel(pt_ref, *refs, group):
    lp_refs, o_ref, carry_ref = refs[:group], refs[group], refs[group + 1]

    @pl.when(pl.program_id(1) == 0)
    def _():
        carry_ref[...] = jnp.zeros_like(carry_ref)

    p = lp_refs[0].shape[0]
    upper = (_iota((p, p), 1) > _iota((p, p), 0)).astype(F32)
    for i in range(group):
        lp = lp_refs[i][...]
        o_ref[group - 1 - i] = -(_dot(upper, lp, HI) + carry_ref[0:1, :])
        carry_ref[0:1, :] = carry_ref[0:1, :] + jnp.sum(lp, axis=0, keepdims=True)


def _fpast(logf_pool, page_table):
    db, n_pages = page_table.shape
    _, p, nh = logf_pool.shape
    group = max(k for k in (16, 8, 4, 2, 1) if n_pages % k == 0)
    nblk = n_pages // group

    def lp_spec(i):
        return pl.BlockSpec((None, p, nh),
                            lambda b, j, pt, i=i: (pt[b, n_pages - 1 - (j * group + i)], 0, 0))

    return pl.pallas_call(
        functools.partial(_fpast_kernel, group=group),
        grid_spec=pltpu.PrefetchScalarGridSpec(
            num_scalar_prefetch=1, grid=(db, nblk),
            in_specs=[lp_spec(i) for i in range(group)],
            out_specs=pl.BlockSpec((None, group, p, nh), lambda b, j, pt: (b, nblk - 1 - j, 0, 0)),
            scratch_shapes=[pltpu.VMEM((SUBLANE, nh), F32)]),
        out_shape=jax.ShapeDtypeStruct((db, n_pages, p, nh), F32),
        compiler_params=_cp("arbitrary", "arbitrary"), name="fpast",
    )(page_table, *([logf_pool] * group))


def _fox_sample_kernel(pt_ref, *refs, nh, scale, n_steps, pp):
    q_ref, k_refs, v_refs = refs[0], refs[1:1 + pp], refs[1 + pp:1 + 2 * pp]
    (fp_ref, kn_ref, vn_ref, fn_ref, o_ref, m_ref, l_ref, acc_ref, bias_ref, s0_ref,
     s1_ref) = refs[1 + 2 * pp:]
    p = pl.program_id(1)
    r = q_ref.shape[0]
    rows = k_refs[0].shape[0]
    lognh = _log2(nh)
    fn = fn_ref[...]

    @pl.when(p == 0)
    def _():
        m_ref[...] = jnp.full_like(m_ref, NEG)
        l_ref[...] = jnp.zeros_like(l_ref)
        acc_ref[...] = jnp.zeros_like(acc_ref)
        s1_ref[...] = jnp.zeros_like(s1_ref)
        match = (_iota((r, rows), 0) & (nh - 1)) == (_iota((r, rows), 1) & (nh - 1))
        bias_ref[...] = jnp.where(match, _row_to_col(fn), NEG)

    q = (q_ref[...] * scale).astype(BF16)

    def update(s, v):
        m, l, acc = _softmax_step((m_ref[...], l_ref[...], acc_ref[...]), s, v)
        m_ref[...], l_ref[...], acc_ref[...] = m, l, acc

    def step(s_w, s_r):
        for i in range(pp):
            s_w[:, i * rows:(i + 1) * rows] = (_dot_nt(q, k_refs[i][...].astype(BF16))
                                               + (bias_ref[...] - fp_ref[i]))
        live = p > 0
        s = s_r[...]
        m_old = m_ref[...]
        m_new = jnp.where(live, jnp.maximum(m_old, jnp.max(s, axis=-1, keepdims=True)), m_old)
        pr = jnp.where(live, jnp.exp(s - m_new), 0.0)
        alpha = jnp.exp(m_old - m_new)
        l_ref[...] = alpha * l_ref[...] + jnp.sum(pr, axis=-1, keepdims=True)
        pr = pr.astype(BF16)
        pv = _dot(pr[:, :rows], v_refs[0][...].astype(BF16))
        for i in range(1, pp):
            pv = pv + _dot(pr[:, i * rows:(i + 1) * rows], v_refs[i][...].astype(BF16))
        acc_ref[...] = alpha * acc_ref[...] + pv
        m_ref[...] = m_new

    @pl.when(p % 2 == 0)
    def _():
        step(s0_ref, s1_ref)

    @pl.when(p % 2 == 1)
    def _():
        step(s1_ref, s0_ref)

    @pl.when(p == n_steps)
    def _():
        rr, cc = _iota((r, r), 0), _iota((r, r), 1)
        ok = ((rr & (nh - 1)) == (cc & (nh - 1))) & ((cc >> lognh) <= (rr >> lognh))
        s2 = _dot_nt(q, kn_ref[...].astype(BF16)) + _row_to_col(fn) - fn
        update(jnp.where(ok, s2, NEG), vn_ref[...].astype(BF16))
        o_ref[...] = (acc_ref[...] / l_ref[...]).astype(o_ref.dtype)


def _fox_sample(q, k_new, v_new, f_new, k_pool, v_pool, f_past, page_table, nh):
    db, r, hd = q.shape
    n_pages = page_table.shape[1]
    rows = k_pool.shape[1]
    tok = pl.BlockSpec((None, r, hd), lambda b, p, pt: (b, 0, 0))
    pp = max(k for k in (4, 2, 1) if n_pages % k == 0)
    n_steps = n_pages // pp
    last = n_steps - 1
    k_pages = [pl.BlockSpec((None, rows, hd),
                            lambda b, p, pt, i=i: (pt[b, jnp.minimum(p, last) * pp + i], 0, 0))
               for i in range(pp)]
    v_pages = [pl.BlockSpec((None, rows, hd),
                            lambda b, p, pt, i=i: (pt[b, jnp.maximum(p - 1, 0) * pp + i], 0, 0))
               for i in range(pp)]
    return pl.pallas_call(
        functools.partial(_fox_sample_kernel, nh=nh, scale=hd ** -0.5, n_steps=n_steps, pp=pp),
        grid_spec=pltpu.PrefetchScalarGridSpec(
            num_scalar_prefetch=1, grid=(db, n_steps + 1),
            in_specs=[tok] + k_pages + v_pages + [
                pl.BlockSpec((None, pp, 1, rows), lambda b, p, pt: (b, jnp.minimum(p, last), 0, 0)),
                tok, tok, pl.BlockSpec((None, 1, r), lambda b, p, pt: (b, 0, 0))],
            out_specs=tok,
            scratch_shapes=[pltpu.VMEM((r, 1), F32), pltpu.VMEM((r, 1), F32),
                            pltpu.VMEM((r, hd), F32), pltpu.VMEM((r, rows), F32),
                            pltpu.VMEM((r, pp * rows), F32), pltpu.VMEM((r, pp * rows), F32)]),
        out_shape=jax.ShapeDtypeStruct((db, r, hd), BF16),
        compiler_params=_cp("arbitrary", "arbitrary"), name="fox_sample",
    )(page_table, q, *([k_pool] * pp), *([v_pool] * pp), f_past, k_new, v_new, f_new)


def _dil_prompt_kernel(*refs, groups, tb, scale, par):
    ng = len(groups)
    q_refs, k_refs, v_refs = refs[:ng], refs[ng:2 * ng], refs[2 * ng:3 * ng]
    o_ref, m_ref, l_ref, acc_ref = refs[3 * ng:]
    seq = q_refs[0].shape[0]
    diff = _iota((tb, tb), 0) - _iota((tb, tb), 1)

    def rows_of(res, blk, dil):
        return pl.ds(res + blk * tb * dil, tb, stride=dil) if dil > 1 else pl.ds(blk * tb, tb)

    for g, (window, dil) in enumerate(groups):
        reach = window // dil
        nqb = seq // dil // tb
        back = -(-reach // tb)
        units = [(res, qb) for res in range(dil) for qb in range(nqb)]
        for c0 in range(0, len(units), par):
            chunk = units[c0:c0 + par]
            rows = [rows_of(res, qb, dil) for res, qb in chunk]
            qs = [(q_refs[g][rw, :] * scale).astype(BF16) for rw in rows]
            if g == 0:
                carries = [_softmax_init(tb, LANE) for _ in chunk]
            else:
                carries = [(m_ref[rw, :], l_ref[rw, :], acc_ref[rw, :]) for rw in rows]
            for step in range(back + 1):
                live = [i for i, (_, qb) in enumerate(chunk) if qb - step >= 0]
                if not live:
                    continue
                krows = [rows_of(chunk[i][0], chunk[i][1] - step, dil) for i in live]
                ks = [k_refs[g][rw, :].astype(BF16) for rw in krows]
                vs = [v_refs[g][rw, :].astype(BF16) for rw in krows]
                ss = [_dot_nt(qs[i], k) for i, k in zip(live, ks)]
                if step * tb - (tb - 1) < 0 or step * tb + (tb - 1) > reach:
                    dist = diff + step * tb
                    valid = (dist >= 0) & (dist <= reach)
                    ss = [jnp.where(valid, s, NEG) for s in ss]
                ms = [jnp.maximum(carries[i][0], jnp.max(s, axis=-1, keepdims=True)) for i, s in zip(live, ss)]
                ps = [jnp.exp(s - m) for s, m in zip(ss, ms)]
                alphas = [jnp.exp(carries[i][0] - m) for i, m in zip(live, ms)]
                pvs = [_dot(p.astype(BF16), v) for p, v in zip(ps, vs)]
                for i, m, p, a, pv in zip(live, ms, ps, alphas, pvs):
                    carries[i] = (m, a * carries[i][1] + jnp.sum(p, axis=-1, keepdims=True),
                                  a * carries[i][2] + pv)
            for rw, (m, l, acc) in zip(rows, carries):
                m_ref[rw, :], l_ref[rw, :], acc_ref[rw, :] = m, l, acc
    o_ref[...] = (acc_ref[...] / l_ref[...]).astype(o_ref.dtype)


def _dil_prompt(srcs, nb, s, nh, groups, tb=128, par=8):
    hd = LANE
    ng = len(groups)
    tb = min(tb, s)
    for window, dil in groups:
        assert s % (dil * tb) == 0 and window % dil == 0
    args, specs = [], []
    for which in range(3):
        for g in range(ng):
            arr, col = srcs[g][which]
            args.append(arr)
            specs.append(pl.BlockSpec((s, hd), lambda b, h, col=col: (b, col + h)))
    return pl.pallas_call(
        functools.partial(_dil_prompt_kernel, groups=groups, tb=tb, scale=hd ** -0.5, par=par),
        grid=(nb, nh),
        in_specs=specs,
        out_specs=pl.BlockSpec((s, hd), lambda b, h: (b, h)),
        out_shape=jax.ShapeDtypeStruct((nb * s, nh * hd), BF16),
        scratch_shapes=[pltpu.VMEM((s, 1), F32), pltpu.VMEM((s, 1), F32), pltpu.VMEM((s, hd), F32)],
        compiler_params=_cp("arbitrary", "arbitrary"), name="dil_prompt",
    )(*args)


def _dil_sample_kernel(*refs, groups, wbs, tb, nh, scale, plan):
    ng = len(groups)
    q_refs, kb_refs, vb_refs = refs[:ng], refs[ng:2 * ng], refs[2 * ng:3 * ng]
    kn_refs, vn_refs = refs[3 * ng:4 * ng], refs[4 * ng:5 * ng]
    o_ref, m_ref, l_ref, acc_ref = refs[5 * ng:5 * ng + 4]
    j = pl.program_id(1)
    r = q_refs[0].shape[0]
    rows = tb * nh
    lognh = _log2(nh)
    total = sum(steps for _, steps, _ in plan)

    @pl.when(j == 0)
    def _():
        m_ref[...] = jnp.full_like(m_ref, NEG)
        l_ref[...] = jnp.zeros_like(l_ref)
        acc_ref[...] = jnp.zeros_like(acc_ref)

    def update(s, v, valid):
        m, l, acc = _softmax_step((m_ref[...], l_ref[...], acc_ref[...]),
                                  jnp.where(valid, s, NEG), v, valid)
        m_ref[...], l_ref[...], acc_ref[...] = m, l, acc

    for g, (window, dil) in enumerate(groups):
        start, steps, nmb = plan[g]

        def block(g=g, window=window, dil=dil, start=start, nmb=nmb):
            rr, cc = _iota((r, rows), 0), _iota((r, rows), 1)
            res = (wbs[g] + (j - start) // nmb) % dil
            key_tok = (((j - start) % nmb) * tb + (cc >> lognh)) * dil + res
            delta = wbs[g] + (rr >> lognh) - key_tok
            valid = (((rr & (nh - 1)) == (cc & (nh - 1))) & ((delta & (dil - 1)) == 0)
                     & (delta <= window))
            q = (q_refs[g][...] * scale).astype(BF16)
            k = kb_refs[g][...].reshape(rows, LANE).astype(BF16)
            v = vb_refs[g][...].reshape(rows, LANE).astype(BF16)
            update(_dot_nt(q, k), v, valid)

        pl.when((j >= start) & (j < start + steps))(block)

    @pl.when(j == total - 1)
    def _():
        rr, cc = _iota((r, r), 0), _iota((r, r), 1)
        delta = (rr >> lognh) - (cc >> lognh)
        match = (rr & (nh - 1)) == (cc & (nh - 1))
        for g, (window, dil) in enumerate(groups):
            valid = match & (delta >= 0) & ((delta & (dil - 1)) == 0) & (delta <= window)
            q = (q_refs[g][...] * scale).astype(BF16)
            update(_dot_nt(q, kn_refs[g][...].astype(BF16)), vn_refs[g][...].astype(BF16), valid)
        o_ref[...] = (acc_ref[...] / l_ref[...]).astype(o_ref.dtype)


def _dil_sample(qs, kns, vns, kbufs, vbufs, groups, nh, t):
    db, r, hd = qs[0].shape
    ng = len(groups)
    wbs = [kb.shape[1] for kb in kbufs]
    assert all(wb % dil == 0 for wb, (_, dil) in zip(wbs, groups))
    per_res = [wb // dil for wb, (_, dil) in zip(wbs, groups)]
    tb = 128
    while any(n % tb for n in per_res):
        tb //= 2
    plan, start = [], 0
    for n, (_, dil) in zip(per_res, groups):
        steps = min(dil, t) * (n // tb)
        plan.append((start, steps, n // tb))
        start += steps
    total = start
    tok = pl.BlockSpec((None, r, hd), lambda b, j: (b, 0, 0))

    def buf_spec(g):
        first, steps, nmb = plan[g]
        dil, wb = groups[g][1], wbs[g]

        def index(b, j):
            sg = jnp.clip(j - first, 0, steps - 1)
            return b, sg % nmb, (wb + sg // nmb) % dil, 0, 0

        return pl.BlockSpec((None, tb, None, nh, hd), index)

    split = lambda a, g: a.reshape(db, per_res[g], groups[g][1], nh, hd)
    kbufs = [split(a, g) for g, a in enumerate(kbufs)]
    vbufs = [split(a, g) for g, a in enumerate(vbufs)]
    bspecs = [buf_spec(g) for g in range(ng)]
    return pl.pallas_call(
        functools.partial(_dil_sample_kernel, groups=groups, wbs=wbs, tb=tb, nh=nh,
                          scale=hd ** -0.5, plan=plan),
        grid=(db, total),
        in_specs=[tok] * ng + bspecs + bspecs + [tok] * (2 * ng),
        out_specs=tok,
        out_shape=jax.ShapeDtypeStruct((db, r, hd), BF16),
        scratch_shapes=[pltpu.VMEM((r, 1), F32), pltpu.VMEM((r, 1), F32), pltpu.VMEM((r, hd), F32)],
        compiler_params=_cp("arbitrary", "arbitrary"), name="dil_sample",
    )(*qs, *kbufs, *vbufs, *kns, *vns)


def _ffn_up_kernel(*refs, kw, tiles_per_batch, t_s, nm):
    nh = kw - 1
    hp_ref, hs_ref, wu_ref, wg_ref, cw_ref = refs[:5]
    halo_refs = refs[5:5 + nh]
    actp_ref, tailp_ref, acts_ref, gates_ref, ext_ref, wub_ref, wgb_ref = refs[5 + nh:]
    m = pl.program_id(1)
    tn = wub_ref.shape[1]

    @pl.when(m == 0)
    def _():
        wub_ref[...] = wu_ref[...].astype(BF16)
        wgb_ref[...] = wg_ref[...].astype(BF16)

    @pl.when((m == 0) & (pl.program_id(0) == 0))
    def _():
        ext_ref[...] = jnp.zeros_like(ext_ref)

    def conv_gelu(gate, up, fix):
        rows = gate.shape[0]
        ext_ref[SUBLANE:SUBLANE + rows, :] = gate
        gc = gate * cw_ref[kw - 1:kw, :]
        for j in range(kw - 1):
            shift = kw - 1 - j
            gc = gc + fix(ext_ref[pl.ds(SUBLANE - shift, rows), :], shift) * cw_ref[j:j + 1, :]
        return _gelu_tanh(gc) * up

    @pl.when(m < nm)
    def _():
        h = hp_ref[...]
        tm = h.shape[0]
        up = _dot(h, wub_ref[...])
        gate = _dot(h, wgb_ref[...])
        prev = jnp.where((m % tiles_per_batch) == 0, 0.0, ext_ref[tm:tm + SUBLANE, :])
        ext_ref[0:SUBLANE, :] = prev
        tailp_ref[...] = gate[tm - SUBLANE:, :]
        actp_ref[...] = conv_gelu(gate, up, lambda sh, shift: sh).astype(actp_ref.dtype)

    @pl.when(m == nm)
    def _():
        h = hs_ref[...]
        up = _dot(h, wub_ref[...])
        gate = _dot(h, wgb_ref[...])
        ext_ref[0:SUBLANE, :] = jnp.zeros((SUBLANE, tn), F32)
        gates_ref[...] = gate
        pos = _iota(gate.shape, 0) & (t_s - 1)
        fix = lambda sh, shift: jnp.where(pos < shift, halo_refs[shift - 1][...], sh)
        acts_ref[...] = conv_gelu(gate, up, fix).astype(acts_ref.dtype)


def _ffn_up(h_p, h_s, wu, wg, layer, cw8, kw, t_p, t_s, halos, tm=512, tn=FFN_TILE):
    m, d = h_p.shape
    ms = h_s.shape[0]
    n_pad = wu.shape[-1]
    tm = min(tm, t_p)
    tn = min(tn, -(-n_pad // LANE) * LANE)
    nm, nn = m // tm, -(-n_pad // tn)
    _log2(t_s)
    last = nm - 1
    s_spec = pl.BlockSpec((ms, tn), lambda j, i: (0, j))
    in_specs = [pl.BlockSpec((tm, d), lambda j, i: (jnp.minimum(i, last), 0)),
                pl.BlockSpec((ms, d), lambda j, i: (0, 0)),
                pl.BlockSpec((None, d, tn), lambda j, i: (layer, 0, j)),
                pl.BlockSpec((None, d, tn), lambda j, i: (layer, 0, j)),
                pl.BlockSpec((SUBLANE, tn), lambda j, i: (0, j))] + [s_spec] * len(halos)
    return pl.pallas_call(
        functools.partial(_ffn_up_kernel, kw=kw, tiles_per_batch=max(t_p // tm, 1), t_s=t_s, nm=nm),
        grid=(nn, nm + 1), in_specs=in_specs,
        out_specs=[pl.BlockSpec((tm, tn), lambda j, i: (jnp.minimum(i, last), j)),
                   pl.BlockSpec((None, SUBLANE, tn), lambda j, i: (jnp.minimum(i, last), 0, j)),
                   s_spec, s_spec],
        out_shape=[jax.ShapeDtypeStruct((m, n_pad), BF16),
                   jax.ShapeDtypeStruct((nm, SUBLANE, n_pad), F32),
                   jax.ShapeDtypeStruct((ms, n_pad), BF16), jax.ShapeDtypeStruct((ms, n_pad), F32)],
        scratch_shapes=[pltpu.VMEM((max(tm, ms) + SUBLANE, tn), F32), pltpu.VMEM((d, tn), BF16),
                        pltpu.VMEM((d, tn), BF16)],
        compiler_params=_cp("arbitrary", "arbitrary"), name="ffn_up",
    )(h_p, h_s, wu, wg, cw8, *halos)


def _even_weights(w_in, conv_w, a_log, dt_bias, b_f, w_out, ha, hb, hd):
    wa, wb = ha * hd, hb * hd
    o2 = 4 * wa
    o4 = o2 + 2 * ha
    o5 = o4 + 3 * wb
    w_t = jnp.swapaxes(w_in, 0, 1)
    w_main = (w_t, w_t[o4:o5])
    small = jnp.swapaxes(jnp.concatenate([w_t[o2:o4], w_t[o5:]], axis=0), 0, 1)
    assert small.shape[1] <= LANE
    w_small = jnp.pad(small, ((0, 0), (0, LANE - small.shape[1])))
    par = jnp.zeros((SUBLANE, LANE), F32)
    par = par.at[0, ha:2 * ha].set(a_log.astype(F32))
    par = par.at[1, ha:2 * ha].set(dt_bias.astype(F32))
    par = par.at[1, 2 * ha:2 * ha + hb].set(b_f.astype(F32))
    return w_main, w_small, par, conv_w, w_out


def _even_mixer(hs, nbs, ts, wts, gnorm, conv_bufs, s0s, ha, hb, hd, past):
    (w_in, w_fox), _, _, _, w_out = wts
    wa, wb = ha * hd, hb * hd
    projs = _matmul_w32([hs[0]], [w_in], F32, out_cols=[4 * wa], tm=1024, w_buffers=1, a2_list=[hs[1]],
                        w_rows=True)
    foxes = _matmul_w32([hs[0]], [w_fox], F32, out_cols=[wb, wb, wb], tm=512, w_buffers=1,
                        a2_list=[hs[1]], w_rows=True)
    cores = [_even_core(hs[g], nbs[g], ts[g], projs[g], foxes[g], wts, gnorm, conv_bufs[g], s0s[g], ha, hb,
                        hd, past if g == 1 else None) for g in (0, 1)]
    outs = _matmul_w32([cores[0][0], cores[0][1]], [w_out, w_out], BF16, tm=512,
                       a2_list=[cores[1][0], cores[1][1]])
    return [(outs[g],) + tuple(cores[g][2:]) for g in (0, 1)]


def _even_core(h, nb, t, proj, fox, wts, gnorm, conv_buf, s0, ha, hb, hd, past):
    _, w_small, par, conv_w, _ = wts
    wa, wb = ha * hd, hb * hd
    kw = conv_w.shape[0]
    q_fox, k_new, v_new = fox
    gates, fcum = _gates(h, w_small, par, nb, t, ha, hb)

    c = DELTA_CHUNK
    tp = -(-t // c) * c
    conv_buf8 = jnp.pad(conv_buf.astype(F32), ((0, 0), (SUBLANE - (kw - 1), 0), (0, 0)))
    if tp == t:
        proj_d, gates_d = proj, gates
    else:
        pad = lambda a: jnp.pad(a.reshape(nb, t, -1), ((0, 0), (0, tp - t), (0, 0))).reshape(nb * tp, -1)
        proj_d, gates_d = pad(proj), pad(gates)
    o_a, s_new = _delta(proj_d, gates_d, conv_w, conv_buf8, s0, gnorm, nb, tp, min(t, c) if tp != t else c,
                        ha)
    if tp != t:
        o_a = o_a.reshape(nb, tp, wa)[:, :t].reshape(nb * t, wa)

    logf = gates[:, 2 * ha:2 * ha + hb].reshape(nb, t, hb)
    f_new = fcum[:, 2 * ha:2 * ha + hb].reshape(nb, t, hb)
    kb, vb = k_new.reshape(nb, t, hb, hd), v_new.reshape(nb, t, hb, hd)
    if past is None:
        f_rows = jnp.swapaxes(f_new, 1, 2).reshape(nb, hb, 1, t)
        o_b = _fox_prompt(q_fox, 0, k_new, v_new, f_rows, nb, t, hb)
    else:
        k_pool, v_pool, logf_pool, page_table = past
        n_pool, page = k_pool.shape[0], k_pool.shape[1]
        f_past = _fpast(logf_pool.astype(F32), page_table)
        f_past = f_past.reshape(nb, page_table.shape[1], 1, page * hb)
        o_b = _fox_sample(q_fox.reshape(nb, t * hb, hd), k_new.reshape(nb, t * hb, hd),
                          v_new.reshape(nb, t * hb, hd), f_new.reshape(nb, 1, t * hb),
                          k_pool.reshape(n_pool, page * hb, hd), v_pool.reshape(n_pool, page * hb, hd),
                          f_past, page_table, hb)
        o_b = o_b.reshape(nb * t, wb)
    keep = min(kw - 1, t)
    raw_tail = proj.reshape(nb, t, -1)[:, t - keep:, :3 * wa]
    new_conv = jnp.concatenate([conv_buf.astype(F32), raw_tail], axis=1)[:, -(kw - 1):]
    return o_a, o_b, s_new, new_conv, kb, vb, logf


def _odd_mixer(hs, nbs, ts, w_in, w_out, nh, hd, bufs):
    ng = len(C_GROUPS)
    wc = nh * hd
    n_main = (3 * ng - 2) * wc
    main, main_s = _matmul_w32([hs[0]], [w_in], F32, out_cols=[n_main], tm=1024, w_buffers=1,
                               a2_list=[hs[1]])
    (k_last, v_last), kv_s = _matmul_w32([hs[0]], [w_in], F32, out_cols=[wc, wc], col0=n_main, tm=512,
                                         w_buffers=1, a2_list=[hs[1]])
    nb, t = nbs[0], ts[0]
    srcs = [((main, g * 3 * nh), (main, g * 3 * nh + nh), (main, g * 3 * nh + 2 * nh))
            for g in range(ng - 1)]
    srcs.append(((main, (ng - 1) * 3 * nh), (k_last, 0), (v_last, 0)))
    o_p = _dil_prompt(srcs, nb, t, nh, C_GROUPS)
    m3 = main.reshape(nb, t, -1)
    bufs_p = []
    for g, (window, _) in enumerate(C_GROUPS):
        keep = min(window, t)
        if g < ng - 1:
            kv = [m3[:, t - keep:, (g * 3 + r) * wc:(g * 3 + r + 1) * wc] for r in (1, 2)]
        else:
            kv = [a.reshape(nb, t, wc)[:, t - keep:] for a in (k_last, v_last)]
        bufs_p.append(tuple(a.reshape(nb, keep, nh, hd) for a in kv))
    o_s, bufs_s = _odd_sample_core(jnp.concatenate([main_s] + list(kv_s), axis=1), nbs[1], ts[1], nh, hd, bufs)
    out_p, out_s = _matmul_w32([o_p], [w_out], BF16, tm=512, a2_list=[o_s])
    return (out_p, bufs_p), (out_s, bufs_s)


def _odd_sample_core(proj, nb, t, nh, hd, bufs):
    ng = len(C_GROUPS)
    new_bufs = []
    p6 = proj.reshape(nb, t, ng, 3, nh, hd)
    flat = lambda a: a.reshape(nb, -1, hd)
    qs = [flat(p6[:, :, g, 0]) for g in range(ng)]
    kns = [flat(p6[:, :, g, 1]) for g in range(ng)]
    vns = [flat(p6[:, :, g, 2]) for g in range(ng)]
    o = _dil_sample(qs, kns, vns, [bk for bk, _ in bufs], [bv for _, bv in bufs], C_GROUPS, nh, t)
    o = o.reshape(nb * t, nh * hd)
    for g in range(ng):
        bk, bv = bufs[g]
        wb = bk.shape[1]
        new_bufs.append((jnp.concatenate([bk, p6[:, :, g, 1]], axis=1)[:, -wb:],
                         jnp.concatenate([bv, p6[:, :, g, 2]], axis=1)[:, -wb:]))
    return o, new_bufs


def _ffn_weights(w_up, w_gate, conv_w, w_down):
    d_ff = w_up.shape[-1]
    kw = conv_w.shape[1]
    cw8 = jnp.pad(conv_w.astype(F32), ((0, 0), (0, SUBLANE - kw), (0, 0)))
    return w_up, w_gate, cw8, w_down.astype(BF16), kw, d_ff


def _conv_ffn(hs, nbs, ts, wts, layer, buf_s):
    wu, wg, cw8, wd, kw, d_ff = wts
    (nb, nbs_), (t, t_s) = nbs, ts
    halos = []
    for shift in range(1, kw):
        hl = jnp.zeros((nbs_, t_s, d_ff), F32)
        hl = hl.at[:, :shift].set(buf_s.astype(F32)[:, (kw - 1) - shift:])
        halos.append(hl.reshape(nbs_ * t_s, d_ff))
    act_p, tail, act_s, gate_s = _ffn_up(hs[0], hs[1], wu, wg, layer, cw8[layer], kw, t, t_s, halos)
    tiles = tail.shape[0] // nb
    gate_tail = tail.reshape(nb, tiles, SUBLANE, d_ff)[:, -1]
    new_buf_p = gate_tail[:, SUBLANE - (kw - 1):]
    new_buf_s = jnp.concatenate([buf_s.astype(F32), gate_s.reshape(nbs_, t_s, d_ff)], axis=1)[:, -(kw - 1):]
    out_p = _matmul([act_p], [wd], BF16, tm=512, tn=512, layer=layer)
    out_s = _matmul([act_s], [wd], BF16, tm=act_s.shape[0], tn=512, layer=layer)
    return (out_p, new_buf_p), (out_s, new_buf_s)


def kernel(x_prompt, x_sample, c_prompt, c_sample, state_delta, state_conv_qkv, cache_k, cache_v, cache_logf, cache_win_k0, cache_win_v0, cache_win_k1, cache_win_v1, cache_win_k2, cache_win_v2, state_ffn_conv, page_table, w_mod, b_mod, norm_pre_mix, norm_post_mix, norm_pre_ffn, norm_post_ffn, w_in_e, conv_a, a_log, dt_bias, gnorm_a, b_forget, w_out_e, w_in_o, w_out_o, w_up, w_gate, conv_ffn_w, w_down):
    bsz, seq, d = x_prompt.shape
    dbsz, dseq, _ = x_sample.shape
    depth = w_mod.shape[0]
    hd = gnorm_a.shape[-1]
    ha, hb, hc = a_log.shape[-1], b_forget.shape[-1], cache_win_k0.shape[3]
    assert hd == LANE and seq % DELTA_CHUNK == 0
    kw_a = conv_a.shape[1]
    win_caches = ((cache_win_k0, cache_win_v0), (cache_win_k1, cache_win_v1), (cache_win_k2, cache_win_v2))

    c_all = jnp.concatenate([c_prompt, c_sample], axis=0)
    c_all = jnp.pad(c_all, ((0, MOD_ROWS - c_all.shape[0]), (0, 0)))
    mod = _mod(c_all, w_mod, b_mod)
    mods = [(mod[l, :bsz].reshape(bsz, 6, d), mod[l, bsz:bsz + dbsz].reshape(dbsz, 6, d))
            for l in range(depth)]

    groups = [dict(x=x_prompt.reshape(bsz * seq, d), nb=bsz, t=seq, gi=0),
              dict(x=x_sample.reshape(dbsz * dseq, d), nb=dbsz, t=dseq, gi=1)]
    for gr in groups:
        m_0 = mods[0][gr["gi"]]
        _, gr["h"] = _post_pre(gr["x"], gr["nb"], gr["t"], shift=m_0[:, 0], scale=m_0[:, 1],
                               w_pre=norm_pre_mix[0])

    fwts = _ffn_weights(w_up, w_gate, conv_ffn_w, w_down)
    outs = {k: ([], []) for k in ("delta", "conv", "k", "v", "logf", "ffn")}
    wk = [([], []) for _ in C_GROUPS]
    wv = [([], []) for _ in C_GROUPS]
    for l in range(depth):
        if l % 2 == 0:
            e = l // 2
            ewts = _even_weights(w_in_e[e], conv_a[e], a_log[e], dt_bias[e], b_forget[e], w_out_e[e],
                                 ha, hb, hd)
        else:
            o = l // 2
            w_in_o_b, w_out_o_b = w_in_o[o], w_out_o[o]
        hs, nbs, ts = [[gr[key] for gr in groups] for key in ("h", "nb", "t")]
        if l % 2 == 0:
            conv_bufs = (jnp.zeros((bsz, kw_a - 1, 3 * ha * hd), F32), state_conv_qkv[e])
            s0s = (jnp.zeros((bsz, ha, hd, hd), F32), state_delta[e])
            past = (cache_k[e], cache_v[e], cache_logf[e], page_table)
            mixed = _even_mixer(hs, nbs, ts, ewts, gnorm_a[e], conv_bufs, s0s, ha, hb, hd, past)
            for gi, (_, s_new, new_conv, kb, vb, logf) in enumerate(mixed):
                for key, val in (("delta", s_new), ("conv", new_conv), ("k", kb), ("v", vb), ("logf", logf)):
                    outs[key][gi].append(val)
        else:
            bufs = tuple((bk[o], bv[o]) for bk, bv in win_caches)
            mixed = _odd_mixer(hs, nbs, ts, w_in_o_b, w_out_o_b, hc, hd, bufs)
            for gi, (_, new_bufs) in enumerate(mixed):
                for g in range(len(C_GROUPS)):
                    wk[g][gi].append(new_bufs[g][0])
                    wv[g][gi].append(new_bufs[g][1])
        for gr in groups:
            gi, nb, t = gr["gi"], gr["nb"], gr["t"]
            m_l = mods[l][gi]
            om = mixed[gi][0]
            gr["x"], gr["h"] = _post_pre(gr["x"], nb, t, o=om, gate=m_l[:, 2], w_post=norm_post_mix[l],
                                         shift=m_l[:, 3], scale=m_l[:, 4], w_pre=norm_pre_ffn[l])
        ffn = _conv_ffn([gr["h"] for gr in groups], nbs, ts, fwts, l, state_ffn_conv[l])
        for gr in groups:
            gi, nb, t = gr["gi"], gr["nb"], gr["t"]
            m_l = mods[l][gi]
            of, new_buf = ffn[gi]
            outs["ffn"][gi].append(new_buf)
            if l + 1 < depth:
                m_n = mods[l + 1][gi]
                gr["x"], gr["h"] = _post_pre(gr["x"], nb, t, o=of, gate=m_l[:, 5], w_post=norm_post_ffn[l],
                                             shift=m_n[:, 0], scale=m_n[:, 1], w_pre=norm_pre_mix[l + 1])
            else:
                gr["x"], _ = _post_pre(gr["x"], nb, t, o=of, gate=m_l[:, 5], w_post=norm_post_ffn[l])

    stk = lambda lst: jnp.stack(lst, axis=0)
    res = [groups[0]["x"].reshape(bsz, seq, d), groups[1]["x"].reshape(dbsz, dseq, d)]
    for key in ("delta", "conv", "k", "v", "logf"):
        res += [stk(outs[key][0]), stk(outs[key][1])]
    for g in range(len(C_GROUPS)):
        res += [stk(wk[g][0]), stk(wk[g][1]), stk(wv[g][0]), stk(wv[g][1])]
    res += [stk(outs["ffn"][0]), stk(outs["ffn"][1])]
    return tuple(res)
```

```python
import functools
import math

import jax
import jax.numpy as jnp
from jax import lax
from jax.experimental import pallas as pl
from jax.experimental.pallas import tpu as pltpu

F32 = jnp.float32
BF16 = jnp.bfloat16
HI = lax.Precision.HIGHEST

C_GROUPS = ((128, 1), (512, 4), (2048, 16))
DELTA_CHUNK = 64
RMS_EPS = 1e-6
L2_EPS = 1e-6
NEG = -1e30
LANE = 128
SUBLANE = 8
VMEM_LIMIT = 58 * 1024 * 1024
MOD_ROWS = 16
FFN_TILE = 512


def _cp(*sem):
    return pltpu.CompilerParams(dimension_semantics=sem, vmem_limit_bytes=VMEM_LIMIT)


def _dot(a, b, prec=None):
    return jnp.dot(a, b, preferred_element_type=F32, precision=prec)


def _dot_nt(a, b, prec=None):
    return lax.dot_general(a, b, (((1,), (1,)), ((), ())), preferred_element_type=F32, precision=prec)


def _dot_tn(a, b, prec=None):
    return lax.dot_general(a, b, (((0,), (0,)), ((), ())), preferred_element_type=F32, precision=prec)


def _dot3(a, b):
    ah, bh = a.astype(BF16), b.astype(BF16)
    al = (a - ah.astype(F32)).astype(BF16)
    bl = (b - bh.astype(F32)).astype(BF16)
    return _dot(ah, bh) + (_dot(ah, bl) + _dot(al, bh))


def _sigmoid(x):
    return 1.0 / (1.0 + jnp.exp(-x))


def _silu(x):
    return x * _sigmoid(x)


def _softplus(x):
    return jnp.maximum(x, 0.0) + jnp.log1p(jnp.exp(-jnp.abs(x)))


def _gelu_tanh(x):
    return 0.5 * x * (1.0 + jnp.tanh(0.7978845608028654 * (x + 0.044715 * (x * x * x))))


def _rms(x, w):
    return x * lax.rsqrt(jnp.mean(x * x, axis=-1, keepdims=True) + RMS_EPS) * w


def _iota(shape, dim):
    return lax.broadcasted_iota(jnp.int32, shape, dim)


def _row_to_col(row):
    n = row.shape[1]
    eye = _iota((n, n), 0) == _iota((n, n), 1)
    return jnp.sum(jnp.where(eye, jnp.broadcast_to(row, (n, n)), 0.0), axis=1, keepdims=True)


def _col_to_row(col):
    n = col.shape[0]
    eye = _iota((n, n), 0) == _iota((n, n), 1)
    return jnp.sum(jnp.where(eye, jnp.broadcast_to(col, (n, n)), 0.0), axis=0, keepdims=True)


def _pick_tile(n, pref):
    if n <= pref:
        return n
    t = (pref // LANE) * LANE
    while t >= LANE:
        if n % t == 0:
            return t
        t -= LANE
    return n


def _log2(n):
    assert n > 0 and n & (n - 1) == 0, n
    return n.bit_length() - 1


def _mod_kernel(c_ref, w_ref, b_ref, o_ref):
    a = _silu(c_ref[...]).astype(BF16)
    o_ref[...] = _dot(a, w_ref[...].astype(BF16)) + b_ref[...]


def _mod(c_all, w_mod, b_mod):
    depth, d, n = w_mod.shape
    r = c_all.shape[0]
    tn = _pick_tile(n, 512)
    return pl.pallas_call(
        _mod_kernel,
        grid=(depth, n // tn),
        in_specs=[pl.BlockSpec((r, d), lambda l, j: (0, 0)),
                  pl.BlockSpec((None, d, tn), lambda l, j: (l, 0, j)),
                  pl.BlockSpec((None, 1, tn), lambda l, j: (l, 0, j))],
        out_specs=pl.BlockSpec((None, r, tn), lambda l, j: (l, 0, j)),
        out_shape=jax.ShapeDtypeStruct((depth, r, n), F32),
        compiler_params=_cp("arbitrary", "arbitrary"),
        name="mod",
    )(c_all, w_mod, b_mod.reshape(depth, 1, n))


def _post_pre_kernel(*refs, has_o, has_h):
    refs = list(refs)
    x = refs.pop(0)[...]
    if has_o:
        o_ref, gate_ref, wpost_ref = refs.pop(0), refs.pop(0), refs.pop(0)
    if has_h:
        shift_ref, scale_ref, wpre_ref = refs.pop(0), refs.pop(0), refs.pop(0)
    if has_o:
        x = x + gate_ref[...] * _rms(o_ref[...].astype(F32), wpost_ref[...])
        refs.pop(0)[...] = x
    if has_h:
        h = _rms(x, wpre_ref[...]) * (1.0 + scale_ref[...]) + shift_ref[...]
        refs.pop(0)[...] = h.astype(BF16)


def _post_pre(x, nb, t, o=None, gate=None, w_post=None, shift=None, scale=None, w_pre=None):
    m, d = x.shape
    has_o, has_h = o is not None, shift is not None
    if t >= 128:
        tr = 256 if t % 256 == 0 else 128
        per = t // tr
        vec = lambda v: v.reshape(nb, 1, d)
        vec_spec = pl.BlockSpec((None, 1, d), lambda i: (i // per, 0, 0))
    else:
        tr = m
        vec = lambda v: jnp.repeat(v, t, axis=0).reshape(1, m, d)
        vec_spec = pl.BlockSpec((None, m, d), lambda i: (0, 0, 0))
    row_spec = pl.BlockSpec((tr, d), lambda i: (i, 0))
    w_spec = pl.BlockSpec((1, d), lambda i: (0, 0))
    args, in_specs, out_specs, out_shape = [x], [row_spec], [], []
    if has_o:
        args += [o, vec(gate), w_post.reshape(1, d)]
        in_specs += [row_spec, vec_spec, w_spec]
        out_specs.append(row_spec)
        out_shape.append(jax.ShapeDtypeStruct((m, d), F32))
    if has_h:
        args += [vec(shift), vec(scale), w_pre.reshape(1, d)]
        in_specs += [vec_spec, vec_spec, w_spec]
        out_specs.append(row_spec)
        out_shape.append(jax.ShapeDtypeStruct((m, d), BF16))
    outs = pl.pallas_call(
        functools.partial(_post_pre_kernel, has_o=has_o, has_h=has_h),
        grid=(m // tr,), in_specs=in_specs, out_specs=out_specs, out_shape=out_shape,
        compiler_params=_cp("arbitrary"), name="post_pre",
    )(*args)
    outs = list(outs)
    x_new = outs.pop(0) if has_o else x
    h = outs.pop(0) if has_h else None
    return x_new, h


def _mm_kernel(*refs, n_pairs, nk):
    a_refs, b_refs = refs[:n_pairs], refs[n_pairs:2 * n_pairs]
    o_ref = refs[2 * n_pairs]
    part = _dot(a_refs[0][...], b_refs[0][...])
    for a_ref, b_ref in zip(a_refs[1:], b_refs[1:]):
        part = part + _dot(a_ref[...], b_ref[...])
    if nk == 1:
        o_ref[...] = part.astype(o_ref.dtype)
        return
    acc_ref = refs[2 * n_pairs + 1]
    k = pl.program_id(2)

    @pl.when(k == 0)
    def _():
        acc_ref[...] = part

    @pl.when(k > 0)
    def _():
        acc_ref[...] += part

    @pl.when(k == nk - 1)
    def _():
        o_ref[...] = acc_ref[...].astype(o_ref.dtype)


def _matmul(a_list, b_list, out_dtype, tm=1024, tn=1024, tk=None, cols=None, layer=None):
    m, kdim = a_list[0].shape
    c0, n = (0, b_list[0].shape[-1]) if cols is None else cols
    tm = min(tm, m)
    tn = _pick_tile(n, tn)
    tk = kdim if tk is None else _pick_tile(kdim, tk)
    nk = kdim // tk
    assert m % tm == 0 and n % tn == 0 and kdim % tk == 0 and c0 % tn == 0
    jb = c0 // tn
    n_pairs = len(a_list)
    if layer is None:
        b_spec = pl.BlockSpec((tk, tn), lambda i, j, k: (k, j + jb))
    else:
        b_spec = pl.BlockSpec((None, tk, tn), lambda i, j, k: (layer, k, j + jb))
    in_specs = [pl.BlockSpec((tm, tk), lambda i, j, k: (i, k))] * n_pairs + [b_spec] * n_pairs
    scratch = [pltpu.VMEM((tm, tn), F32)] if nk > 1 else []
    return pl.pallas_call(
        functools.partial(_mm_kernel, n_pairs=n_pairs, nk=nk),
        grid=(m // tm, n // tn, nk), in_specs=in_specs,
        out_specs=pl.BlockSpec((tm, tn), lambda i, j, k: (i, j)),
        out_shape=jax.ShapeDtypeStruct((m, n), out_dtype),
        scratch_shapes=scratch,
        compiler_params=_cp("arbitrary", "arbitrary", "arbitrary"), name="matmul",
    )(*a_list, *b_list)


def _mmw_kernel(*refs, n_pairs, ranges, nm, extra, w_rows):
    n_out = len(ranges)
    mul = _dot_nt if w_rows else _dot
    a_refs, refs = refs[:n_pairs], refs[n_pairs:]
    if extra:
        a2_refs, refs = refs[:n_pairs], refs[n_pairs:]
    b_refs, refs = refs[:n_pairs], refs[n_pairs:]
    o_refs, refs = refs[:n_out], refs[n_out:]
    if extra:
        o2_refs, refs = refs[:n_out], refs[n_out:]
    wb_refs = refs
    j, i = pl.program_id(0), pl.program_id(1)

    @pl.when(i == 0)
    def _():
        for b_ref, wb_ref in zip(b_refs, wb_refs):
            wb_ref[...] = b_ref[...].astype(BF16)

    def product(lhs_refs, out_refs):
        acc = mul(lhs_refs[0][...], wb_refs[0][...])
        for a_ref, wb_ref in zip(lhs_refs[1:], wb_refs[1:]):
            acc = acc + mul(a_ref[...], wb_ref[...])
        if n_out == 1:
            out_refs[0][...] = acc.astype(out_refs[0].dtype)
            return
        for o_ref, (lo, hi) in zip(out_refs, ranges):
            @pl.when((j >= lo) & (j < hi))
            def _(o_ref=o_ref):
                o_ref[...] = acc.astype(o_ref.dtype)

    if not extra:
        product(a_refs, o_refs)
        return

    @pl.when(i < nm)
    def _():
        product(a_refs, o_refs)

    @pl.when(i == nm)
    def _():
        product(a2_refs, o2_refs)


def _matmul_w32(a_list, w_list, out_dtype, out_cols=None, col0=0, tm=512, tn=1024, w_buffers=2,
                a2_list=None, w_rows=False):
    m, kdim = a_list[0].shape
    tm = min(tm, m)
    out_cols = [w_list[0].shape[1]] if out_cols is None else out_cols
    n = sum(out_cols)
    tn = _pick_tile(math.gcd(col0, *out_cols), tn)
    assert m % tm == 0
    nm = m // tm
    jb = col0 // tn
    row_blk = [p if w.shape[0] != kdim else 0 for p, w in enumerate(w_list)]
    ranges, lo = [], 0
    for c in out_cols:
        ranges.append((lo, lo + c // tn))
        lo += c // tn
    n_pairs = len(a_list)
    extra = a2_list is not None
    m2 = a2_list[0].shape[0] if extra else 0

    def out_map(lo, hi, second=False):
        def index(j, i):
            inside = (j >= lo) & (j < hi)
            ii = jnp.where(inside, jnp.minimum(i, nm - 1), jnp.where(j < lo, 0, nm - 1))
            return (0 if second else ii), jnp.clip(j - lo, 0, hi - lo - 1)
        return index

    in_specs = [pl.BlockSpec((tm, kdim), lambda j, i: (jnp.minimum(i, nm - 1), 0))] * n_pairs
    out_specs = [pl.BlockSpec((tm, tn), out_map(lo, hi)) for lo, hi in ranges]
    out_shape = [jax.ShapeDtypeStruct((m, c), out_dtype) for c in out_cols]
    args = list(a_list)
    if extra:
        in_specs += [pl.BlockSpec((m2, kdim), lambda j, i: (0, 0))] * n_pairs
        out_specs += [pl.BlockSpec((m2, tn), out_map(lo, hi, True)) for lo, hi in ranges]
        out_shape += [jax.ShapeDtypeStruct((m2, c), out_dtype) for c in out_cols]
        args += list(a2_list)
    if w_rows:
        assert n_pairs == 1 and w_list[0].shape[1] == kdim
        w_block = (tn, kdim)
        in_specs += [pl.BlockSpec(w_block, lambda j, i: (j + jb, 0), pipeline_mode=pl.Buffered(w_buffers))]
    else:
        w_block = (kdim, tn)
        in_specs += [pl.BlockSpec(w_block, lambda j, i, rb=rb: (rb, j + jb),
                                  pipeline_mode=pl.Buffered(w_buffers)) for rb in row_blk]
    outs = pl.pallas_call(
        functools.partial(_mmw_kernel, n_pairs=n_pairs, ranges=ranges, nm=nm, extra=extra, w_rows=w_rows),
        grid=(n // tn, nm + (1 if extra else 0)),
        in_specs=in_specs, out_specs=out_specs, out_shape=out_shape,
        scratch_shapes=[pltpu.VMEM(w_block, BF16)] * n_pairs,
        compiler_params=_cp("arbitrary", "arbitrary"), name="matmul_w32",
    )(*args, *w_list)
    k = len(out_cols)
    first = outs[0] if k == 1 else list(outs[:k])
    if not extra:
        return first
    return first, (outs[k] if k == 1 else list(outs[k:]))


def _gates_kernel(h_ref, w_ref, par_ref, g_ref, f_ref, carry_ref, *, ha, hb, seg):
    @pl.when(pl.program_id(1) == 0)
    def _():
        carry_ref[...] = jnp.zeros_like(carry_ref)

    p = _dot(h_ref[...], w_ref[...].astype(BF16))
    tr = p.shape[0]
    lane = _iota(p.shape, 1)
    a_log, bias = par_ref[0:1, :], par_ref[1:2, :]
    x = p + bias
    beta = _sigmoid(p)
    g = -jnp.exp(a_log) * _softplus(x)
    logf = -_softplus(-x)
    out = jnp.where(lane < ha, beta,
                    jnp.where(lane < 2 * ha, g, jnp.where(lane < 2 * ha + hb, logf, 0.0)))
    g_ref[...] = out
    rr, cc = _iota((tr, tr), 0), _iota((tr, tr), 1)
    tri = rr >= cc
    if seg is not None:
        tri = tri & ((rr >> _log2(seg)) == (cc >> _log2(seg)))
    cs = _dot(tri.astype(F32), out, HI) + carry_ref[0:1, :]
    f_ref[...] = cs
    carry_ref[0:1, :] = cs[tr - 1:tr, :]


def _gates(h, w_small, par, nb, t, ha, hb):
    m, d = h.shape
    if t >= 128:
        tr, per, seg, nb_grid = min(t, 512), t // min(t, 512), None, nb
    else:
        tr, per, seg, nb_grid = m, 1, t, 1
    spec = pl.BlockSpec((tr, LANE), lambda b, j: (b * per + j, 0))
    return pl.pallas_call(
        functools.partial(_gates_kernel, ha=ha, hb=hb, seg=seg),
        grid=(nb_grid, per),
        in_specs=[pl.BlockSpec((tr, d), lambda b, j: (b * per + j, 0)),
                  pl.BlockSpec((d, LANE), lambda b, j: (0, 0)),
                  pl.BlockSpec((SUBLANE, LANE), lambda b, j: (0, 0))],
        out_specs=[spec, spec],
        out_shape=[jax.ShapeDtypeStruct((m, LANE), F32)] * 2,
        scratch_shapes=[pltpu.VMEM((SUBLANE, LANE), F32)],
        compiler_params=_cp("arbitrary", "arbitrary"), name="gates",
    )(h, w_small, par)


def _each(f, *lists):
    return [f(*args) for args in zip(*lists)]


def _unit_lower_inverse(lows):
    c = lows[0].shape[0]
    assert c == 64
    r, s = _iota((c, c), 0), _iota((c, c), 1)
    eye = (r == s).astype(F32)
    same16 = (r >> 4) == (s >> 4)
    same32 = (r >> 5) == (s >> 5)
    ld = _each(lambda low: jnp.where(same16, low, 0.0), lows)
    x = _each(lambda a: eye - a, ld)
    p = _each(lambda a: _dot3(a, a), ld)
    for _ in range(2):
        xp = _each(lambda a, b: _dot3(jnp.concatenate([a, b], axis=0), b), x, p)
        x = _each(lambda a, b: a + b[:c], x, xp)
        p = _each(lambda b: b[c:], xp)
    x = _each(lambda a, b: a + _dot3(a, b), x, p)
    for keep in (same32 & jnp.logical_not(same16), jnp.logical_not(same32)):
        off = _each(lambda low: jnp.where(keep, low, 0.0), lows)
        y = _each(_dot3, x, off)
        x = _each(lambda a, b: a - _dot3(b, a), x, y)
    return x


def _delta_kernel(q_ref, k_ref, v_ref, z_ref, gt_ref, cwq_ref, cwk_ref, cwv_ref,
                  cbq_ref, cbk_ref, cbv_ref, s0_ref, gn_ref, o_ref, sfin_ref, ext_ref, s_ref,
                  *, hpb, ha, kw, valid_rows, nc):
    c_idx = pl.program_id(2)
    hblk = pl.program_id(1)
    c = q_ref.shape[0]
    hd = LANE

    @pl.when(c_idx == 0)
    def _():
        s_ref[...] = s0_ref[...]
        for i, cb in enumerate((cbq_ref, cbk_ref, cbv_ref)):
            ext_ref[i, 0:SUBLANE, :] = cb[...]

    conv = []
    for i, (r_ref, cw_ref) in enumerate(((q_ref, cwq_ref), (k_ref, cwk_ref), (v_ref, cwv_ref))):
        ext_ref[i, SUBLANE:SUBLANE + c, :] = r_ref[...]
        acc = ext_ref[i, SUBLANE:SUBLANE + c, :] * cw_ref[kw - 1:kw, :]
        for j in range(kw - 1):
            acc = acc + ext_ref[i, pl.ds(SUBLANE - (kw - 1) + j, c), :] * cw_ref[j:j + 1, :]
        ext_ref[i, 0:SUBLANE, :] = ext_ref[i, c:c + SUBLANE, :]
        acc = _silu(acc)
        if valid_rows < c:
            acc = jnp.where(_iota(acc.shape, 0) < valid_rows, acc, 0.0)
        conv.append(acc)
    xq, xk, xv = conv

    gt = gt_ref[...]
    rr, ss = _iota((c, c), 0), _iota((c, c), 1)
    incl, strict = rr >= ss, rr > ss
    gcum = _dot(incl.astype(F32), gt, HI)
    lane = _iota(gt.shape, 1)
    gn = gn_ref[...]
    heads = list(range(hpb))
    sls = [slice(j * hd, (j + 1) * hd) for j in heads]
    beta = [jnp.sum(jnp.where(lane == hblk * hpb + j, gt, 0.0), axis=1, keepdims=True) for j in heads]
    gc = [jnp.sum(jnp.where(lane == ha + hblk * hpb + j, gcum, 0.0), axis=1, keepdims=True) for j in heads]
    qh = [xq[:, sl] for sl in sls]
    qh = _each(lambda a: a * lax.rsqrt(jnp.sum(a * a, axis=-1, keepdims=True) + L2_EPS) * (hd ** -0.5), qh)
    kh = [xk[:, sl] for sl in sls]
    kh = _each(lambda a: a * lax.rsqrt(jnp.sum(a * a, axis=-1, keepdims=True) + L2_EPS), kh)
    decay = _each(lambda g: jnp.where(incl, jnp.exp(g - _col_to_row(g)), 0.0), gc)
    kb = _each(lambda a, b: a * b, kh, beta)
    vb = [xv[:, sl] * b for sl, b in zip(sls, beta)]
    egc = _each(jnp.exp, gc)
    kk = _each(lambda a, q, k: _dot_nt(jnp.concatenate([a, q], axis=0).astype(BF16), k.astype(BF16)),
               kb, qh, kh)
    low = _each(lambda a, d: jnp.where(strict, a[:c] * d, 0.0), kk, decay)
    attn = _each(lambda a, d: (a[c:] * d).astype(BF16), kk, decay)
    tmat = _unit_lower_inverse(low)
    uw = _each(lambda t, v, k, e: _dot(t.astype(BF16), jnp.concatenate([v, k * e], axis=1).astype(BF16)),
               tmat, vb, kb, egc)
    st = [s_ref[j] for j in heads]
    ws = _each(lambda a, q, e, s: _dot(jnp.concatenate([a[:, hd:], q * e], axis=0).astype(BF16),
                                       s.astype(BF16)), uw, qh, egc, st)
    v_new = _each(lambda a, b: (a[:, :hd] - b[:c]).astype(BF16), uw, ws)
    o = _each(lambda b, a, v: b[c:] + _dot(a, v), ws, attn, v_new)
    g_last = _each(lambda g: g[c - 1:c, :], gc)
    k_dec = _each(lambda k, gl, g: (k * jnp.exp(gl - g)).astype(BF16), kh, g_last, gc)
    s_new = _each(lambda s, gl, k, v: s * jnp.exp(gl) + _dot_tn(k, v), st, g_last, k_dec, v_new)
    for j in heads:
        s_ref[j] = s_new[j]
        o_ref[:, sls[j]] = (_rms(o[j], gn) * _silu(z_ref[:, sls[j]])).astype(o_ref.dtype)

    @pl.when(c_idx == nc - 1)
    def _():
        sfin_ref[...] = s_ref[...]


def _delta(proj, gates, conv_w, conv_buf8, s0, gnorm, nb, tp, valid_rows, ha, hpb=16):
    c = DELTA_CHUNK
    hd = LANE
    hpb = min(hpb, ha)
    w = hpb * hd
    nc = tp // c
    nhb = ha // hpb
    kw = conv_w.shape[0]
    cw8 = jnp.pad(conv_w, ((0, SUBLANE - kw), (0, 0)))

    def col(off):
        return pl.BlockSpec((c, w), lambda b, h, i, off=off: (b * nc + i, off * nhb + h))

    def cwspec(off):
        return pl.BlockSpec((SUBLANE, w), lambda b, h, i, off=off: (0, off * nhb + h))

    def cbspec(off):
        return pl.BlockSpec((None, SUBLANE, w), lambda b, h, i, off=off: (b, 0, off * nhb + h))

    state_spec = pl.BlockSpec((None, hpb, hd, hd), lambda b, h, i: (b, h, 0, 0))
    return pl.pallas_call(
        functools.partial(_delta_kernel, hpb=hpb, ha=ha, kw=kw, valid_rows=valid_rows, nc=nc),
        grid=(nb, nhb, nc),
        in_specs=[col(0), col(1), col(2), col(3),
                  pl.BlockSpec((c, LANE), lambda b, h, i: (b * nc + i, 0)),
                  cwspec(0), cwspec(1), cwspec(2), cbspec(0), cbspec(1), cbspec(2),
                  state_spec, pl.BlockSpec((1, hd), lambda b, h, i: (0, 0))],
        out_specs=[pl.BlockSpec((c, w), lambda b, h, i: (b * nc + i, h)), state_spec],
        out_shape=[jax.ShapeDtypeStruct((nb * tp, ha * hd), BF16),
                   jax.ShapeDtypeStruct(s0.shape, F32)],
        scratch_shapes=[pltpu.VMEM((3, c + SUBLANE, w), F32), pltpu.VMEM((hpb, hd, hd), F32)],
        compiler_params=_cp("arbitrary", "arbitrary", "arbitrary"), name="delta",
    )(proj, proj, proj, proj, gates, cw8, cw8, cw8, conv_buf8, conv_buf8, conv_buf8, s0,
      gnorm.reshape(1, hd))


def _softmax_step(carry, s, v, valid=None):
    m, l, acc = carry
    m_new = jnp.maximum(m, jnp.max(s, axis=-1, keepdims=True))
    alpha = jnp.exp(m - m_new)
    p = jnp.exp(s - m_new)
    if valid is not None:
        p = jnp.where(valid, p, 0.0)
    l = alpha * l + jnp.sum(p, axis=-1, keepdims=True)
    acc = alpha * acc + _dot(p.astype(BF16), v)
    return m_new, l, acc


def _softmax_init(rows, hd):
    return (jnp.full((rows, 1), NEG, F32), jnp.zeros((rows, 1), F32), jnp.zeros((rows, hd), F32))


def _fox_prompt_kernel(q_ref, k_ref, v_ref, f_ref, o_ref, kb_ref, vb_ref, *, tq, tk, scale, par):
    seq = q_ref.shape[0]
    kb_ref[...] = k_ref[...].astype(BF16)
    vb_ref[...] = v_ref[...].astype(BF16)
    col_minus_row = _iota((tq, tk), 1) - _iota((tq, tk), 0)
    nqb = seq // tq
    for c0 in range(0, nqb, par):
        chunk = list(range(c0, min(c0 + par, nqb)))
        qs = [(q_ref[pl.ds(qb * tq, tq), :] * scale).astype(BF16) for qb in chunk]
        fqs = [_row_to_col(f_ref[:, pl.ds(qb * tq, tq)]) for qb in chunk]
        carries = [_softmax_init(tq, LANE) for _ in chunk]
        n_steps = [-(-((qb + 1) * tq) // tk) for qb in chunk]
        for step in range(max(n_steps)):
            live = [i for i in range(len(chunk)) if step < n_steps[i]]
            k = kb_ref[pl.ds(step * tk, tk), :]
            v = vb_ref[pl.ds(step * tk, tk), :]
            fk = f_ref[:, pl.ds(step * tk, tk)]
            ss = [_dot_nt(qs[i], k) + (fqs[i] - fk) for i in live]
            for n, i in enumerate(live):
                if step * tk + tk - 1 > chunk[i] * tq:
                    ss[n] = jnp.where(col_minus_row <= chunk[i] * tq - step * tk, ss[n], NEG)
            ms = [jnp.maximum(carries[i][0], jnp.max(s, axis=-1, keepdims=True)) for i, s in zip(live, ss)]
            ps = [jnp.exp(s - m) for s, m in zip(ss, ms)]
            alphas = [jnp.exp(carries[i][0] - m) for i, m in zip(live, ms)]
            pvs = [_dot(p.astype(BF16), v) for p in ps]
            for i, m, p, a, pv in zip(live, ms, ps, alphas, pvs):
                carries[i] = (m, a * carries[i][1] + jnp.sum(p, axis=-1, keepdims=True),
                              a * carries[i][2] + pv)
        for qb, (_, l, acc) in zip(chunk, carries):
            o_ref[pl.ds(qb * tq, tq), :] = (acc / l).astype(o_ref.dtype)


def _fox_prompt(q_arr, q_col, k_arr, v_arr, f_rows, nb, s, nh, tq=128, tk=512, par=8):
    tq, tk = min(tq, s), min(tk, s)
    assert s % tq == 0 and s % tk == 0
    hd = LANE
    return pl.pallas_call(
        functools.partial(_fox_prompt_kernel, tq=tq, tk=tk, scale=hd ** -0.5, par=par),
        grid=(nb, nh),
        in_specs=[pl.BlockSpec((s, hd), lambda b, h: (b, q_col + h)),
                  pl.BlockSpec((s, hd), lambda b, h: (b, h)),
                  pl.BlockSpec((s, hd), lambda b, h: (b, h)),
                  pl.BlockSpec((None, None, 1, s), lambda b, h: (b, h, 0, 0))],
        out_specs=pl.BlockSpec((s, hd), lambda b, h: (b, h)),
        out_shape=jax.ShapeDtypeStruct((nb * s, nh * hd), BF16),
        scratch_shapes=[pltpu.VMEM((s, hd), BF16), pltpu.VMEM((s, hd), BF16)],
        compiler_params=_cp("arbitrary", "arbitrary"), name="fox_prompt",
    )(q_arr, k_arr, v_arr, f_rows)


def _fpast_kernel(pt_ref, *refs, group):
    lp_refs, o_ref, carry_ref = refs[:group], refs[group], refs[group + 1]

    @pl.when(pl.program_id(1) == 0)
    def _():
        carry_ref[...] = jnp.zeros_like(carry_ref)

    p = lp_refs[0].shape[0]
    upper = (_iota((p, p), 1) > _iota((p, p), 0)).astype(F32)
    for i in range(group):
        lp = lp_refs[i][...]
        o_ref[group - 1 - i] = -(_dot(upper, lp, HI) + carry_ref[0:1, :])
        carry_ref[0:1, :] = carry_ref[0:1, :] + jnp.sum(lp, axis=0, keepdims=True)


def _fpast(logf_pool, page_table):
    db, n_pages = page_table.shape
    _, p, nh = logf_pool.shape
    group = max(k for k in (16, 8, 4, 2, 1) if n_pages % k == 0)
    nblk = n_pages // group

    def lp_spec(i):
        return pl.BlockSpec((None, p, nh),
                            lambda b, j, pt, i=i: (pt[b, n_pages - 1 - (j * group + i)], 0, 0))

    return pl.pallas_call(
        functools.partial(_fpast_kernel, group=group),
        grid_spec=pltpu.PrefetchScalarGridSpec(
            num_scalar_prefetch=1, grid=(db, nblk),
            in_specs=[lp_spec(i) for i in range(group)],
            out_specs=pl.BlockSpec((None, group, p, nh), lambda b, j, pt: (b, nblk - 1 - j, 0, 0)),
            scratch_shapes=[pltpu.VMEM((SUBLANE, nh), F32)]),
        out_shape=jax.ShapeDtypeStruct((db, n_pages, p, nh), F32),
        compiler_params=_cp("arbitrary", "arbitrary"), name="fpast",
    )(page_table, *([logf_pool] * group))


def _fox_sample_kernel(pt_ref, *refs, nh, scale, n_steps, pp):
    q_ref, k_refs, v_refs = refs[0], refs[1:1 + pp], refs[1 + pp:1 + 2 * pp]
    (fp_ref, kn_ref, vn_ref, fn_ref, o_ref, m_ref, l_ref, acc_ref, bias_ref, s0_ref,
     s1_ref) = refs[1 + 2 * pp:]
    p = pl.program_id(1)
    r = q_ref.shape[0]
    rows = k_refs[0].shape[0]
    lognh = _log2(nh)
    fn = fn_ref[...]

    @pl.when(p == 0)
    def _():
        m_ref[...] = jnp.full_like(m_ref, NEG)
        l_ref[...] = jnp.zeros_like(l_ref)
        acc_ref[...] = jnp.zeros_like(acc_ref)
        s1_ref[...] = jnp.zeros_like(s1_ref)
        match = (_iota((r, rows), 0) & (nh - 1)) == (_iota((r, rows), 1) & (nh - 1))
        bias_ref[...] = jnp.where(match, _row_to_col(fn), NEG)

    q = (q_ref[...] * scale).astype(BF16)

    def update(s, v):
        m, l, acc = _softmax_step((m_ref[...], l_ref[...], acc_ref[...]), s, v)
        m_ref[...], l_ref[...], acc_ref[...] = m, l, acc

    def step(s_w, s_r):
        for i in range(pp):
            s_w[:, i * rows:(i + 1) * rows] = (_dot_nt(q, k_refs[i][...].astype(BF16))
                                               + (bias_ref[...] - fp_ref[i]))
        live = p > 0
        s = s_r[...]
        m_old = m_ref[...]
        m_new = jnp.where(live, jnp.maximum(m_old, jnp.max(s, axis=-1, keepdims=True)), m_old)
        pr = jnp.where(live, jnp.exp(s - m_new), 0.0)
        alpha = jnp.exp(m_old - m_new)
        l_ref[...] = alpha * l_ref[...] + jnp.sum(pr, axis=-1, keepdims=True)
        pr = pr.astype(BF16)
        pv = _dot(pr[:, :rows], v_refs[0][...].astype(BF16))
        for i in range(1, pp):
            pv = pv + _dot(pr[:, i * rows:(i + 1) * rows], v_refs[i][...].astype(BF16))
        acc_ref[...] = alpha * acc_ref[...] + pv
        m_ref[...] = m_new

    @pl.when(p % 2 == 0)
    def _():
        step(s0_ref, s1_ref)

    @pl.when(p % 2 == 1)
    def _():
        step(s1_ref, s0_ref)

    @pl.when(p == n_steps)
    def _():
        rr, cc = _iota((r, r), 0), _iota((r, r), 1)
        ok = ((rr & (nh - 1)) == (cc & (nh - 1))) & ((cc >> lognh) <= (rr >> lognh))
        s2 = _dot_nt(q, kn_ref[...].astype(BF16)) + _row_to_col(fn) - fn
        update(jnp.where(ok, s2, NEG), vn_ref[...].astype(BF16))
        o_ref[...] = (acc_ref[...] / l_ref[...]).astype(o_ref.dtype)


def _fox_sample(q, k_new, v_new, f_new, k_pool, v_pool, f_past, page_table, nh):
    db, r, hd = q.shape
    n_pages = page_table.shape[1]
    rows = k_pool.shape[1]
    tok = pl.BlockSpec((None, r, hd), lambda b, p, pt: (b, 0, 0))
    pp = max(k for k in (4, 2, 1) if n_pages % k == 0)
    n_steps = n_pages // pp
    last = n_steps - 1
    k_pages = [pl.BlockSpec((None, rows, hd),
                            lambda b, p, pt, i=i: (pt[b, jnp.minimum(p, last) * pp + i], 0, 0))
               for i in range(pp)]
    v_pages = [pl.BlockSpec((None, rows, hd),
                            lambda b, p, pt, i=i: (pt[b, jnp.maximum(p - 1, 0) * pp + i], 0, 0))
               for i in range(pp)]
    return pl.pallas_call(
        functools.partial(_fox_sample_kernel, nh=nh, scale=hd ** -0.5, n_steps=n_steps, pp=pp),
        grid_spec=pltpu.PrefetchScalarGridSpec(
            num_scalar_prefetch=1, grid=(db, n_steps + 1),
            in_specs=[tok] + k_pages + v_pages + [
                pl.BlockSpec((None, pp, 1, rows), lambda b, p, pt: (b, jnp.minimum(p, last), 0, 0)),
                tok, tok, pl.BlockSpec((None, 1, r), lambda b, p, pt: (b, 0, 0))],
            out_specs=tok,
            scratch_shapes=[pltpu.VMEM((r, 1), F32), pltpu.VMEM((r, 1), F32),
                            pltpu.VMEM((r, hd), F32), pltpu.VMEM((r, rows), F32),
                            pltpu.VMEM((r, pp * rows), F32), pltpu.VMEM((r, pp * rows), F32)]),
        out_shape=jax.ShapeDtypeStruct((db, r, hd), BF16),
        compiler_params=_cp("arbitrary", "arbitrary"), name="fox_sample",
    )(page_table, q, *([k_pool] * pp), *([v_pool] * pp), f_past, k_new, v_new, f_new)


def _dil_prompt_kernel(*refs, groups, tb, scale, par):
    ng = len(groups)
    q_refs, k_refs, v_refs = refs[:ng], refs[ng:2 * ng], refs[2 * ng:3 * ng]
    o_ref, m_ref, l_ref, acc_ref = refs[3 * ng:]
    seq = q_refs[0].shape[0]
    diff = _iota((tb, tb), 0) - _iota((tb, tb), 1)

    def rows_of(res, blk, dil):
        return pl.ds(res + blk * tb * dil, tb, stride=dil) if dil > 1 else pl.ds(blk * tb, tb)

    for g, (window, dil) in enumerate(groups):
        reach = window // dil
        nqb = seq // dil // tb
        back = -(-reach // tb)
        units = [(res, qb) for res in range(dil) for qb in range(nqb)]
        for c0 in range(0, len(units), par):
            chunk = units[c0:c0 + par]
            rows = [rows_of(res, qb, dil) for res, qb in chunk]
            qs = [(q_refs[g][rw, :] * scale).astype(BF16) for rw in rows]
            if g == 0:
                carries = [_softmax_init(tb, LANE) for _ in chunk]
            else:
                carries = [(m_ref[rw, :], l_ref[rw, :], acc_ref[rw, :]) for rw in rows]
            for step in range(back + 1):
                live = [i for i, (_, qb) in enumerate(chunk) if qb - step >= 0]
                if not live:
                    continue
                krows = [rows_of(chunk[i][0], chunk[i][1] - step, dil) for i in live]
                ks = [k_refs[g][rw, :].astype(BF16) for rw in krows]
                vs = [v_refs[g][rw, :].astype(BF16) for rw in krows]
                ss = [_dot_nt(qs[i], k) for i, k in zip(live, ks)]
                if step * tb - (tb - 1) < 0 or step * tb + (tb - 1) > reach:
                    dist = diff + step * tb
                    valid = (dist >= 0) & (dist <= reach)
                    ss = [jnp.where(valid, s, NEG) for s in ss]
                ms = [jnp.maximum(carries[i][0], jnp.max(s, axis=-1, keepdims=True)) for i, s in zip(live, ss)]
                ps = [jnp.exp(s - m) for s, m in zip(ss, ms)]
                alphas = [jnp.exp(carries[i][0] - m) for i, m in zip(live, ms)]
                pvs = [_dot(p.astype(BF16), v) for p, v in zip(ps, vs)]
                for i, m, p, a, pv in zip(live, ms, ps, alphas, pvs):
                    carries[i] = (m, a * carries[i][1] + jnp.sum(p, axis=-1, keepdims=True),
                                  a * carries[i][2] + pv)
            for rw, (m, l, acc) in zip(rows, carries):
                m_ref[rw, :], l_ref[rw, :], acc_ref[rw, :] = m, l, acc
    o_ref[...] = (acc_ref[...] / l_ref[...]).astype(o_ref.dtype)


def _dil_prompt(srcs, nb, s, nh, groups, tb=128, par=8):
    hd = LANE
    ng = len(groups)
    tb = min(tb, s)
    for window, dil in groups:
        assert s % (dil * tb) == 0 and window % dil == 0
    args, specs = [], []
    for which in range(3):
        for g in range(ng):
            arr, col = srcs[g][which]
            args.append(arr)
            specs.append(pl.BlockSpec((s, hd), lambda b, h, col=col: (b, col + h)))
    return pl.pallas_call(
        functools.partial(_dil_prompt_kernel, groups=groups, tb=tb, scale=hd ** -0.5, par=par),
        grid=(nb, nh),
        in_specs=specs,
        out_specs=pl.BlockSpec((s, hd), lambda b, h: (b, h)),
        out_shape=jax.ShapeDtypeStruct((nb * s, nh * hd), BF16),
        scratch_shapes=[pltpu.VMEM((s, 1), F32), pltpu.VMEM((s, 1), F32), pltpu.VMEM((s, hd), F32)],
        compiler_params=_cp("arbitrary", "arbitrary"), name="dil_prompt",
    )(*args)


def _dil_sample_kernel(*refs, groups, wbs, tb, nh, scale, plan):
    ng = len(groups)
    q_refs, kb_refs, vb_refs = refs[:ng], refs[ng:2 * ng], refs[2 * ng:3 * ng]
    kn_refs, vn_refs = refs[3 * ng:4 * ng], refs[4 * ng:5 * ng]
    o_ref, m_ref, l_ref, acc_ref = refs[5 * ng:5 * ng + 4]
    j = pl.program_id(1)
    r = q_refs[0].shape[0]
    rows = tb * nh
    lognh = _log2(nh)
    total = sum(steps for _, steps, _ in plan)

    @pl.when(j == 0)
    def _():
        m_ref[...] = jnp.full_like(m_ref, NEG)
        l_ref[...] = jnp.zeros_like(l_ref)
        acc_ref[...] = jnp.zeros_like(acc_ref)

    def update(s, v, valid):
        m, l, acc = _softmax_step((m_ref[...], l_ref[...], acc_ref[...]),
                                  jnp.where(valid, s, NEG), v, valid)
        m_ref[...], l_ref[...], acc_ref[...] = m, l, acc

    for g, (window, dil) in enumerate(groups):
        start, steps, nmb = plan[g]

        def block(g=g, window=window, dil=dil, start=start, nmb=nmb):
            rr, cc = _iota((r, rows), 0), _iota((r, rows), 1)
            res = (wbs[g] + (j - start) // nmb) % dil
            key_tok = (((j - start) % nmb) * tb + (cc >> lognh)) * dil + res
            delta = wbs[g] + (rr >> lognh) - key_tok
            valid = (((rr & (nh - 1)) == (cc & (nh - 1))) & ((delta & (dil - 1)) == 0)
                     & (delta <= window))
            q = (q_refs[g][...] * scale).astype(BF16)
            k = kb_refs[g][...].reshape(rows, LANE).astype(BF16)
            v = vb_refs[g][...].reshape(rows, LANE).astype(BF16)
            update(_dot_nt(q, k), v, valid)

        pl.when((j >= start) & (j < start + steps))(block)

    @pl.when(j == total - 1)
    def _():
        rr, cc = _iota((r, r), 0), _iota((r, r), 1)
        delta = (rr >> lognh) - (cc >> lognh)
        match = (rr & (nh - 1)) == (cc & (nh - 1))
        for g, (window, dil) in enumerate(groups):
            valid = match & (delta >= 0) & ((delta & (dil - 1)) == 0) & (delta <= window)
            q = (q_refs[g][...] * scale).astype(BF16)
            update(_dot_nt(q, kn_refs[g][...].astype(BF16)), vn_refs[g][...].astype(BF16), valid)
        o_ref[...] = (acc_ref[...] / l_ref[...]).astype(o_ref.dtype)


def _dil_sample(qs, kns, vns, kbufs, vbufs, groups, nh, t):
    db, r, hd = qs[0].shape
    ng = len(groups)
    wbs = [kb.shape[1] for kb in kbufs]
    assert all(wb % dil == 0 for wb, (_, dil) in zip(wbs, groups))
    per_res = [wb // dil for wb, (_, dil) in zip(wbs, groups)]
    tb = 128
    while any(n % tb for n in per_res):
        tb //= 2
    plan, start = [], 0
    for n, (_, dil) in zip(per_res, groups):
        steps = min(dil, t) * (n // tb)
        plan.append((start, steps, n // tb))
        start += steps
    total = start
    tok = pl.BlockSpec((None, r, hd), lambda b, j: (b, 0, 0))

    def buf_spec(g):
        first, steps, nmb = plan[g]
        dil, wb = groups[g][1], wbs[g]

        def index(b, j):
            sg = jnp.clip(j - first, 0, steps - 1)
            return b, sg % nmb, (wb + sg // nmb) % dil, 0, 0

        return pl.BlockSpec((None, tb, None, nh, hd), index)

    split = lambda a, g: a.reshape(db, per_res[g], groups[g][1], nh, hd)
    kbufs = [split(a, g) for g, a in enumerate(kbufs)]
    vbufs = [split(a, g) for g, a in enumerate(vbufs)]
    bspecs = [buf_spec(g) for g in range(ng)]
    return pl.pallas_call(
        functools.partial(_dil_sample_kernel, groups=groups, wbs=wbs, tb=tb, nh=nh,
                          scale=hd ** -0.5, plan=plan),
        grid=(db, total),
        in_specs=[tok] * ng + bspecs + bspecs + [tok] * (2 * ng),
        out_specs=tok,
        out_shape=jax.ShapeDtypeStruct((db, r, hd), BF16),
        scratch_shapes=[pltpu.VMEM((r, 1), F32), pltpu.VMEM((r, 1), F32), pltpu.VMEM((r, hd), F32)],
        compiler_params=_cp("arbitrary", "arbitrary"), name="dil_sample",
    )(*qs, *kbufs, *vbufs, *kns, *vns)


def _ffn_up_kernel(*refs, kw, tiles_per_batch, t_s, nm):
    nh = kw - 1
    hp_ref, hs_ref, wu_ref, wg_ref, cw_ref = refs[:5]
    halo_refs = refs[5:5 + nh]
    actp_ref, tailp_ref, acts_ref, gates_ref, ext_ref, wub_ref, wgb_ref = refs[5 + nh:]
    m = pl.program_id(1)
    tn = wub_ref.shape[1]

    @pl.when(m == 0)
    def _():
        wub_ref[...] = wu_ref[...].astype(BF16)
        wgb_ref[...] = wg_ref[...].astype(BF16)

    @pl.when((m == 0) & (pl.program_id(0) == 0))
    def _():
        ext_ref[...] = jnp.zeros_like(ext_ref)

    def conv_gelu(gate, up, fix):
        rows = gate.shape[0]
        ext_ref[SUBLANE:SUBLANE + rows, :] = gate
        gc = gate * cw_ref[kw - 1:kw, :]
        for j in range(kw - 1):
            shift = kw - 1 - j
            gc = gc + fix(ext_ref[pl.ds(SUBLANE - shift, rows), :], shift) * cw_ref[j:j + 1, :]
        return _gelu_tanh(gc) * up

    @pl.when(m < nm)
    def _():
        h = hp_ref[...]
        tm = h.shape[0]
        hw = tn // 2
        for c in range(2):
            cs = slice(c * hw, (c + 1) * hw)
            up = _dot(h, wub_ref[:, cs])
            gate = _dot(h, wgb_ref[:, cs])
            prev = jnp.where((m % tiles_per_batch) == 0, 0.0, ext_ref[tm:tm + SUBLANE, cs])
            ext_ref[0:SUBLANE, cs] = prev
            tailp_ref[:, cs] = gate[tm - SUBLANE:, :]
            ext_ref[SUBLANE:SUBLANE + tm, cs] = gate
            gc = gate * cw_ref[kw - 1:kw, cs]
            for j in range(kw - 1):
                gc = gc + ext_ref[pl.ds(SUBLANE - (kw - 1 - j), tm), cs] * cw_ref[j:j + 1, cs]
            actp_ref[:, cs] = (_gelu_tanh(gc) * up).astype(actp_ref.dtype)

    @pl.when(m == nm)
    def _():
        h = hs_ref[...]
        up = _dot(h, wub_ref[...])
        gate = _dot(h, wgb_ref[...])
        ext_ref[0:SUBLANE, :] = jnp.zeros((SUBLANE, tn), F32)
        gates_ref[...] = gate
        pos = _iota(gate.shape, 0) & (t_s - 1)
        fix = lambda sh, shift: jnp.where(pos < shift, halo_refs[shift - 1][...], sh)
        acts_ref[...] = conv_gelu(gate, up, fix).astype(acts_ref.dtype)


def _ffn_up(h_p, h_s, wu, wg, layer, cw8, kw, t_p, t_s, halos, tm=512, tn=FFN_TILE):
    m, d = h_p.shape
    ms = h_s.shape[0]
    n_pad = wu.shape[-1]
    tm = min(tm, t_p)
    tn = min(tn, -(-n_pad // LANE) * LANE)
    nm, nn = m // tm, -(-n_pad // tn)
    _log2(t_s)
    last = nm - 1
    s_spec = pl.BlockSpec((ms, tn), lambda j, i: (0, j))
    in_specs = [pl.BlockSpec((tm, d), lambda j, i: (jnp.minimum(i, last), 0)),
                pl.BlockSpec((ms, d), lambda j, i: (0, 0)),
                pl.BlockSpec((None, d, tn), lambda j, i: (layer, 0, j)),
                pl.BlockSpec((None, d, tn), lambda j, i: (layer, 0, j)),
                pl.BlockSpec((SUBLANE, tn), lambda j, i: (0, j))] + [s_spec] * len(halos)
    return pl.pallas_call(
        functools.partial(_ffn_up_kernel, kw=kw, tiles_per_batch=max(t_p // tm, 1), t_s=t_s, nm=nm),
        grid=(nn, nm + 1), in_specs=in_specs,
        out_specs=[pl.BlockSpec((tm, tn), lambda j, i: (jnp.minimum(i, last), j)),
                   pl.BlockSpec((None, SUBLANE, tn), lambda j, i: (jnp.minimum(i, last), 0, j)),
                   s_spec, s_spec],
        out_shape=[jax.ShapeDtypeStruct((m, n_pad), BF16),
                   jax.ShapeDtypeStruct((nm, SUBLANE, n_pad), F32),
                   jax.ShapeDtypeStruct((ms, n_pad), BF16), jax.ShapeDtypeStruct((ms, n_pad), F32)],
        scratch_shapes=[pltpu.VMEM((max(tm, ms) + SUBLANE, tn), F32), pltpu.VMEM((d, tn), BF16),
                        pltpu.VMEM((d, tn), BF16)],
        compiler_params=_cp("arbitrary", "arbitrary"), name="ffn_up",
    )(h_p, h_s, wu, wg, cw8, *halos)


def _even_weights(w_in, conv_w, a_log, dt_bias, b_f, w_out, ha, hb, hd):
    wa, wb = ha * hd, hb * hd
    o2 = 4 * wa
    o4 = o2 + 2 * ha
    o5 = o4 + 3 * wb
    w_t = jnp.swapaxes(w_in, 0, 1)
    w_main = (w_t, w_t[o4:o5])
    small = jnp.swapaxes(jnp.concatenate([w_t[o2:o4], w_t[o5:]], axis=0), 0, 1)
    assert small.shape[1] <= LANE
    w_small = jnp.pad(small, ((0, 0), (0, LANE - small.shape[1])))
    par = jnp.zeros((SUBLANE, LANE), F32)
    par = par.at[0, ha:2 * ha].set(a_log.astype(F32))
    par = par.at[1, ha:2 * ha].set(dt_bias.astype(F32))
    par = par.at[1, 2 * ha:2 * ha + hb].set(b_f.astype(F32))
    return w_main, w_small, par, conv_w, w_out


def _even_mixer(hs, nbs, ts, wts, gnorm, conv_bufs, s0s, ha, hb, hd, past):
    (w_in, w_fox), _, _, _, w_out = wts
    wa, wb = ha * hd, hb * hd
    projs = _matmul_w32([hs[0]], [w_in], F32, out_cols=[4 * wa], tm=1024, w_buffers=1, a2_list=[hs[1]],
                        w_rows=True)
    foxes = _matmul_w32([hs[0]], [w_fox], F32, out_cols=[wb, wb, wb], tm=512, w_buffers=1,
                        a2_list=[hs[1]], w_rows=True)
    cores = [_even_core(hs[g], nbs[g], ts[g], projs[g], foxes[g], wts, gnorm, conv_bufs[g], s0s[g], ha, hb,
                        hd, past if g == 1 else None) for g in (0, 1)]
    outs = _matmul_w32([cores[0][0], cores[0][1]], [w_out, w_out], BF16, tm=512,
                       a2_list=[cores[1][0], cores[1][1]])
    return [(outs[g],) + tuple(cores[g][2:]) for g in (0, 1)]


def _even_core(h, nb, t, proj, fox, wts, gnorm, conv_buf, s0, ha, hb, hd, past):
    _, w_small, par, conv_w, _ = wts
    wa, wb = ha * hd, hb * hd
    kw = conv_w.shape[0]
    q_fox, k_new, v_new = fox
    gates, fcum = _gates(h, w_small, par, nb, t, ha, hb)

    c = DELTA_CHUNK
    tp = -(-t // c) * c
    conv_buf8 = jnp.pad(conv_buf.astype(F32), ((0, 0), (SUBLANE - (kw - 1), 0), (0, 0)))
    if tp == t:
        proj_d, gates_d = proj, gates
    else:
        pad = lambda a: jnp.pad(a.reshape(nb, t, -1), ((0, 0), (0, tp - t), (0, 0))).reshape(nb * tp, -1)
        proj_d, gates_d = pad(proj), pad(gates)
    o_a, s_new = _delta(proj_d, gates_d, conv_w, conv_buf8, s0, gnorm, nb, tp, min(t, c) if tp != t else c,
                        ha)
    if tp != t:
        o_a = o_a.reshape(nb, tp, wa)[:, :t].reshape(nb * t, wa)

    logf = gates[:, 2 * ha:2 * ha + hb].reshape(nb, t, hb)
    f_new = fcum[:, 2 * ha:2 * ha + hb].reshape(nb, t, hb)
    kb, vb = k_new.reshape(nb, t, hb, hd), v_new.reshape(nb, t, hb, hd)
    if past is None:
        f_rows = jnp.swapaxes(f_new, 1, 2).reshape(nb, hb, 1, t)
        o_b = _fox_prompt(q_fox, 0, k_new, v_new, f_rows, nb, t, hb)
    else:
        k_pool, v_pool, logf_pool, page_table = past
        n_pool, page = k_pool.shape[0], k_pool.shape[1]
        f_past = _fpast(logf_pool.astype(F32), page_table)
        f_past = f_past.reshape(nb, page_table.shape[1], 1, page * hb)
        o_b = _fox_sample(q_fox.reshape(nb, t * hb, hd), k_new.reshape(nb, t * hb, hd),
                          v_new.reshape(nb, t * hb, hd), f_new.reshape(nb, 1, t * hb),
                          k_pool.reshape(n_pool, page * hb, hd), v_pool.reshape(n_pool, page * hb, hd),
                          f_past, page_table, hb)
        o_b = o_b.reshape(nb * t, wb)
    keep = min(kw - 1, t)
    raw_tail = proj.reshape(nb, t, -1)[:, t - keep:, :3 * wa]
    new_conv = jnp.concatenate([conv_buf.astype(F32), raw_tail], axis=1)[:, -(kw - 1):]
    return o_a, o_b, s_new, new_conv, kb, vb, logf


def _odd_mixer(hs, nbs, ts, w_in, w_out, nh, hd, bufs):
    ng = len(C_GROUPS)
    wc = nh * hd
    n_main = (3 * ng - 2) * wc
    main, main_s = _matmul_w32([hs[0]], [w_in], F32, out_cols=[n_main], tm=1024, w_buffers=1,
                               a2_list=[hs[1]])
    (k_last, v_last), kv_s = _matmul_w32([hs[0]], [w_in], F32, out_cols=[wc, wc], col0=n_main, tm=512,
                                         w_buffers=1, a2_list=[hs[1]])
    nb, t = nbs[0], ts[0]
    srcs = [((main, g * 3 * nh), (main, g * 3 * nh + nh), (main, g * 3 * nh + 2 * nh))
            for g in range(ng - 1)]
    srcs.append(((main, (ng - 1) * 3 * nh), (k_last, 0), (v_last, 0)))
    o_p = _dil_prompt(srcs, nb, t, nh, C_GROUPS)
    m3 = main.reshape(nb, t, -1)
    bufs_p = []
    for g, (window, _) in enumerate(C_GROUPS):
        keep = min(window, t)
        if g < ng - 1:
            kv = [m3[:, t - keep:, (g * 3 + r) * wc:(g * 3 + r + 1) * wc] for r in (1, 2)]
        else:
            kv = [a.reshape(nb, t, wc)[:, t - keep:] for a in (k_last, v_last)]
        bufs_p.append(tuple(a.reshape(nb, keep, nh, hd) for a in kv))
    o_s, bufs_s = _odd_sample_core(jnp.concatenate([main_s] + list(kv_s), axis=1), nbs[1], ts[1], nh, hd, bufs)
    out_p, out_s = _matmul_w32([o_p], [w_out], BF16, tm=512, a2_list=[o_s])
    return (out_p, bufs_p), (out_s, bufs_s)


def _odd_sample_core(proj, nb, t, nh, hd, bufs):
    ng = len(C_GROUPS)
    new_bufs = []
    p6 = proj.reshape(nb, t, ng, 3, nh, hd)
    flat = lambda a: a.reshape(nb, -1, hd)
    qs = [flat(p6[:, :, g, 0]) for g in range(ng)]
    kns = [flat(p6[:, :, g, 1]) for g in range(ng)]
    vns = [flat(p6[:, :, g, 2]) for g in range(ng)]
    o = _dil_sample(qs, kns, vns, [bk for bk, _ in bufs], [bv for _, bv in bufs], C_GROUPS, nh, t)
    o = o.reshape(nb * t, nh * hd)
    for g in range(ng):
        bk, bv = bufs[g]
        wb = bk.shape[1]
        new_bufs.append((jnp.concatenate([bk, p6[:, :, g, 1]], axis=1)[:, -wb:],
                         jnp.concatenate([bv, p6[:, :, g, 2]], axis=1)[:, -wb:]))
    return o, new_bufs


def _ffn_weights(w_up, w_gate, conv_w, w_down):
    d_ff = w_up.shape[-1]
    kw = conv_w.shape[1]
    cw8 = jnp.pad(conv_w.astype(F32), ((0, 0), (0, SUBLANE - kw), (0, 0)))
    return w_up, w_gate, cw8, w_down.astype(BF16), kw, d_ff


def _conv_ffn(hs, nbs, ts, wts, layer, buf_s):
    wu, wg, cw8, wd, kw, d_ff = wts
    (nb, nbs_), (t, t_s) = nbs, ts
    halos = []
    for shift in range(1, kw):
        hl = jnp.zeros((nbs_, t_s, d_ff), F32)
        hl = hl.at[:, :shift].set(buf_s.astype(F32)[:, (kw - 1) - shift:])
        halos.append(hl.reshape(nbs_ * t_s, d_ff))
    act_p, tail, act_s, gate_s = _ffn_up(hs[0], hs[1], wu, wg, layer, cw8[layer], kw, t, t_s, halos)
    tiles = tail.shape[0] // nb
    gate_tail = tail.reshape(nb, tiles, SUBLANE, d_ff)[:, -1]
    new_buf_p = gate_tail[:, SUBLANE - (kw - 1):]
    new_buf_s = jnp.concatenate([buf_s.astype(F32), gate_s.reshape(nbs_, t_s, d_ff)], axis=1)[:, -(kw - 1):]
    out_p = _matmul([act_p], [wd], BF16, tm=512, tn=512, layer=layer)
    out_s = _matmul([act_s], [wd], BF16, tm=act_s.shape[0], tn=512, layer=layer)
    return (out_p, new_buf_p), (out_s, new_buf_s)


def kernel(x_prompt, x_sample, c_prompt, c_sample, state_delta, state_conv_qkv, cache_k, cache_v, cache_logf, cache_win_k0, cache_win_v0, cache_win_k1, cache_win_v1, cache_win_k2, cache_win_v2, state_ffn_conv, page_table, w_mod, b_mod, norm_pre_mix, norm_post_mix, norm_pre_ffn, norm_post_ffn, w_in_e, conv_a, a_log, dt_bias, gnorm_a, b_forget, w_out_e, w_in_o, w_out_o, w_up, w_gate, conv_ffn_w, w_down):
    bsz, seq, d = x_prompt.shape
    dbsz, dseq, _ = x_sample.shape
    depth = w_mod.shape[0]
    hd = gnorm_a.shape[-1]
    ha, hb, hc = a_log.shape[-1], b_forget.shape[-1], cache_win_k0.shape[3]
    assert hd == LANE and seq % DELTA_CHUNK == 0
    kw_a = conv_a.shape[1]
    win_caches = ((cache_win_k0, cache_win_v0), (cache_win_k1, cache_win_v1), (cache_win_k2, cache_win_v2))

    c_all = jnp.concatenate([c_prompt, c_sample], axis=0)
    c_all = jnp.pad(c_all, ((0, MOD_ROWS - c_all.shape[0]), (0, 0)))
    mod = _mod(c_all, w_mod, b_mod)
    mods = [(mod[l, :bsz].reshape(bsz, 6, d), mod[l, bsz:bsz + dbsz].reshape(dbsz, 6, d))
            for l in range(depth)]

    groups = [dict(x=x_prompt.reshape(bsz * seq, d), nb=bsz, t=seq, gi=0),
              dict(x=x_sample.reshape(dbsz * dseq, d), nb=dbsz, t=dseq, gi=1)]
    for gr in groups:
        m_0 = mods[0][gr["gi"]]
        _, gr["h"] = _post_pre(gr["x"], gr["nb"], gr["t"], shift=m_0[:, 0], scale=m_0[:, 1],
                               w_pre=norm_pre_mix[0])

    fwts = _ffn_weights(w_up, w_gate, conv_ffn_w, w_down)
    outs = {k: ([], []) for k in ("delta", "conv", "k", "v", "logf", "ffn")}
    wk = [([], []) for _ in C_GROUPS]
    wv = [([], []) for _ in C_GROUPS]
    for l in range(depth):
        if l % 2 == 0:
            e = l // 2
            ewts = _even_weights(w_in_e[e], conv_a[e], a_log[e], dt_bias[e], b_forget[e], w_out_e[e],
                                 ha, hb, hd)
        else:
            o = l // 2
            w_in_o_b, w_out_o_b = w_in_o[o], w_out_o[o]
        hs, nbs, ts = [[gr[key] for gr in groups] for key in ("h", "nb", "t")]
        if l % 2 == 0:
            conv_bufs = (jnp.zeros((bsz, kw_a - 1, 3 * ha * hd), F32), state_conv_qkv[e])
            s0s = (jnp.zeros((bsz, ha, hd, hd), F32), state_delta[e])
            past = (cache_k[e], cache_v[e], cache_logf[e], page_table)
            mixed = _even_mixer(hs, nbs, ts, ewts, gnorm_a[e], conv_bufs, s0s, ha, hb, hd, past)
            for gi, (_, s_new, new_conv, kb, vb, logf) in enumerate(mixed):
                for key, val in (("delta", s_new), ("conv", new_conv), ("k", kb), ("v", vb), ("logf", logf)):
                    outs[key][gi].append(val)
        else:
            bufs = tuple((bk[o], bv[o]) for bk, bv in win_caches)
            mixed = _odd_mixer(hs, nbs, ts, w_in_o_b, w_out_o_b, hc, hd, bufs)
            for gi, (_, new_bufs) in enumerate(mixed):
                for g in range(len(C_GROUPS)):
                    wk[g][gi].append(new_bufs[g][0])
                    wv[g][gi].append(new_bufs[g][1])
        for gr in groups:
            gi, nb, t = gr["gi"], gr["nb"], gr["t"]
            m_l = mods[l][gi]
            om = mixed[gi][0]
            gr["x"], gr["h"] = _post_pre(gr["x"], nb, t, o=om, gate=m_l[:, 2], w_post=norm_post_mix[l],
                                         shift=m_l[:, 3], scale=m_l[:, 4], w_pre=norm_pre_ffn[l])
        ffn = _conv_ffn([gr["h"] for gr in groups], nbs, ts, fwts, l, state_ffn_conv[l])
        for gr in groups:
            gi, nb, t = gr["gi"], gr["nb"], gr["t"]
            m_l = mods[l][gi]
            of, new_buf = ffn[gi]
            outs["ffn"][gi].append(new_buf)
            if l + 1 < depth:
                m_n = mods[l + 1][gi]
                gr["x"], gr["h"] = _post_pre(gr["x"], nb, t, o=of, gate=m_l[:, 5], w_post=norm_post_ffn[l],
                                             shift=m_n[:, 0], scale=m_n[:, 1], w_pre=norm_pre_mix[l + 1])
            else:
                gr["x"], _ = _post_pre(gr["x"], nb, t, o=of, gate=m_l[:, 5], w_post=norm_post_ffn[l])

    stk = lambda lst: jnp.stack(lst, axis=0)
    res = [groups[0]["x"].reshape(bsz, seq, d), groups[1]["x"].reshape(dbsz, dseq, d)]
    for key in ("delta", "conv", "k", "v", "logf"):
        res += [stk(outs[key][0]), stk(outs[key][1])]
    for g in range(len(C_GROUPS)):
        res += [stk(wk[g][0]), stk(wk[g][1]), stk(wv[g][0]), stk(wv[g][1])]
    res += [stk(outs["ffn"][0]), stk(outs["ffn"][1])]
    return tuple(res)
```
